```python
import jax, jax.numpy as jnp
from jax import lax
import numpy as np

D_MODEL = 1024
BATCH = 8
SEQ = 4096
DEPTH = 4

HEAD_DIM = 64
N_HEADS = D_MODEL // HEAD_DIM
N_A = max(1, DEPTH // 2)
N_B = DEPTH - N_A
DECAY_LORA = 64
AAA_LORA = 64
MV_LORA = 32
GATE_LORA = 160
N_GROUPS = 4
EXP_PER_GROUP = 8
N_EXPERTS = N_GROUPS * EXP_PER_GROUP
TOP_K = 2
D_EXPERT = 512
EXPERT_BLOCK = 128
Q_BLOCK = 128
DEEPNORM_ALPHA = (2 * DEPTH) ** 0.25
DEEPNORM_BETA = (8 * DEPTH) ** -0.25
LN_EPS = 1e-5
GN_EPS = 64e-5
QK_EPS = 1e-6

kernel_name = 'yoco_rwkv7_fox_hier_moe_deepnorm_adaln'


def layer_norm(x, g, b, eps=LN_EPS):
    xf = x.astype(jnp.float32)
    mu = xf.mean(-1, keepdims=True)
    var = jnp.square(xf - mu).mean(-1, keepdims=True)
    return ((xf - mu) * lax.rsqrt(var + eps)).astype(x.dtype) * g + b


def head_rms(t, g, eps=QK_EPS):
    tf = t.astype(jnp.float32)
    return (tf * lax.rsqrt(jnp.mean(tf * tf, -1, keepdims=True) + eps)).astype(t.dtype) * g


def wkv7_scan(r, w, k, v, a, b):
    B, S, H, N = r.shape

    def step(state, inp):
        r_t, w_t, k_t, v_t, a_t, b_t = inp
        sa = jnp.einsum('bhij,bhj->bhi', state, a_t)
        state = (state * w_t[:, :, None, :] + sa[..., None] * b_t[:, :, None, :]
                 + v_t[..., None] * k_t[:, :, None, :])
        return state, jnp.einsum('bhij,bhj->bhi', state, r_t)

    tm = lambda t: jnp.moveaxis(t, 1, 0)
    state0 = jnp.zeros((B, H, N, N), jnp.float32)
    _, y = lax.scan(step, state0, (tm(r), tm(w), tm(k), tm(v), tm(a), tm(b)))
    return jnp.moveaxis(y, 0, 1)


def rwkv7_time_mix(h, v_first, mu, w_rkv, w0, w1, w2, a0, a1, a2, g1, g2,
                   k_k, k_a, r_k, gn_g, gn_b, w_o, v_mix):
    B, S, D = h.shape
    f32 = jnp.float32
    xx = jnp.pad(h, ((0, 0), (1, 0), (0, 0)))[:, :-1] - h
    xr, xw, xk, xv, xa, xg = [h + xx * mu[i] for i in range(6)]
    r, k, v = jnp.einsum('nbsd,nde->nbse', jnp.stack([xr, xk, xv]), w_rkv)
    w_log = -jnp.exp((-jax.nn.softplus(-(w0 + jnp.tanh(xw @ w1) @ w2)) - 0.5).astype(f32))
    if v_mix is None:
        v_first = v
    else:
        v0, v1, v2 = v_mix
        v = v + (v_first - v) * jax.nn.sigmoid(v0 + (xv @ v1) @ v2)
    a = jax.nn.sigmoid(a0 + (xa @ a1) @ a2)
    g = jax.nn.sigmoid(xg @ g1) @ g2
    heads = lambda t: t.reshape(B, S, N_HEADS, HEAD_DIM).astype(f32)
    kk = heads(k * k_k)
    kk = kk / jnp.maximum(jnp.sqrt(jnp.sum(kk * kk, -1, keepdims=True)), 1e-12)
    k = k * (1 + (a - 1) * k_a)
    rh, kh, vh, ah = heads(r), heads(k), heads(v), heads(a)
    y = wkv7_scan(rh, jnp.exp(heads(w_log)), kh, vh, -kk, kk * ah)
    ym = y.mean(-1, keepdims=True)
    yv = jnp.square(y - ym).mean(-1, keepdims=True)
    y = ((y - ym) * lax.rsqrt(yv + GN_EPS)).reshape(B, S, D).astype(h.dtype) * gn_g + gn_b
    bonus = (jnp.sum(rh * kh * r_k, -1, keepdims=True) * vh).reshape(B, S, D).astype(h.dtype)
    return ((y + bonus) * g) @ w_o, v_first


def shared_kv(x, cs, ada_w, ada_b, w_kvf, b_f, k_norm):
    B, S, D = x.shape
    shift, scale = jnp.split(cs @ ada_w + ada_b, 2, axis=-1)
    hk = x * (1 + scale[:, None]) + shift[:, None]
    kvf = hk @ w_kvf
    k = head_rms(kvf[..., :D].reshape(B, S, N_HEADS, HEAD_DIM), k_norm)
    v = kvf[..., D:2 * D].reshape(B, S, N_HEADS, HEAD_DIM)
    log_f = jax.nn.log_sigmoid((kvf[..., 2 * D:] + b_f).astype(jnp.float32))
    fcum = jnp.cumsum(log_f, axis=1)
    return (k.transpose(0, 2, 1, 3), v.transpose(0, 2, 1, 3), fcum.transpose(0, 2, 1))


def forgetting_attention(q, k, v, fcum):
    S = q.shape[2]
    scale = HEAD_DIM ** -0.5
    outs = []
    for i in range(S // Q_BLOCK):
        lo, hi = i * Q_BLOCK, (i + 1) * Q_BLOCK
        logits = (jnp.einsum('bhqd,bhkd->bhqk', q[:, :, lo:hi], k[:, :, :hi]).astype(jnp.float32) * scale
                  + (fcum[:, :, lo:hi, None] - fcum[:, :, None, :hi]))
        causal = (lo + jnp.arange(Q_BLOCK))[:, None] >= jnp.arange(hi)[None, :]
        logits = jnp.where(causal, logits, -jnp.inf)
        p = jax.nn.softmax(logits, axis=-1).astype(v.dtype)
        outs.append(jnp.einsum('bhqk,bhkd->bhqd', p, v[:, :, :hi]))
    return jnp.concatenate(outs, axis=2)


def fox_layer(h, kv, w_qg, q_norm, w_o):
    B, S, D = h.shape
    k, v, fcum = kv
    qg = h @ w_qg
    q = head_rms(qg[..., :D].reshape(B, S, N_HEADS, HEAD_DIM), q_norm).transpose(0, 2, 1, 3)
    o = forgetting_attention(q, k, v, fcum).transpose(0, 2, 1, 3).reshape(B, S, D)
    return (o * jax.nn.sigmoid(qg[..., D:])) @ w_o


def grouped_experts(xf, expert_idx, gates, w_gate, w_up, w_down):
    T, D = xf.shape
    A = T * TOP_K
    n_blocks = -(-(A + N_EXPERTS * EXPERT_BLOCK) // EXPERT_BLOCK)
    n_slots = n_blocks * EXPERT_BLOCK
    flat_e = expert_idx.reshape(A)
    order = jnp.argsort(flat_e)
    sorted_e = flat_e[order]
    counts = jnp.bincount(flat_e, length=N_EXPERTS)
    padded = (counts + EXPERT_BLOCK - 1) // EXPERT_BLOCK * EXPERT_BLOCK
    pad_end = jnp.cumsum(padded)
    pad_start = pad_end - padded
    start = jnp.cumsum(counts) - counts
    dest_sorted = pad_start[sorted_e] + (jnp.arange(A) - start[sorted_e])
    dest = jnp.zeros((A,), jnp.int32).at[order].set(dest_sorted.astype(jnp.int32))
    slot_tok = jnp.zeros((n_slots,), jnp.int32).at[dest].set(jnp.arange(A, dtype=jnp.int32) // TOP_K)
    block_exp = jnp.minimum(jnp.searchsorted(pad_end, jnp.arange(n_blocks) * EXPERT_BLOCK, side='right'),
                            N_EXPERTS - 1)
    xs = xf[slot_tok].reshape(n_blocks, EXPERT_BLOCK, D)

    def one_block(args):
        xb, e = args
        hb = jax.nn.silu(xb @ w_gate[e]) * (xb @ w_up[e])
        return hb @ w_down[e]

    ys = lax.map(one_block, (xs, block_exp)).reshape(n_slots, D)
    y_assign = ys[dest].reshape(T, TOP_K, D)
    return jnp.einsum('tk,tkd->td', gates, y_assign)


def hier_moe(h, w_grp, b_grp, w_exp, b_exp, w_gate, w_up, w_down):
    B, S, D = h.shape
    T = B * S
    hf = h.reshape(T, D)
    p_grp = jax.nn.softmax((hf @ w_grp).astype(jnp.float32) + b_grp, axis=-1)
    g_sel = jnp.argmax(p_grp, axis=-1)
    p_g = jnp.take_along_axis(p_grp, g_sel[:, None], axis=-1)
    e_logits = ((hf @ w_exp).astype(jnp.float32) + b_exp).reshape(T, N_GROUPS, EXP_PER_GROUP)
    e_logits = jnp.take_along_axis(e_logits, g_sel[:, None, None], axis=1)[:, 0]
    top_p, top_i = lax.top_k(jax.nn.softmax(e_logits, axis=-1), TOP_K)
    gates = (p_g * top_p / jnp.sum(top_p, -1, keepdims=True)).astype(h.dtype)
    expert_idx = (g_sel[:, None] * EXP_PER_GROUP + top_i).astype(jnp.int32)
    return grouped_experts(hf, expert_idx, gates, w_gate, w_up, w_down).reshape(B, S, D)


def setup_inputs(seed: int = 0) -> dict:
    key = jax.random.key(seed)
    ks = iter(jax.random.split(key, 64))
    nrm = lambda shape, s: jax.random.normal(next(ks), shape, jnp.float32) * s
    uni = lambda shape, lo, hi: jax.random.uniform(next(ks), shape, jnp.float32, lo, hi)
    D, H, N = D_MODEL, N_HEADS, HEAD_DIM
    s_in = D ** -0.5
    beta = DEEPNORM_BETA
    nv = N_A - 1
    return {
        'x': nrm((BATCH, SEQ, D), 1.0),
        'c': nrm((BATCH, D), 1.0),
        'ada_w': nrm((DEPTH, D, 6 * D), 0.1 * s_in),
        'ada_b': nrm((DEPTH, 6 * D), 0.01),
        'ln_g': 1.0 + nrm((DEPTH, 2, D), 0.02),
        'ln_b': nrm((DEPTH, 2, D), 0.02),
        'rw_mu': uni((N_A, 6, D), 0.0, 1.0),
        'rw_rkv': nrm((N_A, 3, D, D), s_in) * jnp.array([1.0, 1.0, beta], jnp.float32)[None, :, None, None],
        'rw_w0': uni((N_A, D), -6.0, -1.0),
        'rw_w1': nrm((N_A, D, DECAY_LORA), s_in),
        'rw_w2': nrm((N_A, DECAY_LORA, D), 0.1 * DECAY_LORA ** -0.5),
        'rw_a0': nrm((N_A, D), 0.1),
        'rw_a1': nrm((N_A, D, AAA_LORA), s_in),
        'rw_a2': nrm((N_A, AAA_LORA, D), 0.1 * AAA_LORA ** -0.5),
        'rw_g1': nrm((N_A, D, GATE_LORA), s_in),
        'rw_g2': nrm((N_A, GATE_LORA, D), GATE_LORA ** -0.5),
        'rw_kk': 0.85 + nrm((N_A, D), 0.02),
        'rw_ka': 1.0 + nrm((N_A, D), 0.02),
        'rw_rk': nrm((N_A, H, N), 0.1),
        'rw_gn_g': 1.0 + nrm((N_A, D), 0.02),
        'rw_gn_b': nrm((N_A, D), 0.02),
        'rw_wo': nrm((N_A, D, D), s_in * beta),
        'rw_v0': 1.0 + nrm((nv, D), 0.1),
        'rw_v1': nrm((nv, D, MV_LORA), s_in),
        'rw_v2': nrm((nv, MV_LORA, D), 0.1 * MV_LORA ** -0.5),
        'kv_ada_w': nrm((D, 2 * D), 0.1 * s_in),
        'kv_ada_b': nrm((2 * D,), 0.01),
        'kv_w': jnp.concatenate([nrm((D, D), s_in), nrm((D, D), s_in * beta), nrm((D, H), s_in)], axis=1),
        'kv_fb': uni((H,), 0.0, 5.0),
        'kv_knorm': 1.0 + nrm((N,), 0.02),
        'fx_wqg': nrm((N_B, D, 2 * D), s_in),
        'fx_qnorm': 1.0 + nrm((N_B, N), 0.02),
        'fx_wo': nrm((N_B, D, D), s_in * beta),
        'moe_wgrp': nrm((DEPTH, D, N_GROUPS), s_in),
        'moe_bgrp': nrm((DEPTH, N_GROUPS), 0.01),
        'moe_wexp': nrm((DEPTH, D, N_EXPERTS), s_in),
        'moe_bexp': nrm((DEPTH, N_EXPERTS), 0.01),
        'moe_wgate': nrm((DEPTH, N_EXPERTS, D, D_EXPERT), s_in),
        'moe_wup': nrm((DEPTH, N_EXPERTS, D, D_EXPERT), s_in),
        'moe_wdown': nrm((DEPTH, N_EXPERTS, D_EXPERT, D), D_EXPERT ** -0.5 * beta),
    }


def reference(x, c, ada_w, ada_b, ln_g, ln_b, rw_mu, rw_rkv, rw_w0, rw_w1, rw_w2, rw_a0, rw_a1, rw_a2,
              rw_g1, rw_g2, rw_kk, rw_ka, rw_rk, rw_gn_g, rw_gn_b, rw_wo, rw_v0, rw_v1, rw_v2,
              kv_ada_w, kv_ada_b, kv_w, kv_fb, kv_knorm, fx_wqg, fx_qnorm, fx_wo,
              moe_wgrp, moe_bgrp, moe_wexp, moe_bexp, moe_wgate, moe_wup, moe_wdown):
    cs = jax.nn.silu(c)
    kv = None
    v_first = None
    for l in range(DEPTH):
        mod = cs @ ada_w[l] + ada_b[l]
        sh_m, sc_m, gt_m, sh_f, sc_f, gt_f = [m[:, None, :] for m in jnp.split(mod, 6, axis=-1)]
        h = x * (1 + sc_m) + sh_m
        if l < N_A:
            v_mix = None if l == 0 else (rw_v0[l - 1], rw_v1[l - 1], rw_v2[l - 1])
            y, v_first = rwkv7_time_mix(h, v_first, rw_mu[l], rw_rkv[l], rw_w0[l], rw_w1[l], rw_w2[l],
                                        rw_a0[l], rw_a1[l], rw_a2[l], rw_g1[l], rw_g2[l], rw_kk[l],
                                        rw_ka[l], rw_rk[l], rw_gn_g[l], rw_gn_b[l], rw_wo[l], v_mix)
        else:
            j = l - N_A
            y = fox_layer(h, kv, fx_wqg[j], fx_qnorm[j], fx_wo[j])
        x = layer_norm(DEEPNORM_ALPHA * x + (1 + gt_m) * y, ln_g[l, 0], ln_b[l, 0])
        h = x * (1 + sc_f) + sh_f
        y = hier_moe(h, moe_wgrp[l], moe_bgrp[l], moe_wexp[l], moe_bexp[l],
                     moe_wgate[l], moe_wup[l], moe_wdown[l])
        x = layer_norm(DEEPNORM_ALPHA * x + (1 + gt_f) * y, ln_g[l, 1], ln_b[l, 1])
        if l == N_A - 1:
            kv = shared_kv(x, cs, kv_ada_w, kv_ada_b, kv_w, kv_fb, kv_knorm)
    return x
```

```python
import functools
import math

import jax
import jax.numpy as jnp
from jax import lax
from jax.experimental import pallas as pl
from jax.experimental.pallas import tpu as pltpu

HEAD_DIM = 64
N_GROUPS = 4
EXP_PER_GROUP = 8
N_EXPERTS = N_GROUPS * EXP_PER_GROUP
TOP_K = 2
DEPTH = 4
N_A = 2
DEEPNORM_ALPHA = (2 * DEPTH) ** 0.25
LN_EPS = 1e-5
GN_EPS = 64e-5
QK_EPS = 1e-6

LANES = 128
VMEM_LIMIT_BYTES = 56 * 1024 * 1024
WKV_CHUNK = 64
WKV_TILE = 256
ROW_TILE = 512
RWKV_PRE_TILE = 256
ATTN_TILE = 512
MOE_BLOCK = 256
NEG_BIG = -1e30

f32 = jnp.float32
bf16 = jnp.bfloat16
HIGHEST = lax.Precision.HIGHEST


def _cparams(*sem):
    return pltpu.CompilerParams(dimension_semantics=sem, vmem_limit_bytes=VMEM_LIMIT_BYTES)


def _sigmoid(x):
    return 1.0 / (1.0 + jnp.exp(-x))


def _dot(a, b):
    return jnp.dot(a, b, preferred_element_type=f32)


def _dot_nt(a, b):
    return lax.dot_general(a, b, (((1,), (1,)), ((), ())), preferred_element_type=f32)


def _dot_tn(a, b):
    return lax.dot_general(a, b, (((0,), (0,)), ((), ())), preferred_element_type=f32)


def _full_spec(shape):
    n = len(shape)
    return pl.BlockSpec(shape, lambda *_: (0,) * n)


def _layer_norm_rows(u, g, b):
    mu = jnp.mean(u, axis=-1, keepdims=True)
    d = u - mu
    var = jnp.mean(d * d, axis=-1, keepdims=True)
    return d * lax.rsqrt(var + LN_EPS) * g + b


def _mod_kernel(c_ref, w_ref, b_ref, o_ref):
    c = c_ref[...]
    cs = c * _sigmoid(c)
    o_ref[0] = jnp.dot(cs, w_ref[0], preferred_element_type=f32, precision=HIGHEST) + b_ref[0]


def _adaln_mod(c, w, b):
    L, D, N = w.shape
    B = c.shape[0]
    tn = min(N, 1024)
    return pl.pallas_call(
        _mod_kernel,
        grid=(L, N // tn),
        in_specs=[
            _full_spec((B, D)),
            pl.BlockSpec((1, D, tn), lambda l, j: (l, 0, j)),
            pl.BlockSpec((1, 1, tn), lambda l, j: (l, 0, j)),
        ],
        out_specs=pl.BlockSpec((1, B, tn), lambda l, j: (l, 0, j)),
        out_shape=jax.ShapeDtypeStruct((L, B, N), f32),
        compiler_params=_cparams("arbitrary", "arbitrary"),
    )(c, w, b.reshape(L, 1, N))


def _rwkv_pre_kernel(has_vmix, x_ref, xp_ref, mod_ref, mu_ref, wrkv_ref, w0_ref, w1_ref, w2_ref,
                     a0_ref, a1_ref, a2_ref, g1_ref, g2_ref, kkp_ref, kap_ref, *rest):
    if has_vmix:
        v0_ref, v1_ref, v2_ref, vf_ref = rest[:4]
        rest = rest[4:]
    r_ref, k_ref, v_ref, wl_ref, kk_ref, b_ref, g_ref = rest
    D = x_ref.shape[-1]
    sh = mod_ref[0, :, 0:D]
    sc = mod_ref[0, :, D:2 * D]
    h = x_ref[0] * (1.0 + sc) + sh
    hp = xp_ref[0][7:8, :] * (1.0 + sc) + sh
    hp = jnp.where(pl.program_id(1) == 0, 0.0, hp)
    row = lax.broadcasted_iota(jnp.int32, h.shape, 0)
    h_prev = jnp.where(row == 0, hp, pltpu.roll(h, 1, axis=0))
    xx = h_prev - h
    mix = lambda j: (h + xx * mu_ref[j:j + 1, :]).astype(bf16)
    xr, xw, xk, xv, xa, xg = [mix(j) for j in range(6)]
    r = _dot(xr, wrkv_ref[0])
    k = _dot(xk, wrkv_ref[1])
    v = _dot(xv, wrkv_ref[2])
    zw = w0_ref[...] + _dot(jnp.tanh(_dot(xw, w1_ref[...])).astype(bf16), w2_ref[...])
    wl_ref[0] = -math.exp(-0.5) * _sigmoid(zw)
    a = _sigmoid(a0_ref[...] + _dot(_dot(xa, a1_ref[...]).astype(bf16), a2_ref[...]))
    g = _dot(_sigmoid(_dot(xg, g1_ref[...])).astype(bf16), g2_ref[...])
    if has_vmix:
        vmix = _sigmoid(v0_ref[...] + _dot(_dot(xv, v1_ref[...]).astype(bf16), v2_ref[...]))
        v = v + (vf_ref[0].astype(f32) - v) * vmix
    r_ref[0] = r.astype(bf16)
    v_ref[0] = v.astype(bf16)
    g_ref[0] = g.astype(bf16)
    k_ref[0] = (k * (1.0 + (a - 1.0) * kap_ref[...])).astype(bf16)
    kk = k * kkp_ref[...]
    for hd in range(D // HEAD_DIM):
        sl = slice(hd * HEAD_DIM, (hd + 1) * HEAD_DIM)
        seg = kk[:, sl]
        nrm = jnp.sqrt(jnp.sum(seg * seg, axis=-1, keepdims=True))
        seg = seg / jnp.maximum(nrm, 1e-12)
        kk_ref[0, :, sl] = seg.astype(bf16)
        b_ref[0, :, sl] = (seg * a[:, sl]).astype(bf16)


def _rwkv_pre(x, modl, mu, wrkv, w0, w1, w2, a0, a1, a2, g1, g2, kkp, kap, vmix):
    B, S, D = x.shape
    tm = min(RWKV_PRE_TILE, S)
    row = lambda t: t.reshape(1, D)
    args = [x, x, modl, mu, wrkv.astype(bf16), row(w0), w1.astype(bf16), w2.astype(bf16), row(a0),
            a1.astype(bf16), a2.astype(bf16), g1.astype(bf16), g2.astype(bf16), row(kkp), row(kap)]
    tile = pl.BlockSpec((1, tm, D), lambda b, i: (b, i, 0))
    in_specs = [
        tile,
        pl.BlockSpec((1, 8, D), lambda b, i: (b, jnp.maximum(i * (tm // 8) - 1, 0), 0)),
        pl.BlockSpec((1, 1, modl.shape[-1]), lambda b, i: (b, 0, 0)),
    ] + [_full_spec(a.shape) for a in args[3:]]
    if vmix is not None:
        v0, v1, v2, v_first = vmix
        extra = [row(v0), v1.astype(bf16), v2.astype(bf16)]
        args += extra + [v_first]
        in_specs += [_full_spec(a.shape) for a in extra] + [tile]
    out_bf = jax.ShapeDtypeStruct((B, S, D), bf16)
    out_shape = [out_bf, out_bf, out_bf, jax.ShapeDtypeStruct((B, S, D), f32), out_bf, out_bf, out_bf]
    return pl.pallas_call(
        functools.partial(_rwkv_pre_kernel, vmix is not None),
        grid=(B, S // tm),
        in_specs=in_specs,
        out_specs=[tile] * 7,
        out_shape=out_shape,
        compiler_params=_cparams("arbitrary", "arbitrary"),
    )(*args)


def _unit_lower_inverse(L):
    C = L.shape[0]
    ri = lax.broadcasted_iota(jnp.int32, (C, C), 0)
    ci = lax.broadcasted_iota(jnp.int32, (C, C), 1)
    P = jnp.where(ri == ci, 1.0, 0.0) + L
    Lk = L
    n = 1
    while 2 * n < C:
        Lb = Lk.astype(bf16)
        Lk = _dot(Lb, Lb)
        P = P + _dot(Lk.astype(bf16), P.astype(bf16))
        n *= 2
    return P


def _wkv_kernel(C, r_ref, k_ref, v_ref, wl_ref, kk_ref, b_ref, g_ref, rk_ref, gng_ref, gnb_ref,
                o_ref, state_ref):
    @pl.when(pl.program_id(2) == 0)
    def _():
        state_ref[...] = jnp.zeros_like(state_ref)

    T = r_ref.shape[1]
    N = HEAD_DIM
    ri = lax.broadcasted_iota(jnp.int32, (C, C), 0)
    ci = lax.broadcasted_iota(jnp.int32, (C, C), 1)
    tri_incl = jnp.where(ri >= ci, 1.0, 0.0).astype(f32)
    r2 = lax.broadcasted_iota(jnp.int32, (2 * C, 2 * C), 0)
    c2 = lax.broadcasted_iota(jnp.int32, (2 * C, 2 * C), 1)
    cc = jnp.where(c2 >= C, c2 - C, c2)
    keep = jnp.where(r2 >= C, r2 - C + 1, r2) > cc

    for c in range(T // C):
        sl = pl.ds(c * C, C)
        wl = wl_ref[0, sl, :]
        cum = jnp.dot(tri_incl, wl, preferred_element_type=f32, precision=HIGHEST)
        g_incl = jnp.exp(cum)
        g_prev = jnp.exp(cum - wl)
        g_inv = jnp.exp(-cum)
        g_last = g_incl[C - 1:C, :]
        r = r_ref[0, sl, :].astype(f32)
        k = k_ref[0, sl, :].astype(f32)
        v16 = v_ref[0, sl, :]
        kk = kk_ref[0, sl, :].astype(f32)
        b = b_ref[0, sl, :].astype(f32)
        at = (-kk * g_prev).astype(bf16)
        rt = (r * g_incl).astype(bf16)
        btf = b * g_inv
        ktf = k * g_inv
        bt = btf.astype(bf16)
        kt = ktf.astype(bf16)
        bc = (btf * g_last).astype(bf16)
        kc = (ktf * g_last).astype(bf16)
        rkr = r * k * rk_ref[...]
        ys = []
        bonus = []
        for hh in range(LANES // N):
            ls = slice(hh * N, (hh + 1) * N)
            X = jnp.concatenate([at[:, ls], rt[:, ls]], axis=0)
            Z = jnp.concatenate([bt[:, ls], kt[:, ls]], axis=0)
            Zc = jnp.concatenate([bc[:, ls], kc[:, ls]], axis=0)
            vh = v16[:, ls]
            M1 = jnp.where(keep, _dot_nt(X, Z), 0.0)
            Tinv = _unit_lower_inverse(M1[:C, :C])
            S0 = state_ref[hh]
            M2 = _dot_nt(X, S0.astype(bf16))
            Wm = _dot(M1[:C, C:].astype(bf16), vh) + M2[:C]
            U = _dot(Tinv.astype(bf16), Wm.astype(bf16))
            UV = jnp.concatenate([U.astype(bf16), vh], axis=0)
            ys.append(_dot(M1[C:, :].astype(bf16), UV) + M2[C:])
            state_ref[hh] = S0 * g_last[:, ls] + _dot_tn(UV, Zc)
            bonus.append(jnp.sum(rkr[:, ls], axis=-1, keepdims=True) * vh.astype(f32))
        yn = []
        for y in ys:
            ym = jnp.mean(y, axis=-1, keepdims=True)
            yd = y - ym
            yv = jnp.mean(yd * yd, axis=-1, keepdims=True)
            yn.append(yd * lax.rsqrt(yv + GN_EPS))
        yn = jnp.concatenate(yn, axis=1)
        bonus = jnp.concatenate(bonus, axis=1)
        z = (yn * gng_ref[...] + gnb_ref[...] + bonus) * g_ref[0, sl, :].astype(f32)
        o_ref[0, sl, :] = z.astype(bf16)


def _wkv7(r, k, v, wl, kk, b, g, r_k, gn_g, gn_b):
    B, S, D = r.shape
    T = min(WKV_TILE, S)
    C = min(WKV_CHUNK, T)
    tile = pl.BlockSpec((1, T, LANES), lambda bb, p, i: (bb, i, p))
    vec = pl.BlockSpec((1, LANES), lambda bb, p, i: (0, p))
    return pl.pallas_call(
        functools.partial(_wkv_kernel, C),
        grid=(B, D // LANES, S // T),
        in_specs=[tile] * 7 + [vec] * 3,
        out_specs=tile,
        out_shape=jax.ShapeDtypeStruct((B, S, D), bf16),
        scratch_shapes=[pltpu.VMEM((LANES // HEAD_DIM, HEAD_DIM, HEAD_DIM), f32)],
        compiler_params=_cparams("arbitrary", "arbitrary", "arbitrary"),
    )(r, k, v, wl, kk, b, g, r_k.reshape(1, D), gn_g.reshape(1, D), gn_b.reshape(1, D))


def _proj_ln_kernel(gate_idx, z_ref, w_ref, x_ref, mod_ref, lng_ref, lnb_ref, o_ref):
    D = x_ref.shape[-1]
    gt = mod_ref[0, :, gate_idx * D:(gate_idx + 1) * D]
    y = _dot(z_ref[0], w_ref[...])
    u = DEEPNORM_ALPHA * x_ref[0] + (1.0 + gt) * y
    o_ref[0] = _layer_norm_rows(u, lng_ref[...], lnb_ref[...])


def _proj_ln(z, w, x, modl, gate_idx, ln_g, ln_b):
    B, S, D = x.shape
    tm = min(ROW_TILE, S)
    tile = pl.BlockSpec((1, tm, D), lambda b, i: (b, i, 0))
    return pl.pallas_call(
        functools.partial(_proj_ln_kernel, gate_idx),
        grid=(B, S // tm),
        in_specs=[tile, _full_spec(w.shape), tile,
                  pl.BlockSpec((1, 1, modl.shape[-1]), lambda b, i: (b, 0, 0)),
                  _full_spec((1, D)), _full_spec((1, D))],
        out_specs=tile,
        out_shape=jax.ShapeDtypeStruct((B, S, D), f32),
        compiler_params=_cparams("arbitrary", "arbitrary"),
    )(z, w.astype(bf16), x, modl, ln_g.reshape(1, D), ln_b.reshape(1, D))


def _route_kernel(x_ref, mod_ref, w_ref, b_ref, h_ref, idx_ref, gate_ref):
    D = x_ref.shape[-1]
    sh = mod_ref[0, :, 3 * D:4 * D]
    sc = mod_ref[0, :, 4 * D:5 * D]
    h = x_ref[0] * (1.0 + sc) + sh
    h_ref[0] = h.astype(bf16)
    lg = jnp.dot(h, w_ref[...], preferred_element_type=f32, precision=HIGHEST) + b_ref[...]
    lane = lax.broadcasted_iota(jnp.int32, lg.shape, 1)
    glog = jnp.where(lane < N_GROUPS, lg, -jnp.inf)
    gmax = jnp.max(glog, axis=-1, keepdims=True)
    p_g = 1.0 / jnp.sum(jnp.exp(glog - gmax), axis=-1, keepdims=True)
    g_sel = jnp.min(jnp.where(glog == gmax, lane, LANES), axis=-1, keepdims=True)
    e_lo = N_GROUPS + EXP_PER_GROUP * g_sel
    elog = jnp.where((lane >= e_lo) & (lane < e_lo + EXP_PER_GROUP), lg, -jnp.inf)
    e1 = jnp.max(elog, axis=-1, keepdims=True)
    esum = jnp.sum(jnp.exp(elog - e1), axis=-1, keepdims=True)
    i1 = jnp.min(jnp.where(elog == e1, lane, LANES), axis=-1, keepdims=True)
    elog2 = jnp.where(lane == i1, -jnp.inf, elog)
    e2 = jnp.max(elog2, axis=-1, keepdims=True)
    i2 = jnp.min(jnp.where(elog2 == e2, lane, LANES), axis=-1, keepdims=True)
    p1 = 1.0 / esum
    p2 = jnp.exp(e2 - e1) / esum
    psum = p1 + p2
    idx_ref[0] = jnp.where(lane == 0, i1 - N_GROUPS, jnp.where(lane == 1, i2 - N_GROUPS, 0))
    gate_ref[0] = jnp.where(lane == 0, p_g * p1 / psum, jnp.where(lane == 1, p_g * p2 / psum, 0.0))


def _route(x, modl, w_grp, b_grp, w_exp, b_exp):
    B, S, D = x.shape
    tm = min(ROW_TILE, S)
    pad = LANES - N_GROUPS - N_EXPERTS
    w = jnp.concatenate([w_grp, w_exp, jnp.zeros((D, pad), f32)], axis=1)
    b = jnp.concatenate([b_grp, b_exp, jnp.zeros((pad,), f32)]).reshape(1, LANES)
    tile = pl.BlockSpec((1, tm, D), lambda bb, i: (bb, i, 0))
    small = pl.BlockSpec((1, tm, LANES), lambda bb, i: (bb, i, 0))
    return pl.pallas_call(
        _route_kernel,
        grid=(B, S // tm),
        in_specs=[tile, pl.BlockSpec((1, 1, modl.shape[-1]), lambda bb, i: (bb, 0, 0)),
                  _full_spec(w.shape), _full_spec(b.shape)],
        out_specs=[tile, small, small],
        out_shape=[jax.ShapeDtypeStruct((B, S, D), bf16),
                   jax.ShapeDtypeStruct((B, S, LANES), jnp.int32),
                   jax.ShapeDtypeStruct((B, S, LANES), f32)],
        compiler_params=_cparams("arbitrary", "arbitrary"),
    )(x, modl, w, b)


def _expert_kernel(be_ref, nu_ref, x_ref, wg_ref, wu_ref, wd_ref, o_ref):
    i = pl.program_id(0)

    @pl.when(i < nu_ref[0])
    def _():
        x = x_ref[...]
        hg = _dot(x, wg_ref[0])
        hu = _dot(x, wu_ref[0])
        o_ref[...] = _dot((hg * _sigmoid(hg) * hu).astype(bf16), wd_ref[0])

    @pl.when(i >= nu_ref[0])
    def _():
        o_ref[...] = jnp.zeros_like(o_ref)


def _expert_ffn(xs, block_exp, n_used, wg, wu, wd):
    n_slots, D = xs.shape
    E, _, F = wg.shape
    blk = MOE_BLOCK
    row_map = lambda i, be, nu: (jnp.minimum(i, nu[0] - 1), 0)
    grid_spec = pltpu.PrefetchScalarGridSpec(
        num_scalar_prefetch=2,
        grid=(n_slots // blk,),
        in_specs=[
            pl.BlockSpec((blk, D), row_map),
            pl.BlockSpec((1, D, F), lambda i, be, nu: (be[i], 0, 0)),
            pl.BlockSpec((1, D, F), lambda i, be, nu: (be[i], 0, 0)),
            pl.BlockSpec((1, F, D), lambda i, be, nu: (be[i], 0, 0)),
        ],
        out_specs=pl.BlockSpec((blk, D), lambda i, be, nu: (i, 0)),
    )
    return pl.pallas_call(
        _expert_kernel,
        grid_spec=grid_spec,
        out_shape=jax.ShapeDtypeStruct((n_slots, D), f32),
        compiler_params=_cparams("arbitrary"),
    )(block_exp, n_used, xs, wg, wu, wd)


def _dispatch(expert_idx, blk):
    T = expert_idx.shape[0]
    A = T * TOP_K
    n_blocks = A // blk + N_EXPERTS
    flat_e = expert_idx.reshape(A)
    order = jnp.argsort(flat_e)
    sorted_e = flat_e[order]
    counts = jnp.bincount(flat_e, length=N_EXPERTS)
    padded = (counts + blk - 1) // blk * blk
    pad_end = jnp.cumsum(padded)
    pad_start = pad_end - padded
    start = jnp.cumsum(counts) - counts
    dest_sorted = pad_start[sorted_e] + (jnp.arange(A) - start[sorted_e])
    dest = jnp.zeros((A,), jnp.int32).at[order].set(dest_sorted.astype(jnp.int32))
    slot_tok = jnp.zeros((n_blocks * blk,), jnp.int32).at[dest].set(jnp.arange(A, dtype=jnp.int32) // TOP_K)
    block_exp = jnp.minimum(jnp.searchsorted(pad_end, jnp.arange(n_blocks) * blk, side='right'),
                            N_EXPERTS - 1).astype(jnp.int32)
    n_used = (pad_end[-1] // blk).astype(jnp.int32).reshape(1)
    return dest, slot_tok, block_exp, n_used


def _combine_ln_kernel(x_ref, y0_ref, y1_ref, gate_ref, mod_ref, lng_ref, lnb_ref, o_ref):
    D = x_ref.shape[-1]
    gt = mod_ref[0, :, 5 * D:6 * D]
    gates = gate_ref[0]
    y = gates[:, 0:1] * y0_ref[0] + gates[:, 1:2] * y1_ref[0]
    u = DEEPNORM_ALPHA * x_ref[0] + (1.0 + gt) * y
    o_ref[0] = _layer_norm_rows(u, lng_ref[...], lnb_ref[...])


def _combine_ln(x, y0, y1, gates, modl, ln_g, ln_b):
    B, S, D = x.shape
    tm = min(ROW_TILE, S)
    tile = pl.BlockSpec((1, tm, D), lambda b, i: (b, i, 0))
    return pl.pallas_call(
        _combine_ln_kernel,
        grid=(B, S // tm),
        in_specs=[tile, tile, tile, pl.BlockSpec((1, tm, LANES), lambda b, i: (b, i, 0)),
                  pl.BlockSpec((1, 1, modl.shape[-1]), lambda b, i: (b, 0, 0)),
                  _full_spec((1, D)), _full_spec((1, D))],
        out_specs=tile,
        out_shape=jax.ShapeDtypeStruct((B, S, D), f32),
        compiler_params=_cparams("arbitrary", "arbitrary"),
    )(x, y0, y1, gates, modl, ln_g.reshape(1, D), ln_b.reshape(1, D))


def _moe_layer(x, modl, w_grp, b_grp, w_exp, b_exp, wg, wu, wd, ln_g, ln_b):
    B, S, D = x.shape
    T = B * S
    hb, idx, gates = _route(x, modl, w_grp, b_grp, w_exp, b_exp)
    expert_idx = idx.reshape(T, LANES)[:, :TOP_K]
    dest, slot_tok, block_exp, n_used = _dispatch(expert_idx, MOE_BLOCK)
    xs = jnp.take(hb.reshape(T, D), slot_tok, axis=0)
    ys = _expert_ffn(xs, block_exp, n_used, wg.astype(bf16), wu.astype(bf16), wd.astype(bf16))
    dest = dest.reshape(T, TOP_K)
    y0 = jnp.take(ys, dest[:, 0], axis=0).reshape(B, S, D)
    y1 = jnp.take(ys, dest[:, 1], axis=0).reshape(B, S, D)
    return _combine_ln(x, y0, y1, gates, modl, ln_g, ln_b)


def _shared_kv_kernel(x_ref, mod_ref, wk_ref, wv_ref, wf_ref, fb_ref, kn_ref, k_ref, v_ref, f_ref, carry_ref):
    @pl.when(pl.program_id(1) == 0)
    def _():
        carry_ref[...] = jnp.zeros_like(carry_ref)

    D = x_ref.shape[-1]
    tm = x_ref.shape[1]
    shift = mod_ref[0, :, 0:D]
    scale = mod_ref[0, :, D:2 * D]
    hk = x_ref[0] * (1.0 + scale) + shift
    hb = hk.astype(bf16)
    k = _dot(hb, wk_ref[...])
    v_ref[0] = _dot(hb, wv_ref[...]).astype(bf16)
    for hd in range(D // HEAD_DIM):
        sl = slice(hd * HEAD_DIM, (hd + 1) * HEAD_DIM)
        seg = k[:, sl]
        ms = jnp.mean(seg * seg, axis=-1, keepdims=True)
        k_ref[0, :, sl] = (seg * lax.rsqrt(ms + QK_EPS) * kn_ref[...]).astype(bf16)
    f = jnp.dot(hk, wf_ref[...], preferred_element_type=f32, precision=HIGHEST) + fb_ref[...]
    log_f = jnp.minimum(f, 0.0) - jnp.log(1.0 + jnp.exp(-jnp.abs(f)))
    row = lax.broadcasted_iota(jnp.int32, log_f.shape, 0)
    acc = log_f
    d = 1
    while d < tm:
        acc = acc + jnp.where(row >= d, pltpu.roll(acc, d, axis=0), 0.0)
        d *= 2
    acc = acc + carry_ref[...]
    f_ref[0] = acc
    carry_ref[...] = acc[tm - 1:tm, :]


def _shared_kv(x, kvmod, w_kvf, b_f, k_norm):
    B, S, D = x.shape
    H = D // HEAD_DIM
    tm = min(ROW_TILE, S)
    wk = w_kvf[:, :D].astype(bf16)
    wv = w_kvf[:, D:2 * D].astype(bf16)
    wf = jnp.concatenate([w_kvf[:, 2 * D:], jnp.zeros((D, LANES - H), f32)], axis=1)
    fb = jnp.concatenate([b_f, jnp.zeros((LANES - H,), f32)]).reshape(1, LANES)
    tile = pl.BlockSpec((1, tm, D), lambda b, i: (b, i, 0))
    k, v, fcum = pl.pallas_call(
        _shared_kv_kernel,
        grid=(B, S // tm),
        in_specs=[tile, pl.BlockSpec((1, 1, 2 * D), lambda b, i: (b, 0, 0)),
                  _full_spec(wk.shape), _full_spec(wv.shape), _full_spec(wf.shape),
                  _full_spec(fb.shape), _full_spec((1, HEAD_DIM))],
        out_specs=[tile, tile, pl.BlockSpec((1, tm, LANES), lambda b, i: (b, i, 0))],
        out_shape=[jax.ShapeDtypeStruct((B, S, D), bf16), jax.ShapeDtypeStruct((B, S, D), bf16),
                   jax.ShapeDtypeStruct((B, S, LANES), f32)],
        scratch_shapes=[pltpu.VMEM((1, LANES), f32)],
        compiler_params=_cparams("arbitrary", "arbitrary"),
    )(x, kvmod, wk, wv, wf, fb, k_norm.reshape(1, HEAD_DIM))
    return k, v, fcum[:, :, :H]


def _fox_pre_kernel(x_ref, mod_ref, wq_ref, wg_ref, qn_ref, q_ref, gate_ref):
    D = x_ref.shape[-1]
    sh = mod_ref[0, :, 0:D]
    sc = mod_ref[0, :, D:2 * D]
    hb = (x_ref[0] * (1.0 + sc) + sh).astype(bf16)
    q = _dot(hb, wq_ref[...])
    gate_ref[0] = _sigmoid(_dot(hb, wg_ref[...])).astype(bf16)
    qscale = HEAD_DIM ** -0.5
    for hd in range(D // HEAD_DIM):
        sl = slice(hd * HEAD_DIM, (hd + 1) * HEAD_DIM)
        seg = q[:, sl]
        ms = jnp.mean(seg * seg, axis=-1, keepdims=True)
        q_ref[0, :, sl] = (seg * lax.rsqrt(ms + QK_EPS) * qn_ref[...] * qscale).astype(bf16)


def _fox_pre(x, modl, w_qg, q_norm):
    B, S, D = x.shape
    tm = min(ROW_TILE, S)
    tile = pl.BlockSpec((1, tm, D), lambda b, i: (b, i, 0))
    wq = w_qg[:, :D].astype(bf16)
    wg = w_qg[:, D:].astype(bf16)
    out = jax.ShapeDtypeStruct((B, S, D), bf16)
    return pl.pallas_call(
        _fox_pre_kernel,
        grid=(B, S // tm),
        in_specs=[tile, pl.BlockSpec((1, 1, modl.shape[-1]), lambda b, i: (b, 0, 0)),
                  _full_spec(wq.shape), _full_spec(wg.shape), _full_spec((1, HEAD_DIM))],
        out_specs=[tile, tile],
        out_shape=[out, out],
        compiler_params=_cparams("arbitrary", "arbitrary"),
    )(x, modl, wq, wg, q_norm.reshape(1, HEAD_DIM))


def _fox_attn_kernel(q_ref, k_ref, v_ref, fc_ref, fr_ref, gate_ref, o_ref, m_ref, l_ref, acc_ref):
    i = pl.program_id(2)
    j = pl.program_id(3)
    N = HEAD_DIM
    nh = LANES // N

    @pl.when(j == 0)
    def _():
        m_ref[...] = jnp.full_like(m_ref, NEG_BIG)
        l_ref[...] = jnp.zeros_like(l_ref)
        acc_ref[...] = jnp.zeros_like(acc_ref)

    def step(diagonal):
        for hh in range(nh):
            ls = slice(hh * N, (hh + 1) * N)
            s = _dot_nt(q_ref[0, :, ls], k_ref[0, :, ls])
            s = s + (fc_ref[0, 0][:, hh:hh + 1] - fr_ref[0, 0][hh:hh + 1, :])
            if diagonal:
                ri = lax.broadcasted_iota(jnp.int32, s.shape, 0)
                ci = lax.broadcasted_iota(jnp.int32, s.shape, 1)
                s = jnp.where(ri >= ci, s, -jnp.inf)
            m_prev = m_ref[hh]
            m_new = jnp.maximum(m_prev, jnp.max(s, axis=-1, keepdims=True))
            alpha = jnp.exp(m_prev - m_new)
            p = jnp.exp(s - m_new)
            l_ref[hh] = alpha * l_ref[hh] + jnp.sum(p, axis=-1, keepdims=True)
            acc_ref[hh] = alpha * acc_ref[hh] + _dot(p.astype(bf16), v_ref[0, :, ls])
            m_ref[hh] = m_new

    @pl.when(j < i)
    def _():
        step(False)

    @pl.when(j == i)
    def _():
        step(True)
        o = jnp.concatenate([acc_ref[hh] / l_ref[hh] for hh in range(nh)], axis=1)
        o_ref[0] = (o * gate_ref[0].astype(f32)).astype(bf16)


def _fox_attn(q, k, v, fcum, gate):
    B, S, D = q.shape
    t = min(ATTN_TILE, S)
    P = D // LANES
    nh = LANES // HEAD_DIM
    fc = fcum.reshape(B, S, P, nh).transpose(0, 2, 1, 3)
    fr = fc.transpose(0, 1, 3, 2)
    qtile = pl.BlockSpec((1, t, LANES), lambda b, p, i, j: (b, i, p))
    ktile = pl.BlockSpec((1, t, LANES), lambda b, p, i, j: (b, jnp.minimum(i, j), p))
    return pl.pallas_call(
        _fox_attn_kernel,
        grid=(B, P, S // t, S // t),
        in_specs=[qtile, ktile, ktile,
                  pl.BlockSpec((1, 1, t, nh), lambda b, p, i, j: (b, p, i, 0)),
                  pl.BlockSpec((1, 1, nh, t), lambda b, p, i, j: (b, p, 0, jnp.minimum(i, j))),
                  qtile],
        out_specs=qtile,
        out_shape=jax.ShapeDtypeStruct((B, S, D), bf16),
        scratch_shapes=[pltpu.VMEM((nh, t, 1), f32), pltpu.VMEM((nh, t, 1), f32),
                        pltpu.VMEM((nh, t, HEAD_DIM), f32)],
        compiler_params=_cparams("arbitrary", "arbitrary", "arbitrary", "arbitrary"),
    )(q, k, v, fc, fr, gate)


def kernel(x, c, ada_w, ada_b, ln_g, ln_b, rw_mu, rw_rkv, rw_w0, rw_w1, rw_w2, rw_a0, rw_a1, rw_a2, rw_g1, rw_g2, rw_kk, rw_ka, rw_rk, rw_gn_g, rw_gn_b, rw_wo, rw_v0, rw_v1, rw_v2, kv_ada_w, kv_ada_b, kv_w, kv_fb, kv_knorm, fx_wqg, fx_qnorm, fx_wo, moe_wgrp, moe_bgrp, moe_wexp, moe_bexp, moe_wgate, moe_wup, moe_wdown):
    B, S, D = x.shape
    depth = ada_w.shape[0]
    n_a = rw_mu.shape[0]
    mod = _adaln_mod(c, ada_w, ada_b)
    kvmod = _adaln_mod(c, kv_ada_w[None], kv_ada_b[None])[0].reshape(B, 1, 2 * D)
    kv = None
    v_first = None
    for l in range(depth):
        modl = mod[l].reshape(B, 1, 6 * D)
        if l < n_a:
            vmix = None if l == 0 else (rw_v0[l - 1], rw_v1[l - 1], rw_v2[l - 1], v_first)
            r, k, v, wl, kk, b, g = _rwkv_pre(x, modl, rw_mu[l], rw_rkv[l], rw_w0[l], rw_w1[l], rw_w2[l],
                                              rw_a0[l], rw_a1[l], rw_a2[l], rw_g1[l], rw_g2[l],
                                              rw_kk[l], rw_ka[l], vmix)
            if l == 0:
                v_first = v
            z = _wkv7(r, k, v, wl, kk, b, g, rw_rk[l].reshape(D), rw_gn_g[l], rw_gn_b[l])
            w_o = rw_wo[l]
        else:
            j = l - n_a
            q, gate = _fox_pre(x, modl, fx_wqg[j], fx_qnorm[j])
            z = _fox_attn(q, kv[0], kv[1], kv[2], gate)
            w_o = fx_wo[j]
        x = _proj_ln(z, w_o, x, modl, 2, ln_g[l, 0], ln_b[l, 0])
        x = _moe_layer(x, modl, moe_wgrp[l], moe_bgrp[l], moe_wexp[l], moe_bexp[l],
                       moe_wgate[l], moe_wup[l], moe_wdown[l], ln_g[l, 1], ln_b[l, 1])
        if l == n_a - 1:
            kv = _shared_kv(x, kvmod, kv_w, kv_fb, kv_knorm)
    return x
```

```python
import functools
import math

import jax
import jax.numpy as jnp
from jax import lax
from jax.experimental import pallas as pl
from jax.experimental.pallas import tpu as pltpu

HEAD_DIM = 64
N_GROUPS = 4
EXP_PER_GROUP = 8
N_EXPERTS = N_GROUPS * EXP_PER_GROUP
TOP_K = 2
DEPTH = 4
N_A = 2
DEEPNORM_ALPHA = (2 * DEPTH) ** 0.25
LN_EPS = 1e-5
GN_EPS = 64e-5
QK_EPS = 1e-6

LANES = 128
VMEM_LIMIT_BYTES = 56 * 1024 * 1024
WKV_CHUNK = 64
WKV_TILE = 128
ROW_TILE = 512
RWKV_PRE_TILE = 256
ATTN_TILE = 512
MOE_BLOCK = 256
NEG_BIG = -1e30
LOG2E = math.log2(math.e)

f32 = jnp.float32
bf16 = jnp.bfloat16
HIGHEST = lax.Precision.HIGHEST


def _cparams(*sem):
    return pltpu.CompilerParams(dimension_semantics=sem, vmem_limit_bytes=VMEM_LIMIT_BYTES)


def _sigmoid(x):
    return 1.0 / (1.0 + jnp.exp(-x))


def _dot(a, b):
    return jnp.dot(a, b, preferred_element_type=f32)


def _dot_nt(a, b):
    return lax.dot_general(a, b, (((1,), (1,)), ((), ())), preferred_element_type=f32)


def _dot_tn(a, b):
    return lax.dot_general(a, b, (((0,), (0,)), ((), ())), preferred_element_type=f32)


def _full_spec(shape):
    n = len(shape)
    return pl.BlockSpec(shape, lambda *_: (0,) * n)


def _layer_norm_rows(u, g, b):
    mu = jnp.mean(u, axis=-1, keepdims=True)
    d = u - mu
    var = jnp.mean(d * d, axis=-1, keepdims=True)
    return d * lax.rsqrt(var + LN_EPS) * g + b


def _mod_kernel(c_ref, w_ref, b_ref, o_ref):
    c = c_ref[...]
    cs = c * _sigmoid(c)
    o_ref[0] = jnp.dot(cs, w_ref[0], preferred_element_type=f32, precision=HIGHEST) + b_ref[0]


def _adaln_mod(c, w, b):
    L, D, N = w.shape
    B = c.shape[0]
    tn = min(N, 1024)
    return pl.pallas_call(
        _mod_kernel,
        name="adaln_mod",
        grid=(L, N // tn),
        in_specs=[
            _full_spec((B, D)),
            pl.BlockSpec((1, D, tn), lambda l, j: (l, 0, j)),
            pl.BlockSpec((1, 1, tn), lambda l, j: (l, 0, j)),
        ],
        out_specs=pl.BlockSpec((1, B, tn), lambda l, j: (l, 0, j)),
        out_shape=jax.ShapeDtypeStruct((L, B, N), f32),
        compiler_params=_cparams("arbitrary", "arbitrary"),
    )(c, w, b.reshape(L, 1, N))


def _rwkv_pre_kernel(has_vmix, x_ref, xp_ref, mod_ref, mu_ref, wrkv_ref, w0_ref, w1_ref, w2_ref,
                     a0_ref, a1_ref, a2_ref, g1_ref, g2_ref, kkp_ref, kap_ref, *rest):
    if has_vmix:
        v0_ref, v1_ref, v2_ref, vf_ref = rest[:4]
        rest = rest[4:]
    r_ref, k_ref, v_ref, wl_ref, kk_ref, b_ref, g_ref = rest
    D = x_ref.shape[-1]
    sh = mod_ref[0, :, 0:D]
    sc = mod_ref[0, :, D:2 * D]
    h = x_ref[0] * (1.0 + sc) + sh
    hp = xp_ref[0][7:8, :] * (1.0 + sc) + sh
    hp = jnp.where(pl.program_id(1) == 0, 0.0, hp)
    row = lax.broadcasted_iota(jnp.int32, h.shape, 0)
    h_prev = jnp.where(row == 0, hp, pltpu.roll(h, 1, axis=0))
    xx = h_prev - h
    mix = lambda j: (h + xx * mu_ref[j:j + 1, :]).astype(bf16)
    xr, xw, xk, xv, xa, xg = [mix(j) for j in range(6)]
    r = _dot(xr, wrkv_ref[0])
    k = _dot(xk, wrkv_ref[1])
    v = _dot(xv, wrkv_ref[2])
    zw = w0_ref[...] + _dot(jnp.tanh(_dot(xw, w1_ref[...])).astype(bf16), w2_ref[...])
    wl_ref[0] = -math.exp(-0.5) * _sigmoid(zw)
    a = _sigmoid(a0_ref[...] + _dot(_dot(xa, a1_ref[...]).astype(bf16), a2_ref[...]))
    g = _dot(_sigmoid(_dot(xg, g1_ref[...])).astype(bf16), g2_ref[...])
    if has_vmix:
        vmix = _sigmoid(v0_ref[...] + _dot(_dot(xv, v1_ref[...]).astype(bf16), v2_ref[...]))
        v = v + (vf_ref[0].astype(f32) - v) * vmix
    r_ref[0] = r.astype(bf16)
    v_ref[0] = v.astype(bf16)
    g_ref[0] = g.astype(bf16)
    k_ref[0] = (k * (1.0 + (a - 1.0) * kap_ref[...])).astype(bf16)
    kk = k * kkp_ref[...]
    for hd in range(D // HEAD_DIM):
        sl = slice(hd * HEAD_DIM, (hd + 1) * HEAD_DIM)
        seg = kk[:, sl]
        nrm = jnp.sqrt(jnp.sum(seg * seg, axis=-1, keepdims=True))
        seg = seg / jnp.maximum(nrm, 1e-12)
        kk_ref[0, :, sl] = seg.astype(bf16)
        b_ref[0, :, sl] = (seg * a[:, sl]).astype(bf16)


def _rwkv_pre(x, modl, mu, wrkv, w0, w1, w2, a0, a1, a2, g1, g2, kkp, kap, vmix):
    B, S, D = x.shape
    tm = min(RWKV_PRE_TILE, S)
    row = lambda t: t.reshape(1, D)
    args = [x, x, modl, mu, wrkv.astype(bf16), row(w0), w1.astype(bf16), w2.astype(bf16), row(a0),
            a1.astype(bf16), a2.astype(bf16), g1.astype(bf16), g2.astype(bf16), row(kkp), row(kap)]
    tile = pl.BlockSpec((1, tm, D), lambda b, i: (b, i, 0))
    in_specs = [
        tile,
        pl.BlockSpec((1, 8, D), lambda b, i: (b, jnp.maximum(i * (tm // 8) - 1, 0), 0)),
        pl.BlockSpec((1, 1, modl.shape[-1]), lambda b, i: (b, 0, 0)),
    ] + [_full_spec(a.shape) for a in args[3:]]
    if vmix is not None:
        v0, v1, v2, v_first = vmix
        extra = [row(v0), v1.astype(bf16), v2.astype(bf16)]
        args += extra + [v_first]
        in_specs += [_full_spec(a.shape) for a in extra] + [tile]
    out_bf = jax.ShapeDtypeStruct((B, S, D), bf16)
    out_shape = [out_bf, out_bf, out_bf, jax.ShapeDtypeStruct((B, S, D), f32), out_bf, out_bf, out_bf]
    return pl.pallas_call(
        functools.partial(_rwkv_pre_kernel, vmix is not None),
        name="rwkv_pre",
        grid=(B, S // tm),
        in_specs=in_specs,
        out_specs=[tile] * 7,
        out_shape=out_shape,
        compiler_params=_cparams("arbitrary", "arbitrary"),
    )(*args)


def _head_sum(x, jbd):
    xh = x.astype(bf16)
    xl = (x - xh.astype(f32)).astype(bf16)
    return _dot(xh, jbd) + _dot(xl, jbd)


def _wkv_kernel(C, r_ref, k_ref, v_ref, wl_ref, kk_ref, b_ref, g_ref, rk_ref, gng_ref, gnb_ref,
                o_ref, state_ref):
    @pl.when(pl.program_id(1) == 0)
    def _():
        state_ref[...] = jnp.zeros_like(state_ref)

    T, D = r_ref.shape[1], r_ref.shape[2]
    N = HEAD_DIM
    P = D // LANES
    C2 = 2 * C
    ri = lax.broadcasted_iota(jnp.int32, (C, C), 0)
    ci = lax.broadcasted_iota(jnp.int32, (C, C), 1)
    tri_incl = jnp.where(ri >= ci, 1.0, 0.0).astype(f32)
    r2 = lax.broadcasted_iota(jnp.int32, (C2, C2), 0)
    c2 = lax.broadcasted_iota(jnp.int32, (C2, C2), 1)
    dlt = jnp.where(jnp.where(r2 >= C, 1, 0) == jnp.where(c2 >= C, 1, 0), r2 - c2, -1)
    strict = dlt > 0
    incl = dlt >= 0
    eye2 = jnp.where(r2 == c2, 1.0, 0.0).astype(f32)
    h0 = lax.broadcasted_iota(jnp.int32, (C, LANES), 1) < N
    rs = lax.broadcasted_iota(jnp.int32, (LANES, LANES), 0)
    cs = lax.broadcasted_iota(jnp.int32, (LANES, LANES), 1)
    sbd = jnp.where(rs >= N, 1, 0) == jnp.where(cs >= N, 1, 0)
    jbd = jnp.where(sbd, 1.0, 0.0).astype(bf16)
    stack2 = lambda t: jnp.concatenate([t, t], axis=0)

    for c in range(T // C):
        sl = pl.ds(c * C, C)
        wl = wl_ref[0, sl, :]
        cum = jnp.dot(tri_incl, wl, preferred_element_type=f32, precision=HIGHEST)
        g_incl = jnp.exp(cum)
        g_prev = jnp.exp(cum - wl)
        g_inv = jnp.exp(-cum)
        g_last = g_incl[C - 1:C, :]
        r = r_ref[0, sl, :].astype(f32)
        k = k_ref[0, sl, :].astype(f32)
        v16 = v_ref[0, sl, :]
        kk = kk_ref[0, sl, :].astype(f32)
        b = b_ref[0, sl, :].astype(f32)
        af = -kk * g_prev
        rf = r * g_incl
        btf = b * g_inv
        ktf = k * g_inv
        a16 = af.astype(bf16)
        r16 = rf.astype(bf16)
        bt = btf.astype(bf16)
        kt = ktf.astype(bf16)
        bc = (btf * g_last).astype(bf16)
        kc = (ktf * g_last).astype(bf16)
        rkr = r * k * rk_ref[...]
        lanes = [slice(p * LANES, (p + 1) * LANES) for p in range(P)]

        Ls, Aak, Arow, Vst = [], [], [], []
        for lp in lanes:
            afp, rfp = af[:, lp], rf[:, lp]
            X4 = jnp.concatenate([jnp.where(h0, afp, 0.0), jnp.where(h0, 0.0, afp),
                                  jnp.where(h0, rfp, 0.0), jnp.where(h0, 0.0, rfp)], axis=0).astype(bf16)
            Mb = _dot_nt(X4, stack2(bt[:, lp]))
            Mk = _dot_nt(X4, stack2(kt[:, lp]))
            Ls.append(jnp.where(strict, Mb[:C2], 0.0))
            Aak.append(jnp.where(strict, Mk[:C2], 0.0).astype(bf16))
            Arow.append(jnp.concatenate([jnp.where(incl, Mb[C2:], 0.0), jnp.where(incl, Mk[C2:], 0.0)],
                                        axis=1).astype(bf16))
            Vst.append(stack2(v16[:, lp]))
        Ps = [eye2 + L for L in Ls]
        n = 1
        while 2 * n < C:
            Lb = [L.astype(bf16) for L in Ls]
            Ls = [_dot(x, x) for x in Lb]
            Ps = [Pm + _dot(L.astype(bf16), Pm.astype(bf16)) for L, Pm in zip(Ls, Ps)]
            n *= 2
        AV = [_dot(a, vs) for a, vs in zip(Aak, Vst)]
        ys = []
        for p, lp in enumerate(lanes):
            S0 = state_ref[p]
            M2 = _dot_nt(jnp.concatenate([a16[:, lp], r16[:, lp]], axis=0), S0.astype(bf16))
            Wst = AV[p] + stack2(M2[:C])
            Ust = _dot(Ps[p].astype(bf16), Wst.astype(bf16))
            Yst = _dot(Arow[p], jnp.concatenate([Ust.astype(bf16), Vst[p]], axis=0)) + stack2(M2[C:])
            ys.append(jnp.where(h0, Yst[:C], Yst[C:]))
            u16 = jnp.where(h0, Ust[:C], Ust[C:]).astype(bf16)
            upd = _dot_tn(jnp.concatenate([u16, v16[:, lp]], axis=0),
                          jnp.concatenate([bc[:, lp], kc[:, lp]], axis=0))
            state_ref[p] = S0 * g_last[:, lp] + jnp.where(sbd, upd, 0.0)
        for p, lp in enumerate(lanes):
            y = ys[p]
            ym = _head_sum(y, jbd) * (1.0 / N)
            yd = y - ym
            yv = _head_sum(yd * yd, jbd) * (1.0 / N)
            yn = yd * lax.rsqrt(yv + GN_EPS)
            bonus = _head_sum(rkr[:, lp], jbd) * v16[:, lp].astype(f32)
            z = (yn * gng_ref[:, lp] + gnb_ref[:, lp] + bonus) * g_ref[0, sl, lp].astype(f32)
            o_ref[0, sl, lp] = z.astype(bf16)


def _wkv7(r, k, v, wl, kk, b, g, r_k, gn_g, gn_b):
    B, S, D = r.shape
    T = min(WKV_TILE, S)
    C = min(WKV_CHUNK, T)
    tile = pl.BlockSpec((1, T, D), lambda bb, i: (bb, i, 0))
    vec = _full_spec((1, D))
    return pl.pallas_call(
        functools.partial(_wkv_kernel, C),
        name="wkv7",
        grid=(B, S // T),
        in_specs=[tile] * 7 + [vec] * 3,
        out_specs=tile,
        out_shape=jax.ShapeDtypeStruct((B, S, D), bf16),
        scratch_shapes=[pltpu.VMEM((D // LANES, LANES, LANES), f32)],
        compiler_params=_cparams("arbitrary", "arbitrary"),
    )(r, k, v, wl, kk, b, g, r_k.reshape(1, D), gn_g.reshape(1, D), gn_b.reshape(1, D))


def _proj_ln_kernel(gate_idx, z_ref, w_ref, x_ref, mod_ref, lng_ref, lnb_ref, o_ref):
    D = x_ref.shape[-1]
    gt = mod_ref[0, :, gate_idx * D:(gate_idx + 1) * D]
    y = _dot(z_ref[0], w_ref[...])
    u = DEEPNORM_ALPHA * x_ref[0] + (1.0 + gt) * y
    o_ref[0] = _layer_norm_rows(u, lng_ref[...], lnb_ref[...])


def _proj_ln(z, w, x, modl, gate_idx, ln_g, ln_b):
    B, S, D = x.shape
    tm = min(ROW_TILE, S)
    tile = pl.BlockSpec((1, tm, D), lambda b, i: (b, i, 0))
    return pl.pallas_call(
        functools.partial(_proj_ln_kernel, gate_idx),
        name="proj_ln",
        grid=(B, S // tm),
        in_specs=[tile, _full_spec(w.shape), tile,
                  pl.BlockSpec((1, 1, modl.shape[-1]), lambda b, i: (b, 0, 0)),
                  _full_spec((1, D)), _full_spec((1, D))],
        out_specs=tile,
        out_shape=jax.ShapeDtypeStruct((B, S, D), f32),
        compiler_params=_cparams("arbitrary", "arbitrary"),
    )(z, w.astype(bf16), x, modl, ln_g.reshape(1, D), ln_b.reshape(1, D))


def _route_kernel(x_ref, mod_ref, w_ref, b_ref, h_ref, idx_ref, gate_ref):
    D = x_ref.shape[-1]
    sh = mod_ref[0, :, 3 * D:4 * D]
    sc = mod_ref[0, :, 4 * D:5 * D]
    h = x_ref[0] * (1.0 + sc) + sh
    h_ref[0] = h.astype(bf16)
    lg = jnp.dot(h, w_ref[...], preferred_element_type=f32, precision=HIGHEST) + b_ref[...]
    lane = lax.broadcasted_iota(jnp.int32, lg.shape, 1)
    glog = jnp.where(lane < N_GROUPS, lg, -jnp.inf)
    gmax = jnp.max(glog, axis=-1, keepdims=True)
    p_g = 1.0 / jnp.sum(jnp.exp(glog - gmax), axis=-1, keepdims=True)
    g_sel = jnp.min(jnp.where(glog == gmax, lane, LANES), axis=-1, keepdims=True)
    e_lo = N_GROUPS + EXP_PER_GROUP * g_sel
    elog = jnp.where((lane >= e_lo) & (lane < e_lo + EXP_PER_GROUP), lg, -jnp.inf)
    e1 = jnp.max(elog, axis=-1, keepdims=True)
    esum = jnp.sum(jnp.exp(elog - e1), axis=-1, keepdims=True)
    i1 = jnp.min(jnp.where(elog == e1, lane, LANES), axis=-1, keepdims=True)
    elog2 = jnp.where(lane == i1, -jnp.inf, elog)
    e2 = jnp.max(elog2, axis=-1, keepdims=True)
    i2 = jnp.min(jnp.where(elog2 == e2, lane, LANES), axis=-1, keepdims=True)
    p1 = 1.0 / esum
    p2 = jnp.exp(e2 - e1) / esum
    psum = p1 + p2
    idx_ref[0] = jnp.where(lane == 0, i1 - N_GROUPS, jnp.where(lane == 1, i2 - N_GROUPS, 0))
    gate_ref[0] = jnp.where(lane == 0, p_g * p1 / psum, jnp.where(lane == 1, p_g * p2 / psum, 0.0))


def _route(x, modl, w_grp, b_grp, w_exp, b_exp):
    B, S, D = x.shape
    tm = min(ROW_TILE, S)
    pad = LANES - N_GROUPS - N_EXPERTS
    w = jnp.concatenate([w_grp, w_exp, jnp.zeros((D, pad), f32)], axis=1)
    b = jnp.concatenate([b_grp, b_exp, jnp.zeros((pad,), f32)]).reshape(1, LANES)
    tile = pl.BlockSpec((1, tm, D), lambda bb, i: (bb, i, 0))
    small = pl.BlockSpec((1, tm, LANES), lambda bb, i: (bb, i, 0))
    return pl.pallas_call(
        _route_kernel,
        name="moe_route",
        grid=(B, S // tm),
        in_specs=[tile, pl.BlockSpec((1, 1, modl.shape[-1]), lambda bb, i: (bb, 0, 0)),
                  _full_spec(w.shape), _full_spec(b.shape)],
        out_specs=[tile, small, small],
        out_shape=[jax.ShapeDtypeStruct((B, S, D), bf16),
                   jax.ShapeDtypeStruct((B, S, LANES), jnp.int32),
                   jax.ShapeDtypeStruct((B, S, LANES), f32)],
        compiler_params=_cparams("arbitrary", "arbitrary"),
    )(x, modl, w, b)


def _expert_kernel(be_ref, nu_ref, x_ref, wg_ref, wu_ref, wd_ref, o_ref):
    i = pl.program_id(0)

    @pl.when(i < nu_ref[0])
    def _():
        x = x_ref[...]
        hg = _dot(x, wg_ref[0])
        hu = _dot(x, wu_ref[0])
        o_ref[...] = _dot((hg * _sigmoid(hg) * hu).astype(bf16), wd_ref[0])

    @pl.when(i >= nu_ref[0])
    def _():
        o_ref[...] = jnp.zeros_like(o_ref)


def _expert_ffn(xs, block_exp, n_used, wg, wu, wd):
    n_slots, D = xs.shape
    E, _, F = wg.shape
    blk = MOE_BLOCK
    row_map = lambda i, be, nu: (jnp.minimum(i, nu[0] - 1), 0)
    grid_spec = pltpu.PrefetchScalarGridSpec(
        num_scalar_prefetch=2,
        grid=(n_slots // blk,),
        in_specs=[
            pl.BlockSpec((blk, D), row_map),
            pl.BlockSpec((1, D, F), lambda i, be, nu: (be[i], 0, 0)),
            pl.BlockSpec((1, D, F), lambda i, be, nu: (be[i], 0, 0)),
            pl.BlockSpec((1, F, D), lambda i, be, nu: (be[i], 0, 0)),
        ],
        out_specs=pl.BlockSpec((blk, D), lambda i, be, nu: (i, 0)),
    )
    return pl.pallas_call(
        _expert_kernel,
        name="moe_experts",
        grid_spec=grid_spec,
        out_shape=jax.ShapeDtypeStruct((n_slots, D), f32),
        compiler_params=_cparams("arbitrary"),
    )(block_exp, n_used, xs, wg, wu, wd)


def _dispatch(expert_idx, blk):
    T = expert_idx.shape[0]
    A = T * TOP_K
    n_blocks = A // blk + N_EXPERTS
    flat_e = expert_idx.reshape(A)
    order = jnp.argsort(flat_e)
    sorted_e = flat_e[order]
    counts = jnp.bincount(flat_e, length=N_EXPERTS)
    padded = (counts + blk - 1) // blk * blk
    pad_end = jnp.cumsum(padded)
    pad_start = pad_end - padded
    start = jnp.cumsum(counts) - counts
    dest_sorted = pad_start[sorted_e] + (jnp.arange(A) - start[sorted_e])
    dest = jnp.zeros((A,), jnp.int32).at[order].set(dest_sorted.astype(jnp.int32))
    slot_tok = jnp.zeros((n_blocks * blk,), jnp.int32).at[dest].set(jnp.arange(A, dtype=jnp.int32) // TOP_K)
    block_exp = jnp.minimum(jnp.searchsorted(pad_end, jnp.arange(n_blocks) * blk, side='right'),
                            N_EXPERTS - 1).astype(jnp.int32)
    n_used = (pad_end[-1] // blk).astype(jnp.int32).reshape(1)
    return dest, slot_tok, block_exp, n_used


def _combine_ln_kernel(x_ref, y0_ref, y1_ref, gate_ref, mod_ref, lng_ref, lnb_ref, o_ref):
    D = x_ref.shape[-1]
    gt = mod_ref[0, :, 5 * D:6 * D]
    gates = gate_ref[0]
    y = gates[:, 0:1] * y0_ref[0] + gates[:, 1:2] * y1_ref[0]
    u = DEEPNORM_ALPHA * x_ref[0] + (1.0 + gt) * y
    o_ref[0] = _layer_norm_rows(u, lng_ref[...], lnb_ref[...])


def _combine_ln(x, y0, y1, gates, modl, ln_g, ln_b):
    B, S, D = x.shape
    tm = min(ROW_TILE, S)
    tile = pl.BlockSpec((1, tm, D), lambda b, i: (b, i, 0))
    return pl.pallas_call(
        _combine_ln_kernel,
        name="moe_combine_ln",
        grid=(B, S // tm),
        in_specs=[tile, tile, tile, pl.BlockSpec((1, tm, LANES), lambda b, i: (b, i, 0)),
                  pl.BlockSpec((1, 1, modl.shape[-1]), lambda b, i: (b, 0, 0)),
                  _full_spec((1, D)), _full_spec((1, D))],
        out_specs=tile,
        out_shape=jax.ShapeDtypeStruct((B, S, D), f32),
        compiler_params=_cparams("arbitrary", "arbitrary"),
    )(x, y0, y1, gates, modl, ln_g.reshape(1, D), ln_b.reshape(1, D))


def _moe_layer(x, modl, w_grp, b_grp, w_exp, b_exp, wg, wu, wd, ln_g, ln_b):
    B, S, D = x.shape
    T = B * S
    hb, idx, gates = _route(x, modl, w_grp, b_grp, w_exp, b_exp)
    expert_idx = idx.reshape(T, LANES)[:, :TOP_K]
    dest, slot_tok, block_exp, n_used = _dispatch(expert_idx, MOE_BLOCK)
    xs = jnp.take(hb.reshape(T, D), slot_tok, axis=0)
    ys = _expert_ffn(xs, block_exp, n_used, wg.astype(bf16), wu.astype(bf16), wd.astype(bf16))
    dest = dest.reshape(T, TOP_K)
    y0 = jnp.take(ys, dest[:, 0], axis=0).reshape(B, S, D)
    y1 = jnp.take(ys, dest[:, 1], axis=0).reshape(B, S, D)
    return _combine_ln(x, y0, y1, gates, modl, ln_g, ln_b)


def _bias_lanes(fcol, is_key):
    t0 = fcol.astype(bf16).astype(f32)
    r1 = fcol - t0
    t1 = r1.astype(bf16).astype(f32)
    t2 = (r1 - t1).astype(bf16).astype(f32)
    lane = lax.broadcasted_iota(jnp.int32, (fcol.shape[0], HEAD_DIM), 1)
    if is_key:
        out = jnp.where(lane < 3, 1.0, jnp.where(lane == 3, -t0, jnp.where(lane == 4, -t1,
                                                                           jnp.where(lane == 5, -t2, 0.0))))
    else:
        out = jnp.where(lane == 0, t0, jnp.where(lane == 1, t1, jnp.where(lane == 2, t2,
                                                                          jnp.where(lane < 6, 1.0, 0.0))))
    return out.astype(bf16)


def _shared_kv_kernel(x_ref, mod_ref, wk_ref, wvt_ref, wf_ref, fb_ref, kn_ref, ka_ref, vt_ref, f_ref, carry_ref):
    @pl.when(pl.program_id(1) == 0)
    def _():
        carry_ref[...] = jnp.zeros_like(carry_ref)

    D = x_ref.shape[-1]
    tm = x_ref.shape[1]
    shift = mod_ref[0, :, 0:D]
    scale = mod_ref[0, :, D:2 * D]
    hk = x_ref[0] * (1.0 + scale) + shift
    hb = hk.astype(bf16)
    k = _dot(hb, wk_ref[...])
    vt_ref[0] = _dot_nt(wvt_ref[...], hb).astype(bf16)
    f = jnp.dot(hk, wf_ref[...], preferred_element_type=f32, precision=HIGHEST) + fb_ref[...]
    log_f = jnp.minimum(f, 0.0) - jnp.log(1.0 + jnp.exp(-jnp.abs(f)))
    row = lax.broadcasted_iota(jnp.int32, log_f.shape, 0)
    acc = log_f
    d = 1
    while d < tm:
        acc = acc + jnp.where(row >= d, pltpu.roll(acc, d, axis=0), 0.0)
        d *= 2
    acc = acc + carry_ref[...]
    f_ref[0] = acc
    carry_ref[...] = acc[tm - 1:tm, :]
    f2 = acc * LOG2E
    for hd in range(D // HEAD_DIM):
        sl = slice(hd * HEAD_DIM, (hd + 1) * HEAD_DIM)
        seg = k[:, sl]
        ms = jnp.mean(seg * seg, axis=-1, keepdims=True)
        ka_ref[0, :, 2 * hd * HEAD_DIM:(2 * hd + 1) * HEAD_DIM] = (
            seg * lax.rsqrt(ms + QK_EPS) * kn_ref[...]).astype(bf16)
        ka_ref[0, :, (2 * hd + 1) * HEAD_DIM:(2 * hd + 2) * HEAD_DIM] = _bias_lanes(f2[:, hd:hd + 1], True)


def _shared_kv(x, kvmod, w_kvf, b_f, k_norm):
    B, S, D = x.shape
    H = D // HEAD_DIM
    tm = min(ROW_TILE, S)
    wk = w_kvf[:, :D].astype(bf16)
    wvt = w_kvf[:, D:2 * D].T.astype(bf16)
    wf = jnp.concatenate([w_kvf[:, 2 * D:], jnp.zeros((D, LANES - H), f32)], axis=1)
    fb = jnp.concatenate([b_f, jnp.zeros((LANES - H,), f32)]).reshape(1, LANES)
    tile = pl.BlockSpec((1, tm, D), lambda b, i: (b, i, 0))
    return pl.pallas_call(
        _shared_kv_kernel,
        name="shared_kv",
        grid=(B, S // tm),
        in_specs=[tile, pl.BlockSpec((1, 1, 2 * D), lambda b, i: (b, 0, 0)),
                  _full_spec(wk.shape), _full_spec(wvt.shape), _full_spec(wf.shape),
                  _full_spec(fb.shape), _full_spec((1, HEAD_DIM))],
        out_specs=[pl.BlockSpec((1, tm, 2 * D), lambda b, i: (b, i, 0)),
                   pl.BlockSpec((1, D, tm), lambda b, i: (b, 0, i)),
                   pl.BlockSpec((1, tm, LANES), lambda b, i: (b, i, 0))],
        out_shape=[jax.ShapeDtypeStruct((B, S, 2 * D), bf16), jax.ShapeDtypeStruct((B, D, S), bf16),
                   jax.ShapeDtypeStruct((B, S, LANES), f32)],
        scratch_shapes=[pltpu.VMEM((1, LANES), f32)],
        compiler_params=_cparams("arbitrary", "arbitrary"),
    )(x, kvmod, wk, wvt, wf, fb, k_norm.reshape(1, HEAD_DIM))


def _fox_pre_kernel(x_ref, mod_ref, wq_ref, wg_ref, qn_ref, f_ref, qa_ref, gate_ref):
    D = x_ref.shape[-1]
    sh = mod_ref[0, :, 0:D]
    sc = mod_ref[0, :, D:2 * D]
    hb = (x_ref[0] * (1.0 + sc) + sh).astype(bf16)
    q = _dot(hb, wq_ref[...])
    gate_ref[0] = _sigmoid(_dot(hb, wg_ref[...])).astype(bf16)
    qscale = HEAD_DIM ** -0.5 * LOG2E
    f2 = f_ref[0] * LOG2E
    for hd in range(D // HEAD_DIM):
        sl = slice(hd * HEAD_DIM, (hd + 1) * HEAD_DIM)
        seg = q[:, sl]
        ms = jnp.mean(seg * seg, axis=-1, keepdims=True)
        qa_ref[0, :, 2 * hd * HEAD_DIM:(2 * hd + 1) * HEAD_DIM] = (
            seg * lax.rsqrt(ms + QK_EPS) * qn_ref[...] * qscale).astype(bf16)
        qa_ref[0, :, (2 * hd + 1) * HEAD_DIM:(2 * hd + 2) * HEAD_DIM] = _bias_lanes(f2[:, hd:hd + 1], False)


def _fox_pre(x, modl, w_qg, q_norm, fcum):
    B, S, D = x.shape
    tm = min(ROW_TILE, S)
    tile = pl.BlockSpec((1, tm, D), lambda b, i: (b, i, 0))
    wq = w_qg[:, :D].astype(bf16)
    wg = w_qg[:, D:].astype(bf16)
    return pl.pallas_call(
        _fox_pre_kernel,
        name="fox_pre",
        grid=(B, S // tm),
        in_specs=[tile, pl.BlockSpec((1, 1, modl.shape[-1]), lambda b, i: (b, 0, 0)),
                  _full_spec(wq.shape), _full_spec(wg.shape), _full_spec((1, HEAD_DIM)),
                  pl.BlockSpec((1, tm, LANES), lambda b, i: (b, i, 0))],
        out_specs=[pl.BlockSpec((1, tm, 2 * D), lambda b, i: (b, i, 0)), tile],
        out_shape=[jax.ShapeDtypeStruct((B, S, 2 * D), bf16), jax.ShapeDtypeStruct((B, S, D), bf16)],
        compiler_params=_cparams("arbitrary", "arbitrary"),
    )(x, modl, wq, wg, q_norm.reshape(1, HEAD_DIM), fcum)


def _fox_attn_kernel(qa_ref, ka_ref, vt_ref, gate_ref, o_ref, m_ref, l_ref, acc_ref):
    i = pl.program_id(2)
    j = pl.program_id(3)
    N = HEAD_DIM
    nh = LANES // N

    @pl.when(j == 0)
    def _():
        m_ref[...] = jnp.full_like(m_ref, NEG_BIG)
        l_ref[...] = jnp.zeros_like(l_ref)
        acc_ref[...] = jnp.zeros_like(acc_ref)

    def step(diagonal):
        for hh in range(nh):
            hl = slice(hh * LANES, (hh + 1) * LANES)
            st = _dot_nt(ka_ref[0, :, hl], qa_ref[0, :, hl])
            if diagonal:
                key = lax.broadcasted_iota(jnp.int32, st.shape, 0)
                qry = lax.broadcasted_iota(jnp.int32, st.shape, 1)
                st = jnp.where(key <= qry, st, -jnp.inf)
            m_prev = m_ref[hh]
            m_new = jnp.maximum(m_prev, jnp.max(st, axis=0, keepdims=True))
            alpha = jnp.exp2(m_prev - m_new)
            p = jnp.exp2(st - m_new)
            l_ref[hh] = alpha * l_ref[hh] + jnp.sum(p, axis=0, keepdims=True)
            acc_ref[hh] = alpha * acc_ref[hh] + _dot(vt_ref[0, hh * N:(hh + 1) * N, :], p.astype(bf16))
            m_ref[hh] = m_new

    @pl.when(j < i)
    def _():
        step(False)

    @pl.when(j == i)
    def _():
        step(True)
        ot = jnp.concatenate([acc_ref[hh] * (1.0 / l_ref[hh]) for hh in range(nh)], axis=0)
        o_ref[0] = (ot.T * gate_ref[0].astype(f32)).astype(bf16)


def _fox_attn(qa, ka, vt, gate):
    B, S, D = gate.shape
    t = min(ATTN_TILE, S)
    P = D // LANES
    nh = LANES // HEAD_DIM
    return pl.pallas_call(
        _fox_attn_kernel,
        name="fox_attn",
        grid=(B, P, S // t, S // t),
        in_specs=[pl.BlockSpec((1, t, 2 * LANES), lambda b, p, i, j: (b, i, p)),
                  pl.BlockSpec((1, t, 2 * LANES), lambda b, p, i, j: (b, jnp.minimum(i, j), p)),
                  pl.BlockSpec((1, LANES, t), lambda b, p, i, j: (b, p, jnp.minimum(i, j))),
                  pl.BlockSpec((1, t, LANES), lambda b, p, i, j: (b, i, p))],
        out_specs=pl.BlockSpec((1, t, LANES), lambda b, p, i, j: (b, i, p)),
        out_shape=jax.ShapeDtypeStruct((B, S, D), bf16),
        scratch_shapes=[pltpu.VMEM((nh, 1, t), f32), pltpu.VMEM((nh, 1, t), f32),
                        pltpu.VMEM((nh, HEAD_DIM, t), f32)],
        compiler_params=_cparams("arbitrary", "arbitrary", "arbitrary", "arbitrary"),
    )(qa, ka, vt, gate)


def kernel(x, c, ada_w, ada_b, ln_g, ln_b, rw_mu, rw_rkv, rw_w0, rw_w1, rw_w2, rw_a0, rw_a1, rw_a2, rw_g1, rw_g2, rw_kk, rw_ka, rw_rk, rw_gn_g, rw_gn_b, rw_wo, rw_v0, rw_v1, rw_v2, kv_ada_w, kv_ada_b, kv_w, kv_fb, kv_knorm, fx_wqg, fx_qnorm, fx_wo, moe_wgrp, moe_bgrp, moe_wexp, moe_bexp, moe_wgate, moe_wup, moe_wdown):
    B, S, D = x.shape
    depth = ada_w.shape[0]
    n_a = rw_mu.shape[0]
    mod = _adaln_mod(c, ada_w, ada_b)
    kvmod = _adaln_mod(c, kv_ada_w[None], kv_ada_b[None])[0].reshape(B, 1, 2 * D)
    kv = None
    v_first = None
    for l in range(depth):
        modl = mod[l].reshape(B, 1, 6 * D)
        if l < n_a:
            vmix = None if l == 0 else (rw_v0[l - 1], rw_v1[l - 1], rw_v2[l - 1], v_first)
            r, k, v, wl, kk, b, g = _rwkv_pre(x, modl, rw_mu[l], rw_rkv[l], rw_w0[l], rw_w1[l], rw_w2[l],
                                              rw_a0[l], rw_a1[l], rw_a2[l], rw_g1[l], rw_g2[l],
                                              rw_kk[l], rw_ka[l], vmix)
            if l == 0:
                v_first = v
            z = _wkv7(r, k, v, wl, kk, b, g, rw_rk[l].reshape(D), rw_gn_g[l], rw_gn_b[l])
            w_o = rw_wo[l]
        else:
            j = l - n_a
            ka, vt, fcum = kv
            qa, gate = _fox_pre(x, modl, fx_wqg[j], fx_qnorm[j], fcum)
            z = _fox_attn(qa, ka, vt, gate)
            w_o = fx_wo[j]
        x = _proj_ln(z, w_o, x, modl, 2, ln_g[l, 0], ln_b[l, 0])
        x = _moe_layer(x, modl, moe_wgrp[l], moe_bgrp[l], moe_wexp[l], moe_bexp[l],
                       moe_wgate[l], moe_wup[l], moe_wdown[l], ln_g[l, 1], ln_b[l, 1])
        if l == n_a - 1:
            kv = _shared_kv(x, kvmod, kv_w, kv_fb, kv_knorm)
    return x
```

```python
import functools
import math

import jax
import jax.numpy as jnp
from jax import lax
from jax.experimental import pallas as pl
from jax.experimental.pallas import tpu as pltpu

HEAD_DIM = 64
N_GROUPS = 4
EXP_PER_GROUP = 8
N_EXPERTS = N_GROUPS * EXP_PER_GROUP
TOP_K = 2
DEPTH = 4
N_A = 2
DEEPNORM_ALPHA = (2 * DEPTH) ** 0.25
LN_EPS = 1e-5
GN_EPS = 64e-5
QK_EPS = 1e-6

LANES = 128
VMEM_LIMIT_BYTES = 56 * 1024 * 1024
WKV_CHUNK = 64
WKV_TILE = 128
ROW_TILE = 512
RWKV_PRE_TILE = 256
ATTN_Q_TILE = 1024
ATTN_K_TILE = 512
ATTN_FLAG_MASK = 1
ATTN_FLAG_LAST = 2
MOE_BLOCK = 256
NEG_BIG = -1e30
LOG2E = math.log2(math.e)

f32 = jnp.float32
bf16 = jnp.bfloat16
HIGHEST = lax.Precision.HIGHEST


def _cparams(*sem):
    return pltpu.CompilerParams(dimension_semantics=sem, vmem_limit_bytes=VMEM_LIMIT_BYTES)


def _sigmoid(x):
    return 1.0 / (1.0 + jnp.exp(-x))


def _dot(a, b):
    return jnp.dot(a, b, preferred_element_type=f32)


def _dot_nt(a, b):
    return lax.dot_general(a, b, (((1,), (1,)), ((), ())), preferred_element_type=f32)


def _dot_tn(a, b):
    return lax.dot_general(a, b, (((0,), (0,)), ((), ())), preferred_element_type=f32)


def _full_spec(shape):
    n = len(shape)
    return pl.BlockSpec(shape, lambda *_: (0,) * n)


def _layer_norm_rows(u, g, b):
    mu = jnp.mean(u, axis=-1, keepdims=True)
    d = u - mu
    var = jnp.mean(d * d, axis=-1, keepdims=True)
    return d * lax.rsqrt(var + LN_EPS) * g + b


def _mod_kernel(c_ref, w_ref, b_ref, o_ref):
    c = c_ref[...]
    cs = c * _sigmoid(c)
    o_ref[0] = jnp.dot(cs, w_ref[0], preferred_element_type=f32, precision=HIGHEST) + b_ref[0]


def _adaln_mod(c, w, b):
    L, D, N = w.shape
    B = c.shape[0]
    tn = min(N, 1024)
    return pl.pallas_call(
        _mod_kernel,
        name="adaln_mod",
        grid=(L, N // tn),
        in_specs=[
            _full_spec((B, D)),
            pl.BlockSpec((1, D, tn), lambda l, j: (l, 0, j)),
            pl.BlockSpec((1, 1, tn), lambda l, j: (l, 0, j)),
        ],
        out_specs=pl.BlockSpec((1, B, tn), lambda l, j: (l, 0, j)),
        out_shape=jax.ShapeDtypeStruct((L, B, N), f32),
        compiler_params=_cparams("arbitrary", "arbitrary"),
    )(c, w, b.reshape(L, 1, N))


def _rwkv_pre_kernel(has_vmix, x_ref, xp_ref, mod_ref, mu_ref, wrkv_ref, w0_ref, w1_ref, w2_ref,
                     a0_ref, a1_ref, a2_ref, g1_ref, g2_ref, kkp_ref, kap_ref, *rest):
    if has_vmix:
        v0_ref, v1_ref, v2_ref, vf_ref = rest[:4]
        rest = rest[4:]
    r_ref, k_ref, v_ref, wl_ref, kk_ref, b_ref, g_ref = rest
    D = x_ref.shape[-1]
    sh = mod_ref[0, :, 0:D]
    sc = mod_ref[0, :, D:2 * D]
    h = x_ref[0] * (1.0 + sc) + sh
    hp = xp_ref[0][7:8, :] * (1.0 + sc) + sh
    hp = jnp.where(pl.program_id(1) == 0, 0.0, hp)
    row = lax.broadcasted_iota(jnp.int32, h.shape, 0)
    h_prev = jnp.where(row == 0, hp, pltpu.roll(h, 1, axis=0))
    xx = h_prev - h
    mix = lambda j: (h + xx * mu_ref[j:j + 1, :]).astype(bf16)
    xr, xw, xk, xv, xa, xg = [mix(j) for j in range(6)]
    r = _dot(xr, wrkv_ref[0])
    k = _dot(xk, wrkv_ref[1])
    v = _dot(xv, wrkv_ref[2])
    zw = w0_ref[...] + _dot(jnp.tanh(_dot(xw, w1_ref[...])).astype(bf16), w2_ref[...])
    wl_ref[0] = -math.exp(-0.5) * _sigmoid(zw)
    a = _sigmoid(a0_ref[...] + _dot(_dot(xa, a1_ref[...]).astype(bf16), a2_ref[...]))
    g = _dot(_sigmoid(_dot(xg, g1_ref[...])).astype(bf16), g2_ref[...])
    if has_vmix:
        vmix = _sigmoid(v0_ref[...] + _dot(_dot(xv, v1_ref[...]).astype(bf16), v2_ref[...]))
        v = v + (vf_ref[0].astype(f32) - v) * vmix
    r_ref[0] = r.astype(bf16)
    v_ref[0] = v.astype(bf16)
    g_ref[0] = g.astype(bf16)
    k_ref[0] = (k * (1.0 + (a - 1.0) * kap_ref[...])).astype(bf16)
    kk = k * kkp_ref[...]
    for hd in range(D // HEAD_DIM):
        sl = slice(hd * HEAD_DIM, (hd + 1) * HEAD_DIM)
        seg = kk[:, sl]
        nrm = jnp.sqrt(jnp.sum(seg * seg, axis=-1, keepdims=True))
        seg = seg / jnp.maximum(nrm, 1e-12)
        kk_ref[0, :, sl] = seg.astype(bf16)
        b_ref[0, :, sl] = (seg * a[:, sl]).astype(bf16)


def _rwkv_pre(x, modl, mu, wrkv, w0, w1, w2, a0, a1, a2, g1, g2, kkp, kap, vmix):
    B, S, D = x.shape
    tm = min(RWKV_PRE_TILE, S)
    row = lambda t: t.reshape(1, D)
    args = [x, x, modl, mu, wrkv.astype(bf16), row(w0), w1.astype(bf16), w2.astype(bf16), row(a0),
            a1.astype(bf16), a2.astype(bf16), g1.astype(bf16), g2.astype(bf16), row(kkp), row(kap)]
    tile = pl.BlockSpec((1, tm, D), lambda b, i: (b, i, 0))
    in_specs = [
        tile,
        pl.BlockSpec((1, 8, D), lambda b, i: (b, jnp.maximum(i * (tm // 8) - 1, 0), 0)),
        pl.BlockSpec((1, 1, modl.shape[-1]), lambda b, i: (b, 0, 0)),
    ] + [_full_spec(a.shape) for a in args[3:]]
    if vmix is not None:
        v0, v1, v2, v_first = vmix
        extra = [row(v0), v1.astype(bf16), v2.astype(bf16)]
        args += extra + [v_first]
        in_specs += [_full_spec(a.shape) for a in extra] + [tile]
    out_bf = jax.ShapeDtypeStruct((B, S, D), bf16)
    out_shape = [out_bf, out_bf, out_bf, jax.ShapeDtypeStruct((B, S, D), f32), out_bf, out_bf, out_bf]
    return pl.pallas_call(
        functools.partial(_rwkv_pre_kernel, vmix is not None),
        name="rwkv_pre",
        grid=(B, S // tm),
        in_specs=in_specs,
        out_specs=[tile] * 7,
        out_shape=out_shape,
        compiler_params=_cparams("arbitrary", "arbitrary"),
    )(*args)


def _head_sum(x, jbd):
    xh = x.astype(bf16)
    xl = (x - xh.astype(f32)).astype(bf16)
    return _dot(xh, jbd) + _dot(xl, jbd)


def _wkv_kernel(C, r_ref, k_ref, v_ref, wl_ref, kk_ref, b_ref, g_ref, rk_ref, gng_ref, gnb_ref,
                o_ref, state_ref):
    @pl.when(pl.program_id(1) == 0)
    def _():
        state_ref[...] = jnp.zeros_like(state_ref)

    T, D = r_ref.shape[1], r_ref.shape[2]
    N = HEAD_DIM
    P = D // LANES
    C2 = 2 * C
    ri = lax.broadcasted_iota(jnp.int32, (C, C), 0)
    ci = lax.broadcasted_iota(jnp.int32, (C, C), 1)
    tri_incl = jnp.where(ri >= ci, 1.0, 0.0).astype(f32)
    r2 = lax.broadcasted_iota(jnp.int32, (C2, C2), 0)
    c2 = lax.broadcasted_iota(jnp.int32, (C2, C2), 1)
    dlt = jnp.where(jnp.where(r2 >= C, 1, 0) == jnp.where(c2 >= C, 1, 0), r2 - c2, -1)
    strict = dlt > 0
    incl = dlt >= 0
    eye2 = jnp.where(r2 == c2, 1.0, 0.0).astype(f32)
    h0 = lax.broadcasted_iota(jnp.int32, (C, LANES), 1) < N
    rs = lax.broadcasted_iota(jnp.int32, (LANES, LANES), 0)
    cs = lax.broadcasted_iota(jnp.int32, (LANES, LANES), 1)
    sbd = jnp.where(rs >= N, 1, 0) == jnp.where(cs >= N, 1, 0)
    jbd = jnp.where(sbd, 1.0, 0.0).astype(bf16)
    stack2 = lambda t: jnp.concatenate([t, t], axis=0)

    for c in range(T // C):
        sl = pl.ds(c * C, C)
        wl = wl_ref[0, sl, :]
        cum = jnp.dot(tri_incl, wl, preferred_element_type=f32, precision=HIGHEST)
        g_incl = jnp.exp(cum)
        g_prev = jnp.exp(cum - wl)
        g_inv = jnp.exp(-cum)
        g_last = g_incl[C - 1:C, :]
        r = r_ref[0, sl, :].astype(f32)
        k = k_ref[0, sl, :].astype(f32)
        v16 = v_ref[0, sl, :]
        kk = kk_ref[0, sl, :].astype(f32)
        b = b_ref[0, sl, :].astype(f32)
        af = -kk * g_prev
        rf = r * g_incl
        btf = b * g_inv
        ktf = k * g_inv
        a16 = af.astype(bf16)
        r16 = rf.astype(bf16)
        bt = btf.astype(bf16)
        kt = ktf.astype(bf16)
        bc = (btf * g_last).astype(bf16)
        kc = (ktf * g_last).astype(bf16)
        rkr = r * k * rk_ref[...]
        lanes = [slice(p * LANES, (p + 1) * LANES) for p in range(P)]

        Ls, Aak, Arow, Vst = [], [], [], []
        for lp in lanes:
            afp, rfp = af[:, lp], rf[:, lp]
            X4 = jnp.concatenate([jnp.where(h0, afp, 0.0), jnp.where(h0, 0.0, afp),
                                  jnp.where(h0, rfp, 0.0), jnp.where(h0, 0.0, rfp)], axis=0).astype(bf16)
            Mb = _dot_nt(X4, stack2(bt[:, lp]))
            Mk = _dot_nt(X4, stack2(kt[:, lp]))
            Ls.append(jnp.where(strict, Mb[:C2], 0.0))
            Aak.append(jnp.where(strict, Mk[:C2], 0.0).astype(bf16))
            Arow.append(jnp.concatenate([jnp.where(incl, Mb[C2:], 0.0), jnp.where(incl, Mk[C2:], 0.0)],
                                        axis=1).astype(bf16))
            Vst.append(stack2(v16[:, lp]))
        Ps = [eye2 + L for L in Ls]
        n = 1
        while 2 * n < C:
            Lb = [L.astype(bf16) for L in Ls]
            Ls = [_dot(x, x) for x in Lb]
            Ps = [Pm + _dot(L.astype(bf16), Pm.astype(bf16)) for L, Pm in zip(Ls, Ps)]
            n *= 2
        AV = [_dot(a, vs) for a, vs in zip(Aak, Vst)]
        ys = []
        for p, lp in enumerate(lanes):
            S0 = state_ref[p]
            M2 = _dot_nt(jnp.concatenate([a16[:, lp], r16[:, lp]], axis=0), S0.astype(bf16))
            Wst = AV[p] + stack2(M2[:C])
            Ust = _dot(Ps[p].astype(bf16), Wst.astype(bf16))
            Yst = _dot(Arow[p], jnp.concatenate([Ust.astype(bf16), Vst[p]], axis=0)) + stack2(M2[C:])
            ys.append(jnp.where(h0, Yst[:C], Yst[C:]))
            u16 = jnp.where(h0, Ust[:C], Ust[C:]).astype(bf16)
            upd = _dot_tn(jnp.concatenate([u16, v16[:, lp]], axis=0),
                          jnp.concatenate([bc[:, lp], kc[:, lp]], axis=0))
            state_ref[p] = S0 * g_last[:, lp] + jnp.where(sbd, upd, 0.0)
        for p, lp in enumerate(lanes):
            y = ys[p]
            ym = _head_sum(y, jbd) * (1.0 / N)
            yd = y - ym
            yv = _head_sum(yd * yd, jbd) * (1.0 / N)
            yn = yd * lax.rsqrt(yv + GN_EPS)
            bonus = _head_sum(rkr[:, lp], jbd) * v16[:, lp].astype(f32)
            z = (yn * gng_ref[:, lp] + gnb_ref[:, lp] + bonus) * g_ref[0, sl, lp].astype(f32)
            o_ref[0, sl, lp] = z.astype(bf16)


def _wkv7(r, k, v, wl, kk, b, g, r_k, gn_g, gn_b):
    B, S, D = r.shape
    T = min(WKV_TILE, S)
    C = min(WKV_CHUNK, T)
    tile = pl.BlockSpec((1, T, D), lambda bb, i: (bb, i, 0))
    vec = _full_spec((1, D))
    return pl.pallas_call(
        functools.partial(_wkv_kernel, C),
        name="wkv7",
        grid=(B, S // T),
        in_specs=[tile] * 7 + [vec] * 3,
        out_specs=tile,
        out_shape=jax.ShapeDtypeStruct((B, S, D), bf16),
        scratch_shapes=[pltpu.VMEM((D // LANES, LANES, LANES), f32)],
        compiler_params=_cparams("arbitrary", "arbitrary"),
    )(r, k, v, wl, kk, b, g, r_k.reshape(1, D), gn_g.reshape(1, D), gn_b.reshape(1, D))


def _proj_ln_kernel(gate_idx, z_ref, w_ref, x_ref, mod_ref, lng_ref, lnb_ref, o_ref):
    D = x_ref.shape[-1]
    gt = mod_ref[0, :, gate_idx * D:(gate_idx + 1) * D]
    y = _dot(z_ref[0], w_ref[...])
    u = DEEPNORM_ALPHA * x_ref[0] + (1.0 + gt) * y
    o_ref[0] = _layer_norm_rows(u, lng_ref[...], lnb_ref[...])


def _proj_ln(z, w, x, modl, gate_idx, ln_g, ln_b):
    B, S, D = x.shape
    tm = min(ROW_TILE, S)
    tile = pl.BlockSpec((1, tm, D), lambda b, i: (b, i, 0))
    return pl.pallas_call(
        functools.partial(_proj_ln_kernel, gate_idx),
        name="proj_ln",
        grid=(B, S // tm),
        in_specs=[tile, _full_spec(w.shape), tile,
                  pl.BlockSpec((1, 1, modl.shape[-1]), lambda b, i: (b, 0, 0)),
                  _full_spec((1, D)), _full_spec((1, D))],
        out_specs=tile,
        out_shape=jax.ShapeDtypeStruct((B, S, D), f32),
        compiler_params=_cparams("arbitrary", "arbitrary"),
    )(z, w.astype(bf16), x, modl, ln_g.reshape(1, D), ln_b.reshape(1, D))


def _route_kernel(x_ref, mod_ref, w_ref, b_ref, h_ref, idx_ref, gate_ref, cnt_ref):
    @pl.when((pl.program_id(0) == 0) & (pl.program_id(1) == 0))
    def _():
        cnt_ref[...] = jnp.zeros_like(cnt_ref)

    D = x_ref.shape[-1]
    sh = mod_ref[0, :, 3 * D:4 * D]
    sc = mod_ref[0, :, 4 * D:5 * D]
    h = x_ref[0] * (1.0 + sc) + sh
    h_ref[0] = h.astype(bf16)
    lg = jnp.dot(h, w_ref[...], preferred_element_type=f32, precision=HIGHEST) + b_ref[...]
    lane = lax.broadcasted_iota(jnp.int32, lg.shape, 1)
    glog = jnp.where(lane < N_GROUPS, lg, -jnp.inf)
    gmax = jnp.max(glog, axis=-1, keepdims=True)
    p_g = 1.0 / jnp.sum(jnp.exp(glog - gmax), axis=-1, keepdims=True)
    g_sel = jnp.min(jnp.where(glog == gmax, lane, LANES), axis=-1, keepdims=True)
    e_lo = N_GROUPS + EXP_PER_GROUP * g_sel
    elog = jnp.where((lane >= e_lo) & (lane < e_lo + EXP_PER_GROUP), lg, -jnp.inf)
    e1 = jnp.max(elog, axis=-1, keepdims=True)
    esum = jnp.sum(jnp.exp(elog - e1), axis=-1, keepdims=True)
    i1 = jnp.min(jnp.where(elog == e1, lane, LANES), axis=-1, keepdims=True)
    elog2 = jnp.where(lane == i1, -jnp.inf, elog)
    e2 = jnp.max(elog2, axis=-1, keepdims=True)
    i2 = jnp.min(jnp.where(elog2 == e2, lane, LANES), axis=-1, keepdims=True)
    p1 = 1.0 / esum
    p2 = jnp.exp(e2 - e1) / esum
    psum = p1 + p2
    gate_ref[0] = jnp.where(lane == 0, p_g * p1 / psum, jnp.where(lane == 1, p_g * p2 / psum, 0.0))
    tm = lg.shape[0]
    onehot = jnp.where(lane == i1, 1.0, 0.0) + jnp.where(lane == i2, 1.0, 0.0)
    ri = lax.broadcasted_iota(jnp.int32, (tm, tm), 0)
    ci = lax.broadcasted_iota(jnp.int32, (tm, tm), 1)
    before = _dot(jnp.where(ri > ci, 1.0, 0.0).astype(bf16), onehot.astype(bf16)) + cnt_ref[...]
    rank1 = jnp.sum(jnp.where(lane == i1, before, 0.0), axis=-1, keepdims=True).astype(jnp.int32)
    rank2 = jnp.sum(jnp.where(lane == i2, before, 0.0), axis=-1, keepdims=True).astype(jnp.int32)
    idx_ref[0] = jnp.where(lane == 0, i1 - N_GROUPS, jnp.where(lane == 1, i2 - N_GROUPS,
                           jnp.where(lane == 2, rank1, jnp.where(lane == 3, rank2, 0))))
    cnt_ref[...] = before[tm - 1:tm, :] + onehot[tm - 1:tm, :]


def _route(x, modl, w_grp, b_grp, w_exp, b_exp):
    B, S, D = x.shape
    tm = min(ROW_TILE, S)
    pad = LANES - N_GROUPS - N_EXPERTS
    w = jnp.concatenate([w_grp, w_exp, jnp.zeros((D, pad), f32)], axis=1)
    b = jnp.concatenate([b_grp, b_exp, jnp.zeros((pad,), f32)]).reshape(1, LANES)
    tile = pl.BlockSpec((1, tm, D), lambda bb, i: (bb, i, 0))
    small = pl.BlockSpec((1, tm, LANES), lambda bb, i: (bb, i, 0))
    return pl.pallas_call(
        _route_kernel,
        name="moe_route",
        grid=(B, S // tm),
        in_specs=[tile, pl.BlockSpec((1, 1, modl.shape[-1]), lambda bb, i: (bb, 0, 0)),
                  _full_spec(w.shape), _full_spec(b.shape)],
        out_specs=[tile, small, small, _full_spec((1, LANES))],
        out_shape=[jax.ShapeDtypeStruct((B, S, D), bf16),
                   jax.ShapeDtypeStruct((B, S, LANES), jnp.int32),
                   jax.ShapeDtypeStruct((B, S, LANES), f32),
                   jax.ShapeDtypeStruct((1, LANES), f32)],
        compiler_params=_cparams("arbitrary", "arbitrary"),
    )(x, modl, w, b)


def _expert_kernel(be_ref, nu_ref, x_ref, wg_ref, wu_ref, wd_ref, o_ref):
    i = pl.program_id(0)

    @pl.when(i < nu_ref[0])
    def _():
        x = x_ref[...]
        hg = _dot(x, wg_ref[0])
        hu = _dot(x, wu_ref[0])
        o_ref[...] = _dot((hg * _sigmoid(hg) * hu).astype(bf16), wd_ref[0]).astype(o_ref.dtype)

    @pl.when(i >= nu_ref[0])
    def _():
        o_ref[...] = jnp.zeros_like(o_ref)


def _expert_ffn(xs, block_exp, n_used, wg, wu, wd):
    n_slots, D = xs.shape
    E, _, F = wg.shape
    blk = MOE_BLOCK
    row_map = lambda i, be, nu: (jnp.minimum(i, nu[0] - 1), 0)
    grid_spec = pltpu.PrefetchScalarGridSpec(
        num_scalar_prefetch=2,
        grid=(n_slots // blk,),
        in_specs=[
            pl.BlockSpec((blk, D), row_map),
            pl.BlockSpec((1, D, F), lambda i, be, nu: (be[i], 0, 0)),
            pl.BlockSpec((1, D, F), lambda i, be, nu: (be[i], 0, 0)),
            pl.BlockSpec((1, F, D), lambda i, be, nu: (be[i], 0, 0)),
        ],
        out_specs=pl.BlockSpec((blk, D), lambda i, be, nu: (i, 0)),
    )
    return pl.pallas_call(
        _expert_kernel,
        name="moe_experts",
        grid_spec=grid_spec,
        out_shape=jax.ShapeDtypeStruct((n_slots, D), bf16),
        compiler_params=_cparams("arbitrary"),
    )(block_exp, n_used, xs, wg, wu, wd)


def _take_rows(table, idx):
    return table.at[idx].get(mode="promise_in_bounds")


def _dispatch(expert_idx, rank, counts, blk):
    T = expert_idx.shape[0]
    A = T * TOP_K
    n_blocks = A // blk + N_EXPERTS
    experts = jnp.arange(N_EXPERTS, dtype=jnp.int32)
    padded = (counts + blk - 1) // blk * blk
    pad_end = jnp.cumsum(padded)
    pad_start = pad_end - padded
    start = jnp.cumsum(counts) - counts
    dest = rank + jnp.sum(jnp.where(expert_idx[..., None] == experts, pad_start, 0), axis=-1)
    block_exp = jnp.minimum(jnp.sum(((jnp.arange(n_blocks, dtype=jnp.int32) * blk)[:, None] >= pad_end[None, :])
                                    .astype(jnp.int32), axis=1), N_EXPERTS - 1)
    order = jnp.argsort(expert_idx.reshape(A)).astype(jnp.int32)
    shift = jnp.sum(jnp.where(block_exp[:, None] == experts, start - pad_start, 0), axis=-1)
    pos = jnp.arange(n_blocks * blk, dtype=jnp.int32) + jnp.repeat(shift, blk)
    slot_tok = _take_rows(order, jnp.clip(pos, 0, A - 1)) // TOP_K
    n_used = (pad_end[-1] // blk).astype(jnp.int32).reshape(1)
    return dest, slot_tok, block_exp, n_used


def _combine_ln_kernel(x_ref, y0_ref, y1_ref, gate_ref, mod_ref, lng_ref, lnb_ref, o_ref):
    D = x_ref.shape[-1]
    gt = mod_ref[0, :, 5 * D:6 * D]
    gates = gate_ref[0]
    y = gates[:, 0:1] * y0_ref[0].astype(f32) + gates[:, 1:2] * y1_ref[0].astype(f32)
    u = DEEPNORM_ALPHA * x_ref[0] + (1.0 + gt) * y
    o_ref[0] = _layer_norm_rows(u, lng_ref[...], lnb_ref[...])


def _combine_ln(x, y0, y1, gates, modl, ln_g, ln_b):
    B, S, D = x.shape
    tm = min(ROW_TILE, S)
    tile = pl.BlockSpec((1, tm, D), lambda b, i: (b, i, 0))
    return pl.pallas_call(
        _combine_ln_kernel,
        name="moe_combine_ln",
        grid=(B, S // tm),
        in_specs=[tile, tile, tile, pl.BlockSpec((1, tm, LANES), lambda b, i: (b, i, 0)),
                  pl.BlockSpec((1, 1, modl.shape[-1]), lambda b, i: (b, 0, 0)),
                  _full_spec((1, D)), _full_spec((1, D))],
        out_specs=tile,
        out_shape=jax.ShapeDtypeStruct((B, S, D), f32),
        compiler_params=_cparams("arbitrary", "arbitrary"),
    )(x, y0, y1, gates, modl, ln_g.reshape(1, D), ln_b.reshape(1, D))


def _moe_layer(x, modl, w_grp, b_grp, w_exp, b_exp, wg, wu, wd, ln_g, ln_b):
    B, S, D = x.shape
    T = B * S
    hb, idx, gates, cnt = _route(x, modl, w_grp, b_grp, w_exp, b_exp)
    idx = idx.reshape(T, LANES)
    counts = cnt[0, N_GROUPS:N_GROUPS + N_EXPERTS].astype(jnp.int32)
    dest, slot_tok, block_exp, n_used = _dispatch(idx[:, :TOP_K], idx[:, TOP_K:2 * TOP_K], counts, MOE_BLOCK)
    xs = _take_rows(hb.reshape(T, D), slot_tok)
    ys = _expert_ffn(xs, block_exp, n_used, wg.astype(bf16), wu.astype(bf16), wd.astype(bf16))
    y0 = _take_rows(ys, dest[:, 0]).reshape(B, S, D)
    y1 = _take_rows(ys, dest[:, 1]).reshape(B, S, D)
    return _combine_ln(x, y0, y1, gates, modl, ln_g, ln_b)


def _bias_lanes(fcol, is_key):
    t0 = fcol.astype(bf16).astype(f32)
    r1 = fcol - t0
    t1 = r1.astype(bf16).astype(f32)
    t2 = (r1 - t1).astype(bf16).astype(f32)
    lane = lax.broadcasted_iota(jnp.int32, (fcol.shape[0], HEAD_DIM), 1)
    if is_key:
        out = jnp.where(lane < 3, 1.0, jnp.where(lane == 3, -t0, jnp.where(lane == 4, -t1,
                                                                           jnp.where(lane == 5, -t2, 0.0))))
    else:
        out = jnp.where(lane == 0, t0, jnp.where(lane == 1, t1, jnp.where(lane == 2, t2,
                                                                          jnp.where(lane < 6, 1.0, 0.0))))
    return out.astype(bf16)


def _shared_kv_kernel(x_ref, mod_ref, wk_ref, wvt_ref, wf_ref, fb_ref, kn_ref, ka_ref, vt_ref, f_ref, carry_ref):
    @pl.when(pl.program_id(1) == 0)
    def _():
        carry_ref[...] = jnp.zeros_like(carry_ref)

    D = x_ref.shape[-1]
    tm = x_ref.shape[1]
    shift = mod_ref[0, :, 0:D]
    scale = mod_ref[0, :, D:2 * D]
    hk = x_ref[0] * (1.0 + scale) + shift
    hb = hk.astype(bf16)
    k = _dot(hb, wk_ref[...])
    vt_ref[0] = _dot_nt(wvt_ref[...], hb).astype(bf16)
    f = jnp.dot(hk, wf_ref[...], preferred_element_type=f32, precision=HIGHEST) + fb_ref[...]
    log_f = jnp.minimum(f, 0.0) - jnp.log(1.0 + jnp.exp(-jnp.abs(f)))
    row = lax.broadcasted_iota(jnp.int32, log_f.shape, 0)
    acc = log_f
    d = 1
    while d < tm:
        acc = acc + jnp.where(row >= d, pltpu.roll(acc, d, axis=0), 0.0)
        d *= 2
    acc = acc + carry_ref[...]
    f_ref[0] = acc
    carry_ref[...] = acc[tm - 1:tm, :]
    f2 = acc * LOG2E
    for hd in range(D // HEAD_DIM):
        sl = slice(hd * HEAD_DIM, (hd + 1) * HEAD_DIM)
        seg = k[:, sl]
        ms = jnp.mean(seg * seg, axis=-1, keepdims=True)
        ka_ref[0, :, 2 * hd * HEAD_DIM:(2 * hd + 1) * HEAD_DIM] = (
            seg * lax.rsqrt(ms + QK_EPS) * kn_ref[...]).astype(bf16)
        ka_ref[0, :, (2 * hd + 1) * HEAD_DIM:(2 * hd + 2) * HEAD_DIM] = _bias_lanes(f2[:, hd:hd + 1], True)


def _shared_kv(x, kvmod, w_kvf, b_f, k_norm):
    B, S, D = x.shape
    H = D // HEAD_DIM
    tm = min(ROW_TILE, S)
    wk = w_kvf[:, :D].astype(bf16)
    wvt = w_kvf[:, D:2 * D].T.astype(bf16)
    wf = jnp.concatenate([w_kvf[:, 2 * D:], jnp.zeros((D, LANES - H), f32)], axis=1)
    fb = jnp.concatenate([b_f, jnp.zeros((LANES - H,), f32)]).reshape(1, LANES)
    tile = pl.BlockSpec((1, tm, D), lambda b, i: (b, i, 0))
    return pl.pallas_call(
        _shared_kv_kernel,
        name="shared_kv",
        grid=(B, S // tm),
        in_specs=[tile, pl.BlockSpec((1, 1, 2 * D), lambda b, i: (b, 0, 0)),
                  _full_spec(wk.shape), _full_spec(wvt.shape), _full_spec(wf.shape),
                  _full_spec(fb.shape), _full_spec((1, HEAD_DIM))],
        out_specs=[pl.BlockSpec((1, tm, 2 * D), lambda b, i: (b, i, 0)),
                   pl.BlockSpec((1, D, tm), lambda b, i: (b, 0, i)),
                   pl.BlockSpec((1, tm, LANES), lambda b, i: (b, i, 0))],
        out_shape=[jax.ShapeDtypeStruct((B, S, 2 * D), bf16), jax.ShapeDtypeStruct((B, D, S), bf16),
                   jax.ShapeDtypeStruct((B, S, LANES), f32)],
        scratch_shapes=[pltpu.VMEM((1, LANES), f32)],
        compiler_params=_cparams("arbitrary", "arbitrary"),
    )(x, kvmod, wk, wvt, wf, fb, k_norm.reshape(1, HEAD_DIM))


def _fox_pre_kernel(x_ref, mod_ref, wq_ref, wg_ref, qn_ref, f_ref, qa_ref, gate_ref):
    D = x_ref.shape[-1]
    sh = mod_ref[0, :, 0:D]
    sc = mod_ref[0, :, D:2 * D]
    hb = (x_ref[0] * (1.0 + sc) + sh).astype(bf16)
    q = _dot(hb, wq_ref[...])
    gate_ref[0] = _sigmoid(_dot(hb, wg_ref[...])).astype(bf16)
    qscale = HEAD_DIM ** -0.5 * LOG2E
    f2 = f_ref[0] * LOG2E
    for hd in range(D // HEAD_DIM):
        sl = slice(hd * HEAD_DIM, (hd + 1) * HEAD_DIM)
        seg = q[:, sl]
        ms = jnp.mean(seg * seg, axis=-1, keepdims=True)
        qa_ref[0, :, 2 * hd * HEAD_DIM:(2 * hd + 1) * HEAD_DIM] = (
            seg * lax.rsqrt(ms + QK_EPS) * qn_ref[...] * qscale).astype(bf16)
        qa_ref[0, :, (2 * hd + 1) * HEAD_DIM:(2 * hd + 2) * HEAD_DIM] = _bias_lanes(f2[:, hd:hd + 1], False)


def _fox_pre(x, modl, w_qg, q_norm, fcum):
    B, S, D = x.shape
    tm = min(ROW_TILE, S)
    tile = pl.BlockSpec((1, tm, D), lambda b, i: (b, i, 0))
    wq = w_qg[:, :D].astype(bf16)
    wg = w_qg[:, D:].astype(bf16)
    return pl.pallas_call(
        _fox_pre_kernel,
        name="fox_pre",
        grid=(B, S // tm),
        in_specs=[tile, pl.BlockSpec((1, 1, modl.shape[-1]), lambda b, i: (b, 0, 0)),
                  _full_spec(wq.shape), _full_spec(wg.shape), _full_spec((1, HEAD_DIM)),
                  pl.BlockSpec((1, tm, LANES), lambda b, i: (b, i, 0))],
        out_specs=[pl.BlockSpec((1, tm, 2 * D), lambda b, i: (b, i, 0)), tile],
        out_shape=[jax.ShapeDtypeStruct((B, S, 2 * D), bf16), jax.ShapeDtypeStruct((B, S, D), bf16)],
        compiler_params=_cparams("arbitrary", "arbitrary"),
    )(x, modl, wq, wg, q_norm.reshape(1, HEAD_DIM), fcum)


def _fox_attn_kernel(qi_ref, kj_ref, flag_ref, qa_ref, ka_ref, vt_ref, gate_ref, o_ref, m_ref, l_ref, acc_ref):
    s = pl.program_id(2)
    i = qi_ref[s]
    j = kj_ref[s]
    flags = flag_ref[s]
    N = HEAD_DIM
    nh = LANES // N
    tk, tq = ka_ref.shape[1], qa_ref.shape[1]

    @pl.when(j == 0)
    def _():
        m_ref[...] = jnp.full_like(m_ref, NEG_BIG)
        l_ref[...] = jnp.zeros_like(l_ref)
        acc_ref[...] = jnp.zeros_like(acc_ref)

    def step(masked):
        for hh in range(nh):
            hl = slice(hh * LANES, (hh + 1) * LANES)
            st = _dot_nt(ka_ref[0, :, hl], qa_ref[0, :, hl])
            if masked:
                key = j * tk + lax.broadcasted_iota(jnp.int32, st.shape, 0)
                qry = i * tq + lax.broadcasted_iota(jnp.int32, st.shape, 1)
                st = jnp.where(key <= qry, st, -jnp.inf)
            m_prev = m_ref[hh]
            m_new = jnp.maximum(m_prev, jnp.max(st, axis=0, keepdims=True))
            alpha = jnp.exp2(m_prev - m_new)
            p = jnp.exp2(st - m_new)
            l_ref[hh] = alpha * l_ref[hh] + jnp.sum(p, axis=0, keepdims=True)
            acc_ref[hh] = alpha * acc_ref[hh] + _dot(vt_ref[0, hh * N:(hh + 1) * N, :], p.astype(bf16))
            m_ref[hh] = m_new

    @pl.when((flags & ATTN_FLAG_MASK) == 0)
    def _():
        step(False)

    @pl.when((flags & ATTN_FLAG_MASK) != 0)
    def _():
        step(True)

    @pl.when((flags & ATTN_FLAG_LAST) != 0)
    def _():
        ot = jnp.concatenate([acc_ref[hh] * (1.0 / l_ref[hh]) for hh in range(nh)], axis=0)
        o_ref[0] = (ot.T * gate_ref[0].astype(f32)).astype(bf16)


def _fox_attn(qa, ka, vt, gate):
    B, S, D = gate.shape
    tq = min(ATTN_Q_TILE, S)
    tk = min(ATTN_K_TILE, S)
    P = D // LANES
    nh = LANES // HEAD_DIM
    qi, kj, flags = [], [], []
    for i in range(S // tq):
        last = ((i + 1) * tq - 1) // tk
        for j in range(last + 1):
            qi.append(i)
            kj.append(j)
            needs_mask = (j + 1) * tk - 1 > i * tq
            flags.append((ATTN_FLAG_MASK if needs_mask else 0) | (ATTN_FLAG_LAST if j == last else 0))
    tables = [jnp.asarray(t, jnp.int32) for t in (qi, kj, flags)]
    grid_spec = pltpu.PrefetchScalarGridSpec(
        num_scalar_prefetch=3,
        grid=(B, P, len(qi)),
        in_specs=[pl.BlockSpec((1, tq, 2 * LANES), lambda b, p, s, qi, kj, fl: (b, qi[s], p)),
                  pl.BlockSpec((1, tk, 2 * LANES), lambda b, p, s, qi, kj, fl: (b, kj[s], p)),
                  pl.BlockSpec((1, LANES, tk), lambda b, p, s, qi, kj, fl: (b, p, kj[s])),
                  pl.BlockSpec((1, tq, LANES), lambda b, p, s, qi, kj, fl: (b, qi[s], p))],
        out_specs=pl.BlockSpec((1, tq, LANES), lambda b, p, s, qi, kj, fl: (b, qi[s], p)),
        scratch_shapes=[pltpu.VMEM((nh, 1, tq), f32), pltpu.VMEM((nh, 1, tq), f32),
                        pltpu.VMEM((nh, HEAD_DIM, tq), f32)],
    )
    return pl.pallas_call(
        _fox_attn_kernel,
        name="fox_attn",
        grid_spec=grid_spec,
        out_shape=jax.ShapeDtypeStruct((B, S, D), bf16),
        compiler_params=_cparams("arbitrary", "arbitrary", "arbitrary"),
    )(*tables, qa, ka, vt, gate)


def kernel(x, c, ada_w, ada_b, ln_g, ln_b, rw_mu, rw_rkv, rw_w0, rw_w1, rw_w2, rw_a0, rw_a1, rw_a2, rw_g1, rw_g2, rw_kk, rw_ka, rw_rk, rw_gn_g, rw_gn_b, rw_wo, rw_v0, rw_v1, rw_v2, kv_ada_w, kv_ada_b, kv_w, kv_fb, kv_knorm, fx_wqg, fx_qnorm, fx_wo, moe_wgrp, moe_bgrp, moe_wexp, moe_bexp, moe_wgate, moe_wup, moe_wdown):
    B, S, D = x.shape
    depth = ada_w.shape[0]
    n_a = rw_mu.shape[0]
    mod = _adaln_mod(c, ada_w, ada_b)
    kvmod = _adaln_mod(c, kv_ada_w[None], kv_ada_b[None])[0].reshape(B, 1, 2 * D)
    kv = None
    v_first = None
    for l in range(depth):
        modl = mod[l].reshape(B, 1, 6 * D)
        if l < n_a:
            vmix = None if l == 0 else (rw_v0[l - 1], rw_v1[l - 1], rw_v2[l - 1], v_first)
            r, k, v, wl, kk, b, g = _rwkv_pre(x, modl, rw_mu[l], rw_rkv[l], rw_w0[l], rw_w1[l], rw_w2[l],
                                              rw_a0[l], rw_a1[l], rw_a2[l], rw_g1[l], rw_g2[l],
                                              rw_kk[l], rw_ka[l], vmix)
            if l == 0:
                v_first = v
            z = _wkv7(r, k, v, wl, kk, b, g, rw_rk[l].reshape(D), rw_gn_g[l], rw_gn_b[l])
            w_o = rw_wo[l]
        else:
            j = l - n_a
            ka, vt, fcum = kv
            qa, gate = _fox_pre(x, modl, fx_wqg[j], fx_qnorm[j], fcum)
            z = _fox_attn(qa, ka, vt, gate)
            w_o = fx_wo[j]
        x = _proj_ln(z, w_o, x, modl, 2, ln_g[l, 0], ln_b[l, 0])
        x = _moe_layer(x, modl, moe_wgrp[l], moe_bgrp[l], moe_wexp[l], moe_bexp[l],
                       moe_wgate[l], moe_wup[l], moe_wdown[l], ln_g[l, 1], ln_b[l, 1])
        if l == n_a - 1:
            kv = _shared_kv(x, kvmod, kv_w, kv_fb, kv_knorm)
    return x
```

```python
import functools
import math

import jax
import jax.numpy as jnp
import numpy as np
from jax import lax
from jax.experimental import pallas as pl
from jax.experimental.pallas import tpu as pltpu

HEAD_DIM = 64
N_GROUPS = 4
EXP_PER_GROUP = 8
N_EXPERTS = N_GROUPS * EXP_PER_GROUP
TOP_K = 2
DEPTH = 4
N_A = 2
DEEPNORM_ALPHA = (2 * DEPTH) ** 0.25
LN_EPS = 1e-5
GN_EPS = 64e-5
QK_EPS = 1e-6

LANES = 128
VMEM_LIMIT_BYTES = 56 * 1024 * 1024
WKV_CHUNK = 64
WKV_TILE = 128
ROW_TILE = 512
RWKV_PRE_TILE = 256
ATTN_Q_TILE = 1024
ATTN_K_TILE = 512
ATTN_FLAG_MASK = 1
ATTN_FLAG_LAST = 2
MOE_BLOCK = 256
NEG_BIG = -1e30
LOG2E = math.log2(math.e)

f32 = jnp.float32
bf16 = jnp.bfloat16
HIGHEST = lax.Precision.HIGHEST


def _cparams(*sem):
    return pltpu.CompilerParams(dimension_semantics=sem, vmem_limit_bytes=VMEM_LIMIT_BYTES)


def _sigmoid(x):
    return 1.0 / (1.0 + jnp.exp(-x))


def _dot(a, b):
    return jnp.dot(a, b, preferred_element_type=f32)


def _dot_nt(a, b):
    return lax.dot_general(a, b, (((1,), (1,)), ((), ())), preferred_element_type=f32)


def _dot_tn(a, b):
    return lax.dot_general(a, b, (((0,), (0,)), ((), ())), preferred_element_type=f32)


def _full_spec(shape):
    n = len(shape)
    return pl.BlockSpec(shape, lambda *_: (0,) * n)


def _pair_ones():
    rs = lax.broadcasted_iota(jnp.int32, (LANES, LANES), 0)
    cs = lax.broadcasted_iota(jnp.int32, (LANES, LANES), 1)
    return jnp.where(jnp.where(rs >= HEAD_DIM, 1, 0) == jnp.where(cs >= HEAD_DIM, 1, 0), 1.0, 0.0).astype(bf16)


def _head_sum(x, jbd):
    xh = x.astype(bf16)
    xl = (x - xh.astype(f32)).astype(bf16)
    return _dot(xh, jbd) + _dot(xl, jbd)


def _bias_split(f):
    t0 = f.astype(bf16)
    r1 = f - t0.astype(f32)
    t1 = r1.astype(bf16)
    t2 = (r1 - t1.astype(f32)).astype(bf16)
    return jnp.concatenate([t0, t1, t2], axis=1)


def _bias_placement(D, is_key):
    H = D // HEAD_DIM
    place = np.zeros((3 * LANES, D), np.float32)
    ones = np.zeros((1, D), np.float32)
    for h in range(H):
        for i in range(3):
            place[i * LANES + h, h * HEAD_DIM + (3 + i if is_key else i)] = -1.0 if is_key else 1.0
            ones[0, h * HEAD_DIM + (i if is_key else 3 + i)] = 1.0
    return jnp.asarray(place, bf16), jnp.asarray(ones, f32)


def _layer_norm_rows(u, g, b):
    mu = jnp.mean(u, axis=-1, keepdims=True)
    d = u - mu
    var = jnp.mean(d * d, axis=-1, keepdims=True)
    return d * lax.rsqrt(var + LN_EPS) * g + b


def _mod_kernel(c_ref, w_ref, b_ref, o_ref):
    c = c_ref[...]
    cs = c * _sigmoid(c)
    o_ref[0] = jnp.dot(cs, w_ref[0], preferred_element_type=f32, precision=HIGHEST) + b_ref[0]


def _adaln_mod(c, w, b):
    L, D, N = w.shape
    B = c.shape[0]
    tn = min(N, 1024)
    return pl.pallas_call(
        _mod_kernel,
        name="adaln_mod",
        grid=(L, N // tn),
        in_specs=[
            _full_spec((B, D)),
            pl.BlockSpec((1, D, tn), lambda l, j: (l, 0, j)),
            pl.BlockSpec((1, 1, tn), lambda l, j: (l, 0, j)),
        ],
        out_specs=pl.BlockSpec((1, B, tn), lambda l, j: (l, 0, j)),
        out_shape=jax.ShapeDtypeStruct((L, B, N), f32),
        compiler_params=_cparams("arbitrary", "arbitrary"),
    )(c, w, b.reshape(L, 1, N))


def _rwkv_pre_kernel(has_vmix, x_ref, xp_ref, mod_ref, mu_ref, wrkv_ref, w0_ref, w1_ref, w2_ref,
                     a0_ref, a1_ref, a2_ref, g1_ref, g2_ref, kkp_ref, kap_ref, *rest):
    if has_vmix:
        v0_ref, v1_ref, v2_ref, vf_ref = rest[:4]
        rest = rest[4:]
    r_ref, k_ref, v_ref, wl_ref, kk_ref, b_ref, g_ref = rest
    D = x_ref.shape[-1]
    sh = mod_ref[0, :, 0:D]
    sc = mod_ref[0, :, D:2 * D]
    h = x_ref[0] * (1.0 + sc) + sh
    hp = xp_ref[0][7:8, :] * (1.0 + sc) + sh
    hp = jnp.where(pl.program_id(1) == 0, 0.0, hp)
    row = lax.broadcasted_iota(jnp.int32, h.shape, 0)
    h_prev = jnp.where(row == 0, hp, pltpu.roll(h, 1, axis=0))
    xx = h_prev - h
    mix = lambda j: (h + xx * mu_ref[j:j + 1, :]).astype(bf16)
    xr, xw, xk, xv, xa, xg = [mix(j) for j in range(6)]
    r = _dot(xr, wrkv_ref[0])
    k = _dot(xk, wrkv_ref[1])
    v = _dot(xv, wrkv_ref[2])
    zw = w0_ref[...] + _dot(jnp.tanh(_dot(xw, w1_ref[...])).astype(bf16), w2_ref[...])
    wl_ref[0] = -math.exp(-0.5) * _sigmoid(zw)
    a = _sigmoid(a0_ref[...] + _dot(_dot(xa, a1_ref[...]).astype(bf16), a2_ref[...]))
    g = _dot(_sigmoid(_dot(xg, g1_ref[...])).astype(bf16), g2_ref[...])
    if has_vmix:
        vmix = _sigmoid(v0_ref[...] + _dot(_dot(xv, v1_ref[...]).astype(bf16), v2_ref[...]))
        v = v + (vf_ref[0].astype(f32) - v) * vmix
    r_ref[0] = r.astype(bf16)
    v_ref[0] = v.astype(bf16)
    g_ref[0] = g.astype(bf16)
    k_ref[0] = (k * (1.0 + (a - 1.0) * kap_ref[...])).astype(bf16)
    kk = k * kkp_ref[...]
    jbd = _pair_ones()
    for p in range(D // LANES):
        lp = slice(p * LANES, (p + 1) * LANES)
        seg = kk[:, lp]
        nrm = jnp.sqrt(_head_sum(seg * seg, jbd))
        seg = seg / jnp.maximum(nrm, 1e-12)
        kk_ref[0, :, lp] = seg.astype(bf16)
        b_ref[0, :, lp] = (seg * a[:, lp]).astype(bf16)


def _rwkv_pre(x, modl, mu, wrkv, w0, w1, w2, a0, a1, a2, g1, g2, kkp, kap, vmix):
    B, S, D = x.shape
    tm = min(RWKV_PRE_TILE, S)
    row = lambda t: t.reshape(1, D)
    args = [x, x, modl, mu, wrkv.astype(bf16), row(w0), w1.astype(bf16), w2.astype(bf16), row(a0),
            a1.astype(bf16), a2.astype(bf16), g1.astype(bf16), g2.astype(bf16), row(kkp), row(kap)]
    tile = pl.BlockSpec((1, tm, D), lambda b, i: (b, i, 0))
    in_specs = [
        tile,
        pl.BlockSpec((1, 8, D), lambda b, i: (b, jnp.maximum(i * (tm // 8) - 1, 0), 0)),
        pl.BlockSpec((1, 1, modl.shape[-1]), lambda b, i: (b, 0, 0)),
    ] + [_full_spec(a.shape) for a in args[3:]]
    if vmix is not None:
        v0, v1, v2, v_first = vmix
        extra = [row(v0), v1.astype(bf16), v2.astype(bf16)]
        args += extra + [v_first]
        in_specs += [_full_spec(a.shape) for a in extra] + [tile]
    out_bf = jax.ShapeDtypeStruct((B, S, D), bf16)
    out_shape = [out_bf, out_bf, out_bf, jax.ShapeDtypeStruct((B, S, D), f32), out_bf, out_bf, out_bf]
    return pl.pallas_call(
        functools.partial(_rwkv_pre_kernel, vmix is not None),
        name="rwkv_pre",
        grid=(B, S // tm),
        in_specs=in_specs,
        out_specs=[tile] * 7,
        out_shape=out_shape,
        compiler_params=_cparams("arbitrary", "arbitrary"),
    )(*args)


def _wkv_kernel(C, r_ref, k_ref, v_ref, wl_ref, kk_ref, b_ref, g_ref, rk_ref, gng_ref, gnb_ref,
                o_ref, state_ref):
    @pl.when(pl.program_id(1) == 0)
    def _():
        state_ref[...] = jnp.zeros_like(state_ref)

    T, D = r_ref.shape[1], r_ref.shape[2]
    N = HEAD_DIM
    P = D // LANES
    C2 = 2 * C
    ri = lax.broadcasted_iota(jnp.int32, (C, C), 0)
    ci = lax.broadcasted_iota(jnp.int32, (C, C), 1)
    tri_incl = jnp.where(ri >= ci, 1.0, 0.0).astype(f32)
    r2 = lax.broadcasted_iota(jnp.int32, (C2, C2), 0)
    c2 = lax.broadcasted_iota(jnp.int32, (C2, C2), 1)
    dlt = jnp.where(jnp.where(r2 >= C, 1, 0) == jnp.where(c2 >= C, 1, 0), r2 - c2, -1)
    strict = dlt > 0
    incl = dlt >= 0
    eye2 = jnp.where(r2 == c2, 1.0, 0.0).astype(f32)
    h0 = lax.broadcasted_iota(jnp.int32, (C, LANES), 1) < N
    rs = lax.broadcasted_iota(jnp.int32, (LANES, LANES), 0)
    cs = lax.broadcasted_iota(jnp.int32, (LANES, LANES), 1)
    sbd = jnp.where(rs >= N, 1, 0) == jnp.where(cs >= N, 1, 0)
    jbd = jnp.where(sbd, 1.0, 0.0).astype(bf16)
    stack2 = lambda t: jnp.concatenate([t, t], axis=0)

    lanes = [slice(p * LANES, (p + 1) * LANES) for p in range(P)]
    nc = T // C
    units = [(c, p) for c in range(nc) for p in range(P)]
    hi_lo = lambda t: (t.astype(bf16), (t - t.astype(bf16).astype(f32)).astype(bf16))

    ch = []
    for c in range(nc):
        sl = pl.ds(c * C, C)
        wl = wl_ref[0, sl, :]
        cum = jnp.dot(tri_incl, wl, preferred_element_type=f32, precision=HIGHEST)
        g_incl = jnp.exp(cum)
        g_inv = jnp.exp(-cum)
        g_last = g_incl[C - 1:C, :]
        r = r_ref[0, sl, :].astype(f32)
        k = k_ref[0, sl, :].astype(f32)
        btf = b_ref[0, sl, :].astype(f32) * g_inv
        ktf = k * g_inv
        ch.append(dict(
            sl=sl, g_last=g_last, v16=v_ref[0, sl, :],
            af=-kk_ref[0, sl, :].astype(f32) * jnp.exp(cum - wl), rf=r * g_incl,
            bt=btf.astype(bf16), kt=ktf.astype(bf16),
            bc=(btf * g_last).astype(bf16), kc=(ktf * g_last).astype(bf16),
            rkr=r * k * rk_ref[...]))

    Ls, Aak, Arow, Vst, X2 = [], [], [], [], []
    for c, p in units:
        d, lp = ch[c], lanes[p]
        afp, rfp = d["af"][:, lp], d["rf"][:, lp]
        X4 = jnp.concatenate([jnp.where(h0, afp, 0.0), jnp.where(h0, 0.0, afp),
                              jnp.where(h0, rfp, 0.0), jnp.where(h0, 0.0, rfp)], axis=0).astype(bf16)
        Mb = _dot_nt(X4, stack2(d["bt"][:, lp]))
        Mk = _dot_nt(X4, stack2(d["kt"][:, lp]))
        Ls.append(jnp.where(strict, Mb[:C2], 0.0))
        Aak.append(jnp.where(strict, Mk[:C2], 0.0).astype(bf16))
        Arow.append(jnp.concatenate([jnp.where(incl, Mb[C2:], 0.0), jnp.where(incl, Mk[C2:], 0.0)],
                                    axis=1).astype(bf16))
        Vst.append(stack2(d["v16"][:, lp]))
        X2.append(jnp.concatenate([afp, rfp], axis=0).astype(bf16))
    Ps = [eye2 + L for L in Ls]
    n = 1
    while 2 * n < C:
        Lb = [L.astype(bf16) for L in Ls]
        Ls = [_dot(x, x) for x in Lb]
        Ps = [Pm + _dot(L.astype(bf16), Pm.astype(bf16)) for L, Pm in zip(Ls, Ps)]
        n *= 2
    Ps = [Pm.astype(bf16) for Pm in Ps]
    AV = [_dot(a, vs) for a, vs in zip(Aak, Vst)]
    ys = []
    state = [state_ref[p] for p in range(P)]
    for c in range(nc):
        d = ch[c]
        us = range(c * P, (c + 1) * P)
        M2 = [_dot_nt(X2[u], state[u - c * P].astype(bf16)) for u in us]
        Ust = [_dot(Ps[u], (AV[u] + stack2(m2[:C])).astype(bf16)) for u, m2 in zip(us, M2)]
        Yst = [_dot(Arow[u], jnp.concatenate([ust.astype(bf16), Vst[u]], axis=0)) + stack2(m2[C:])
               for u, ust, m2 in zip(us, Ust, M2)]
        ys += [jnp.where(h0, yst[:C], yst[C:]) for yst in Yst]
        for p, ust in enumerate(Ust):
            lp = lanes[p]
            u16 = jnp.where(h0, ust[:C], ust[C:]).astype(bf16)
            upd = _dot_tn(jnp.concatenate([u16, d["v16"][:, lp]], axis=0),
                          jnp.concatenate([d["bc"][:, lp], d["kc"][:, lp]], axis=0))
            state[p] = state[p] * d["g_last"][:, lp] + jnp.where(sbd, upd, 0.0)
    for p in range(P):
        state_ref[p] = state[p]
    sums = []
    for u, (c, p) in enumerate(units):
        sums.append(_dot(jnp.concatenate(hi_lo(ys[u]) + hi_lo(ch[c]["rkr"][:, lanes[p]]), axis=0), jbd))
    yds = [ys[u] - (sums[u][:C] + sums[u][C:C2]) * (1.0 / N) for u in range(len(units))]
    var = [_dot(jnp.concatenate(hi_lo(yd * yd), axis=0), jbd) for yd in yds]
    for u, (c, p) in enumerate(units):
        d, lp = ch[c], lanes[p]
        yn = yds[u] * lax.rsqrt((var[u][:C] + var[u][C:]) * (1.0 / N) + GN_EPS)
        bonus = (sums[u][C2:C2 + C] + sums[u][C2 + C:]) * d["v16"][:, lp].astype(f32)
        z = (yn * gng_ref[:, lp] + gnb_ref[:, lp] + bonus) * g_ref[0, d["sl"], lp].astype(f32)
        o_ref[0, d["sl"], lp] = z.astype(bf16)


def _wkv7(r, k, v, wl, kk, b, g, r_k, gn_g, gn_b):
    B, S, D = r.shape
    T = min(WKV_TILE, S)
    C = min(WKV_CHUNK, T)
    tile = pl.BlockSpec((1, T, D), lambda bb, i: (bb, i, 0))
    vec = _full_spec((1, D))
    return pl.pallas_call(
        functools.partial(_wkv_kernel, C),
        name="wkv7",
        grid=(B, S // T),
        in_specs=[tile] * 7 + [vec] * 3,
        out_specs=tile,
        out_shape=jax.ShapeDtypeStruct((B, S, D), bf16),
        scratch_shapes=[pltpu.VMEM((D // LANES, LANES, LANES), f32)],
        compiler_params=_cparams("arbitrary", "arbitrary"),
    )(r, k, v, wl, kk, b, g, r_k.reshape(1, D), gn_g.reshape(1, D), gn_b.reshape(1, D))


def _proj_ln_kernel(gate_idx, z_ref, w_ref, x_ref, mod_ref, lng_ref, lnb_ref, o_ref):
    D = x_ref.shape[-1]
    gt = mod_ref[0, :, gate_idx * D:(gate_idx + 1) * D]
    y = _dot(z_ref[0], w_ref[...])
    u = DEEPNORM_ALPHA * x_ref[0] + (1.0 + gt) * y
    o_ref[0] = _layer_norm_rows(u, lng_ref[...], lnb_ref[...])


def _proj_ln(z, w, x, modl, gate_idx, ln_g, ln_b):
    B, S, D = x.shape
    tm = min(ROW_TILE, S)
    tile = pl.BlockSpec((1, tm, D), lambda b, i: (b, i, 0))
    return pl.pallas_call(
        functools.partial(_proj_ln_kernel, gate_idx),
        name="proj_ln",
        grid=(B, S // tm),
        in_specs=[tile, _full_spec(w.shape), tile,
                  pl.BlockSpec((1, 1, modl.shape[-1]), lambda b, i: (b, 0, 0)),
                  _full_spec((1, D)), _full_spec((1, D))],
        out_specs=tile,
        out_shape=jax.ShapeDtypeStruct((B, S, D), f32),
        compiler_params=_cparams("arbitrary", "arbitrary"),
    )(z, w.astype(bf16), x, modl, ln_g.reshape(1, D), ln_b.reshape(1, D))


def _route_kernel(x_ref, mod_ref, w_ref, b_ref, h_ref, idx_ref, gate_ref, cnt_ref):
    @pl.when((pl.program_id(0) == 0) & (pl.program_id(1) == 0))
    def _():
        cnt_ref[...] = jnp.zeros_like(cnt_ref)

    D = x_ref.shape[-1]
    sh = mod_ref[0, :, 3 * D:4 * D]
    sc = mod_ref[0, :, 4 * D:5 * D]
    h = x_ref[0] * (1.0 + sc) + sh
    h_ref[0] = h.astype(bf16)
    lg = jnp.dot(h, w_ref[...], preferred_element_type=f32, precision=HIGHEST) + b_ref[...]
    lane = lax.broadcasted_iota(jnp.int32, lg.shape, 1)
    glog = jnp.where(lane < N_GROUPS, lg, -jnp.inf)
    gmax = jnp.max(glog, axis=-1, keepdims=True)
    p_g = 1.0 / jnp.sum(jnp.exp(glog - gmax), axis=-1, keepdims=True)
    g_sel = jnp.min(jnp.where(glog == gmax, lane, LANES), axis=-1, keepdims=True)
    e_lo = N_GROUPS + EXP_PER_GROUP * g_sel
    elog = jnp.where((lane >= e_lo) & (lane < e_lo + EXP_PER_GROUP), lg, -jnp.inf)
    e1 = jnp.max(elog, axis=-1, keepdims=True)
    esum = jnp.sum(jnp.exp(elog - e1), axis=-1, keepdims=True)
    i1 = jnp.min(jnp.where(elog == e1, lane, LANES), axis=-1, keepdims=True)
    elog2 = jnp.where(lane == i1, -jnp.inf, elog)
    e2 = jnp.max(elog2, axis=-1, keepdims=True)
    i2 = jnp.min(jnp.where(elog2 == e2, lane, LANES), axis=-1, keepdims=True)
    p1 = 1.0 / esum
    p2 = jnp.exp(e2 - e1) / esum
    psum = p1 + p2
    gate_ref[0] = jnp.where(lane == 0, p_g * p1 / psum, jnp.where(lane == 1, p_g * p2 / psum, 0.0))
    tm = lg.shape[0]
    onehot = jnp.where(lane == i1, 1.0, 0.0) + jnp.where(lane == i2, 1.0, 0.0)
    ri = lax.broadcasted_iota(jnp.int32, (tm, tm), 0)
    ci = lax.broadcasted_iota(jnp.int32, (tm, tm), 1)
    before = _dot(jnp.where(ri > ci, 1.0, 0.0).astype(bf16), onehot.astype(bf16)) + cnt_ref[...]
    rank1 = jnp.sum(jnp.where(lane == i1, before, 0.0), axis=-1, keepdims=True).astype(jnp.int32)
    rank2 = jnp.sum(jnp.where(lane == i2, before, 0.0), axis=-1, keepdims=True).astype(jnp.int32)
    idx_ref[0] = jnp.where(lane == 0, i1 - N_GROUPS, jnp.where(lane == 1, i2 - N_GROUPS,
                           jnp.where(lane == 2, rank1, jnp.where(lane == 3, rank2, 0))))
    cnt_ref[...] = before[tm - 1:tm, :] + onehot[tm - 1:tm, :]


def _route(x, modl, w_grp, b_grp, w_exp, b_exp):
    B, S, D = x.shape
    tm = min(ROW_TILE, S)
    pad = LANES - N_GROUPS - N_EXPERTS
    w = jnp.concatenate([w_grp, w_exp, jnp.zeros((D, pad), f32)], axis=1)
    b = jnp.concatenate([b_grp, b_exp, jnp.zeros((pad,), f32)]).reshape(1, LANES)
    tile = pl.BlockSpec((1, tm, D), lambda bb, i: (bb, i, 0))
    small = pl.BlockSpec((1, tm, LANES), lambda bb, i: (bb, i, 0))
    return pl.pallas_call(
        _route_kernel,
        name="moe_route",
        grid=(B, S // tm),
        in_specs=[tile, pl.BlockSpec((1, 1, modl.shape[-1]), lambda bb, i: (bb, 0, 0)),
                  _full_spec(w.shape), _full_spec(b.shape)],
        out_specs=[tile, small, small, _full_spec((1, LANES))],
        out_shape=[jax.ShapeDtypeStruct((B, S, D), bf16),
                   jax.ShapeDtypeStruct((B, S, LANES), jnp.int32),
                   jax.ShapeDtypeStruct((B, S, LANES), f32),
                   jax.ShapeDtypeStruct((1, LANES), f32)],
        compiler_params=_cparams("arbitrary", "arbitrary"),
    )(x, modl, w, b)


def _expert_kernel(be_ref, nu_ref, x_ref, wg_ref, wu_ref, wd_ref, o_ref):
    i = pl.program_id(0)

    @pl.when(i < nu_ref[0])
    def _():
        x = x_ref[...]
        hg = _dot(x, wg_ref[0])
        hu = _dot(x, wu_ref[0])
        o_ref[...] = _dot((hg * _sigmoid(hg) * hu).astype(bf16), wd_ref[0]).astype(o_ref.dtype)

    @pl.when(i >= nu_ref[0])
    def _():
        o_ref[...] = jnp.zeros_like(o_ref)


def _expert_ffn(xs, block_exp, n_used, wg, wu, wd):
    n_slots, D = xs.shape
    E, _, F = wg.shape
    blk = MOE_BLOCK
    row_map = lambda i, be, nu: (jnp.minimum(i, nu[0] - 1), 0)
    grid_spec = pltpu.PrefetchScalarGridSpec(
        num_scalar_prefetch=2,
        grid=(n_slots // blk,),
        in_specs=[
            pl.BlockSpec((blk, D), row_map),
            pl.BlockSpec((1, D, F), lambda i, be, nu: (be[i], 0, 0)),
            pl.BlockSpec((1, D, F), lambda i, be, nu: (be[i], 0, 0)),
            pl.BlockSpec((1, F, D), lambda i, be, nu: (be[i], 0, 0)),
        ],
        out_specs=pl.BlockSpec((blk, D), lambda i, be, nu: (i, 0)),
    )
    return pl.pallas_call(
        _expert_kernel,
        name="moe_experts",
        grid_spec=grid_spec,
        out_shape=jax.ShapeDtypeStruct((n_slots, D), bf16),
        compiler_params=_cparams("arbitrary"),
    )(block_exp, n_used, xs, wg, wu, wd)


def _take_rows(table, idx):
    return table.at[idx].get(mode="promise_in_bounds")


def _dispatch(expert_idx, rank, counts, blk):
    T = expert_idx.shape[0]
    A = T * TOP_K
    n_blocks = A // blk + N_EXPERTS
    experts = jnp.arange(N_EXPERTS, dtype=jnp.int32)
    padded = (counts + blk - 1) // blk * blk
    pad_end = jnp.cumsum(padded)
    pad_start = pad_end - padded
    start = jnp.cumsum(counts) - counts
    dest = rank + jnp.sum(jnp.where(expert_idx[..., None] == experts, pad_start, 0), axis=-1)
    block_exp = jnp.minimum(jnp.sum(((jnp.arange(n_blocks, dtype=jnp.int32) * blk)[:, None] >= pad_end[None, :])
                                    .astype(jnp.int32), axis=1), N_EXPERTS - 1)
    order = jnp.argsort(expert_idx.reshape(A)).astype(jnp.int32)
    shift = jnp.sum(jnp.where(block_exp[:, None] == experts, start - pad_start, 0), axis=-1)
    pos = jnp.arange(n_blocks * blk, dtype=jnp.int32) + jnp.repeat(shift, blk)
    slot_tok = _take_rows(order, jnp.clip(pos, 0, A - 1)) // TOP_K
    n_used = (pad_end[-1] // blk).astype(jnp.int32).reshape(1)
    return dest, slot_tok, block_exp, n_used


def _combine_ln_kernel(x_ref, y0_ref, y1_ref, gate_ref, mod_ref, lng_ref, lnb_ref, o_ref):
    D = x_ref.shape[-1]
    gt = mod_ref[0, :, 5 * D:6 * D]
    gates = gate_ref[0]
    y = gates[:, 0:1] * y0_ref[0].astype(f32) + gates[:, 1:2] * y1_ref[0].astype(f32)
    u = DEEPNORM_ALPHA * x_ref[0] + (1.0 + gt) * y
    o_ref[0] = _layer_norm_rows(u, lng_ref[...], lnb_ref[...])


def _combine_ln(x, y0, y1, gates, modl, ln_g, ln_b):
    B, S, D = x.shape
    tm = min(ROW_TILE, S)
    tile = pl.BlockSpec((1, tm, D), lambda b, i: (b, i, 0))
    return pl.pallas_call(
        _combine_ln_kernel,
        name="moe_combine_ln",
        grid=(B, S // tm),
        in_specs=[tile, tile, tile, pl.BlockSpec((1, tm, LANES), lambda b, i: (b, i, 0)),
                  pl.BlockSpec((1, 1, modl.shape[-1]), lambda b, i: (b, 0, 0)),
                  _full_spec((1, D)), _full_spec((1, D))],
        out_specs=tile,
        out_shape=jax.ShapeDtypeStruct((B, S, D), f32),
        compiler_params=_cparams("arbitrary", "arbitrary"),
    )(x, y0, y1, gates, modl, ln_g.reshape(1, D), ln_b.reshape(1, D))


def _moe_layer(x, modl, w_grp, b_grp, w_exp, b_exp, wg, wu, wd, ln_g, ln_b):
    B, S, D = x.shape
    T = B * S
    hb, idx, gates, cnt = _route(x, modl, w_grp, b_grp, w_exp, b_exp)
    idx = idx.reshape(T, LANES)
    counts = cnt[0, N_GROUPS:N_GROUPS + N_EXPERTS].astype(jnp.int32)
    dest, slot_tok, block_exp, n_used = _dispatch(idx[:, :TOP_K], idx[:, TOP_K:2 * TOP_K], counts, MOE_BLOCK)
    xs = _take_rows(hb.reshape(T, D), slot_tok)
    ys = _expert_ffn(xs, block_exp, n_used, wg.astype(bf16), wu.astype(bf16), wd.astype(bf16))
    y0 = _take_rows(ys, dest[:, 0]).reshape(B, S, D)
    y1 = _take_rows(ys, dest[:, 1]).reshape(B, S, D)
    return _combine_ln(x, y0, y1, gates, modl, ln_g, ln_b)


def _shared_kv_kernel(x_ref, mod_ref, wk_ref, wvt_ref, wf_ref, fb_ref, kn_ref, place_ref, ones_ref,
                      k_ref, kb_ref, vt_ref, f_ref, carry_ref):
    @pl.when(pl.program_id(1) == 0)
    def _():
        carry_ref[...] = jnp.zeros_like(carry_ref)

    D = x_ref.shape[-1]
    tm = x_ref.shape[1]
    shift = mod_ref[0, :, 0:D]
    scale = mod_ref[0, :, D:2 * D]
    hk = x_ref[0] * (1.0 + scale) + shift
    hb = hk.astype(bf16)
    k = _dot(hb, wk_ref[...])
    vt_ref[0] = _dot_nt(wvt_ref[...], hb).astype(bf16)
    f = jnp.dot(hk, wf_ref[...], preferred_element_type=f32, precision=HIGHEST) + fb_ref[...]
    log_f = jnp.minimum(f, 0.0) - jnp.log(1.0 + jnp.exp(-jnp.abs(f)))
    row = lax.broadcasted_iota(jnp.int32, log_f.shape, 0)
    acc = log_f
    d = 1
    while d < tm:
        acc = acc + jnp.where(row >= d, pltpu.roll(acc, d, axis=0), 0.0)
        d *= 2
    acc = acc + carry_ref[...]
    f_ref[0] = acc
    carry_ref[...] = acc[tm - 1:tm, :]
    kb_ref[0] = (_dot(_bias_split(acc * LOG2E), place_ref[...]) + ones_ref[...]).astype(bf16)
    jbd = _pair_ones()
    for p in range(D // LANES):
        lp = slice(p * LANES, (p + 1) * LANES)
        seg = k[:, lp]
        ms = _head_sum(seg * seg, jbd) * (1.0 / HEAD_DIM)
        k_ref[0, :, lp] = (seg * lax.rsqrt(ms + QK_EPS) * kn_ref[...]).astype(bf16)


def _shared_kv(x, kvmod, w_kvf, b_f, k_norm):
    B, S, D = x.shape
    H = D // HEAD_DIM
    tm = min(ROW_TILE, S)
    wk = w_kvf[:, :D].astype(bf16)
    wvt = w_kvf[:, D:2 * D].T.astype(bf16)
    wf = jnp.concatenate([w_kvf[:, 2 * D:], jnp.zeros((D, LANES - H), f32)], axis=1)
    fb = jnp.concatenate([b_f, jnp.zeros((LANES - H,), f32)]).reshape(1, LANES)
    place, ones = _bias_placement(D, True)
    tile = pl.BlockSpec((1, tm, D), lambda b, i: (b, i, 0))
    act = jax.ShapeDtypeStruct((B, S, D), bf16)
    return pl.pallas_call(
        _shared_kv_kernel,
        name="shared_kv",
        grid=(B, S // tm),
        in_specs=[tile, pl.BlockSpec((1, 1, 2 * D), lambda b, i: (b, 0, 0)),
                  _full_spec(wk.shape), _full_spec(wvt.shape), _full_spec(wf.shape),
                  _full_spec(fb.shape), _full_spec((1, LANES)), _full_spec(place.shape), _full_spec(ones.shape)],
        out_specs=[tile, tile, pl.BlockSpec((1, D, tm), lambda b, i: (b, 0, i)),
                   pl.BlockSpec((1, tm, LANES), lambda b, i: (b, i, 0))],
        out_shape=[act, act, jax.ShapeDtypeStruct((B, D, S), bf16), jax.ShapeDtypeStruct((B, S, LANES), f32)],
        scratch_shapes=[pltpu.VMEM((1, LANES), f32)],
        compiler_params=_cparams("arbitrary", "arbitrary"),
    )(x, kvmod, wk, wvt, wf, fb, jnp.tile(k_norm, LANES // HEAD_DIM).reshape(1, LANES), place, ones)


def _fox_pre_kernel(x_ref, mod_ref, wq_ref, wg_ref, qn_ref, f_ref, place_ref, ones_ref, q_ref, qb_ref, gate_ref):
    D = x_ref.shape[-1]
    sh = mod_ref[0, :, 0:D]
    sc = mod_ref[0, :, D:2 * D]
    hb = (x_ref[0] * (1.0 + sc) + sh).astype(bf16)
    q = _dot(hb, wq_ref[...])
    gate_ref[0] = _sigmoid(_dot(hb, wg_ref[...])).astype(bf16)
    qb_ref[0] = (_dot(_bias_split(f_ref[0] * LOG2E), place_ref[...]) + ones_ref[...]).astype(bf16)
    qscale = HEAD_DIM ** -0.5 * LOG2E
    jbd = _pair_ones()
    for p in range(D // LANES):
        lp = slice(p * LANES, (p + 1) * LANES)
        seg = q[:, lp]
        ms = _head_sum(seg * seg, jbd) * (1.0 / HEAD_DIM)
        q_ref[0, :, lp] = (seg * lax.rsqrt(ms + QK_EPS) * (qn_ref[...] * qscale)).astype(bf16)


def _fox_pre(x, modl, w_qg, q_norm, fcum):
    B, S, D = x.shape
    tm = min(ROW_TILE, S)
    tile = pl.BlockSpec((1, tm, D), lambda b, i: (b, i, 0))
    wq = w_qg[:, :D].astype(bf16)
    wg = w_qg[:, D:].astype(bf16)
    place, ones = _bias_placement(D, False)
    act = jax.ShapeDtypeStruct((B, S, D), bf16)
    return pl.pallas_call(
        _fox_pre_kernel,
        name="fox_pre",
        grid=(B, S // tm),
        in_specs=[tile, pl.BlockSpec((1, 1, modl.shape[-1]), lambda b, i: (b, 0, 0)),
                  _full_spec(wq.shape), _full_spec(wg.shape), _full_spec((1, LANES)),
                  pl.BlockSpec((1, tm, LANES), lambda b, i: (b, i, 0)),
                  _full_spec(place.shape), _full_spec(ones.shape)],
        out_specs=[tile, tile, tile],
        out_shape=[act, act, act],
        compiler_params=_cparams("arbitrary", "arbitrary"),
    )(x, modl, wq, wg, jnp.tile(q_norm, LANES // HEAD_DIM).reshape(1, LANES), fcum, place, ones)


def _fox_attn_kernel(qi_ref, kj_ref, flag_ref, q_ref, qb_ref, k_ref, kb_ref, vt_ref, gate_ref, o_ref,
                     qm_ref, m_ref, l_ref, acc_ref):
    s = pl.program_id(2)
    i = qi_ref[s]
    j = kj_ref[s]
    flags = flag_ref[s]
    N = HEAD_DIM
    nh = LANES // N
    tk, tq = k_ref.shape[1], q_ref.shape[1]

    @pl.when(j == 0)
    def _():
        m_ref[...] = jnp.full_like(m_ref, NEG_BIG)
        l_ref[...] = jnp.zeros_like(l_ref)
        acc_ref[...] = jnp.zeros_like(acc_ref)
        qcat = jnp.concatenate([q_ref[0], qb_ref[0]], axis=1).astype(f32)
        head = (lax.broadcasted_iota(jnp.int32, qcat.shape, 1) % LANES) // N
        for hh in range(nh):
            qm_ref[hh] = jnp.where(head == hh, qcat, 0.0).astype(bf16)

    def step(masked):
        kcat = jnp.concatenate([k_ref[0], kb_ref[0]], axis=1)
        for hh in range(nh):
            st = _dot_nt(kcat, qm_ref[hh])
            if masked:
                key = j * tk + lax.broadcasted_iota(jnp.int32, st.shape, 0)
                qry = i * tq + lax.broadcasted_iota(jnp.int32, st.shape, 1)
                st = jnp.where(key <= qry, st, -jnp.inf)
            m_prev = m_ref[hh]
            m_new = jnp.maximum(m_prev, jnp.max(st, axis=0, keepdims=True))
            alpha = jnp.exp2(m_prev - m_new)
            p = jnp.exp2(st - m_new)
            l_ref[hh] = alpha * l_ref[hh] + jnp.sum(p, axis=0, keepdims=True)
            acc_ref[hh] = alpha * acc_ref[hh] + _dot(vt_ref[0, hh * N:(hh + 1) * N, :], p.astype(bf16))
            m_ref[hh] = m_new

    @pl.when((flags & ATTN_FLAG_MASK) == 0)
    def _():
        step(False)

    @pl.when((flags & ATTN_FLAG_MASK) != 0)
    def _():
        step(True)

    @pl.when((flags & ATTN_FLAG_LAST) != 0)
    def _():
        ot = jnp.concatenate([acc_ref[hh] * (1.0 / l_ref[hh]) for hh in range(nh)], axis=0)
        o_ref[0] = (ot.T * gate_ref[0].astype(f32)).astype(bf16)


def _fox_attn(q, qb, k, kb, vt, gate):
    B, S, D = gate.shape
    tq = min(ATTN_Q_TILE, S)
    tk = min(ATTN_K_TILE, S)
    P = D // LANES
    nh = LANES // HEAD_DIM
    qi, kj, flags = [], [], []
    for i in range(S // tq):
        last = ((i + 1) * tq - 1) // tk
        for j in range(last + 1):
            qi.append(i)
            kj.append(j)
            needs_mask = (j + 1) * tk - 1 > i * tq
            flags.append((ATTN_FLAG_MASK if needs_mask else 0) | (ATTN_FLAG_LAST if j == last else 0))
    tables = [jnp.asarray(t, jnp.int32) for t in (qi, kj, flags)]
    qtile = pl.BlockSpec((1, tq, LANES), lambda b, p, s, qi, kj, fl: (b, qi[s], p))
    ktile = pl.BlockSpec((1, tk, LANES), lambda b, p, s, qi, kj, fl: (b, kj[s], p))
    grid_spec = pltpu.PrefetchScalarGridSpec(
        num_scalar_prefetch=3,
        grid=(B, P, len(qi)),
        in_specs=[qtile, qtile, ktile, ktile,
                  pl.BlockSpec((1, LANES, tk), lambda b, p, s, qi, kj, fl: (b, p, kj[s])),
                  qtile],
        out_specs=qtile,
        scratch_shapes=[pltpu.VMEM((nh, tq, 2 * LANES), bf16),
                        pltpu.VMEM((nh, 1, tq), f32), pltpu.VMEM((nh, 1, tq), f32),
                        pltpu.VMEM((nh, HEAD_DIM, tq), f32)],
    )
    return pl.pallas_call(
        _fox_attn_kernel,
        name="fox_attn",
        grid_spec=grid_spec,
        out_shape=jax.ShapeDtypeStruct((B, S, D), bf16),
        compiler_params=_cparams("arbitrary", "arbitrary", "arbitrary"),
    )(*tables, q, qb, k, kb, vt, gate)


def kernel(x, c, ada_w, ada_b, ln_g, ln_b, rw_mu, rw_rkv, rw_w0, rw_w1, rw_w2, rw_a0, rw_a1, rw_a2, rw_g1, rw_g2, rw_kk, rw_ka, rw_rk, rw_gn_g, rw_gn_b, rw_wo, rw_v0, rw_v1, rw_v2, kv_ada_w, kv_ada_b, kv_w, kv_fb, kv_knorm, fx_wqg, fx_qnorm, fx_wo, moe_wgrp, moe_bgrp, moe_wexp, moe_bexp, moe_wgate, moe_wup, moe_wdown):
    B, S, D = x.shape
    depth = ada_w.shape[0]
    n_a = rw_mu.shape[0]
    mod = _adaln_mod(c, ada_w, ada_b)
    kvmod = _adaln_mod(c, kv_ada_w[None], kv_ada_b[None])[0].reshape(B, 1, 2 * D)
    kv = None
    v_first = None
    for l in range(depth):
        modl = mod[l].reshape(B, 1, 6 * D)
        if l < n_a:
            vmix = None if l == 0 else (rw_v0[l - 1], rw_v1[l - 1], rw_v2[l - 1], v_first)
            r, k, v, wl, kk, b, g = _rwkv_pre(x, modl, rw_mu[l], rw_rkv[l], rw_w0[l], rw_w1[l], rw_w2[l],
                                              rw_a0[l], rw_a1[l], rw_a2[l], rw_g1[l], rw_g2[l],
                                              rw_kk[l], rw_ka[l], vmix)
            if l == 0:
                v_first = v
            z = _wkv7(r, k, v, wl, kk, b, g, rw_rk[l].reshape(D), rw_gn_g[l], rw_gn_b[l])
            w_o = rw_wo[l]
        else:
            j = l - n_a
            k, kb, vt, fcum = kv
            q, qb, gate = _fox_pre(x, modl, fx_wqg[j], fx_qnorm[j], fcum)
            z = _fox_attn(q, qb, k, kb, vt, gate)
            w_o = fx_wo[j]
        x = _proj_ln(z, w_o, x, modl, 2, ln_g[l, 0], ln_b[l, 0])
        x = _moe_layer(x, modl, moe_wgrp[l], moe_bgrp[l], moe_wexp[l], moe_bexp[l],
                       moe_wgate[l], moe_wup[l], moe_wdown[l], ln_g[l, 1], ln_b[l, 1])
        if l == n_a - 1:
            kv = _shared_kv(x, kvmod, kv_w, kv_fb, kv_knorm)
    return x
```

```python
import functools
import math

import jax
import jax.numpy as jnp
import numpy as np
from jax import lax
from jax.experimental import pallas as pl
from jax.experimental.pallas import tpu as pltpu

HEAD_DIM = 64
N_GROUPS = 4
EXP_PER_GROUP = 8
N_EXPERTS = N_GROUPS * EXP_PER_GROUP
TOP_K = 2
DEPTH = 4
N_A = 2
DEEPNORM_ALPHA = (2 * DEPTH) ** 0.25
LN_EPS = 1e-5
GN_EPS = 64e-5
QK_EPS = 1e-6

LANES = 128
VMEM_LIMIT_BYTES = 56 * 1024 * 1024
WKV_CHUNK = 64
WKV_TILE = 128
ROW_TILE = 512
RWKV_PRE_TILE = 512
ATTN_Q_TILE = 1024
ATTN_K_TILE = 512
ATTN_FLAG_MASK = 1
ATTN_FLAG_LAST = 2
ATTN_FLAG_UPPER = 4
MOE_BLOCK = 256
NEG_BIG = -1e30
LOG2E = math.log2(math.e)

f32 = jnp.float32
bf16 = jnp.bfloat16
HIGHEST = lax.Precision.HIGHEST


def _cparams(*sem):
    return pltpu.CompilerParams(dimension_semantics=sem, vmem_limit_bytes=VMEM_LIMIT_BYTES)


def _sigmoid(x):
    return 1.0 / (1.0 + jnp.exp(-x))


def _dot(a, b):
    return jnp.dot(a, b, preferred_element_type=f32)


def _dot_nt(a, b):
    return lax.dot_general(a, b, (((1,), (1,)), ((), ())), preferred_element_type=f32)


def _dot_tn(a, b):
    return lax.dot_general(a, b, (((0,), (0,)), ((), ())), preferred_element_type=f32)


def _full_spec(shape):
    n = len(shape)
    return pl.BlockSpec(shape, lambda *_: (0,) * n)


def _pair_ones():
    rs = lax.broadcasted_iota(jnp.int32, (LANES, LANES), 0)
    cs = lax.broadcasted_iota(jnp.int32, (LANES, LANES), 1)
    return jnp.where(jnp.where(rs >= HEAD_DIM, 1, 0) == jnp.where(cs >= HEAD_DIM, 1, 0), 1.0, 0.0).astype(bf16)


def _head_sum(x, jbd):
    xh = x.astype(bf16)
    xl = (x - xh.astype(f32)).astype(bf16)
    return _dot(xh, jbd) + _dot(xl, jbd)


def _bias_split(f):
    t0 = f.astype(bf16)
    r1 = f - t0.astype(f32)
    t1 = r1.astype(bf16)
    t2 = (r1 - t1.astype(f32)).astype(bf16)
    return jnp.concatenate([t0, t1, t2], axis=1)


def _bias_placement(D, is_key):
    H = D // HEAD_DIM
    place = np.zeros((3 * LANES, D), np.float32)
    ones = np.zeros((1, D), np.float32)
    for h in range(H):
        for i in range(3):
            place[i * LANES + h, h * HEAD_DIM + (3 + i if is_key else i)] = -1.0 if is_key else 1.0
            ones[0, h * HEAD_DIM + (i if is_key else 3 + i)] = 1.0
    return jnp.asarray(place, bf16), jnp.asarray(ones, f32)


def _layer_norm_rows(u, g, b):
    mu = jnp.mean(u, axis=-1, keepdims=True)
    d = u - mu
    var = jnp.mean(d * d, axis=-1, keepdims=True)
    return d * lax.rsqrt(var + LN_EPS) * g + b


def _mod_kernel(c_ref, w_ref, b_ref, o_ref):
    c = c_ref[...]
    cs = c * _sigmoid(c)
    o_ref[0] = jnp.dot(cs, w_ref[0], preferred_element_type=f32, precision=HIGHEST) + b_ref[0]


def _adaln_mod(c, w, b):
    L, D, N = w.shape
    B = c.shape[0]
    tn = min(N, 1024)
    return pl.pallas_call(
        _mod_kernel,
        name="adaln_mod",
        grid=(L, N // tn),
        in_specs=[
            _full_spec((B, D)),
            pl.BlockSpec((1, D, tn), lambda l, j: (l, 0, j)),
            pl.BlockSpec((1, 1, tn), lambda l, j: (l, 0, j)),
        ],
        out_specs=pl.BlockSpec((1, B, tn), lambda l, j: (l, 0, j)),
        out_shape=jax.ShapeDtypeStruct((L, B, N), f32),
        compiler_params=_cparams("arbitrary", "arbitrary"),
    )(c, w, b.reshape(L, 1, N))


def _rwkv_pre_kernel(has_vmix, x_ref, xp_ref, mod_ref, mu_ref, wrkv_ref, w0_ref, w1_ref, w2_ref,
                     a0_ref, a1_ref, a2_ref, g1_ref, g2_ref, kkp_ref, kap_ref, *rest):
    if has_vmix:
        v0_ref, v1_ref, v2_ref, vf_ref = rest[:4]
        rest = rest[4:]
    r_ref, k_ref, v_ref, wl_ref, kk_ref, b_ref, g_ref = rest
    D = x_ref.shape[-1]
    sh = mod_ref[0, :, 0:D]
    sc = mod_ref[0, :, D:2 * D]
    h = x_ref[0] * (1.0 + sc) + sh
    hp = xp_ref[0][7:8, :] * (1.0 + sc) + sh
    hp = jnp.where(pl.program_id(1) == 0, 0.0, hp)
    row = lax.broadcasted_iota(jnp.int32, h.shape, 0)
    h_prev = jnp.where(row == 0, hp, pltpu.roll(h, 1, axis=0))
    xx = h_prev - h
    mix = lambda j: (h + xx * mu_ref[j:j + 1, :]).astype(bf16)
    xr, xw, xk, xv, xa, xg = [mix(j) for j in range(6)]
    r = _dot(xr, wrkv_ref[0])
    k = _dot(xk, wrkv_ref[1])
    v = _dot(xv, wrkv_ref[2])
    zw = w0_ref[...] + _dot(jnp.tanh(_dot(xw, w1_ref[...])).astype(bf16), w2_ref[...])
    wl_ref[0] = -math.exp(-0.5) * _sigmoid(zw)
    a = _sigmoid(a0_ref[...] + _dot(_dot(xa, a1_ref[...]).astype(bf16), a2_ref[...]))
    g = _dot(_sigmoid(_dot(xg, g1_ref[...])).astype(bf16), g2_ref[...])
    if has_vmix:
        vmix = _sigmoid(v0_ref[...] + _dot(_dot(xv, v1_ref[...]).astype(bf16), v2_ref[...]))
        v = v + (vf_ref[0].astype(f32) - v) * vmix
    r_ref[0] = r.astype(bf16)
    v_ref[0] = v.astype(bf16)
    g_ref[0] = g.astype(bf16)
    k_ref[0] = (k * (1.0 + (a - 1.0) * kap_ref[...])).astype(bf16)
    kk = k * kkp_ref[...]
    jbd = _pair_ones()
    for p in range(D // LANES):
        lp = slice(p * LANES, (p + 1) * LANES)
        seg = kk[:, lp]
        nrm = jnp.sqrt(_head_sum(seg * seg, jbd))
        seg = seg / jnp.maximum(nrm, 1e-12)
        kk_ref[0, :, lp] = seg.astype(bf16)
        b_ref[0, :, lp] = (seg * a[:, lp]).astype(bf16)


def _rwkv_pre(x, modl, mu, wrkv, w0, w1, w2, a0, a1, a2, g1, g2, kkp, kap, vmix):
    B, S, D = x.shape
    tm = min(RWKV_PRE_TILE, S)
    row = lambda t: t.reshape(1, D)
    args = [x, x, modl, mu, wrkv.astype(bf16), row(w0), w1.astype(bf16), w2.astype(bf16), row(a0),
            a1.astype(bf16), a2.astype(bf16), g1.astype(bf16), g2.astype(bf16), row(kkp), row(kap)]
    tile = pl.BlockSpec((1, tm, D), lambda b, i: (b, i, 0))
    in_specs = [
        tile,
        pl.BlockSpec((1, 8, D), lambda b, i: (b, jnp.maximum(i * (tm // 8) - 1, 0), 0)),
        pl.BlockSpec((1, 1, modl.shape[-1]), lambda b, i: (b, 0, 0)),
    ] + [_full_spec(a.shape) for a in args[3:]]
    if vmix is not None:
        v0, v1, v2, v_first = vmix
        extra = [row(v0), v1.astype(bf16), v2.astype(bf16)]
        args += extra + [v_first]
        in_specs += [_full_spec(a.shape) for a in extra] + [tile]
    out_bf = jax.ShapeDtypeStruct((B, S, D), bf16)
    out_shape = [out_bf, out_bf, out_bf, jax.ShapeDtypeStruct((B, S, D), f32), out_bf, out_bf, out_bf]
    return pl.pallas_call(
        functools.partial(_rwkv_pre_kernel, vmix is not None),
        name="rwkv_pre",
        grid=(B, S // tm),
        in_specs=in_specs,
        out_specs=[tile] * 7,
        out_shape=out_shape,
        compiler_params=_cparams("arbitrary", "arbitrary"),
    )(*args)


def _wkv_kernel(C, r_ref, k_ref, v_ref, wl_ref, kk_ref, b_ref, g_ref, rk_ref, gng_ref, gnb_ref,
                o_ref, state_ref):
    @pl.when(pl.program_id(1) == 0)
    def _():
        state_ref[...] = jnp.zeros_like(state_ref)

    T, D = r_ref.shape[1], r_ref.shape[2]
    N = HEAD_DIM
    P = D // LANES
    C2 = 2 * C
    ri = lax.broadcasted_iota(jnp.int32, (C, C), 0)
    ci = lax.broadcasted_iota(jnp.int32, (C, C), 1)
    tri_incl = jnp.where(ri >= ci, 1.0, 0.0).astype(f32)
    r2 = lax.broadcasted_iota(jnp.int32, (C2, C2), 0)
    c2 = lax.broadcasted_iota(jnp.int32, (C2, C2), 1)
    dlt = jnp.where(jnp.where(r2 >= C, 1, 0) == jnp.where(c2 >= C, 1, 0), r2 - c2, -1)
    strict = dlt > 0
    incl = dlt >= 0
    eye2 = jnp.where(r2 == c2, 1.0, 0.0).astype(f32)
    h0 = lax.broadcasted_iota(jnp.int32, (C, LANES), 1) < N
    rs = lax.broadcasted_iota(jnp.int32, (LANES, LANES), 0)
    cs = lax.broadcasted_iota(jnp.int32, (LANES, LANES), 1)
    sbd = jnp.where(rs >= N, 1, 0) == jnp.where(cs >= N, 1, 0)
    jbd = jnp.where(sbd, 1.0, 0.0).astype(bf16)
    stack2 = lambda t: jnp.concatenate([t, t], axis=0)

    lanes = [slice(p * LANES, (p + 1) * LANES) for p in range(P)]
    nc = T // C
    units = [(c, p) for c in range(nc) for p in range(P)]
    hi_lo = lambda t: (t.astype(bf16), (t - t.astype(bf16).astype(f32)).astype(bf16))

    ch = []
    for c in range(nc):
        sl = pl.ds(c * C, C)
        wl = wl_ref[0, sl, :]
        cum = jnp.dot(tri_incl, wl, preferred_element_type=f32, precision=HIGHEST)
        g_incl = jnp.exp(cum)
        g_inv = jnp.exp(-cum)
        g_last = g_incl[C - 1:C, :]
        r = r_ref[0, sl, :].astype(f32)
        k = k_ref[0, sl, :].astype(f32)
        btf = b_ref[0, sl, :].astype(f32) * g_inv
        ktf = k * g_inv
        ch.append(dict(
            sl=sl, g_last=g_last, v16=v_ref[0, sl, :],
            af=-kk_ref[0, sl, :].astype(f32) * jnp.exp(cum - wl), rf=r * g_incl,
            bt=btf.astype(bf16), kt=ktf.astype(bf16),
            bc=(btf * g_last).astype(bf16), kc=(ktf * g_last).astype(bf16),
            rkr=r * k * rk_ref[...]))

    Ls, Aak, Arow, Vst, X2 = [], [], [], [], []
    for c, p in units:
        d, lp = ch[c], lanes[p]
        afp, rfp = d["af"][:, lp], d["rf"][:, lp]
        X4 = jnp.concatenate([jnp.where(h0, afp, 0.0), jnp.where(h0, 0.0, afp),
                              jnp.where(h0, rfp, 0.0), jnp.where(h0, 0.0, rfp)], axis=0).astype(bf16)
        Mb = _dot_nt(X4, stack2(d["bt"][:, lp]))
        Mk = _dot_nt(X4, stack2(d["kt"][:, lp]))
        Ls.append(jnp.where(strict, Mb[:C2], 0.0))
        Aak.append(jnp.where(strict, Mk[:C2], 0.0).astype(bf16))
        Arow.append(jnp.concatenate([jnp.where(incl, Mb[C2:], 0.0), jnp.where(incl, Mk[C2:], 0.0)],
                                    axis=1).astype(bf16))
        Vst.append(stack2(d["v16"][:, lp]))
        X2.append(jnp.concatenate([afp, rfp], axis=0).astype(bf16))
    Ps = [eye2 + L for L in Ls]
    n = 1
    while 2 * n < C:
        Lb = [L.astype(bf16) for L in Ls]
        Ls = [_dot(x, x) for x in Lb]
        Ps = [Pm + _dot(L.astype(bf16), Pm.astype(bf16)) for L, Pm in zip(Ls, Ps)]
        n *= 2
    Ps = [Pm.astype(bf16) for Pm in Ps]
    AV = [_dot(a, vs) for a, vs in zip(Aak, Vst)]
    ys = []
    state = [state_ref[p] for p in range(P)]
    for c in range(nc):
        d = ch[c]
        us = range(c * P, (c + 1) * P)
        M2 = [_dot_nt(X2[u], state[u - c * P].astype(bf16)) for u in us]
        Ust = [_dot(Ps[u], (AV[u] + stack2(m2[:C])).astype(bf16)) for u, m2 in zip(us, M2)]
        Yst = [_dot(Arow[u], jnp.concatenate([ust.astype(bf16), Vst[u]], axis=0)) + stack2(m2[C:])
               for u, ust, m2 in zip(us, Ust, M2)]
        ys += [jnp.where(h0, yst[:C], yst[C:]) for yst in Yst]
        for p, ust in enumerate(Ust):
            lp = lanes[p]
            u16 = jnp.where(h0, ust[:C], ust[C:]).astype(bf16)
            upd = _dot_tn(jnp.concatenate([u16, d["v16"][:, lp]], axis=0),
                          jnp.concatenate([d["bc"][:, lp], d["kc"][:, lp]], axis=0))
            state[p] = state[p] * d["g_last"][:, lp] + jnp.where(sbd, upd, 0.0)
    for p in range(P):
        state_ref[p] = state[p]
    sums = []
    for u, (c, p) in enumerate(units):
        sums.append(_dot(jnp.concatenate(hi_lo(ys[u]) + hi_lo(ch[c]["rkr"][:, lanes[p]]), axis=0), jbd))
    yds = [ys[u] - (sums[u][:C] + sums[u][C:C2]) * (1.0 / N) for u in range(len(units))]
    var = [_dot(jnp.concatenate(hi_lo(yd * yd), axis=0), jbd) for yd in yds]
    for u, (c, p) in enumerate(units):
        d, lp = ch[c], lanes[p]
        yn = yds[u] * lax.rsqrt((var[u][:C] + var[u][C:]) * (1.0 / N) + GN_EPS)
        bonus = (sums[u][C2:C2 + C] + sums[u][C2 + C:]) * d["v16"][:, lp].astype(f32)
        z = (yn * gng_ref[:, lp] + gnb_ref[:, lp] + bonus) * g_ref[0, d["sl"], lp].astype(f32)
        o_ref[0, d["sl"], lp] = z.astype(bf16)


def _wkv7(r, k, v, wl, kk, b, g, r_k, gn_g, gn_b):
    B, S, D = r.shape
    T = min(WKV_TILE, S)
    C = min(WKV_CHUNK, T)
    tile = pl.BlockSpec((1, T, D), lambda bb, i: (bb, i, 0))
    vec = _full_spec((1, D))
    return pl.pallas_call(
        functools.partial(_wkv_kernel, C),
        name="wkv7",
        grid=(B, S // T),
        in_specs=[tile] * 7 + [vec] * 3,
        out_specs=tile,
        out_shape=jax.ShapeDtypeStruct((B, S, D), bf16),
        scratch_shapes=[pltpu.VMEM((D // LANES, LANES, LANES), f32)],
        compiler_params=_cparams("arbitrary", "arbitrary"),
    )(r, k, v, wl, kk, b, g, r_k.reshape(1, D), gn_g.reshape(1, D), gn_b.reshape(1, D))


def _proj_ln_kernel(gate_idx, z_ref, w_ref, x_ref, mod_ref, lng_ref, lnb_ref, o_ref):
    D = x_ref.shape[-1]
    gt = mod_ref[0, :, gate_idx * D:(gate_idx + 1) * D]
    y = _dot(z_ref[0], w_ref[...])
    u = DEEPNORM_ALPHA * x_ref[0] + (1.0 + gt) * y
    o_ref[0] = _layer_norm_rows(u, lng_ref[...], lnb_ref[...])


def _proj_ln(z, w, x, modl, gate_idx, ln_g, ln_b):
    B, S, D = x.shape
    tm = min(ROW_TILE, S)
    tile = pl.BlockSpec((1, tm, D), lambda b, i: (b, i, 0))
    return pl.pallas_call(
        functools.partial(_proj_ln_kernel, gate_idx),
        name="proj_ln",
        grid=(B, S // tm),
        in_specs=[tile, _full_spec(w.shape), tile,
                  pl.BlockSpec((1, 1, modl.shape[-1]), lambda b, i: (b, 0, 0)),
                  _full_spec((1, D)), _full_spec((1, D))],
        out_specs=tile,
        out_shape=jax.ShapeDtypeStruct((B, S, D), f32),
        compiler_params=_cparams("arbitrary", "arbitrary"),
    )(z, w.astype(bf16), x, modl, ln_g.reshape(1, D), ln_b.reshape(1, D))


def _route_kernel(x_ref, mod_ref, w_ref, b_ref, h_ref, idx_ref, gate_ref, cnt_ref):
    @pl.when((pl.program_id(0) == 0) & (pl.program_id(1) == 0))
    def _():
        cnt_ref[...] = jnp.zeros_like(cnt_ref)

    D = x_ref.shape[-1]
    sh = mod_ref[0, :, 3 * D:4 * D]
    sc = mod_ref[0, :, 4 * D:5 * D]
    h = x_ref[0] * (1.0 + sc) + sh
    h_ref[0] = h.astype(bf16)
    lg = jnp.dot(h, w_ref[...], preferred_element_type=f32, precision=HIGHEST) + b_ref[...]
    lane = lax.broadcasted_iota(jnp.int32, lg.shape, 1)
    glog = jnp.where(lane < N_GROUPS, lg, -jnp.inf)
    gmax = jnp.max(glog, axis=-1, keepdims=True)
    p_g = 1.0 / jnp.sum(jnp.exp(glog - gmax), axis=-1, keepdims=True)
    g_sel = jnp.min(jnp.where(glog == gmax, lane, LANES), axis=-1, keepdims=True)
    e_lo = N_GROUPS + EXP_PER_GROUP * g_sel
    elog = jnp.where((lane >= e_lo) & (lane < e_lo + EXP_PER_GROUP), lg, -jnp.inf)
    e1 = jnp.max(elog, axis=-1, keepdims=True)
    esum = jnp.sum(jnp.exp(elog - e1), axis=-1, keepdims=True)
    i1 = jnp.min(jnp.where(elog == e1, lane, LANES), axis=-1, keepdims=True)
    elog2 = jnp.where(lane == i1, -jnp.inf, elog)
    e2 = jnp.max(elog2, axis=-1, keepdims=True)
    i2 = jnp.min(jnp.where(elog2 == e2, lane, LANES), axis=-1, keepdims=True)
    p1 = 1.0 / esum
    p2 = jnp.exp(e2 - e1) / esum
    psum = p1 + p2
    gate_ref[0] = jnp.where(lane == 0, p_g * p1 / psum, jnp.where(lane == 1, p_g * p2 / psum, 0.0))
    tm = lg.shape[0]
    onehot = jnp.where(lane == i1, 1.0, 0.0) + jnp.where(lane == i2, 1.0, 0.0)
    ri = lax.broadcasted_iota(jnp.int32, (tm, tm), 0)
    ci = lax.broadcasted_iota(jnp.int32, (tm, tm), 1)
    before = _dot(jnp.where(ri > ci, 1.0, 0.0).astype(bf16), onehot.astype(bf16)) + cnt_ref[...]
    rank1 = jnp.sum(jnp.where(lane == i1, before, 0.0), axis=-1, keepdims=True).astype(jnp.int32)
    rank2 = jnp.sum(jnp.where(lane == i2, before, 0.0), axis=-1, keepdims=True).astype(jnp.int32)
    idx_ref[0] = jnp.where(lane == 0, i1 - N_GROUPS, jnp.where(lane == 1, i2 - N_GROUPS,
                           jnp.where(lane == 2, rank1, jnp.where(lane == 3, rank2, 0))))
    cnt_ref[...] = before[tm - 1:tm, :] + onehot[tm - 1:tm, :]


def _route(x, modl, w_grp, b_grp, w_exp, b_exp):
    B, S, D = x.shape
    tm = min(ROW_TILE, S)
    pad = LANES - N_GROUPS - N_EXPERTS
    w = jnp.concatenate([w_grp, w_exp, jnp.zeros((D, pad), f32)], axis=1)
    b = jnp.concatenate([b_grp, b_exp, jnp.zeros((pad,), f32)]).reshape(1, LANES)
    tile = pl.BlockSpec((1, tm, D), lambda bb, i: (bb, i, 0))
    small = pl.BlockSpec((1, tm, LANES), lambda bb, i: (bb, i, 0))
    return pl.pallas_call(
        _route_kernel,
        name="moe_route",
        grid=(B, S // tm),
        in_specs=[tile, pl.BlockSpec((1, 1, modl.shape[-1]), lambda bb, i: (bb, 0, 0)),
                  _full_spec(w.shape), _full_spec(b.shape)],
        out_specs=[tile, small, small, _full_spec((1, LANES))],
        out_shape=[jax.ShapeDtypeStruct((B, S, D), bf16),
                   jax.ShapeDtypeStruct((B, S, LANES), jnp.int32),
                   jax.ShapeDtypeStruct((B, S, LANES), f32),
                   jax.ShapeDtypeStruct((1, LANES), f32)],
        compiler_params=_cparams("arbitrary", "arbitrary"),
    )(x, modl, w, b)


def _expert_kernel(be_ref, nu_ref, x_ref, wg_ref, wu_ref, wd_ref, o_ref, wg16, wu16, wd16):
    i = pl.program_id(0)

    @pl.when((i == 0) | (be_ref[i] != be_ref[jnp.maximum(i - 1, 0)]))
    def _():
        wg16[...] = wg_ref[0].astype(bf16)
        wu16[...] = wu_ref[0].astype(bf16)
        wd16[...] = wd_ref[0].astype(bf16)

    @pl.when(i < nu_ref[0])
    def _():
        x = x_ref[...]
        hg = _dot(x, wg16[...])
        hu = _dot(x, wu16[...])
        o_ref[...] = _dot((hg * _sigmoid(hg) * hu).astype(bf16), wd16[...]).astype(o_ref.dtype)

    @pl.when(i >= nu_ref[0])
    def _():
        o_ref[...] = jnp.zeros_like(o_ref)


def _expert_ffn(xs, block_exp, n_used, wg, wu, wd):
    n_slots, D = xs.shape
    E, _, F = wg.shape
    blk = MOE_BLOCK
    row_map = lambda i, be, nu: (jnp.minimum(i, nu[0] - 1), 0)
    grid_spec = pltpu.PrefetchScalarGridSpec(
        num_scalar_prefetch=2,
        grid=(n_slots // blk,),
        in_specs=[
            pl.BlockSpec((blk, D), row_map),
            pl.BlockSpec((1, D, F), lambda i, be, nu: (be[i], 0, 0)),
            pl.BlockSpec((1, D, F), lambda i, be, nu: (be[i], 0, 0)),
            pl.BlockSpec((1, F, D), lambda i, be, nu: (be[i], 0, 0)),
        ],
        out_specs=pl.BlockSpec((blk, D), lambda i, be, nu: (i, 0)),
        scratch_shapes=[pltpu.VMEM((D, F), bf16), pltpu.VMEM((D, F), bf16), pltpu.VMEM((F, D), bf16)],
    )
    return pl.pallas_call(
        _expert_kernel,
        name="moe_experts",
        grid_spec=grid_spec,
        out_shape=jax.ShapeDtypeStruct((n_slots, D), bf16),
        compiler_params=_cparams("arbitrary"),
    )(block_exp, n_used, xs, wg, wu, wd)


def _take_rows(table, idx):
    return table.at[idx].get(mode="promise_in_bounds")


def _dispatch(expert_idx, rank, counts, blk):
    T = expert_idx.shape[0]
    A = T * TOP_K
    n_blocks = A // blk + N_EXPERTS
    experts = jnp.arange(N_EXPERTS, dtype=jnp.int32)
    padded = (counts + blk - 1) // blk * blk
    pad_end = jnp.cumsum(padded)
    pad_start = pad_end - padded
    start = jnp.cumsum(counts) - counts
    dest = rank + jnp.sum(jnp.where(expert_idx[..., None] == experts, pad_start, 0), axis=-1)
    block_exp = jnp.minimum(jnp.sum(((jnp.arange(n_blocks, dtype=jnp.int32) * blk)[:, None] >= pad_end[None, :])
                                    .astype(jnp.int32), axis=1), N_EXPERTS - 1)
    order = jnp.argsort(expert_idx.reshape(A)).astype(jnp.int32)
    shift = jnp.sum(jnp.where(block_exp[:, None] == experts, start - pad_start, 0), axis=-1)
    pos = jnp.arange(n_blocks * blk, dtype=jnp.int32) + jnp.repeat(shift, blk)
    slot_tok = _take_rows(order, jnp.clip(pos, 0, A - 1)) // TOP_K
    n_used = (pad_end[-1] // blk).astype(jnp.int32).reshape(1)
    return dest, slot_tok, block_exp, n_used


def _combine_ln_kernel(x_ref, y0_ref, y1_ref, gate_ref, mod_ref, lng_ref, lnb_ref, o_ref):
    D = x_ref.shape[-1]
    gt = mod_ref[0, :, 5 * D:6 * D]
    gates = gate_ref[0]
    y = gates[:, 0:1] * y0_ref[0].astype(f32) + gates[:, 1:2] * y1_ref[0].astype(f32)
    u = DEEPNORM_ALPHA * x_ref[0] + (1.0 + gt) * y
    o_ref[0] = _layer_norm_rows(u, lng_ref[...], lnb_ref[...])


def _combine_ln(x, y0, y1, gates, modl, ln_g, ln_b):
    B, S, D = x.shape
    tm = min(ROW_TILE, S)
    tile = pl.BlockSpec((1, tm, D), lambda b, i: (b, i, 0))
    return pl.pallas_call(
        _combine_ln_kernel,
        name="moe_combine_ln",
        grid=(B, S // tm),
        in_specs=[tile, tile, tile, pl.BlockSpec((1, tm, LANES), lambda b, i: (b, i, 0)),
                  pl.BlockSpec((1, 1, modl.shape[-1]), lambda b, i: (b, 0, 0)),
                  _full_spec((1, D)), _full_spec((1, D))],
        out_specs=tile,
        out_shape=jax.ShapeDtypeStruct((B, S, D), f32),
        compiler_params=_cparams("arbitrary", "arbitrary"),
    )(x, y0, y1, gates, modl, ln_g.reshape(1, D), ln_b.reshape(1, D))


def _moe_layer(x, modl, w_grp, b_grp, w_exp, b_exp, wg, wu, wd, ln_g, ln_b):
    B, S, D = x.shape
    T = B * S
    hb, idx, gates, cnt = _route(x, modl, w_grp, b_grp, w_exp, b_exp)
    idx = idx.reshape(T, LANES)
    counts = cnt[0, N_GROUPS:N_GROUPS + N_EXPERTS].astype(jnp.int32)
    dest, slot_tok, block_exp, n_used = _dispatch(idx[:, :TOP_K], idx[:, TOP_K:2 * TOP_K], counts, MOE_BLOCK)
    xs = _take_rows(hb.reshape(T, D), slot_tok)
    ys = _expert_ffn(xs, block_exp, n_used, wg, wu, wd)
    y0 = _take_rows(ys, dest[:, 0]).reshape(B, S, D)
    y1 = _take_rows(ys, dest[:, 1]).reshape(B, S, D)
    return _combine_ln(x, y0, y1, gates, modl, ln_g, ln_b)


def _shared_kv_kernel(x_ref, mod_ref, wk_ref, wvt_ref, wf_ref, fb_ref, kn_ref, place_ref, ones_ref,
                      k_ref, kb_ref, vt_ref, f_ref, carry_ref):
    @pl.when(pl.program_id(1) == 0)
    def _():
        carry_ref[...] = jnp.zeros_like(carry_ref)

    D = x_ref.shape[-1]
    tm = x_ref.shape[1]
    shift = mod_ref[0, :, 0:D]
    scale = mod_ref[0, :, D:2 * D]
    hk = x_ref[0] * (1.0 + scale) + shift
    hb = hk.astype(bf16)
    k = _dot(hb, wk_ref[...])
    vt_ref[0] = _dot_nt(wvt_ref[...], hb).astype(bf16)
    f = jnp.dot(hk, wf_ref[...], preferred_element_type=f32, precision=HIGHEST) + fb_ref[...]
    log_f = jnp.minimum(f, 0.0) - jnp.log(1.0 + jnp.exp(-jnp.abs(f)))
    row = lax.broadcasted_iota(jnp.int32, log_f.shape, 0)
    acc = log_f
    d = 1
    while d < tm:
        acc = acc + jnp.where(row >= d, pltpu.roll(acc, d, axis=0), 0.0)
        d *= 2
    acc = acc + carry_ref[...]
    f_ref[0] = acc
    carry_ref[...] = acc[tm - 1:tm, :]
    kb_ref[0] = (_dot(_bias_split(acc * LOG2E), place_ref[...]) + ones_ref[...]).astype(bf16)
    jbd = _pair_ones()
    for p in range(D // LANES):
        lp = slice(p * LANES, (p + 1) * LANES)
        seg = k[:, lp]
        ms = _head_sum(seg * seg, jbd) * (1.0 / HEAD_DIM)
        k_ref[0, :, lp] = (seg * lax.rsqrt(ms + QK_EPS) * kn_ref[...]).astype(bf16)


def _shared_kv(x, kvmod, w_kvf, b_f, k_norm):
    B, S, D = x.shape
    H = D // HEAD_DIM
    tm = min(ROW_TILE, S)
    wk = w_kvf[:, :D].astype(bf16)
    wvt = w_kvf[:, D:2 * D].T.astype(bf16)
    wf = jnp.concatenate([w_kvf[:, 2 * D:], jnp.zeros((D, LANES - H), f32)], axis=1)
    fb = jnp.concatenate([b_f, jnp.zeros((LANES - H,), f32)]).reshape(1, LANES)
    place, ones = _bias_placement(D, True)
    tile = pl.BlockSpec((1, tm, D), lambda b, i: (b, i, 0))
    act = jax.ShapeDtypeStruct((B, S, D), bf16)
    return pl.pallas_call(
        _shared_kv_kernel,
        name="shared_kv",
        grid=(B, S // tm),
        in_specs=[tile, pl.BlockSpec((1, 1, 2 * D), lambda b, i: (b, 0, 0)),
                  _full_spec(wk.shape), _full_spec(wvt.shape), _full_spec(wf.shape),
                  _full_spec(fb.shape), _full_spec((1, LANES)), _full_spec(place.shape), _full_spec(ones.shape)],
        out_specs=[tile, tile, pl.BlockSpec((1, D, tm), lambda b, i: (b, 0, i)),
                   pl.BlockSpec((1, tm, LANES), lambda b, i: (b, i, 0))],
        out_shape=[act, act, jax.ShapeDtypeStruct((B, D, S), bf16), jax.ShapeDtypeStruct((B, S, LANES), f32)],
        scratch_shapes=[pltpu.VMEM((1, LANES), f32)],
        compiler_params=_cparams("arbitrary", "arbitrary"),
    )(x, kvmod, wk, wvt, wf, fb, jnp.tile(k_norm, LANES // HEAD_DIM).reshape(1, LANES), place, ones)


def _fox_pre_kernel(x_ref, mod_ref, wq_ref, wg_ref, qn_ref, f_ref, place_ref, ones_ref, q_ref, qb_ref, gate_ref):
    D = x_ref.shape[-1]
    sh = mod_ref[0, :, 0:D]
    sc = mod_ref[0, :, D:2 * D]
    hb = (x_ref[0] * (1.0 + sc) + sh).astype(bf16)
    q = _dot(hb, wq_ref[...])
    gate_ref[0] = _sigmoid(_dot(hb, wg_ref[...])).astype(bf16)
    qb_ref[0] = (_dot(_bias_split(f_ref[0] * LOG2E), place_ref[...]) + ones_ref[...]).astype(bf16)
    qscale = HEAD_DIM ** -0.5 * LOG2E
    jbd = _pair_ones()
    for p in range(D // LANES):
        lp = slice(p * LANES, (p + 1) * LANES)
        seg = q[:, lp]
        ms = _head_sum(seg * seg, jbd) * (1.0 / HEAD_DIM)
        q_ref[0, :, lp] = (seg * lax.rsqrt(ms + QK_EPS) * (qn_ref[...] * qscale)).astype(bf16)


def _fox_pre(x, modl, w_qg, q_norm, fcum):
    B, S, D = x.shape
    tm = min(ROW_TILE, S)
    tile = pl.BlockSpec((1, tm, D), lambda b, i: (b, i, 0))
    wq = w_qg[:, :D].astype(bf16)
    wg = w_qg[:, D:].astype(bf16)
    place, ones = _bias_placement(D, False)
    act = jax.ShapeDtypeStruct((B, S, D), bf16)
    return pl.pallas_call(
        _fox_pre_kernel,
        name="fox_pre",
        grid=(B, S // tm),
        in_specs=[tile, pl.BlockSpec((1, 1, modl.shape[-1]), lambda b, i: (b, 0, 0)),
                  _full_spec(wq.shape), _full_spec(wg.shape), _full_spec((1, LANES)),
                  pl.BlockSpec((1, tm, LANES), lambda b, i: (b, i, 0)),
                  _full_spec(place.shape), _full_spec(ones.shape)],
        out_specs=[tile, tile, tile],
        out_shape=[act, act, act],
        compiler_params=_cparams("arbitrary", "arbitrary"),
    )(x, modl, wq, wg, jnp.tile(q_norm, LANES // HEAD_DIM).reshape(1, LANES), fcum, place, ones)


def _fox_attn_kernel(qi_ref, kj_ref, flag_ref, q_ref, qb_ref, k_ref, kb_ref, vt_ref, gate_ref, o_ref,
                     qm_ref, m_ref, l_ref, acc_ref):
    s = pl.program_id(2)
    i = qi_ref[s]
    j = kj_ref[s]
    flags = flag_ref[s]
    N = HEAD_DIM
    nh = LANES // N
    tk, tq = k_ref.shape[1], q_ref.shape[1]

    @pl.when(j == 0)
    def _():
        m_ref[...] = jnp.full_like(m_ref, NEG_BIG)
        l_ref[...] = jnp.zeros_like(l_ref)
        acc_ref[...] = jnp.zeros_like(acc_ref)
        qcat = jnp.concatenate([q_ref[0], qb_ref[0]], axis=1).astype(f32)
        head = (lax.broadcasted_iota(jnp.int32, qcat.shape, 1) % LANES) // N
        for hh in range(nh):
            qm_ref[hh] = jnp.where(head == hh, qcat, 0.0).astype(bf16)

    def step(masked, q_lo):
        kcat = jnp.concatenate([k_ref[0], kb_ref[0]], axis=1)
        qs = slice(q_lo, tq)
        for hh in range(nh):
            st = _dot_nt(kcat, qm_ref[hh, qs, :])
            if masked:
                key = j * tk + lax.broadcasted_iota(jnp.int32, st.shape, 0)
                qry = i * tq + q_lo + lax.broadcasted_iota(jnp.int32, st.shape, 1)
                st = jnp.where(key <= qry, st, -jnp.inf)
            m_prev = m_ref[hh, :, qs]
            m_new = jnp.maximum(m_prev, jnp.max(st, axis=0, keepdims=True))
            alpha = jnp.exp2(m_prev - m_new)
            p = jnp.exp2(st - m_new)
            l_ref[hh, :, qs] = alpha * l_ref[hh, :, qs] + jnp.sum(p, axis=0, keepdims=True)
            acc_ref[hh, :, qs] = alpha * acc_ref[hh, :, qs] + _dot(vt_ref[0, hh * N:(hh + 1) * N, :], p.astype(bf16))
            m_ref[hh, :, qs] = m_new

    @pl.when((flags & (ATTN_FLAG_MASK | ATTN_FLAG_UPPER)) == 0)
    def _():
        step(False, 0)

    @pl.when((flags & (ATTN_FLAG_MASK | ATTN_FLAG_UPPER)) == ATTN_FLAG_MASK)
    def _():
        step(True, 0)

    @pl.when((flags & ATTN_FLAG_UPPER) != 0)
    def _():
        step(True, tq // 2)

    @pl.when((flags & ATTN_FLAG_LAST) != 0)
    def _():
        ot = jnp.concatenate([acc_ref[hh] * (1.0 / l_ref[hh]) for hh in range(nh)], axis=0)
        o_ref[0] = (ot.T * gate_ref[0].astype(f32)).astype(bf16)


def _fox_attn(q, qb, k, kb, vt, gate):
    B, S, D = gate.shape
    tq = min(ATTN_Q_TILE, S)
    tk = min(ATTN_K_TILE, S)
    P = D // LANES
    nh = LANES // HEAD_DIM
    qi, kj, flags = [], [], []
    for i in range(S // tq):
        last = ((i + 1) * tq - 1) // tk
        for j in range(last + 1):
            qi.append(i)
            kj.append(j)
            needs_mask = (j + 1) * tk - 1 > i * tq
            upper_only = j * tk >= i * tq + tq // 2
            flags.append((ATTN_FLAG_MASK if needs_mask else 0) | (ATTN_FLAG_LAST if j == last else 0)
                         | (ATTN_FLAG_UPPER if upper_only else 0))
    tables = [jnp.asarray(t, jnp.int32) for t in (qi, kj, flags)]
    qtile = pl.BlockSpec((1, tq, LANES), lambda b, p, s, qi, kj, fl: (b, qi[s], p))
    ktile = pl.BlockSpec((1, tk, LANES), lambda b, p, s, qi, kj, fl: (b, kj[s], p))
    grid_spec = pltpu.PrefetchScalarGridSpec(
        num_scalar_prefetch=3,
        grid=(B, P, len(qi)),
        in_specs=[qtile, qtile, ktile, ktile,
                  pl.BlockSpec((1, LANES, tk), lambda b, p, s, qi, kj, fl: (b, p, kj[s])),
                  qtile],
        out_specs=qtile,
        scratch_shapes=[pltpu.VMEM((nh, tq, 2 * LANES), bf16),
                        pltpu.VMEM((nh, 1, tq), f32), pltpu.VMEM((nh, 1, tq), f32),
                        pltpu.VMEM((nh, HEAD_DIM, tq), f32)],
    )
    return pl.pallas_call(
        _fox_attn_kernel,
        name="fox_attn",
        grid_spec=grid_spec,
        out_shape=jax.ShapeDtypeStruct((B, S, D), bf16),
        compiler_params=_cparams("arbitrary", "arbitrary", "arbitrary"),
    )(*tables, q, qb, k, kb, vt, gate)


def kernel(x, c, ada_w, ada_b, ln_g, ln_b, rw_mu, rw_rkv, rw_w0, rw_w1, rw_w2, rw_a0, rw_a1, rw_a2, rw_g1, rw_g2, rw_kk, rw_ka, rw_rk, rw_gn_g, rw_gn_b, rw_wo, rw_v0, rw_v1, rw_v2, kv_ada_w, kv_ada_b, kv_w, kv_fb, kv_knorm, fx_wqg, fx_qnorm, fx_wo, moe_wgrp, moe_bgrp, moe_wexp, moe_bexp, moe_wgate, moe_wup, moe_wdown):
    B, S, D = x.shape
    depth = ada_w.shape[0]
    n_a = rw_mu.shape[0]
    mod = _adaln_mod(c, ada_w, ada_b)
    kvmod = _adaln_mod(c, kv_ada_w[None], kv_ada_b[None])[0].reshape(B, 1, 2 * D)
    kv = None
    v_first = None
    for l in range(depth):
        modl = mod[l].reshape(B, 1, 6 * D)
        if l < n_a:
            vmix = None if l == 0 else (rw_v0[l - 1], rw_v1[l - 1], rw_v2[l - 1], v_first)
            r, k, v, wl, kk, b, g = _rwkv_pre(x, modl, rw_mu[l], rw_rkv[l], rw_w0[l], rw_w1[l], rw_w2[l],
                                              rw_a0[l], rw_a1[l], rw_a2[l], rw_g1[l], rw_g2[l],
                                              rw_kk[l], rw_ka[l], vmix)
            if l == 0:
                v_first = v
            z = _wkv7(r, k, v, wl, kk, b, g, rw_rk[l].reshape(D), rw_gn_g[l], rw_gn_b[l])
            w_o = rw_wo[l]
        else:
            j = l - n_a
            k, kb, vt, fcum = kv
            q, qb, gate = _fox_pre(x, modl, fx_wqg[j], fx_qnorm[j], fcum)
            z = _fox_attn(q, qb, k, kb, vt, gate)
            w_o = fx_wo[j]
        x = _proj_ln(z, w_o, x, modl, 2, ln_g[l, 0], ln_b[l, 0])
        x = _moe_layer(x, modl, moe_wgrp[l], moe_bgrp[l], moe_wexp[l], moe_bexp[l],
                       moe_wgate[l], moe_wup[l], moe_wdown[l], ln_g[l, 1], ln_b[l, 1])
        if l == n_a - 1:
            kv = _shared_kv(x, kvmod, kv_w, kv_fb, kv_knorm)
    return x
```

```python
import functools
import math

import jax
import jax.numpy as jnp
import numpy as np
from jax import lax
from jax.experimental import pallas as pl
from jax.experimental.pallas import tpu as pltpu

HEAD_DIM = 64
N_GROUPS = 4
EXP_PER_GROUP = 8
N_EXPERTS = N_GROUPS * EXP_PER_GROUP
TOP_K = 2
DEPTH = 4
N_A = 2
DEEPNORM_ALPHA = (2 * DEPTH) ** 0.25
LN_EPS = 1e-5
GN_EPS = 64e-5
QK_EPS = 1e-6

LANES = 128
VMEM_LIMIT_BYTES = 56 * 1024 * 1024
WKV_CHUNK = 64
WKV_TILE = 128
ROW_TILE = 512
RWKV_PRE_TILE = 512
ATTN_Q_TILE = 1024
ATTN_K_TILE = 512
ATTN_PAIRS = 2
ATTN_FLAG_MASK = 1
ATTN_FLAG_LAST = 2
ATTN_FLAG_UPPER = 4
MOE_BLOCK = 256
NEG_BIG = -1e30
LOG2E = math.log2(math.e)

f32 = jnp.float32
bf16 = jnp.bfloat16
HIGHEST = lax.Precision.HIGHEST


def _cparams(*sem):
    return pltpu.CompilerParams(dimension_semantics=sem, vmem_limit_bytes=VMEM_LIMIT_BYTES)


def _sigmoid(x):
    return 1.0 / (1.0 + jnp.exp(-x))


def _dot(a, b):
    return jnp.dot(a, b, preferred_element_type=f32)


def _dot_nt(a, b):
    return lax.dot_general(a, b, (((1,), (1,)), ((), ())), preferred_element_type=f32)


def _dot_tn(a, b):
    return lax.dot_general(a, b, (((0,), (0,)), ((), ())), preferred_element_type=f32)


def _full_spec(shape):
    n = len(shape)
    return pl.BlockSpec(shape, lambda *_: (0,) * n)


def _pair_ones():
    rs = lax.broadcasted_iota(jnp.int32, (LANES, LANES), 0)
    cs = lax.broadcasted_iota(jnp.int32, (LANES, LANES), 1)
    return jnp.where(jnp.where(rs >= HEAD_DIM, 1, 0) == jnp.where(cs >= HEAD_DIM, 1, 0), 1.0, 0.0).astype(bf16)


def _head_sum(x, jbd):
    xh = x.astype(bf16)
    xl = (x - xh.astype(f32)).astype(bf16)
    return _dot(xh, jbd) + _dot(xl, jbd)


def _bias_split(f):
    t0 = f.astype(bf16)
    r1 = f - t0.astype(f32)
    t1 = r1.astype(bf16)
    t2 = (r1 - t1.astype(f32)).astype(bf16)
    return jnp.concatenate([t0, t1, t2], axis=1)


def _bias_placement(D, is_key):
    H = D // HEAD_DIM
    place = np.zeros((3 * LANES, D), np.float32)
    ones = np.zeros((1, D), np.float32)
    for h in range(H):
        for i in range(3):
            place[i * LANES + h, h * HEAD_DIM + (3 + i if is_key else i)] = -1.0 if is_key else 1.0
            ones[0, h * HEAD_DIM + (i if is_key else 3 + i)] = 1.0
    return jnp.asarray(place, bf16), jnp.asarray(ones, f32)


def _layer_norm_rows(u, g, b):
    mu = jnp.mean(u, axis=-1, keepdims=True)
    d = u - mu
    var = jnp.mean(d * d, axis=-1, keepdims=True)
    return d * lax.rsqrt(var + LN_EPS) * g + b


def _mod_kernel(c_ref, w_ref, b_ref, o_ref):
    c = c_ref[...]
    cs = c * _sigmoid(c)
    o_ref[0] = jnp.dot(cs, w_ref[0], preferred_element_type=f32, precision=HIGHEST) + b_ref[0]


def _adaln_mod(c, w, b):
    L, D, N = w.shape
    B = c.shape[0]
    tn = min(N, 1024)
    return pl.pallas_call(
        _mod_kernel,
        name="adaln_mod",
        grid=(L, N // tn),
        in_specs=[
            _full_spec((B, D)),
            pl.BlockSpec((1, D, tn), lambda l, j: (l, 0, j)),
            pl.BlockSpec((1, 1, tn), lambda l, j: (l, 0, j)),
        ],
        out_specs=pl.BlockSpec((1, B, tn), lambda l, j: (l, 0, j)),
        out_shape=jax.ShapeDtypeStruct((L, B, N), f32),
        compiler_params=_cparams("arbitrary", "arbitrary"),
    )(c, w, b.reshape(L, 1, N))


def _rwkv_pre_kernel(has_vmix, x_ref, xp_ref, mod_ref, mu_ref, wrkv_ref, w0_ref, w1_ref, w2_ref,
                     a0_ref, a1_ref, a2_ref, g1_ref, g2_ref, kkp_ref, kap_ref, *rest):
    if has_vmix:
        v0_ref, v1_ref, v2_ref, vf_ref = rest[:4]
        rest = rest[4:]
    r_ref, k_ref, v_ref, wl_ref, kk_ref, b_ref, g_ref = rest
    D = x_ref.shape[-1]
    sh = mod_ref[0, :, 0:D]
    sc = mod_ref[0, :, D:2 * D]
    h = x_ref[0] * (1.0 + sc) + sh
    hp = xp_ref[0][7:8, :] * (1.0 + sc) + sh
    hp = jnp.where(pl.program_id(1) == 0, 0.0, hp)
    row = lax.broadcasted_iota(jnp.int32, h.shape, 0)
    h_prev = jnp.where(row == 0, hp, pltpu.roll(h, 1, axis=0))
    xx = h_prev - h
    mix = lambda j: (h + xx * mu_ref[j:j + 1, :]).astype(bf16)
    xr, xw, xk, xv, xa, xg = [mix(j) for j in range(6)]
    r = _dot(xr, wrkv_ref[0])
    k = _dot(xk, wrkv_ref[1])
    v = _dot(xv, wrkv_ref[2])
    zw = w0_ref[...] + _dot(jnp.tanh(_dot(xw, w1_ref[...])).astype(bf16), w2_ref[...])
    wl_ref[0] = -math.exp(-0.5) * _sigmoid(zw)
    a = _sigmoid(a0_ref[...] + _dot(_dot(xa, a1_ref[...]).astype(bf16), a2_ref[...]))
    g = _dot(_sigmoid(_dot(xg, g1_ref[...])).astype(bf16), g2_ref[...])
    if has_vmix:
        vmix = _sigmoid(v0_ref[...] + _dot(_dot(xv, v1_ref[...]).astype(bf16), v2_ref[...]))
        v = v + (vf_ref[0].astype(f32) - v) * vmix
    r_ref[0] = r.astype(bf16)
    v_ref[0] = v.astype(bf16)
    g_ref[0] = g.astype(bf16)
    k_ref[0] = (k * (1.0 + (a - 1.0) * kap_ref[...])).astype(bf16)
    kk = k * kkp_ref[...]
    jbd = _pair_ones()
    for p in range(D // LANES):
        lp = slice(p * LANES, (p + 1) * LANES)
        seg = kk[:, lp]
        nrm = jnp.sqrt(_head_sum(seg * seg, jbd))
        seg = seg / jnp.maximum(nrm, 1e-12)
        kk_ref[0, :, lp] = seg.astype(bf16)
        b_ref[0, :, lp] = (seg * a[:, lp]).astype(bf16)


def _rwkv_pre(x, modl, mu, wrkv, w0, w1, w2, a0, a1, a2, g1, g2, kkp, kap, vmix):
    B, S, D = x.shape
    tm = min(RWKV_PRE_TILE, S)
    row = lambda t: t.reshape(1, D)
    args = [x, x, modl, mu, wrkv.astype(bf16), row(w0), w1.astype(bf16), w2.astype(bf16), row(a0),
            a1.astype(bf16), a2.astype(bf16), g1.astype(bf16), g2.astype(bf16), row(kkp), row(kap)]
    tile = pl.BlockSpec((1, tm, D), lambda b, i: (b, i, 0))
    in_specs = [
        tile,
        pl.BlockSpec((1, 8, D), lambda b, i: (b, jnp.maximum(i * (tm // 8) - 1, 0), 0)),
        pl.BlockSpec((1, 1, modl.shape[-1]), lambda b, i: (b, 0, 0)),
    ] + [_full_spec(a.shape) for a in args[3:]]
    if vmix is not None:
        v0, v1, v2, v_first = vmix
        extra = [row(v0), v1.astype(bf16), v2.astype(bf16)]
        args += extra + [v_first]
        in_specs += [_full_spec(a.shape) for a in extra] + [tile]
    out_bf = jax.ShapeDtypeStruct((B, S, D), bf16)
    out_shape = [out_bf, out_bf, out_bf, jax.ShapeDtypeStruct((B, S, D), f32), out_bf, out_bf, out_bf]
    return pl.pallas_call(
        functools.partial(_rwkv_pre_kernel, vmix is not None),
        name="rwkv_pre",
        grid=(B, S // tm),
        in_specs=in_specs,
        out_specs=[tile] * 7,
        out_shape=out_shape,
        compiler_params=_cparams("arbitrary", "arbitrary"),
    )(*args)


def _wkv_kernel(C, r_ref, k_ref, v_ref, wl_ref, kk_ref, b_ref, g_ref, rk_ref, gng_ref, gnb_ref,
                o_ref, state_ref):
    @pl.when(pl.program_id(1) == 0)
    def _():
        state_ref[...] = jnp.zeros_like(state_ref)

    T, D = r_ref.shape[1], r_ref.shape[2]
    N = HEAD_DIM
    P = D // LANES
    C2 = 2 * C
    ri = lax.broadcasted_iota(jnp.int32, (C, C), 0)
    ci = lax.broadcasted_iota(jnp.int32, (C, C), 1)
    tri_incl = jnp.where(ri >= ci, 1.0, 0.0).astype(f32)
    r2 = lax.broadcasted_iota(jnp.int32, (C2, C2), 0)
    c2 = lax.broadcasted_iota(jnp.int32, (C2, C2), 1)
    dlt = jnp.where(jnp.where(r2 >= C, 1, 0) == jnp.where(c2 >= C, 1, 0), r2 - c2, -1)
    strict = dlt > 0
    incl = dlt >= 0
    eye2 = jnp.where(r2 == c2, 1.0, 0.0).astype(f32)
    h0 = lax.broadcasted_iota(jnp.int32, (C, LANES), 1) < N
    rs = lax.broadcasted_iota(jnp.int32, (LANES, LANES), 0)
    cs = lax.broadcasted_iota(jnp.int32, (LANES, LANES), 1)
    sbd = jnp.where(rs >= N, 1, 0) == jnp.where(cs >= N, 1, 0)
    jbd = jnp.where(sbd, 1.0, 0.0).astype(bf16)
    stack2 = lambda t: jnp.concatenate([t, t], axis=0)

    lanes = [slice(p * LANES, (p + 1) * LANES) for p in range(P)]
    nc = T // C
    units = [(c, p) for c in range(nc) for p in range(P)]
    hi_lo = lambda t: (t.astype(bf16), (t - t.astype(bf16).astype(f32)).astype(bf16))

    ch = []
    for c in range(nc):
        sl = pl.ds(c * C, C)
        wl = wl_ref[0, sl, :]
        cum = jnp.dot(tri_incl, wl, preferred_element_type=f32, precision=HIGHEST)
        g_incl = jnp.exp(cum)
        g_inv = jnp.exp(-cum)
        g_last = g_incl[C - 1:C, :]
        r = r_ref[0, sl, :].astype(f32)
        k = k_ref[0, sl, :].astype(f32)
        btf = b_ref[0, sl, :].astype(f32) * g_inv
        ktf = k * g_inv
        ch.append(dict(
            sl=sl, g_last=g_last, v16=v_ref[0, sl, :],
            af=-kk_ref[0, sl, :].astype(f32) * jnp.exp(cum - wl), rf=r * g_incl,
            bt=btf.astype(bf16), kt=ktf.astype(bf16),
            bc=(btf * g_last).astype(bf16), kc=(ktf * g_last).astype(bf16),
            rkr=r * k * rk_ref[...]))

    Ls, Aak, Arow, Vst, X2 = [], [], [], [], []
    for c, p in units:
        d, lp = ch[c], lanes[p]
        afp, rfp = d["af"][:, lp], d["rf"][:, lp]
        X4 = jnp.concatenate([jnp.where(h0, afp, 0.0), jnp.where(h0, 0.0, afp),
                              jnp.where(h0, rfp, 0.0), jnp.where(h0, 0.0, rfp)], axis=0).astype(bf16)
        Mb = _dot_nt(X4, stack2(d["bt"][:, lp]))
        Mk = _dot_nt(X4, stack2(d["kt"][:, lp]))
        Ls.append(jnp.where(strict, Mb[:C2], 0.0))
        Aak.append(jnp.where(strict, Mk[:C2], 0.0).astype(bf16))
        Arow.append(jnp.concatenate([jnp.where(incl, Mb[C2:], 0.0), jnp.where(incl, Mk[C2:], 0.0)],
                                    axis=1).astype(bf16))
        Vst.append(stack2(d["v16"][:, lp]))
        X2.append(jnp.concatenate([afp, rfp], axis=0).astype(bf16))
    Ps = [eye2 + L for L in Ls]
    n = 1
    while 2 * n < C:
        Lb = [L.astype(bf16) for L in Ls]
        Ls = [_dot(x, x) for x in Lb]
        Ps = [Pm + _dot(L.astype(bf16), Pm.astype(bf16)) for L, Pm in zip(Ls, Ps)]
        n *= 2
    Ps = [Pm.astype(bf16) for Pm in Ps]
    AV = [_dot(a, vs) for a, vs in zip(Aak, Vst)]
    ys = []
    state = [state_ref[p] for p in range(P)]
    for c in range(nc):
        d = ch[c]
        us = range(c * P, (c + 1) * P)
        M2 = [_dot_nt(X2[u], state[u - c * P].astype(bf16)) for u in us]
        Ust = [_dot(Ps[u], (AV[u] + stack2(m2[:C])).astype(bf16)) for u, m2 in zip(us, M2)]
        Yst = [_dot(Arow[u], jnp.concatenate([ust.astype(bf16), Vst[u]], axis=0)) + stack2(m2[C:])
               for u, ust, m2 in zip(us, Ust, M2)]
        ys += [jnp.where(h0, yst[:C], yst[C:]) for yst in Yst]
        for p, ust in enumerate(Ust):
            lp = lanes[p]
            u16 = jnp.where(h0, ust[:C], ust[C:]).astype(bf16)
            upd = _dot_tn(jnp.concatenate([u16, d["v16"][:, lp]], axis=0),
                          jnp.concatenate([d["bc"][:, lp], d["kc"][:, lp]], axis=0))
            state[p] = state[p] * d["g_last"][:, lp] + jnp.where(sbd, upd, 0.0)
    for p in range(P):
        state_ref[p] = state[p]
    sums = []
    for u, (c, p) in enumerate(units):
        sums.append(_dot(jnp.concatenate(hi_lo(ys[u]) + hi_lo(ch[c]["rkr"][:, lanes[p]]), axis=0), jbd))
    yds = [ys[u] - (sums[u][:C] + sums[u][C:C2]) * (1.0 / N) for u in range(len(units))]
    var = [_dot(jnp.concatenate(hi_lo(yd * yd), axis=0), jbd) for yd in yds]
    for u, (c, p) in enumerate(units):
        d, lp = ch[c], lanes[p]
        yn = yds[u] * lax.rsqrt((var[u][:C] + var[u][C:]) * (1.0 / N) + GN_EPS)
        bonus = (sums[u][C2:C2 + C] + sums[u][C2 + C:]) * d["v16"][:, lp].astype(f32)
        z = (yn * gng_ref[:, lp] + gnb_ref[:, lp] + bonus) * g_ref[0, d["sl"], lp].astype(f32)
        o_ref[0, d["sl"], lp] = z.astype(bf16)


def _wkv7(r, k, v, wl, kk, b, g, r_k, gn_g, gn_b):
    B, S, D = r.shape
    T = min(WKV_TILE, S)
    C = min(WKV_CHUNK, T)
    tile = pl.BlockSpec((1, T, D), lambda bb, i: (bb, i, 0))
    vec = _full_spec((1, D))
    return pl.pallas_call(
        functools.partial(_wkv_kernel, C),
        name="wkv7",
        grid=(B, S // T),
        in_specs=[tile] * 7 + [vec] * 3,
        out_specs=tile,
        out_shape=jax.ShapeDtypeStruct((B, S, D), bf16),
        scratch_shapes=[pltpu.VMEM((D // LANES, LANES, LANES), f32)],
        compiler_params=_cparams("arbitrary", "arbitrary"),
    )(r, k, v, wl, kk, b, g, r_k.reshape(1, D), gn_g.reshape(1, D), gn_b.reshape(1, D))


def _proj_ln_kernel(gate_idx, z_ref, w_ref, x_ref, mod_ref, lng_ref, lnb_ref, o_ref):
    D = x_ref.shape[-1]
    gt = mod_ref[0, :, gate_idx * D:(gate_idx + 1) * D]
    y = _dot(z_ref[0], w_ref[...])
    u = DEEPNORM_ALPHA * x_ref[0] + (1.0 + gt) * y
    o_ref[0] = _layer_norm_rows(u, lng_ref[...], lnb_ref[...])


def _proj_ln(z, w, x, modl, gate_idx, ln_g, ln_b):
    B, S, D = x.shape
    tm = min(ROW_TILE, S)
    tile = pl.BlockSpec((1, tm, D), lambda b, i: (b, i, 0))
    return pl.pallas_call(
        functools.partial(_proj_ln_kernel, gate_idx),
        name="proj_ln",
        grid=(B, S // tm),
        in_specs=[tile, _full_spec(w.shape), tile,
                  pl.BlockSpec((1, 1, modl.shape[-1]), lambda b, i: (b, 0, 0)),
                  _full_spec((1, D)), _full_spec((1, D))],
        out_specs=tile,
        out_shape=jax.ShapeDtypeStruct((B, S, D), f32),
        compiler_params=_cparams("arbitrary", "arbitrary"),
    )(z, w.astype(bf16), x, modl, ln_g.reshape(1, D), ln_b.reshape(1, D))


def _route_kernel(x_ref, mod_ref, w_ref, b_ref, h_ref, idx_ref, gate_ref, cnt_ref):
    @pl.when((pl.program_id(0) == 0) & (pl.program_id(1) == 0))
    def _():
        cnt_ref[...] = jnp.zeros_like(cnt_ref)

    D = x_ref.shape[-1]
    sh = mod_ref[0, :, 3 * D:4 * D]
    sc = mod_ref[0, :, 4 * D:5 * D]
    h = x_ref[0] * (1.0 + sc) + sh
    h_ref[0] = h.astype(bf16)
    lg = jnp.dot(h, w_ref[...], preferred_element_type=f32, precision=HIGHEST) + b_ref[...]
    lane = lax.broadcasted_iota(jnp.int32, lg.shape, 1)
    glog = jnp.where(lane < N_GROUPS, lg, -jnp.inf)
    gmax = jnp.max(glog, axis=-1, keepdims=True)
    p_g = 1.0 / jnp.sum(jnp.exp(glog - gmax), axis=-1, keepdims=True)
    g_sel = jnp.min(jnp.where(glog == gmax, lane, LANES), axis=-1, keepdims=True)
    e_lo = N_GROUPS + EXP_PER_GROUP * g_sel
    elog = jnp.where((lane >= e_lo) & (lane < e_lo + EXP_PER_GROUP), lg, -jnp.inf)
    e1 = jnp.max(elog, axis=-1, keepdims=True)
    esum = jnp.sum(jnp.exp(elog - e1), axis=-1, keepdims=True)
    i1 = jnp.min(jnp.where(elog == e1, lane, LANES), axis=-1, keepdims=True)
    elog2 = jnp.where(lane == i1, -jnp.inf, elog)
    e2 = jnp.max(elog2, axis=-1, keepdims=True)
    i2 = jnp.min(jnp.where(elog2 == e2, lane, LANES), axis=-1, keepdims=True)
    p1 = 1.0 / esum
    p2 = jnp.exp(e2 - e1) / esum
    psum = p1 + p2
    gate_ref[0] = jnp.where(lane == 0, p_g * p1 / psum, jnp.where(lane == 1, p_g * p2 / psum, 0.0))
    tm = lg.shape[0]
    onehot = jnp.where(lane == i1, 1.0, 0.0) + jnp.where(lane == i2, 1.0, 0.0)
    ri = lax.broadcasted_iota(jnp.int32, (tm, tm), 0)
    ci = lax.broadcasted_iota(jnp.int32, (tm, tm), 1)
    before = _dot(jnp.where(ri > ci, 1.0, 0.0).astype(bf16), onehot.astype(bf16)) + cnt_ref[...]
    rank1 = jnp.sum(jnp.where(lane == i1, before, 0.0), axis=-1, keepdims=True).astype(jnp.int32)
    rank2 = jnp.sum(jnp.where(lane == i2, before, 0.0), axis=-1, keepdims=True).astype(jnp.int32)
    idx_ref[0] = jnp.where(lane == 0, i1 - N_GROUPS, jnp.where(lane == 1, i2 - N_GROUPS,
                           jnp.where(lane == 2, rank1, jnp.where(lane == 3, rank2, 0))))
    cnt_ref[...] = before[tm - 1:tm, :] + onehot[tm - 1:tm, :]


def _route(x, modl, w_grp, b_grp, w_exp, b_exp):
    B, S, D = x.shape
    tm = min(ROW_TILE, S)
    pad = LANES - N_GROUPS - N_EXPERTS
    w = jnp.concatenate([w_grp, w_exp, jnp.zeros((D, pad), f32)], axis=1)
    b = jnp.concatenate([b_grp, b_exp, jnp.zeros((pad,), f32)]).reshape(1, LANES)
    tile = pl.BlockSpec((1, tm, D), lambda bb, i: (bb, i, 0))
    small = pl.BlockSpec((1, tm, LANES), lambda bb, i: (bb, i, 0))
    return pl.pallas_call(
        _route_kernel,
        name="moe_route",
        grid=(B, S // tm),
        in_specs=[tile, pl.BlockSpec((1, 1, modl.shape[-1]), lambda bb, i: (bb, 0, 0)),
                  _full_spec(w.shape), _full_spec(b.shape)],
        out_specs=[tile, small, small, _full_spec((1, LANES))],
        out_shape=[jax.ShapeDtypeStruct((B, S, D), bf16),
                   jax.ShapeDtypeStruct((B, S, LANES), jnp.int32),
                   jax.ShapeDtypeStruct((B, S, LANES), f32),
                   jax.ShapeDtypeStruct((1, LANES), f32)],
        compiler_params=_cparams("arbitrary", "arbitrary"),
    )(x, modl, w, b)


def _expert_kernel(be_ref, nu_ref, x_ref, wg_ref, wu_ref, wd_ref, o_ref, wg16, wu16, wd16):
    i = pl.program_id(0)

    @pl.when((i == 0) | (be_ref[i] != be_ref[jnp.maximum(i - 1, 0)]))
    def _():
        wg16[...] = wg_ref[0, 0].astype(bf16)
        wu16[...] = wu_ref[0, 0].astype(bf16)
        wd16[...] = wd_ref[0, 0].astype(bf16)

    @pl.when(i < nu_ref[0])
    def _():
        x = x_ref[...]
        hg = _dot(x, wg16[...])
        hu = _dot(x, wu16[...])
        o_ref[...] = _dot((hg * _sigmoid(hg) * hu).astype(bf16), wd16[...]).astype(o_ref.dtype)

    @pl.when(i >= nu_ref[0])
    def _():
        o_ref[...] = jnp.zeros_like(o_ref)


def _expert_ffn(xs, block_exp, n_used, wg, wu, wd, layer):
    n_slots, D = xs.shape
    F = wg.shape[-1]
    blk = MOE_BLOCK
    row_map = lambda i, be, nu: (jnp.minimum(i, nu[0] - 1), 0)
    w_map = lambda i, be, nu: (layer, be[i], 0, 0)
    grid_spec = pltpu.PrefetchScalarGridSpec(
        num_scalar_prefetch=2,
        grid=(n_slots // blk,),
        in_specs=[
            pl.BlockSpec((blk, D), row_map),
            pl.BlockSpec((1, 1, D, F), w_map),
            pl.BlockSpec((1, 1, D, F), w_map),
            pl.BlockSpec((1, 1, F, D), w_map),
        ],
        out_specs=pl.BlockSpec((blk, D), lambda i, be, nu: (i, 0)),
        scratch_shapes=[pltpu.VMEM((D, F), bf16), pltpu.VMEM((D, F), bf16), pltpu.VMEM((F, D), bf16)],
    )
    return pl.pallas_call(
        _expert_kernel,
        name="moe_experts",
        grid_spec=grid_spec,
        out_shape=jax.ShapeDtypeStruct((n_slots, D), bf16),
        compiler_params=_cparams("arbitrary"),
    )(block_exp, n_used, xs, wg, wu, wd)


def _take_rows(table, idx):
    return table.at[idx].get(mode="promise_in_bounds")


def _dispatch(expert_idx, rank, counts, blk):
    T = expert_idx.shape[0]
    A = T * TOP_K
    n_blocks = A // blk + N_EXPERTS
    experts = jnp.arange(N_EXPERTS, dtype=jnp.int32)
    padded = (counts + blk - 1) // blk * blk
    pad_end = jnp.cumsum(padded)
    pad_start = pad_end - padded
    start = jnp.cumsum(counts) - counts
    dest = rank + jnp.sum(jnp.where(expert_idx[..., None] == experts, pad_start, 0), axis=-1)
    block_exp = jnp.minimum(jnp.sum(((jnp.arange(n_blocks, dtype=jnp.int32) * blk)[:, None] >= pad_end[None, :])
                                    .astype(jnp.int32), axis=1), N_EXPERTS - 1)
    order = jnp.argsort(expert_idx.reshape(A)).astype(jnp.int32)
    shift = jnp.sum(jnp.where(block_exp[:, None] == experts, start - pad_start, 0), axis=-1)
    pos = jnp.arange(n_blocks * blk, dtype=jnp.int32) + jnp.repeat(shift, blk)
    slot_tok = _take_rows(order, jnp.clip(pos, 0, A - 1)) // TOP_K
    n_used = (pad_end[-1] // blk).astype(jnp.int32).reshape(1)
    return dest, slot_tok, block_exp, n_used


def _combine_ln_kernel(x_ref, y0_ref, y1_ref, gate_ref, mod_ref, lng_ref, lnb_ref, o_ref):
    D = x_ref.shape[-1]
    gt = mod_ref[0, :, 5 * D:6 * D]
    gates = gate_ref[0]
    y = gates[:, 0:1] * y0_ref[0].astype(f32) + gates[:, 1:2] * y1_ref[0].astype(f32)
    u = DEEPNORM_ALPHA * x_ref[0] + (1.0 + gt) * y
    o_ref[0] = _layer_norm_rows(u, lng_ref[...], lnb_ref[...])


def _combine_ln(x, y0, y1, gates, modl, ln_g, ln_b):
    B, S, D = x.shape
    tm = min(ROW_TILE, S)
    tile = pl.BlockSpec((1, tm, D), lambda b, i: (b, i, 0))
    return pl.pallas_call(
        _combine_ln_kernel,
        name="moe_combine_ln",
        grid=(B, S // tm),
        in_specs=[tile, tile, tile, pl.BlockSpec((1, tm, LANES), lambda b, i: (b, i, 0)),
                  pl.BlockSpec((1, 1, modl.shape[-1]), lambda b, i: (b, 0, 0)),
                  _full_spec((1, D)), _full_spec((1, D))],
        out_specs=tile,
        out_shape=jax.ShapeDtypeStruct((B, S, D), f32),
        compiler_params=_cparams("arbitrary", "arbitrary"),
    )(x, y0, y1, gates, modl, ln_g.reshape(1, D), ln_b.reshape(1, D))


def _moe_layer(x, modl, w_grp, b_grp, w_exp, b_exp, wg, wu, wd, layer, ln_g, ln_b):
    B, S, D = x.shape
    T = B * S
    hb, idx, gates, cnt = _route(x, modl, w_grp, b_grp, w_exp, b_exp)
    idx = idx.reshape(T, LANES)
    counts = cnt[0, N_GROUPS:N_GROUPS + N_EXPERTS].astype(jnp.int32)
    dest, slot_tok, block_exp, n_used = _dispatch(idx[:, :TOP_K], idx[:, TOP_K:2 * TOP_K], counts, MOE_BLOCK)
    xs = _take_rows(hb.reshape(T, D), slot_tok)
    ys = _expert_ffn(xs, block_exp, n_used, wg, wu, wd, layer)
    y0 = _take_rows(ys, dest[:, 0]).reshape(B, S, D)
    y1 = _take_rows(ys, dest[:, 1]).reshape(B, S, D)
    return _combine_ln(x, y0, y1, gates, modl, ln_g, ln_b)


def _shared_kv_kernel(x_ref, mod_ref, wk_ref, wvt_ref, wf_ref, fb_ref, kn_ref, place_ref, ones_ref,
                      k_ref, kb_ref, vt_ref, f_ref, carry_ref):
    @pl.when(pl.program_id(1) == 0)
    def _():
        carry_ref[...] = jnp.zeros_like(carry_ref)

    D = x_ref.shape[-1]
    tm = x_ref.shape[1]
    shift = mod_ref[0, :, 0:D]
    scale = mod_ref[0, :, D:2 * D]
    hk = x_ref[0] * (1.0 + scale) + shift
    hb = hk.astype(bf16)
    k = _dot(hb, wk_ref[...])
    vt_ref[0] = _dot_nt(wvt_ref[...], hb).astype(bf16)
    f = jnp.dot(hk, wf_ref[...], preferred_element_type=f32, precision=HIGHEST) + fb_ref[...]
    log_f = jnp.minimum(f, 0.0) - jnp.log(1.0 + jnp.exp(-jnp.abs(f)))
    row = lax.broadcasted_iota(jnp.int32, log_f.shape, 0)
    acc = log_f
    d = 1
    while d < tm:
        acc = acc + jnp.where(row >= d, pltpu.roll(acc, d, axis=0), 0.0)
        d *= 2
    acc = acc + carry_ref[...]
    f_ref[0] = acc
    carry_ref[...] = acc[tm - 1:tm, :]
    kb_ref[0] = (_dot(_bias_split(acc * LOG2E), place_ref[...]) + ones_ref[...]).astype(bf16)
    jbd = _pair_ones()
    for p in range(D // LANES):
        lp = slice(p * LANES, (p + 1) * LANES)
        seg = k[:, lp]
        ms = _head_sum(seg * seg, jbd) * (1.0 / HEAD_DIM)
        k_ref[0, :, lp] = (seg * lax.rsqrt(ms + QK_EPS) * kn_ref[...]).astype(bf16)


def _shared_kv(x, kvmod, w_kvf, b_f, k_norm):
    B, S, D = x.shape
    H = D // HEAD_DIM
    tm = min(ROW_TILE, S)
    wk = w_kvf[:, :D].astype(bf16)
    wvt = w_kvf[:, D:2 * D].T.astype(bf16)
    wf = jnp.concatenate([w_kvf[:, 2 * D:], jnp.zeros((D, LANES - H), f32)], axis=1)
    fb = jnp.concatenate([b_f, jnp.zeros((LANES - H,), f32)]).reshape(1, LANES)
    place, ones = _bias_placement(D, True)
    tile = pl.BlockSpec((1, tm, D), lambda b, i: (b, i, 0))
    act = jax.ShapeDtypeStruct((B, S, D), bf16)
    return pl.pallas_call(
        _shared_kv_kernel,
        name="shared_kv",
        grid=(B, S // tm),
        in_specs=[tile, pl.BlockSpec((1, 1, 2 * D), lambda b, i: (b, 0, 0)),
                  _full_spec(wk.shape), _full_spec(wvt.shape), _full_spec(wf.shape),
                  _full_spec(fb.shape), _full_spec((1, LANES)), _full_spec(place.shape), _full_spec(ones.shape)],
        out_specs=[tile, tile, pl.BlockSpec((1, D, tm), lambda b, i: (b, 0, i)),
                   pl.BlockSpec((1, tm, LANES), lambda b, i: (b, i, 0))],
        out_shape=[act, act, jax.ShapeDtypeStruct((B, D, S), bf16), jax.ShapeDtypeStruct((B, S, LANES), f32)],
        scratch_shapes=[pltpu.VMEM((1, LANES), f32)],
        compiler_params=_cparams("arbitrary", "arbitrary"),
    )(x, kvmod, wk, wvt, wf, fb, jnp.tile(k_norm, LANES // HEAD_DIM).reshape(1, LANES), place, ones)


def _fox_pre_kernel(x_ref, mod_ref, wq_ref, wg_ref, qn_ref, f_ref, place_ref, ones_ref, q_ref, qb_ref, gate_ref):
    D = x_ref.shape[-1]
    sh = mod_ref[0, :, 0:D]
    sc = mod_ref[0, :, D:2 * D]
    hb = (x_ref[0] * (1.0 + sc) + sh).astype(bf16)
    q = _dot(hb, wq_ref[...])
    gate_ref[0] = _sigmoid(_dot(hb, wg_ref[...])).astype(bf16)
    qb_ref[0] = (_dot(_bias_split(f_ref[0] * LOG2E), place_ref[...]) + ones_ref[...]).astype(bf16)
    qscale = HEAD_DIM ** -0.5 * LOG2E
    jbd = _pair_ones()
    for p in range(D // LANES):
        lp = slice(p * LANES, (p + 1) * LANES)
        seg = q[:, lp]
        ms = _head_sum(seg * seg, jbd) * (1.0 / HEAD_DIM)
        q_ref[0, :, lp] = (seg * lax.rsqrt(ms + QK_EPS) * (qn_ref[...] * qscale)).astype(bf16)


def _fox_pre(x, modl, w_qg, q_norm, fcum):
    B, S, D = x.shape
    tm = min(ROW_TILE, S)
    tile = pl.BlockSpec((1, tm, D), lambda b, i: (b, i, 0))
    wq = w_qg[:, :D].astype(bf16)
    wg = w_qg[:, D:].astype(bf16)
    place, ones = _bias_placement(D, False)
    act = jax.ShapeDtypeStruct((B, S, D), bf16)
    return pl.pallas_call(
        _fox_pre_kernel,
        name="fox_pre",
        grid=(B, S // tm),
        in_specs=[tile, pl.BlockSpec((1, 1, modl.shape[-1]), lambda b, i: (b, 0, 0)),
                  _full_spec(wq.shape), _full_spec(wg.shape), _full_spec((1, LANES)),
                  pl.BlockSpec((1, tm, LANES), lambda b, i: (b, i, 0)),
                  _full_spec(place.shape), _full_spec(ones.shape)],
        out_specs=[tile, tile, tile],
        out_shape=[act, act, act],
        compiler_params=_cparams("arbitrary", "arbitrary"),
    )(x, modl, wq, wg, jnp.tile(q_norm, LANES // HEAD_DIM).reshape(1, LANES), fcum, place, ones)


def _fox_attn_kernel(qi_ref, kj_ref, flag_ref, q_ref, qb_ref, k_ref, kb_ref, vt_ref, gate_ref, o_ref,
                     qm_ref, m_ref, l_ref, acc_ref):
    s = pl.program_id(2)
    i = qi_ref[s]
    j = kj_ref[s]
    flags = flag_ref[s]
    N = HEAD_DIM
    nh = q_ref.shape[2] // N
    hpp = LANES // N
    tk, tq = k_ref.shape[1], q_ref.shape[1]
    pair = lambda h: slice((h // hpp) * LANES, (h // hpp + 1) * LANES)

    @pl.when(j == 0)
    def _():
        m_ref[...] = jnp.full_like(m_ref, NEG_BIG)
        l_ref[...] = jnp.zeros_like(l_ref)
        acc_ref[...] = jnp.zeros_like(acc_ref)
        for hh in range(nh):
            qcat = jnp.concatenate([q_ref[0, :, pair(hh)], qb_ref[0, :, pair(hh)]], axis=1).astype(f32)
            head = (lax.broadcasted_iota(jnp.int32, qcat.shape, 1) % LANES) // N
            qm_ref[hh] = jnp.where(head == hh % hpp, qcat, 0.0).astype(bf16)

    def step(masked, q_lo):
        qs = slice(q_lo, tq)
        kcat = [jnp.concatenate([k_ref[0, :, pair(hh)], kb_ref[0, :, pair(hh)]], axis=1)
                for hh in range(0, nh, hpp)]
        sts = [_dot_nt(kcat[hh // hpp], qm_ref[hh, qs, :]) for hh in range(nh)]
        for hh, st in enumerate(sts):
            if masked:
                key = j * tk + lax.broadcasted_iota(jnp.int32, st.shape, 0)
                qry = i * tq + q_lo + lax.broadcasted_iota(jnp.int32, st.shape, 1)
                st = jnp.where(key <= qry, st, -jnp.inf)
            m_prev = m_ref[hh, :, qs]
            m_new = jnp.maximum(m_prev, jnp.max(st, axis=0, keepdims=True))
            alpha = jnp.exp2(m_prev - m_new)
            p = jnp.exp2(st - m_new)
            l_ref[hh, :, qs] = alpha * l_ref[hh, :, qs] + jnp.sum(p, axis=0, keepdims=True)
            acc_ref[hh, :, qs] = alpha * acc_ref[hh, :, qs] + _dot(vt_ref[0, hh * N:(hh + 1) * N, :], p.astype(bf16))
            m_ref[hh, :, qs] = m_new

    @pl.when((flags & (ATTN_FLAG_MASK | ATTN_FLAG_UPPER)) == 0)
    def _():
        step(False, 0)

    @pl.when((flags & (ATTN_FLAG_MASK | ATTN_FLAG_UPPER)) == ATTN_FLAG_MASK)
    def _():
        step(True, 0)

    @pl.when((flags & ATTN_FLAG_UPPER) != 0)
    def _():
        step(True, tq // 2)

    @pl.when((flags & ATTN_FLAG_LAST) != 0)
    def _():
        for pr in range(nh // hpp):
            ot = jnp.concatenate([acc_ref[hh] * (1.0 / l_ref[hh]) for hh in range(pr * hpp, (pr + 1) * hpp)],
                                 axis=0)
            lp = slice(pr * LANES, (pr + 1) * LANES)
            o_ref[0, :, lp] = (ot.T * gate_ref[0, :, lp].astype(f32)).astype(bf16)


def _fox_attn(q, qb, k, kb, vt, gate):
    B, S, D = gate.shape
    tq = min(ATTN_Q_TILE, S)
    tk = min(ATTN_K_TILE, S)
    LW = ATTN_PAIRS * LANES
    nh = LW // HEAD_DIM
    qi, kj, flags = [], [], []
    for i in range(S // tq):
        last = ((i + 1) * tq - 1) // tk
        for j in range(last + 1):
            qi.append(i)
            kj.append(j)
            needs_mask = (j + 1) * tk - 1 > i * tq
            upper_only = j * tk >= i * tq + tq // 2
            flags.append((ATTN_FLAG_MASK if needs_mask else 0) | (ATTN_FLAG_LAST if j == last else 0)
                         | (ATTN_FLAG_UPPER if upper_only else 0))
    tables = [jnp.asarray(t, jnp.int32) for t in (qi, kj, flags)]
    qtile = pl.BlockSpec((1, tq, LW), lambda b, p, s, qi, kj, fl: (b, qi[s], p))
    ktile = pl.BlockSpec((1, tk, LW), lambda b, p, s, qi, kj, fl: (b, kj[s], p))
    grid_spec = pltpu.PrefetchScalarGridSpec(
        num_scalar_prefetch=3,
        grid=(B, D // LW, len(qi)),
        in_specs=[qtile, qtile, ktile, ktile,
                  pl.BlockSpec((1, LW, tk), lambda b, p, s, qi, kj, fl: (b, p, kj[s])),
                  qtile],
        out_specs=qtile,
        scratch_shapes=[pltpu.VMEM((nh, tq, 2 * LANES), bf16),
                        pltpu.VMEM((nh, 1, tq), f32), pltpu.VMEM((nh, 1, tq), f32),
                        pltpu.VMEM((nh, HEAD_DIM, tq), f32)],
    )
    return pl.pallas_call(
        _fox_attn_kernel,
        name="fox_attn",
        grid_spec=grid_spec,
        out_shape=jax.ShapeDtypeStruct((B, S, D), bf16),
        compiler_params=_cparams("arbitrary", "arbitrary", "arbitrary"),
    )(*tables, q, qb, k, kb, vt, gate)


def kernel(x, c, ada_w, ada_b, ln_g, ln_b, rw_mu, rw_rkv, rw_w0, rw_w1, rw_w2, rw_a0, rw_a1, rw_a2, rw_g1, rw_g2, rw_kk, rw_ka, rw_rk, rw_gn_g, rw_gn_b, rw_wo, rw_v0, rw_v1, rw_v2, kv_ada_w, kv_ada_b, kv_w, kv_fb, kv_knorm, fx_wqg, fx_qnorm, fx_wo, moe_wgrp, moe_bgrp, moe_wexp, moe_bexp, moe_wgate, moe_wup, moe_wdown):
    B, S, D = x.shape
    depth = ada_w.shape[0]
    n_a = rw_mu.shape[0]
    mod = _adaln_mod(c, ada_w, ada_b)
    kvmod = _adaln_mod(c, kv_ada_w[None], kv_ada_b[None])[0].reshape(B, 1, 2 * D)
    kv = None
    v_first = None
    for l in range(depth):
        modl = mod[l].reshape(B, 1, 6 * D)
        if l < n_a:
            vmix = None if l == 0 else (rw_v0[l - 1], rw_v1[l - 1], rw_v2[l - 1], v_first)
            r, k, v, wl, kk, b, g = _rwkv_pre(x, modl, rw_mu[l], rw_rkv[l], rw_w0[l], rw_w1[l], rw_w2[l],
                                              rw_a0[l], rw_a1[l], rw_a2[l], rw_g1[l], rw_g2[l],
                                              rw_kk[l], rw_ka[l], vmix)
            if l == 0:
                v_first = v
            z = _wkv7(r, k, v, wl, kk, b, g, rw_rk[l].reshape(D), rw_gn_g[l], rw_gn_b[l])
            w_o = rw_wo[l]
        else:
            j = l - n_a
            k, kb, vt, fcum = kv
            q, qb, gate = _fox_pre(x, modl, fx_wqg[j], fx_qnorm[j], fcum)
            z = _fox_attn(q, qb, k, kb, vt, gate)
            w_o = fx_wo[j]
        x = _proj_ln(z, w_o, x, modl, 2, ln_g[l, 0], ln_b[l, 0])
        x = _moe_layer(x, modl, moe_wgrp[l], moe_bgrp[l], moe_wexp[l], moe_bexp[l],
                       moe_wgate, moe_wup, moe_wdown, l, ln_g[l, 1], ln_b[l, 1])
        if l == n_a - 1:
            kv = _shared_kv(x, kvmod, kv_w, kv_fb, kv_knorm)
    return x
```

```python
import functools
import math

import jax
import jax.numpy as jnp
import numpy as np
from jax import lax
from jax.experimental import pallas as pl
from jax.experimental.pallas import tpu as pltpu

HEAD_DIM = 64
N_GROUPS = 4
EXP_PER_GROUP = 8
N_EXPERTS = N_GROUPS * EXP_PER_GROUP
TOP_K = 2
DEPTH = 4
N_A = 2
DEEPNORM_ALPHA = (2 * DEPTH) ** 0.25
LN_EPS = 1e-5
GN_EPS = 64e-5
QK_EPS = 1e-6

LANES = 128
VMEM_LIMIT_BYTES = 56 * 1024 * 1024
WKV_CHUNK = 64
WKV_TILE = 128
ROW_TILE = 512
RWKV_PRE_TILE = 512
ATTN_Q_TILE = 1024
ATTN_K_TILE = 512
ATTN_PAIRS = 2
ATTN_FLAG_MASK = 1
ATTN_FLAG_LAST = 2
ATTN_FLAG_UPPER = 4
MOE_BLOCK = 512
NEG_BIG = -1e30
LOG2E = math.log2(math.e)

f32 = jnp.float32
bf16 = jnp.bfloat16
HIGHEST = lax.Precision.HIGHEST


def _cparams(*sem):
    return pltpu.CompilerParams(dimension_semantics=sem, vmem_limit_bytes=VMEM_LIMIT_BYTES)


def _sigmoid(x):
    return 1.0 / (1.0 + jnp.exp(-x))


def _dot(a, b):
    return jnp.dot(a, b, preferred_element_type=f32)


def _dot_nt(a, b):
    return lax.dot_general(a, b, (((1,), (1,)), ((), ())), preferred_element_type=f32)


def _dot_tn(a, b):
    return lax.dot_general(a, b, (((0,), (0,)), ((), ())), preferred_element_type=f32)


def _full_spec(shape):
    n = len(shape)
    return pl.BlockSpec(shape, lambda *_: (0,) * n)


def _pair_ones():
    rs = lax.broadcasted_iota(jnp.int32, (LANES, LANES), 0)
    cs = lax.broadcasted_iota(jnp.int32, (LANES, LANES), 1)
    return jnp.where(jnp.where(rs >= HEAD_DIM, 1, 0) == jnp.where(cs >= HEAD_DIM, 1, 0), 1.0, 0.0).astype(bf16)


def _head_sum(x, jbd):
    xh = x.astype(bf16)
    xl = (x - xh.astype(f32)).astype(bf16)
    return _dot(xh, jbd) + _dot(xl, jbd)


def _bias_split(f):
    t0 = f.astype(bf16)
    r1 = f - t0.astype(f32)
    t1 = r1.astype(bf16)
    t2 = (r1 - t1.astype(f32)).astype(bf16)
    return jnp.concatenate([t0, t1, t2], axis=1)


def _bias_placement(D, is_key):
    H = D // HEAD_DIM
    place = np.zeros((3 * LANES, D), np.float32)
    ones = np.zeros((1, D), np.float32)
    for h in range(H):
        for i in range(3):
            place[i * LANES + h, h * HEAD_DIM + (3 + i if is_key else i)] = -1.0 if is_key else 1.0
            ones[0, h * HEAD_DIM + (i if is_key else 3 + i)] = 1.0
    return jnp.asarray(place, bf16), jnp.asarray(ones, f32)


def _layer_norm_rows(u, g, b):
    mu = jnp.mean(u, axis=-1, keepdims=True)
    d = u - mu
    var = jnp.mean(d * d, axis=-1, keepdims=True)
    return d * lax.rsqrt(var + LN_EPS) * g + b


def _mod_kernel(c_ref, w_ref, b_ref, o_ref):
    c = c_ref[...]
    cs = c * _sigmoid(c)
    o_ref[0] = jnp.dot(cs, w_ref[0], preferred_element_type=f32, precision=HIGHEST) + b_ref[0]


def _adaln_mod(c, w, b):
    L, D, N = w.shape
    B = c.shape[0]
    tn = min(N, 1024)
    return pl.pallas_call(
        _mod_kernel,
        name="adaln_mod",
        grid=(L, N // tn),
        in_specs=[
            _full_spec((B, D)),
            pl.BlockSpec((1, D, tn), lambda l, j: (l, 0, j)),
            pl.BlockSpec((1, 1, tn), lambda l, j: (l, 0, j)),
        ],
        out_specs=pl.BlockSpec((1, B, tn), lambda l, j: (l, 0, j)),
        out_shape=jax.ShapeDtypeStruct((L, B, N), f32),
        compiler_params=_cparams("arbitrary", "arbitrary"),
    )(c, w, b.reshape(L, 1, N))


def _rwkv_pre_kernel(has_vmix, x_ref, xp_ref, mod_ref, mu_ref, wrkv_ref, w0_ref, w1_ref, w2_ref,
                     a0_ref, a1_ref, a2_ref, g1_ref, g2_ref, kkp_ref, kap_ref, *rest):
    if has_vmix:
        v0_ref, v1_ref, v2_ref, vf_ref = rest[:4]
        rest = rest[4:]
    r_ref, k_ref, v_ref, wl_ref, kk_ref, b_ref, g_ref = rest
    D = x_ref.shape[-1]
    sh = mod_ref[0, :, 0:D]
    sc = mod_ref[0, :, D:2 * D]
    h = x_ref[0] * (1.0 + sc) + sh
    hp = xp_ref[0][7:8, :] * (1.0 + sc) + sh
    hp = jnp.where(pl.program_id(1) == 0, 0.0, hp)
    row = lax.broadcasted_iota(jnp.int32, h.shape, 0)
    h_prev = jnp.where(row == 0, hp, pltpu.roll(h, 1, axis=0))
    xx = h_prev - h
    mix = lambda j: (h + xx * mu_ref[j:j + 1, :]).astype(bf16)
    xr, xw, xk, xv, xa, xg = [mix(j) for j in range(6)]
    r = _dot(xr, wrkv_ref[0])
    k = _dot(xk, wrkv_ref[1])
    v = _dot(xv, wrkv_ref[2])
    zw = w0_ref[...] + _dot(jnp.tanh(_dot(xw, w1_ref[...])).astype(bf16), w2_ref[...])
    wl_ref[0] = -math.exp(-0.5) * _sigmoid(zw)
    a = _sigmoid(a0_ref[...] + _dot(_dot(xa, a1_ref[...]).astype(bf16), a2_ref[...]))
    g = _dot(_sigmoid(_dot(xg, g1_ref[...])).astype(bf16), g2_ref[...])
    if has_vmix:
        vmix = _sigmoid(v0_ref[...] + _dot(_dot(xv, v1_ref[...]).astype(bf16), v2_ref[...]))
        v = v + (vf_ref[0].astype(f32) - v) * vmix
    r_ref[0] = r.astype(bf16)
    v_ref[0] = v.astype(bf16)
    g_ref[0] = g.astype(bf16)
    k_ref[0] = (k * (1.0 + (a - 1.0) * kap_ref[...])).astype(bf16)
    kk = k * kkp_ref[...]
    jbd = _pair_ones()
    for p in range(D // LANES):
        lp = slice(p * LANES, (p + 1) * LANES)
        seg = kk[:, lp]
        nrm = jnp.sqrt(_head_sum(seg * seg, jbd))
        seg = seg / jnp.maximum(nrm, 1e-12)
        kk_ref[0, :, lp] = seg.astype(bf16)
        b_ref[0, :, lp] = (seg * a[:, lp]).astype(bf16)


def _rwkv_pre(x, modl, mu, wrkv, w0, w1, w2, a0, a1, a2, g1, g2, kkp, kap, vmix):
    B, S, D = x.shape
    tm = min(RWKV_PRE_TILE, S)
    row = lambda t: t.reshape(1, D)
    args = [x, x, modl, mu, wrkv.astype(bf16), row(w0), w1.astype(bf16), w2.astype(bf16), row(a0),
            a1.astype(bf16), a2.astype(bf16), g1.astype(bf16), g2.astype(bf16), row(kkp), row(kap)]
    tile = pl.BlockSpec((1, tm, D), lambda b, i: (b, i, 0))
    in_specs = [
        tile,
        pl.BlockSpec((1, 8, D), lambda b, i: (b, jnp.maximum(i * (tm // 8) - 1, 0), 0)),
        pl.BlockSpec((1, 1, modl.shape[-1]), lambda b, i: (b, 0, 0)),
    ] + [_full_spec(a.shape) for a in args[3:]]
    if vmix is not None:
        v0, v1, v2, v_first = vmix
        extra = [row(v0), v1.astype(bf16), v2.astype(bf16)]
        args += extra + [v_first]
        in_specs += [_full_spec(a.shape) for a in extra] + [tile]
    out_bf = jax.ShapeDtypeStruct((B, S, D), bf16)
    out_shape = [out_bf, out_bf, out_bf, jax.ShapeDtypeStruct((B, S, D), f32), out_bf, out_bf, out_bf]
    return pl.pallas_call(
        functools.partial(_rwkv_pre_kernel, vmix is not None),
        name="rwkv_pre",
        grid=(B, S // tm),
        in_specs=in_specs,
        out_specs=[tile] * 7,
        out_shape=out_shape,
        compiler_params=_cparams("arbitrary", "arbitrary"),
    )(*args)


def _wkv_kernel(C, r_ref, k_ref, v_ref, wl_ref, kk_ref, b_ref, g_ref, rk_ref, gng_ref, gnb_ref,
                o_ref, state_ref):
    @pl.when(pl.program_id(1) == 0)
    def _():
        state_ref[...] = jnp.zeros_like(state_ref)

    T, D = r_ref.shape[1], r_ref.shape[2]
    N = HEAD_DIM
    P = D // LANES
    C2 = 2 * C
    ri = lax.broadcasted_iota(jnp.int32, (C, C), 0)
    ci = lax.broadcasted_iota(jnp.int32, (C, C), 1)
    tri_incl = jnp.where(ri >= ci, 1.0, 0.0).astype(f32)
    r2 = lax.broadcasted_iota(jnp.int32, (C2, C2), 0)
    c2 = lax.broadcasted_iota(jnp.int32, (C2, C2), 1)
    dlt = jnp.where(jnp.where(r2 >= C, 1, 0) == jnp.where(c2 >= C, 1, 0), r2 - c2, -1)
    strict = dlt > 0
    incl = dlt >= 0
    eye2 = jnp.where(r2 == c2, 1.0, 0.0).astype(f32)
    h0 = lax.broadcasted_iota(jnp.int32, (C, LANES), 1) < N
    rs = lax.broadcasted_iota(jnp.int32, (LANES, LANES), 0)
    cs = lax.broadcasted_iota(jnp.int32, (LANES, LANES), 1)
    sbd = jnp.where(rs >= N, 1, 0) == jnp.where(cs >= N, 1, 0)
    jbd = jnp.where(sbd, 1.0, 0.0).astype(bf16)
    stack2 = lambda t: jnp.concatenate([t, t], axis=0)

    lanes = [slice(p * LANES, (p + 1) * LANES) for p in range(P)]
    nc = T // C
    units = [(c, p) for c in range(nc) for p in range(P)]
    hi_lo = lambda t: (t.astype(bf16), (t - t.astype(bf16).astype(f32)).astype(bf16))

    ch = []
    for c in range(nc):
        sl = pl.ds(c * C, C)
        wl = wl_ref[0, sl, :]
        cum = jnp.dot(tri_incl, wl, preferred_element_type=f32, precision=HIGHEST)
        g_incl = jnp.exp(cum)
        g_inv = jnp.exp(-cum)
        g_last = g_incl[C - 1:C, :]
        r = r_ref[0, sl, :].astype(f32)
        k = k_ref[0, sl, :].astype(f32)
        btf = b_ref[0, sl, :].astype(f32) * g_inv
        ktf = k * g_inv
        ch.append(dict(
            sl=sl, g_last=g_last, v16=v_ref[0, sl, :],
            af=-kk_ref[0, sl, :].astype(f32) * jnp.exp(cum - wl), rf=r * g_incl,
            bt=btf.astype(bf16), kt=ktf.astype(bf16),
            bc=(btf * g_last).astype(bf16), kc=(ktf * g_last).astype(bf16),
            rkr=r * k * rk_ref[...]))

    Ls, Aak, Arow, Vst, X2 = [], [], [], [], []
    for c, p in units:
        d, lp = ch[c], lanes[p]
        afp, rfp = d["af"][:, lp], d["rf"][:, lp]
        X4 = jnp.concatenate([jnp.where(h0, afp, 0.0), jnp.where(h0, 0.0, afp),
                              jnp.where(h0, rfp, 0.0), jnp.where(h0, 0.0, rfp)], axis=0).astype(bf16)
        Mb = _dot_nt(X4, stack2(d["bt"][:, lp]))
        Mk = _dot_nt(X4, stack2(d["kt"][:, lp]))
        Ls.append(jnp.where(strict, Mb[:C2], 0.0))
        Aak.append(jnp.where(strict, Mk[:C2], 0.0).astype(bf16))
        Arow.append(jnp.concatenate([jnp.where(incl, Mb[C2:], 0.0), jnp.where(incl, Mk[C2:], 0.0)],
                                    axis=1).astype(bf16))
        Vst.append(stack2(d["v16"][:, lp]))
        X2.append(jnp.concatenate([afp, rfp], axis=0).astype(bf16))
    Ps = [eye2 + L for L in Ls]
    n = 1
    while 2 * n < C:
        Lb = [L.astype(bf16) for L in Ls]
        Ls = [_dot(x, x) for x in Lb]
        Ps = [Pm + _dot(L.astype(bf16), Pm.astype(bf16)) for L, Pm in zip(Ls, Ps)]
        n *= 2
    Ps = [Pm.astype(bf16) for Pm in Ps]
    AV = [_dot(a, vs) for a, vs in zip(Aak, Vst)]
    ys = []
    state = [state_ref[p] for p in range(P)]
    for c in range(nc):
        d = ch[c]
        us = range(c * P, (c + 1) * P)
        M2 = [_dot_nt(X2[u], state[u - c * P].astype(bf16)) for u in us]
        Ust = [_dot(Ps[u], (AV[u] + stack2(m2[:C])).astype(bf16)) for u, m2 in zip(us, M2)]
        Yst = [_dot(Arow[u], jnp.concatenate([ust.astype(bf16), Vst[u]], axis=0)) + stack2(m2[C:])
               for u, ust, m2 in zip(us, Ust, M2)]
        ys += [jnp.where(h0, yst[:C], yst[C:]) for yst in Yst]
        for p, ust in enumerate(Ust):
            lp = lanes[p]
            u16 = jnp.where(h0, ust[:C], ust[C:]).astype(bf16)
            upd = _dot_tn(jnp.concatenate([u16, d["v16"][:, lp]], axis=0),
                          jnp.concatenate([d["bc"][:, lp], d["kc"][:, lp]], axis=0))
            state[p] = state[p] * d["g_last"][:, lp] + jnp.where(sbd, upd, 0.0)
    for p in range(P):
        state_ref[p] = state[p]
    sums = []
    for u, (c, p) in enumerate(units):
        sums.append(_dot(jnp.concatenate(hi_lo(ys[u]) + hi_lo(ch[c]["rkr"][:, lanes[p]]), axis=0), jbd))
    yds = [ys[u] - (sums[u][:C] + sums[u][C:C2]) * (1.0 / N) for u in range(len(units))]
    var = [_dot(jnp.concatenate(hi_lo(yd * yd), axis=0), jbd) for yd in yds]
    for u, (c, p) in enumerate(units):
        d, lp = ch[c], lanes[p]
        yn = yds[u] * lax.rsqrt((var[u][:C] + var[u][C:]) * (1.0 / N) + GN_EPS)
        bonus = (sums[u][C2:C2 + C] + sums[u][C2 + C:]) * d["v16"][:, lp].astype(f32)
        z = (yn * gng_ref[:, lp] + gnb_ref[:, lp] + bonus) * g_ref[0, d["sl"], lp].astype(f32)
        o_ref[0, d["sl"], lp] = z.astype(bf16)


def _wkv7(r, k, v, wl, kk, b, g, r_k, gn_g, gn_b):
    B, S, D = r.shape
    T = min(WKV_TILE, S)
    C = min(WKV_CHUNK, T)
    tile = pl.BlockSpec((1, T, D), lambda bb, i: (bb, i, 0))
    vec = _full_spec((1, D))
    return pl.pallas_call(
        functools.partial(_wkv_kernel, C),
        name="wkv7",
        grid=(B, S // T),
        in_specs=[tile] * 7 + [vec] * 3,
        out_specs=tile,
        out_shape=jax.ShapeDtypeStruct((B, S, D), bf16),
        scratch_shapes=[pltpu.VMEM((D // LANES, LANES, LANES), f32)],
        compiler_params=_cparams("arbitrary", "arbitrary"),
    )(r, k, v, wl, kk, b, g, r_k.reshape(1, D), gn_g.reshape(1, D), gn_b.reshape(1, D))


def _proj_ln_kernel(gate_idx, z_ref, w_ref, x_ref, mod_ref, lng_ref, lnb_ref, o_ref):
    D = x_ref.shape[-1]
    gt = mod_ref[0, :, gate_idx * D:(gate_idx + 1) * D]
    y = _dot(z_ref[0], w_ref[...])
    u = DEEPNORM_ALPHA * x_ref[0] + (1.0 + gt) * y
    o_ref[0] = _layer_norm_rows(u, lng_ref[...], lnb_ref[...])


def _proj_ln(z, w, x, modl, gate_idx, ln_g, ln_b):
    B, S, D = x.shape
    tm = min(ROW_TILE, S)
    tile = pl.BlockSpec((1, tm, D), lambda b, i: (b, i, 0))
    return pl.pallas_call(
        functools.partial(_proj_ln_kernel, gate_idx),
        name="proj_ln",
        grid=(B, S // tm),
        in_specs=[tile, _full_spec(w.shape), tile,
                  pl.BlockSpec((1, 1, modl.shape[-1]), lambda b, i: (b, 0, 0)),
                  _full_spec((1, D)), _full_spec((1, D))],
        out_specs=tile,
        out_shape=jax.ShapeDtypeStruct((B, S, D), f32),
        compiler_params=_cparams("arbitrary", "arbitrary"),
    )(z, w.astype(bf16), x, modl, ln_g.reshape(1, D), ln_b.reshape(1, D))


def _route_kernel(x_ref, mod_ref, w_ref, wlo_ref, b_ref, h_ref, idx_ref, gate_ref, cnt_ref):
    @pl.when((pl.program_id(0) == 0) & (pl.program_id(1) == 0))
    def _():
        cnt_ref[...] = jnp.zeros_like(cnt_ref)

    D = x_ref.shape[-1]
    sh = mod_ref[0, :, 3 * D:4 * D]
    sc = mod_ref[0, :, 4 * D:5 * D]
    h = x_ref[0] * (1.0 + sc) + sh
    h_ref[0] = h.astype(bf16)
    h_hi = h.astype(bf16)
    h_lo = (h - h_hi.astype(f32)).astype(bf16)
    lg = _dot(h_hi, w_ref[...]) + _dot(h_hi, wlo_ref[...]) + _dot(h_lo, w_ref[...]) + b_ref[...]
    lane = lax.broadcasted_iota(jnp.int32, lg.shape, 1)
    glog = jnp.where(lane < N_GROUPS, lg, -jnp.inf)
    gmax = jnp.max(glog, axis=-1, keepdims=True)
    p_g = 1.0 / jnp.sum(jnp.exp(glog - gmax), axis=-1, keepdims=True)
    g_sel = jnp.min(jnp.where(glog == gmax, lane, LANES), axis=-1, keepdims=True)
    e_lo = N_GROUPS + EXP_PER_GROUP * g_sel
    elog = jnp.where((lane >= e_lo) & (lane < e_lo + EXP_PER_GROUP), lg, -jnp.inf)
    e1 = jnp.max(elog, axis=-1, keepdims=True)
    esum = jnp.sum(jnp.exp(elog - e1), axis=-1, keepdims=True)
    i1 = jnp.min(jnp.where(elog == e1, lane, LANES), axis=-1, keepdims=True)
    elog2 = jnp.where(lane == i1, -jnp.inf, elog)
    e2 = jnp.max(elog2, axis=-1, keepdims=True)
    i2 = jnp.min(jnp.where(elog2 == e2, lane, LANES), axis=-1, keepdims=True)
    p1 = 1.0 / esum
    p2 = jnp.exp(e2 - e1) / esum
    psum = p1 + p2
    gate_ref[0] = jnp.where(lane == 0, p_g * p1 / psum, jnp.where(lane == 1, p_g * p2 / psum, 0.0))
    tm = lg.shape[0]
    onehot = jnp.where(lane == i1, 1.0, 0.0) + jnp.where(lane == i2, 1.0, 0.0)
    ri = lax.broadcasted_iota(jnp.int32, (tm, tm), 0)
    ci = lax.broadcasted_iota(jnp.int32, (tm, tm), 1)
    before = _dot(jnp.where(ri > ci, 1.0, 0.0).astype(bf16), onehot.astype(bf16)) + cnt_ref[...]
    rank1 = jnp.sum(jnp.where(lane == i1, before, 0.0), axis=-1, keepdims=True).astype(jnp.int32)
    rank2 = jnp.sum(jnp.where(lane == i2, before, 0.0), axis=-1, keepdims=True).astype(jnp.int32)
    idx_ref[0] = jnp.where(lane == 0, i1 - N_GROUPS, jnp.where(lane == 1, i2 - N_GROUPS,
                           jnp.where(lane == 2, rank1, jnp.where(lane == 3, rank2, 0))))
    cnt_ref[...] = before[tm - 1:tm, :] + onehot[tm - 1:tm, :]


def _route(x, modl, w_grp, b_grp, w_exp, b_exp):
    B, S, D = x.shape
    tm = min(ROW_TILE, S)
    pad = LANES - N_GROUPS - N_EXPERTS
    w = jnp.concatenate([w_grp, w_exp, jnp.zeros((D, pad), f32)], axis=1)
    w_hi = w.astype(bf16)
    b = jnp.concatenate([b_grp, b_exp, jnp.zeros((pad,), f32)]).reshape(1, LANES)
    tile = pl.BlockSpec((1, tm, D), lambda bb, i: (bb, i, 0))
    small = pl.BlockSpec((1, tm, LANES), lambda bb, i: (bb, i, 0))
    return pl.pallas_call(
        _route_kernel,
        name="moe_route",
        grid=(B, S // tm),
        in_specs=[tile, pl.BlockSpec((1, 1, modl.shape[-1]), lambda bb, i: (bb, 0, 0)),
                  _full_spec(w.shape), _full_spec(w.shape), _full_spec(b.shape)],
        out_specs=[tile, small, small, _full_spec((1, LANES))],
        out_shape=[jax.ShapeDtypeStruct((B, S, D), bf16),
                   jax.ShapeDtypeStruct((B, S, LANES), jnp.int32),
                   jax.ShapeDtypeStruct((B, S, LANES), f32),
                   jax.ShapeDtypeStruct((1, LANES), f32)],
        compiler_params=_cparams("arbitrary", "arbitrary"),
    )(x, modl, w_hi, (w - w_hi.astype(f32)).astype(bf16), b)


def _expert_kernel(be_ref, nu_ref, x_ref, wg_ref, wu_ref, wd_ref, o_ref, wg16, wu16, wd16):
    i = pl.program_id(0)

    @pl.when((i == 0) | (be_ref[i] != be_ref[jnp.maximum(i - 1, 0)]))
    def _():
        wg16[...] = wg_ref[0, 0].astype(bf16)
        wu16[...] = wu_ref[0, 0].astype(bf16)
        wd16[...] = wd_ref[0, 0].astype(bf16)

    @pl.when(i < nu_ref[0])
    def _():
        x = x_ref[...]
        hg = _dot(x, wg16[...])
        hu = _dot(x, wu16[...])
        o_ref[...] = _dot((hg * _sigmoid(hg) * hu).astype(bf16), wd16[...]).astype(o_ref.dtype)

    @pl.when(i >= nu_ref[0])
    def _():
        o_ref[...] = jnp.zeros_like(o_ref)


def _expert_kernel_inplace(be_ref, nu_ref, x_ref, wg_ref, wu_ref, wd_ref, prev_ref, o_ref, wg16, wu16, wd16):
    del prev_ref
    _expert_kernel(be_ref, nu_ref, x_ref, wg_ref, wu_ref, wd_ref, o_ref, wg16, wu16, wd16)


def _expert_ffn(xs, block_exp, n_used, wg, wu, wd, layer, n_slots, block_offset, ys_prev=None):
    D = xs.shape[1]
    F = wg.shape[-1]
    blk = MOE_BLOCK
    row_map = lambda i, be, nu, *_: (jnp.maximum(jnp.minimum(i, nu[0] - 1), 0), 0)
    w_map = lambda i, be, nu, *_: (layer, be[i], 0, 0)
    in_specs = [
        pl.BlockSpec((blk, D), row_map),
        pl.BlockSpec((1, 1, D, F), w_map),
        pl.BlockSpec((1, 1, D, F), w_map),
        pl.BlockSpec((1, 1, F, D), w_map),
    ]
    args = [block_exp, n_used, xs, wg, wu, wd]
    aliases = {}
    if ys_prev is not None:
        in_specs.append(pl.BlockSpec(memory_space=pl.ANY))
        args.append(ys_prev)
        aliases = {len(args) - 1: 0}
    grid_spec = pltpu.PrefetchScalarGridSpec(
        num_scalar_prefetch=2,
        grid=(block_exp.shape[0],),
        in_specs=in_specs,
        out_specs=pl.BlockSpec((blk, D), lambda i, be, nu, *_: (i + block_offset, 0)),
        scratch_shapes=[pltpu.VMEM((D, F), bf16), pltpu.VMEM((D, F), bf16), pltpu.VMEM((F, D), bf16)],
    )
    return pl.pallas_call(
        _expert_kernel if ys_prev is None else _expert_kernel_inplace,
        name="moe_experts",
        grid_spec=grid_spec,
        out_shape=jax.ShapeDtypeStruct((n_slots, D), bf16),
        input_output_aliases=aliases,
        compiler_params=_cparams("arbitrary"),
    )(*args)


def _take_rows(table, idx):
    return table.at[idx].get(mode="promise_in_bounds")


def _dispatch(expert_idx, rank, counts, blk):
    T = expert_idx.shape[0]
    A = T * TOP_K
    n_blocks = A // blk + N_EXPERTS
    experts = jnp.arange(N_EXPERTS, dtype=jnp.int32)
    padded = (counts + blk - 1) // blk * blk
    pad_end = jnp.cumsum(padded)
    pad_start = pad_end - padded
    start = jnp.cumsum(counts) - counts
    dest = rank + jnp.sum(jnp.where(expert_idx[..., None] == experts, pad_start, 0), axis=-1)
    block_exp = jnp.minimum(jnp.sum(((jnp.arange(n_blocks, dtype=jnp.int32) * blk)[:, None] >= pad_end[None, :])
                                    .astype(jnp.int32), axis=1), N_EXPERTS - 1)
    order = jnp.argsort(expert_idx.reshape(A)).astype(jnp.int32)
    shift = jnp.sum(jnp.where(block_exp[:, None] == experts, start - pad_start, 0), axis=-1)
    pos = jnp.arange(n_blocks * blk, dtype=jnp.int32) + jnp.repeat(shift, blk)
    slot_tok = _take_rows(order, jnp.clip(pos, 0, A - 1)) // TOP_K
    n_used = (pad_end[-1] // blk).astype(jnp.int32).reshape(1)
    return dest, slot_tok, block_exp, n_used


def _combine_ln_kernel(x_ref, y0_ref, y1_ref, gate_ref, mod_ref, lng_ref, lnb_ref, o_ref):
    D = x_ref.shape[-1]
    gt = mod_ref[0, :, 5 * D:6 * D]
    gates = gate_ref[0]
    y = gates[:, 0:1] * y0_ref[0].astype(f32) + gates[:, 1:2] * y1_ref[0].astype(f32)
    u = DEEPNORM_ALPHA * x_ref[0] + (1.0 + gt) * y
    o_ref[0] = _layer_norm_rows(u, lng_ref[...], lnb_ref[...])


def _combine_ln(x, y0, y1, gates, modl, ln_g, ln_b):
    B, S, D = x.shape
    tm = min(ROW_TILE, S)
    tile = pl.BlockSpec((1, tm, D), lambda b, i: (b, i, 0))
    return pl.pallas_call(
        _combine_ln_kernel,
        name="moe_combine_ln",
        grid=(B, S // tm),
        in_specs=[tile, tile, tile, pl.BlockSpec((1, tm, LANES), lambda b, i: (b, i, 0)),
                  pl.BlockSpec((1, 1, modl.shape[-1]), lambda b, i: (b, 0, 0)),
                  _full_spec((1, D)), _full_spec((1, D))],
        out_specs=tile,
        out_shape=jax.ShapeDtypeStruct((B, S, D), f32),
        compiler_params=_cparams("arbitrary", "arbitrary"),
    )(x, y0, y1, gates, modl, ln_g.reshape(1, D), ln_b.reshape(1, D))


def _moe_layer(x, modl, w_grp, b_grp, w_exp, b_exp, wg, wu, wd, layer, ln_g, ln_b):
    B, S, D = x.shape
    T = B * S
    hb, idx, gates, cnt = _route(x, modl, w_grp, b_grp, w_exp, b_exp)
    idx = idx.reshape(T, LANES)
    counts = cnt[0, N_GROUPS:N_GROUPS + N_EXPERTS].astype(jnp.int32)
    dest, slot_tok, block_exp, n_used = _dispatch(idx[:, :TOP_K], idx[:, TOP_K:2 * TOP_K], counts, MOE_BLOCK)
    hb = hb.reshape(T, D)
    nb = block_exp.shape[0]
    half = nb // 2
    cut = half * MOE_BLOCK
    ys = _expert_ffn(_take_rows(hb, slot_tok[:cut]), block_exp[:half], jnp.minimum(n_used, half),
                     wg, wu, wd, layer, nb * MOE_BLOCK, 0)
    ys = _expert_ffn(_take_rows(hb, slot_tok[cut:]), block_exp[half:], jnp.clip(n_used - half, 0, nb - half),
                     wg, wu, wd, layer, nb * MOE_BLOCK, half, ys_prev=ys)
    y0 = _take_rows(ys, dest[:, 0]).reshape(B, S, D)
    y1 = _take_rows(ys, dest[:, 1]).reshape(B, S, D)
    return _combine_ln(x, y0, y1, gates, modl, ln_g, ln_b)


def _shared_kv_kernel(x_ref, mod_ref, wk_ref, wvt_ref, wf_ref, wflo_ref, fb_ref, kn_ref, place_ref, ones_ref,
                      k_ref, kb_ref, vt_ref, f_ref, carry_ref):
    @pl.when(pl.program_id(1) == 0)
    def _():
        carry_ref[...] = jnp.zeros_like(carry_ref)

    D = x_ref.shape[-1]
    tm = x_ref.shape[1]
    shift = mod_ref[0, :, 0:D]
    scale = mod_ref[0, :, D:2 * D]
    hk = x_ref[0] * (1.0 + scale) + shift
    hb = hk.astype(bf16)
    k = _dot(hb, wk_ref[...])
    vt_ref[0] = _dot_nt(wvt_ref[...], hb).astype(bf16)
    h_lo = (hk - hb.astype(f32)).astype(bf16)
    f = _dot(hb, wf_ref[...]) + _dot(hb, wflo_ref[...]) + _dot(h_lo, wf_ref[...]) + fb_ref[...]
    log_f = jnp.minimum(f, 0.0) - jnp.log(1.0 + jnp.exp(-jnp.abs(f)))
    row = lax.broadcasted_iota(jnp.int32, log_f.shape, 0)
    acc = log_f
    d = 1
    while d < tm:
        acc = acc + jnp.where(row >= d, pltpu.roll(acc, d, axis=0), 0.0)
        d *= 2
    acc = acc + carry_ref[...]
    f_ref[0] = acc
    carry_ref[...] = acc[tm - 1:tm, :]
    kb_ref[0] = (_dot(_bias_split(acc * LOG2E), place_ref[...]) + ones_ref[...]).astype(bf16)
    jbd = _pair_ones()
    for p in range(D // LANES):
        lp = slice(p * LANES, (p + 1) * LANES)
        seg = k[:, lp]
        ms = _head_sum(seg * seg, jbd) * (1.0 / HEAD_DIM)
        k_ref[0, :, lp] = (seg * lax.rsqrt(ms + QK_EPS) * kn_ref[...]).astype(bf16)


def _shared_kv(x, kvmod, w_kvf, b_f, k_norm):
    B, S, D = x.shape
    H = D // HEAD_DIM
    tm = min(ROW_TILE, S)
    wk = w_kvf[:, :D].astype(bf16)
    wvt = w_kvf[:, D:2 * D].T.astype(bf16)
    wf = jnp.concatenate([w_kvf[:, 2 * D:], jnp.zeros((D, LANES - H), f32)], axis=1)
    wf_hi = wf.astype(bf16)
    fb = jnp.concatenate([b_f, jnp.zeros((LANES - H,), f32)]).reshape(1, LANES)
    place, ones = _bias_placement(D, True)
    tile = pl.BlockSpec((1, tm, D), lambda b, i: (b, i, 0))
    act = jax.ShapeDtypeStruct((B, S, D), bf16)
    return pl.pallas_call(
        _shared_kv_kernel,
        name="shared_kv",
        grid=(B, S // tm),
        in_specs=[tile, pl.BlockSpec((1, 1, 2 * D), lambda b, i: (b, 0, 0)),
                  _full_spec(wk.shape), _full_spec(wvt.shape), _full_spec(wf.shape), _full_spec(wf.shape),
                  _full_spec(fb.shape), _full_spec((1, LANES)), _full_spec(place.shape), _full_spec(ones.shape)],
        out_specs=[tile, tile, pl.BlockSpec((1, D, tm), lambda b, i: (b, 0, i)),
                   pl.BlockSpec((1, tm, LANES), lambda b, i: (b, i, 0))],
        out_shape=[act, act, jax.ShapeDtypeStruct((B, D, S), bf16), jax.ShapeDtypeStruct((B, S, LANES), f32)],
        scratch_shapes=[pltpu.VMEM((1, LANES), f32)],
        compiler_params=_cparams("arbitrary", "arbitrary"),
    )(x, kvmod, wk, wvt, wf_hi, (wf - wf_hi.astype(f32)).astype(bf16), fb,
      jnp.tile(k_norm, LANES // HEAD_DIM).reshape(1, LANES), place, ones)


def _fox_pre_kernel(x_ref, mod_ref, wq_ref, wg_ref, qn_ref, f_ref, place_ref, ones_ref, q_ref, qb_ref, gate_ref):
    D = x_ref.shape[-1]
    sh = mod_ref[0, :, 0:D]
    sc = mod_ref[0, :, D:2 * D]
    hb = (x_ref[0] * (1.0 + sc) + sh).astype(bf16)
    q = _dot(hb, wq_ref[...])
    gate_ref[0] = _sigmoid(_dot(hb, wg_ref[...])).astype(bf16)
    qb_ref[0] = (_dot(_bias_split(f_ref[0] * LOG2E), place_ref[...]) + ones_ref[...]).astype(bf16)
    qscale = HEAD_DIM ** -0.5 * LOG2E
    jbd = _pair_ones()
    for p in range(D // LANES):
        lp = slice(p * LANES, (p + 1) * LANES)
        seg = q[:, lp]
        ms = _head_sum(seg * seg, jbd) * (1.0 / HEAD_DIM)
        q_ref[0, :, lp] = (seg * lax.rsqrt(ms + QK_EPS) * (qn_ref[...] * qscale)).astype(bf16)


def _fox_pre(x, modl, w_qg, q_norm, fcum):
    B, S, D = x.shape
    tm = min(ROW_TILE, S)
    tile = pl.BlockSpec((1, tm, D), lambda b, i: (b, i, 0))
    wq = w_qg[:, :D].astype(bf16)
    wg = w_qg[:, D:].astype(bf16)
    place, ones = _bias_placement(D, False)
    act = jax.ShapeDtypeStruct((B, S, D), bf16)
    return pl.pallas_call(
        _fox_pre_kernel,
        name="fox_pre",
        grid=(B, S // tm),
        in_specs=[tile, pl.BlockSpec((1, 1, modl.shape[-1]), lambda b, i: (b, 0, 0)),
                  _full_spec(wq.shape), _full_spec(wg.shape), _full_spec((1, LANES)),
                  pl.BlockSpec((1, tm, LANES), lambda b, i: (b, i, 0)),
                  _full_spec(place.shape), _full_spec(ones.shape)],
        out_specs=[tile, tile, tile],
        out_shape=[act, act, act],
        compiler_params=_cparams("arbitrary", "arbitrary"),
    )(x, modl, wq, wg, jnp.tile(q_norm, LANES // HEAD_DIM).reshape(1, LANES), fcum, place, ones)


def _fox_attn_kernel(qi_ref, kj_ref, flag_ref, q_ref, qb_ref, k_ref, kb_ref, vt_ref, gate_ref, o_ref,
                     qm_ref, m_ref, l_ref, acc_ref):
    s = pl.program_id(2)
    i = qi_ref[s]
    j = kj_ref[s]
    flags = flag_ref[s]
    N = HEAD_DIM
    nh = q_ref.shape[2] // N
    hpp = LANES // N
    tk, tq = k_ref.shape[1], q_ref.shape[1]
    pair = lambda h: slice((h // hpp) * LANES, (h // hpp + 1) * LANES)

    @pl.when(j == 0)
    def _():
        m_ref[...] = jnp.full_like(m_ref, NEG_BIG)
        l_ref[...] = jnp.zeros_like(l_ref)
        acc_ref[...] = jnp.zeros_like(acc_ref)
        for hh in range(nh):
            qcat = jnp.concatenate([q_ref[0, :, pair(hh)], qb_ref[0, :, pair(hh)]], axis=1).astype(f32)
            head = (lax.broadcasted_iota(jnp.int32, qcat.shape, 1) % LANES) // N
            qm_ref[hh] = jnp.where(head == hh % hpp, qcat, 0.0).astype(bf16)

    def step(masked, q_lo):
        qs = slice(q_lo, tq)
        kcat = [jnp.concatenate([k_ref[0, :, pair(hh)], kb_ref[0, :, pair(hh)]], axis=1)
                for hh in range(0, nh, hpp)]
        sts = [_dot_nt(kcat[hh // hpp], qm_ref[hh, qs, :]) for hh in range(nh)]
        for hh, st in enumerate(sts):
            if masked:
                key = j * tk + lax.broadcasted_iota(jnp.int32, st.shape, 0)
                qry = i * tq + q_lo + lax.broadcasted_iota(jnp.int32, st.shape, 1)
                st = jnp.where(key <= qry, st, -jnp.inf)
            m_prev = m_ref[hh, :, qs]
            m_new = jnp.maximum(m_prev, jnp.max(st, axis=0, keepdims=True))
            alpha = jnp.exp2(m_prev - m_new)
            p = jnp.exp2(st - m_new)
            l_ref[hh, :, qs] = alpha * l_ref[hh, :, qs] + jnp.sum(p, axis=0, keepdims=True)
            acc_ref[hh, :, qs] = alpha * acc_ref[hh, :, qs] + _dot(vt_ref[0, hh * N:(hh + 1) * N, :], p.astype(bf16))
            m_ref[hh, :, qs] = m_new

    @pl.when((flags & (ATTN_FLAG_MASK | ATTN_FLAG_UPPER)) == 0)
    def _():
        step(False, 0)

    @pl.when((flags & (ATTN_FLAG_MASK | ATTN_FLAG_UPPER)) == ATTN_FLAG_MASK)
    def _():
        step(True, 0)

    @pl.when((flags & ATTN_FLAG_UPPER) != 0)
    def _():
        step(True, tq // 2)

    @pl.when((flags & ATTN_FLAG_LAST) != 0)
    def _():
        for pr in range(nh // hpp):
            ot = jnp.concatenate([acc_ref[hh] * (1.0 / l_ref[hh]) for hh in range(pr * hpp, (pr + 1) * hpp)],
                                 axis=0)
            lp = slice(pr * LANES, (pr + 1) * LANES)
            o_ref[0, :, lp] = (ot.T * gate_ref[0, :, lp].astype(f32)).astype(bf16)


def _fox_attn(q, qb, k, kb, vt, gate):
    B, S, D = gate.shape
    tq = min(ATTN_Q_TILE, S)
    tk = min(ATTN_K_TILE, S)
    LW = ATTN_PAIRS * LANES
    nh = LW // HEAD_DIM
    qi, kj, flags = [], [], []
    for i in range(S // tq):
        last = ((i + 1) * tq - 1) // tk
        for j in range(last + 1):
            qi.append(i)
            kj.append(j)
            needs_mask = (j + 1) * tk - 1 > i * tq
            upper_only = j * tk >= i * tq + tq // 2
            flags.append((ATTN_FLAG_MASK if needs_mask else 0) | (ATTN_FLAG_LAST if j == last else 0)
                         | (ATTN_FLAG_UPPER if upper_only else 0))
    tables = [jnp.asarray(t, jnp.int32) for t in (qi, kj, flags)]
    qtile = pl.BlockSpec((1, tq, LW), lambda b, p, s, qi, kj, fl: (b, qi[s], p))
    ktile = pl.BlockSpec((1, tk, LW), lambda b, p, s, qi, kj, fl: (b, kj[s], p))
    grid_spec = pltpu.PrefetchScalarGridSpec(
        num_scalar_prefetch=3,
        grid=(B, D // LW, len(qi)),
        in_specs=[qtile, qtile, ktile, ktile,
                  pl.BlockSpec((1, LW, tk), lambda b, p, s, qi, kj, fl: (b, p, kj[s])),
                  qtile],
        out_specs=qtile,
        scratch_shapes=[pltpu.VMEM((nh, tq, 2 * LANES), bf16),
                        pltpu.VMEM((nh, 1, tq), f32), pltpu.VMEM((nh, 1, tq), f32),
                        pltpu.VMEM((nh, HEAD_DIM, tq), f32)],
    )
    return pl.pallas_call(
        _fox_attn_kernel,
        name="fox_attn",
        grid_spec=grid_spec,
        out_shape=jax.ShapeDtypeStruct((B, S, D), bf16),
        compiler_params=_cparams("arbitrary", "arbitrary", "arbitrary"),
    )(*tables, q, qb, k, kb, vt, gate)


def kernel(x, c, ada_w, ada_b, ln_g, ln_b, rw_mu, rw_rkv, rw_w0, rw_w1, rw_w2, rw_a0, rw_a1, rw_a2, rw_g1, rw_g2, rw_kk, rw_ka, rw_rk, rw_gn_g, rw_gn_b, rw_wo, rw_v0, rw_v1, rw_v2, kv_ada_w, kv_ada_b, kv_w, kv_fb, kv_knorm, fx_wqg, fx_qnorm, fx_wo, moe_wgrp, moe_bgrp, moe_wexp, moe_bexp, moe_wgate, moe_wup, moe_wdown):
    B, S, D = x.shape
    depth = ada_w.shape[0]
    n_a = rw_mu.shape[0]
    mod = _adaln_mod(c, ada_w, ada_b)
    kvmod = _adaln_mod(c, kv_ada_w[None], kv_ada_b[None])[0].reshape(B, 1, 2 * D)
    kv = None
    v_first = None
    for l in range(depth):
        modl = mod[l].reshape(B, 1, 6 * D)
        if l < n_a:
            vmix = None if l == 0 else (rw_v0[l - 1], rw_v1[l - 1], rw_v2[l - 1], v_first)
            r, k, v, wl, kk, b, g = _rwkv_pre(x, modl, rw_mu[l], rw_rkv[l], rw_w0[l], rw_w1[l], rw_w2[l],
                                              rw_a0[l], rw_a1[l], rw_a2[l], rw_g1[l], rw_g2[l],
                                              rw_kk[l], rw_ka[l], vmix)
            if l == 0:
                v_first = v
            z = _wkv7(r, k, v, wl, kk, b, g, rw_rk[l].reshape(D), rw_gn_g[l], rw_gn_b[l])
            w_o = rw_wo[l]
        else:
            j = l - n_a
            k, kb, vt, fcum = kv
            q, qb, gate = _fox_pre(x, modl, fx_wqg[j], fx_qnorm[j], fcum)
            z = _fox_attn(q, qb, k, kb, vt, gate)
            w_o = fx_wo[j]
        x = _proj_ln(z, w_o, x, modl, 2, ln_g[l, 0], ln_b[l, 0])
        x = _moe_layer(x, modl, moe_wgrp[l], moe_bgrp[l], moe_wexp[l], moe_bexp[l],
                       moe_wgate, moe_wup, moe_wdown, l, ln_g[l, 1], ln_b[l, 1])
        if l == n_a - 1:
            kv = _shared_kv(x, kvmod, kv_w, kv_fb, kv_knorm)
    return x
```

```python
import functools
import math

import jax
import jax.numpy as jnp
import numpy as np
from jax import lax
from jax.experimental import pallas as pl
from jax.experimental.pallas import tpu as pltpu

HEAD_DIM = 64
N_GROUPS = 4
EXP_PER_GROUP = 8
N_EXPERTS = N_GROUPS * EXP_PER_GROUP
TOP_K = 2
DEPTH = 4
N_A = 2
DEEPNORM_ALPHA = (2 * DEPTH) ** 0.25
LN_EPS = 1e-5
GN_EPS = 64e-5
QK_EPS = 1e-6

LANES = 128
VMEM_LIMIT_BYTES = 56 * 1024 * 1024
WKV_CHUNK = 64
WKV_TILE = 128
ROW_TILE = 512
RWKV_PRE_TILE = 512
ATTN_Q_TILE = 1024
ATTN_K_TILE = 512
ATTN_PAIRS = 2
ATTN_FLAG_MASK = 1
ATTN_FLAG_LAST = 2
ATTN_FLAG_UPPER = 4
MOE_BLOCK = 256
MOE_TABLE_PARTS = 1
BATCH_CHAINS = 2
NEG_BIG = -1e30
LOG2E = math.log2(math.e)

f32 = jnp.float32
bf16 = jnp.bfloat16
HIGHEST = lax.Precision.HIGHEST


def _cparams(*sem):
    return pltpu.CompilerParams(dimension_semantics=sem, vmem_limit_bytes=VMEM_LIMIT_BYTES)


def _sigmoid(x):
    return 1.0 / (1.0 + jnp.exp(-x))


def _dot(a, b):
    return jnp.dot(a, b, preferred_element_type=f32)


def _dot_nt(a, b):
    return lax.dot_general(a, b, (((1,), (1,)), ((), ())), preferred_element_type=f32)


def _dot_tn(a, b):
    return lax.dot_general(a, b, (((0,), (0,)), ((), ())), preferred_element_type=f32)


def _full_spec(shape):
    n = len(shape)
    return pl.BlockSpec(shape, lambda *_: (0,) * n)


def _pair_ones():
    rs = lax.broadcasted_iota(jnp.int32, (LANES, LANES), 0)
    cs = lax.broadcasted_iota(jnp.int32, (LANES, LANES), 1)
    return jnp.where(jnp.where(rs >= HEAD_DIM, 1, 0) == jnp.where(cs >= HEAD_DIM, 1, 0), 1.0, 0.0).astype(bf16)


def _head_sum(x, jbd):
    xh = x.astype(bf16)
    xl = (x - xh.astype(f32)).astype(bf16)
    return _dot(xh, jbd) + _dot(xl, jbd)


def _bias_split(f):
    t0 = f.astype(bf16)
    r1 = f - t0.astype(f32)
    t1 = r1.astype(bf16)
    t2 = (r1 - t1.astype(f32)).astype(bf16)
    return jnp.concatenate([t0, t1, t2], axis=1)


def _bias_placement(D, is_key):
    H = D // HEAD_DIM
    place = np.zeros((3 * LANES, D), np.float32)
    ones = np.zeros((1, D), np.float32)
    for h in range(H):
        for i in range(3):
            place[i * LANES + h, h * HEAD_DIM + (3 + i if is_key else i)] = -1.0 if is_key else 1.0
            ones[0, h * HEAD_DIM + (i if is_key else 3 + i)] = 1.0
    return jnp.asarray(place, bf16), jnp.asarray(ones, f32)


def _layer_norm_rows(u, g, b):
    mu = jnp.mean(u, axis=-1, keepdims=True)
    d = u - mu
    var = jnp.mean(d * d, axis=-1, keepdims=True)
    return d * lax.rsqrt(var + LN_EPS) * g + b


def _mod_kernel(c_ref, w_ref, b_ref, o_ref):
    c = c_ref[...]
    cs = c * _sigmoid(c)
    o_ref[0] = jnp.dot(cs, w_ref[0], preferred_element_type=f32, precision=HIGHEST) + b_ref[0]


def _adaln_mod(c, w, b):
    L, D, N = w.shape
    B = c.shape[0]
    tn = min(N, 1024)
    return pl.pallas_call(
        _mod_kernel,
        name="adaln_mod",
        grid=(L, N // tn),
        in_specs=[
            _full_spec((B, D)),
            pl.BlockSpec((1, D, tn), lambda l, j: (l, 0, j)),
            pl.BlockSpec((1, 1, tn), lambda l, j: (l, 0, j)),
        ],
        out_specs=pl.BlockSpec((1, B, tn), lambda l, j: (l, 0, j)),
        out_shape=jax.ShapeDtypeStruct((L, B, N), f32),
        compiler_params=_cparams("arbitrary", "arbitrary"),
    )(c, w, b.reshape(L, 1, N))


def _rwkv_pre_kernel(has_vmix, x_ref, xp_ref, mod_ref, mu_ref, wrkv_ref, w0_ref, w1_ref, w2_ref,
                     a0_ref, a1_ref, a2_ref, g1_ref, g2_ref, kkp_ref, kap_ref, *rest):
    if has_vmix:
        v0_ref, v1_ref, v2_ref, vf_ref = rest[:4]
        rest = rest[4:]
    r_ref, k_ref, v_ref, wl_ref, kk_ref, b_ref, g_ref = rest
    D = x_ref.shape[-1]
    sh = mod_ref[0, :, 0:D]
    sc = mod_ref[0, :, D:2 * D]
    h = x_ref[0] * (1.0 + sc) + sh
    hp = xp_ref[0][7:8, :] * (1.0 + sc) + sh
    hp = jnp.where(pl.program_id(1) == 0, 0.0, hp)
    row = lax.broadcasted_iota(jnp.int32, h.shape, 0)
    h_prev = jnp.where(row == 0, hp, pltpu.roll(h, 1, axis=0))
    xx = h_prev - h
    mix = lambda j: (h + xx * mu_ref[j:j + 1, :]).astype(bf16)
    xr, xw, xk, xv, xa, xg = [mix(j) for j in range(6)]
    r = _dot(xr, wrkv_ref[0])
    k = _dot(xk, wrkv_ref[1])
    v = _dot(xv, wrkv_ref[2])
    zw = w0_ref[...] + _dot(jnp.tanh(_dot(xw, w1_ref[...])).astype(bf16), w2_ref[...])
    wl_ref[0] = -math.exp(-0.5) * _sigmoid(zw)
    a = _sigmoid(a0_ref[...] + _dot(_dot(xa, a1_ref[...]).astype(bf16), a2_ref[...]))
    g = _dot(_sigmoid(_dot(xg, g1_ref[...])).astype(bf16), g2_ref[...])
    if has_vmix:
        vmix = _sigmoid(v0_ref[...] + _dot(_dot(xv, v1_ref[...]).astype(bf16), v2_ref[...]))
        v = v + (vf_ref[0].astype(f32) - v) * vmix
    r_ref[0] = r.astype(bf16)
    v_ref[0] = v.astype(bf16)
    g_ref[0] = g.astype(bf16)
    k_ref[0] = (k * (1.0 + (a - 1.0) * kap_ref[...])).astype(bf16)
    kk = k * kkp_ref[...]
    jbd = _pair_ones()
    for p in range(D // LANES):
        lp = slice(p * LANES, (p + 1) * LANES)
        seg = kk[:, lp]
        nrm = jnp.sqrt(_head_sum(seg * seg, jbd))
        seg = seg / jnp.maximum(nrm, 1e-12)
        kk_ref[0, :, lp] = seg.astype(bf16)
        b_ref[0, :, lp] = (seg * a[:, lp]).astype(bf16)


def _rwkv_pre(x, modl, mu, wrkv, w0, w1, w2, a0, a1, a2, g1, g2, kkp, kap, vmix):
    B, S, D = x.shape
    tm = min(RWKV_PRE_TILE, S)
    row = lambda t: t.reshape(1, D)
    args = [x, x, modl, mu, wrkv.astype(bf16), row(w0), w1.astype(bf16), w2.astype(bf16), row(a0),
            a1.astype(bf16), a2.astype(bf16), g1.astype(bf16), g2.astype(bf16), row(kkp), row(kap)]
    tile = pl.BlockSpec((1, tm, D), lambda b, i: (b, i, 0))
    in_specs = [
        tile,
        pl.BlockSpec((1, 8, D), lambda b, i: (b, jnp.maximum(i * (tm // 8) - 1, 0), 0)),
        pl.BlockSpec((1, 1, modl.shape[-1]), lambda b, i: (b, 0, 0)),
    ] + [_full_spec(a.shape) for a in args[3:]]
    if vmix is not None:
        v0, v1, v2, v_first = vmix
        extra = [row(v0), v1.astype(bf16), v2.astype(bf16)]
        args += extra + [v_first]
        in_specs += [_full_spec(a.shape) for a in extra] + [tile]
    out_bf = jax.ShapeDtypeStruct((B, S, D), bf16)
    out_shape = [out_bf, out_bf, out_bf, jax.ShapeDtypeStruct((B, S, D), f32), out_bf, out_bf, out_bf]
    return pl.pallas_call(
        functools.partial(_rwkv_pre_kernel, vmix is not None),
        name="rwkv_pre",
        grid=(B, S // tm),
        in_specs=in_specs,
        out_specs=[tile] * 7,
        out_shape=out_shape,
        compiler_params=_cparams("arbitrary", "arbitrary"),
    )(*args)


def _wkv_kernel(C, r_ref, k_ref, v_ref, wl_ref, kk_ref, b_ref, g_ref, rk_ref, gng_ref, gnb_ref,
                o_ref, state_ref):
    @pl.when(pl.program_id(1) == 0)
    def _():
        state_ref[...] = jnp.zeros_like(state_ref)

    T, D = r_ref.shape[1], r_ref.shape[2]
    N = HEAD_DIM
    P = D // LANES
    C2 = 2 * C
    ri = lax.broadcasted_iota(jnp.int32, (C, C), 0)
    ci = lax.broadcasted_iota(jnp.int32, (C, C), 1)
    tri_incl = jnp.where(ri >= ci, 1.0, 0.0).astype(f32)
    r2 = lax.broadcasted_iota(jnp.int32, (C2, C2), 0)
    c2 = lax.broadcasted_iota(jnp.int32, (C2, C2), 1)
    dlt = jnp.where(jnp.where(r2 >= C, 1, 0) == jnp.where(c2 >= C, 1, 0), r2 - c2, -1)
    strict = dlt > 0
    incl = dlt >= 0
    eye2 = jnp.where(r2 == c2, 1.0, 0.0).astype(f32)
    h0 = lax.broadcasted_iota(jnp.int32, (C, LANES), 1) < N
    rs = lax.broadcasted_iota(jnp.int32, (LANES, LANES), 0)
    cs = lax.broadcasted_iota(jnp.int32, (LANES, LANES), 1)
    sbd = jnp.where(rs >= N, 1, 0) == jnp.where(cs >= N, 1, 0)
    jbd = jnp.where(sbd, 1.0, 0.0).astype(bf16)
    stack2 = lambda t: jnp.concatenate([t, t], axis=0)

    lanes = [slice(p * LANES, (p + 1) * LANES) for p in range(P)]
    nc = T // C
    units = [(c, p) for c in range(nc) for p in range(P)]
    hi_lo = lambda t: (t.astype(bf16), (t - t.astype(bf16).astype(f32)).astype(bf16))

    ch = []
    for c in range(nc):
        sl = pl.ds(c * C, C)
        wl = wl_ref[0, sl, :]
        cum = jnp.dot(tri_incl, wl, preferred_element_type=f32, precision=HIGHEST)
        g_incl = jnp.exp(cum)
        g_inv = jnp.exp(-cum)
        g_last = g_incl[C - 1:C, :]
        r = r_ref[0, sl, :].astype(f32)
        k = k_ref[0, sl, :].astype(f32)
        btf = b_ref[0, sl, :].astype(f32) * g_inv
        ktf = k * g_inv
        ch.append(dict(
            sl=sl, g_last=g_last, v16=v_ref[0, sl, :],
            af=-kk_ref[0, sl, :].astype(f32) * jnp.exp(cum - wl), rf=r * g_incl,
            bt=btf.astype(bf16), kt=ktf.astype(bf16),
            bc=(btf * g_last).astype(bf16), kc=(ktf * g_last).astype(bf16),
            rkr=r * k * rk_ref[...]))

    Ls, Aak, Arow, Vst, X2 = [], [], [], [], []
    for c, p in units:
        d, lp = ch[c], lanes[p]
        afp, rfp = d["af"][:, lp], d["rf"][:, lp]
        X4 = jnp.concatenate([jnp.where(h0, afp, 0.0), jnp.where(h0, 0.0, afp),
                              jnp.where(h0, rfp, 0.0), jnp.where(h0, 0.0, rfp)], axis=0).astype(bf16)
        Mb = _dot_nt(X4, stack2(d["bt"][:, lp]))
        Mk = _dot_nt(X4, stack2(d["kt"][:, lp]))
        Ls.append(jnp.where(strict, Mb[:C2], 0.0))
        Aak.append(jnp.where(strict, Mk[:C2], 0.0).astype(bf16))
        Arow.append(jnp.concatenate([jnp.where(incl, Mb[C2:], 0.0), jnp.where(incl, Mk[C2:], 0.0)],
                                    axis=1).astype(bf16))
        Vst.append(stack2(d["v16"][:, lp]))
        X2.append(jnp.concatenate([afp, rfp], axis=0).astype(bf16))
    Ps = [eye2 + L for L in Ls]
    n = 1
    while 2 * n < C:
        Lb = [L.astype(bf16) for L in Ls]
        Ls = [_dot(x, x) for x in Lb]
        Ps = [Pm + _dot(L.astype(bf16), Pm.astype(bf16)) for L, Pm in zip(Ls, Ps)]
        n *= 2
    Ps = [Pm.astype(bf16) for Pm in Ps]
    AV = [_dot(a, vs) for a, vs in zip(Aak, Vst)]
    ys = []
    state = [state_ref[p] for p in range(P)]
    for c in range(nc):
        d = ch[c]
        us = range(c * P, (c + 1) * P)
        M2 = [_dot_nt(X2[u], state[u - c * P].astype(bf16)) for u in us]
        Ust = [_dot(Ps[u], (AV[u] + stack2(m2[:C])).astype(bf16)) for u, m2 in zip(us, M2)]
        Yst = [_dot(Arow[u], jnp.concatenate([ust.astype(bf16), Vst[u]], axis=0)) + stack2(m2[C:])
               for u, ust, m2 in zip(us, Ust, M2)]
        ys += [jnp.where(h0, yst[:C], yst[C:]) for yst in Yst]
        for p, ust in enumerate(Ust):
            lp = lanes[p]
            u16 = jnp.where(h0, ust[:C], ust[C:]).astype(bf16)
            upd = _dot_tn(jnp.concatenate([u16, d["v16"][:, lp]], axis=0),
                          jnp.concatenate([d["bc"][:, lp], d["kc"][:, lp]], axis=0))
            state[p] = state[p] * d["g_last"][:, lp] + jnp.where(sbd, upd, 0.0)
    for p in range(P):
        state_ref[p] = state[p]
    sums = []
    for u, (c, p) in enumerate(units):
        sums.append(_dot(jnp.concatenate(hi_lo(ys[u]) + hi_lo(ch[c]["rkr"][:, lanes[p]]), axis=0), jbd))
    yds = [ys[u] - (sums[u][:C] + sums[u][C:C2]) * (1.0 / N) for u in range(len(units))]
    var = [_dot(jnp.concatenate(hi_lo(yd * yd), axis=0), jbd) for yd in yds]
    for u, (c, p) in enumerate(units):
        d, lp = ch[c], lanes[p]
        yn = yds[u] * lax.rsqrt((var[u][:C] + var[u][C:]) * (1.0 / N) + GN_EPS)
        bonus = (sums[u][C2:C2 + C] + sums[u][C2 + C:]) * d["v16"][:, lp].astype(f32)
        z = (yn * gng_ref[:, lp] + gnb_ref[:, lp] + bonus) * g_ref[0, d["sl"], lp].astype(f32)
        o_ref[0, d["sl"], lp] = z.astype(bf16)


def _wkv7(r, k, v, wl, kk, b, g, r_k, gn_g, gn_b):
    B, S, D = r.shape
    T = min(WKV_TILE, S)
    C = min(WKV_CHUNK, T)
    tile = pl.BlockSpec((1, T, D), lambda bb, i: (bb, i, 0))
    vec = _full_spec((1, D))
    return pl.pallas_call(
        functools.partial(_wkv_kernel, C),
        name="wkv7",
        grid=(B, S // T),
        in_specs=[tile] * 7 + [vec] * 3,
        out_specs=tile,
        out_shape=jax.ShapeDtypeStruct((B, S, D), bf16),
        scratch_shapes=[pltpu.VMEM((D // LANES, LANES, LANES), f32)],
        compiler_params=_cparams("arbitrary", "arbitrary"),
    )(r, k, v, wl, kk, b, g, r_k.reshape(1, D), gn_g.reshape(1, D), gn_b.reshape(1, D))


def _proj_ln_kernel(gate_idx, z_ref, w_ref, x_ref, mod_ref, lng_ref, lnb_ref, o_ref):
    D = x_ref.shape[-1]
    gt = mod_ref[0, :, gate_idx * D:(gate_idx + 1) * D]
    y = _dot(z_ref[0], w_ref[...])
    u = DEEPNORM_ALPHA * x_ref[0] + (1.0 + gt) * y
    o_ref[0] = _layer_norm_rows(u, lng_ref[...], lnb_ref[...])


def _proj_ln(z, w, x, modl, gate_idx, ln_g, ln_b):
    B, S, D = x.shape
    tm = min(ROW_TILE, S)
    tile = pl.BlockSpec((1, tm, D), lambda b, i: (b, i, 0))
    return pl.pallas_call(
        functools.partial(_proj_ln_kernel, gate_idx),
        name="proj_ln",
        grid=(B, S // tm),
        in_specs=[tile, _full_spec(w.shape), tile,
                  pl.BlockSpec((1, 1, modl.shape[-1]), lambda b, i: (b, 0, 0)),
                  _full_spec((1, D)), _full_spec((1, D))],
        out_specs=tile,
        out_shape=jax.ShapeDtypeStruct((B, S, D), f32),
        compiler_params=_cparams("arbitrary", "arbitrary"),
    )(z, w.astype(bf16), x, modl, ln_g.reshape(1, D), ln_b.reshape(1, D))


def _route_kernel(x_ref, mod_ref, w_ref, wlo_ref, b_ref, h_ref, idx_ref, gate_ref, cnt_ref):
    @pl.when((pl.program_id(0) == 0) & (pl.program_id(1) == 0))
    def _():
        cnt_ref[...] = jnp.zeros_like(cnt_ref)

    D = x_ref.shape[-1]
    sh = mod_ref[0, :, 3 * D:4 * D]
    sc = mod_ref[0, :, 4 * D:5 * D]
    h = x_ref[0] * (1.0 + sc) + sh
    h_ref[0] = h.astype(bf16)
    h_hi = h.astype(bf16)
    h_lo = (h - h_hi.astype(f32)).astype(bf16)
    lg = _dot(h_hi, w_ref[...]) + _dot(h_hi, wlo_ref[...]) + _dot(h_lo, w_ref[...]) + b_ref[...]
    lane = lax.broadcasted_iota(jnp.int32, lg.shape, 1)
    glog = jnp.where(lane < N_GROUPS, lg, -jnp.inf)
    gmax = jnp.max(glog, axis=-1, keepdims=True)
    p_g = 1.0 / jnp.sum(jnp.exp(glog - gmax), axis=-1, keepdims=True)
    g_sel = jnp.min(jnp.where(glog == gmax, lane, LANES), axis=-1, keepdims=True)
    e_lo = N_GROUPS + EXP_PER_GROUP * g_sel
    elog = jnp.where((lane >= e_lo) & (lane < e_lo + EXP_PER_GROUP), lg, -jnp.inf)
    e1 = jnp.max(elog, axis=-1, keepdims=True)
    esum = jnp.sum(jnp.exp(elog - e1), axis=-1, keepdims=True)
    i1 = jnp.min(jnp.where(elog == e1, lane, LANES), axis=-1, keepdims=True)
    elog2 = jnp.where(lane == i1, -jnp.inf, elog)
    e2 = jnp.max(elog2, axis=-1, keepdims=True)
    i2 = jnp.min(jnp.where(elog2 == e2, lane, LANES), axis=-1, keepdims=True)
    p1 = 1.0 / esum
    p2 = jnp.exp(e2 - e1) / esum
    psum = p1 + p2
    gate_ref[0] = jnp.where(lane == 0, p_g * p1 / psum, jnp.where(lane == 1, p_g * p2 / psum, 0.0))
    tm = lg.shape[0]
    onehot = jnp.where(lane == i1, 1.0, 0.0) + jnp.where(lane == i2, 1.0, 0.0)
    ri = lax.broadcasted_iota(jnp.int32, (tm, tm), 0)
    ci = lax.broadcasted_iota(jnp.int32, (tm, tm), 1)
    before = _dot(jnp.where(ri > ci, 1.0, 0.0).astype(bf16), onehot.astype(bf16)) + cnt_ref[...]
    rank1 = jnp.sum(jnp.where(lane == i1, before, 0.0), axis=-1, keepdims=True).astype(jnp.int32)
    rank2 = jnp.sum(jnp.where(lane == i2, before, 0.0), axis=-1, keepdims=True).astype(jnp.int32)
    idx_ref[0] = jnp.where(lane == 0, i1 - N_GROUPS, jnp.where(lane == 1, i2 - N_GROUPS,
                           jnp.where(lane == 2, rank1, jnp.where(lane == 3, rank2, 0))))
    cnt_ref[...] = before[tm - 1:tm, :] + onehot[tm - 1:tm, :]


def _route(x, modl, w_grp, b_grp, w_exp, b_exp):
    B, S, D = x.shape
    tm = min(ROW_TILE, S)
    pad = LANES - N_GROUPS - N_EXPERTS
    w = jnp.concatenate([w_grp, w_exp, jnp.zeros((D, pad), f32)], axis=1)
    w_hi = w.astype(bf16)
    b = jnp.concatenate([b_grp, b_exp, jnp.zeros((pad,), f32)]).reshape(1, LANES)
    tile = pl.BlockSpec((1, tm, D), lambda bb, i: (bb, i, 0))
    small = pl.BlockSpec((1, tm, LANES), lambda bb, i: (bb, i, 0))
    return pl.pallas_call(
        _route_kernel,
        name="moe_route",
        grid=(B, S // tm),
        in_specs=[tile, pl.BlockSpec((1, 1, modl.shape[-1]), lambda bb, i: (bb, 0, 0)),
                  _full_spec(w.shape), _full_spec(w.shape), _full_spec(b.shape)],
        out_specs=[tile, small, small, _full_spec((1, LANES))],
        out_shape=[jax.ShapeDtypeStruct((B, S, D), bf16),
                   jax.ShapeDtypeStruct((B, S, LANES), jnp.int32),
                   jax.ShapeDtypeStruct((B, S, LANES), f32),
                   jax.ShapeDtypeStruct((1, LANES), f32)],
        compiler_params=_cparams("arbitrary", "arbitrary"),
    )(x, modl, w_hi, (w - w_hi.astype(f32)).astype(bf16), b)


def _expert_kernel(be_ref, nu_ref, x_ref, wg_ref, wu_ref, wd_ref, o_ref, wg16, wu16, wd16):
    i = pl.program_id(0)

    @pl.when((i == 0) | (be_ref[i] != be_ref[jnp.maximum(i - 1, 0)]))
    def _():
        wg16[...] = wg_ref[0, 0].astype(bf16)
        wu16[...] = wu_ref[0, 0].astype(bf16)
        wd16[...] = wd_ref[0, 0].astype(bf16)

    @pl.when(i < nu_ref[0])
    def _():
        x = x_ref[...]
        hg = _dot(x, wg16[...])
        hu = _dot(x, wu16[...])
        o_ref[...] = _dot((hg * _sigmoid(hg) * hu).astype(bf16), wd16[...]).astype(o_ref.dtype)

    @pl.when(i >= nu_ref[0])
    def _():
        o_ref[...] = jnp.zeros_like(o_ref)


def _expert_kernel_inplace(be_ref, nu_ref, x_ref, wg_ref, wu_ref, wd_ref, prev_ref, o_ref, wg16, wu16, wd16):
    del prev_ref
    _expert_kernel(be_ref, nu_ref, x_ref, wg_ref, wu_ref, wd_ref, o_ref, wg16, wu16, wd16)


def _expert_ffn(xs, block_exp, n_used, wg, wu, wd, layer, n_slots, block_offset, ys_prev=None):
    D = xs.shape[1]
    F = wg.shape[-1]
    blk = MOE_BLOCK
    row_map = lambda i, be, nu, *_: (jnp.maximum(jnp.minimum(i, nu[0] - 1), 0), 0)
    w_map = lambda i, be, nu, *_: (layer, be[i], 0, 0)
    in_specs = [
        pl.BlockSpec((blk, D), row_map),
        pl.BlockSpec((1, 1, D, F), w_map),
        pl.BlockSpec((1, 1, D, F), w_map),
        pl.BlockSpec((1, 1, F, D), w_map),
    ]
    args = [block_exp, n_used, xs, wg, wu, wd]
    aliases = {}
    if ys_prev is not None:
        in_specs.append(pl.BlockSpec(memory_space=pl.ANY))
        args.append(ys_prev)
        aliases = {len(args) - 1: 0}
    grid_spec = pltpu.PrefetchScalarGridSpec(
        num_scalar_prefetch=2,
        grid=(block_exp.shape[0],),
        in_specs=in_specs,
        out_specs=pl.BlockSpec((blk, D), lambda i, be, nu, *_: (i + block_offset, 0)),
        scratch_shapes=[pltpu.VMEM((D, F), bf16), pltpu.VMEM((D, F), bf16), pltpu.VMEM((F, D), bf16)],
    )
    return pl.pallas_call(
        _expert_kernel if ys_prev is None else _expert_kernel_inplace,
        name="moe_experts",
        grid_spec=grid_spec,
        out_shape=jax.ShapeDtypeStruct((n_slots, D), bf16),
        input_output_aliases=aliases,
        compiler_params=_cparams("arbitrary"),
    )(*args)


def _take_rows(table, idx):
    return table.at[idx].get(mode="promise_in_bounds")


def _dispatch(expert_idx, rank, counts, blk):
    T = expert_idx.shape[0]
    A = T * TOP_K
    n_blocks = A // blk + N_EXPERTS
    experts = jnp.arange(N_EXPERTS, dtype=jnp.int32)
    padded = (counts + blk - 1) // blk * blk
    pad_end = jnp.cumsum(padded)
    pad_start = pad_end - padded
    start = jnp.cumsum(counts) - counts
    dest = rank + jnp.sum(jnp.where(expert_idx[..., None] == experts, pad_start, 0), axis=-1)
    block_exp = jnp.minimum(jnp.sum(((jnp.arange(n_blocks, dtype=jnp.int32) * blk)[:, None] >= pad_end[None, :])
                                    .astype(jnp.int32), axis=1), N_EXPERTS - 1)
    order = jnp.argsort(expert_idx.reshape(A)).astype(jnp.int32)
    shift = jnp.sum(jnp.where(block_exp[:, None] == experts, start - pad_start, 0), axis=-1)
    pos = jnp.arange(n_blocks * blk, dtype=jnp.int32) + jnp.repeat(shift, blk)
    slot_tok = _take_rows(order, jnp.clip(pos, 0, A - 1)) // TOP_K
    n_used = (pad_end[-1] // blk).astype(jnp.int32).reshape(1)
    return dest, slot_tok, block_exp, n_used


def _combine_ln_kernel(x_ref, y0_ref, y1_ref, gate_ref, mod_ref, lng_ref, lnb_ref, o_ref):
    D = x_ref.shape[-1]
    gt = mod_ref[0, :, 5 * D:6 * D]
    gates = gate_ref[0]
    y = gates[:, 0:1] * y0_ref[0].astype(f32) + gates[:, 1:2] * y1_ref[0].astype(f32)
    u = DEEPNORM_ALPHA * x_ref[0] + (1.0 + gt) * y
    o_ref[0] = _layer_norm_rows(u, lng_ref[...], lnb_ref[...])


def _combine_ln(x, y0, y1, gates, modl, ln_g, ln_b):
    B, S, D = x.shape
    tm = min(ROW_TILE, S)
    tile = pl.BlockSpec((1, tm, D), lambda b, i: (b, i, 0))
    return pl.pallas_call(
        _combine_ln_kernel,
        name="moe_combine_ln",
        grid=(B, S // tm),
        in_specs=[tile, tile, tile, pl.BlockSpec((1, tm, LANES), lambda b, i: (b, i, 0)),
                  pl.BlockSpec((1, 1, modl.shape[-1]), lambda b, i: (b, 0, 0)),
                  _full_spec((1, D)), _full_spec((1, D))],
        out_specs=tile,
        out_shape=jax.ShapeDtypeStruct((B, S, D), f32),
        compiler_params=_cparams("arbitrary", "arbitrary"),
    )(x, y0, y1, gates, modl, ln_g.reshape(1, D), ln_b.reshape(1, D))


def _moe_layer(x, modl, w_grp, b_grp, w_exp, b_exp, wg, wu, wd, layer, ln_g, ln_b):
    B, S, D = x.shape
    T = B * S
    hb, idx, gates, cnt = _route(x, modl, w_grp, b_grp, w_exp, b_exp)
    idx = idx.reshape(T, LANES)
    counts = cnt[0, N_GROUPS:N_GROUPS + N_EXPERTS].astype(jnp.int32)
    dest, slot_tok, block_exp, n_used = _dispatch(idx[:, :TOP_K], idx[:, TOP_K:2 * TOP_K], counts, MOE_BLOCK)
    hb = hb.reshape(T, D)
    nb = block_exp.shape[0]
    ys = None
    for part in range(MOE_TABLE_PARTS):
        lo, hi = part * nb // MOE_TABLE_PARTS, (part + 1) * nb // MOE_TABLE_PARTS
        ys = _expert_ffn(_take_rows(hb, slot_tok[lo * MOE_BLOCK:hi * MOE_BLOCK]), block_exp[lo:hi],
                         jnp.clip(n_used - lo, 0, hi - lo), wg, wu, wd, layer, nb * MOE_BLOCK, lo, ys_prev=ys)
    y0 = _take_rows(ys, dest[:, 0]).reshape(B, S, D)
    y1 = _take_rows(ys, dest[:, 1]).reshape(B, S, D)
    return _combine_ln(x, y0, y1, gates, modl, ln_g, ln_b)


def _shared_kv_kernel(x_ref, mod_ref, wk_ref, wvt_ref, wf_ref, wflo_ref, fb_ref, kn_ref, place_ref, ones_ref,
                      k_ref, kb_ref, vt_ref, f_ref, carry_ref):
    @pl.when(pl.program_id(1) == 0)
    def _():
        carry_ref[...] = jnp.zeros_like(carry_ref)

    D = x_ref.shape[-1]
    tm = x_ref.shape[1]
    shift = mod_ref[0, :, 0:D]
    scale = mod_ref[0, :, D:2 * D]
    hk = x_ref[0] * (1.0 + scale) + shift
    hb = hk.astype(bf16)
    k = _dot(hb, wk_ref[...])
    vt_ref[0] = _dot_nt(wvt_ref[...], hb).astype(bf16)
    h_lo = (hk - hb.astype(f32)).astype(bf16)
    f = _dot(hb, wf_ref[...]) + _dot(hb, wflo_ref[...]) + _dot(h_lo, wf_ref[...]) + fb_ref[...]
    log_f = jnp.minimum(f, 0.0) - jnp.log(1.0 + jnp.exp(-jnp.abs(f)))
    row = lax.broadcasted_iota(jnp.int32, log_f.shape, 0)
    acc = log_f
    d = 1
    while d < tm:
        acc = acc + jnp.where(row >= d, pltpu.roll(acc, d, axis=0), 0.0)
        d *= 2
    acc = acc + carry_ref[...]
    f_ref[0] = acc
    carry_ref[...] = acc[tm - 1:tm, :]
    kb_ref[0] = (_dot(_bias_split(acc * LOG2E), place_ref[...]) + ones_ref[...]).astype(bf16)
    jbd = _pair_ones()
    for p in range(D // LANES):
        lp = slice(p * LANES, (p + 1) * LANES)
        seg = k[:, lp]
        ms = _head_sum(seg * seg, jbd) * (1.0 / HEAD_DIM)
        k_ref[0, :, lp] = (seg * lax.rsqrt(ms + QK_EPS) * kn_ref[...]).astype(bf16)


def _shared_kv(x, kvmod, w_kvf, b_f, k_norm):
    B, S, D = x.shape
    H = D // HEAD_DIM
    tm = min(ROW_TILE, S)
    wk = w_kvf[:, :D].astype(bf16)
    wvt = w_kvf[:, D:2 * D].T.astype(bf16)
    wf = jnp.concatenate([w_kvf[:, 2 * D:], jnp.zeros((D, LANES - H), f32)], axis=1)
    wf_hi = wf.astype(bf16)
    fb = jnp.concatenate([b_f, jnp.zeros((LANES - H,), f32)]).reshape(1, LANES)
    place, ones = _bias_placement(D, True)
    tile = pl.BlockSpec((1, tm, D), lambda b, i: (b, i, 0))
    act = jax.ShapeDtypeStruct((B, S, D), bf16)
    return pl.pallas_call(
        _shared_kv_kernel,
        name="shared_kv",
        grid=(B, S // tm),
        in_specs=[tile, pl.BlockSpec((1, 1, 2 * D), lambda b, i: (b, 0, 0)),
                  _full_spec(wk.shape), _full_spec(wvt.shape), _full_spec(wf.shape), _full_spec(wf.shape),
                  _full_spec(fb.shape), _full_spec((1, LANES)), _full_spec(place.shape), _full_spec(ones.shape)],
        out_specs=[tile, tile, pl.BlockSpec((1, D, tm), lambda b, i: (b, 0, i)),
                   pl.BlockSpec((1, tm, LANES), lambda b, i: (b, i, 0))],
        out_shape=[act, act, jax.ShapeDtypeStruct((B, D, S), bf16), jax.ShapeDtypeStruct((B, S, LANES), f32)],
        scratch_shapes=[pltpu.VMEM((1, LANES), f32)],
        compiler_params=_cparams("arbitrary", "arbitrary"),
    )(x, kvmod, wk, wvt, wf_hi, (wf - wf_hi.astype(f32)).astype(bf16), fb,
      jnp.tile(k_norm, LANES // HEAD_DIM).reshape(1, LANES), place, ones)


def _fox_pre_kernel(x_ref, mod_ref, wq_ref, wg_ref, qn_ref, f_ref, place_ref, ones_ref, q_ref, qb_ref, gate_ref):
    D = x_ref.shape[-1]
    sh = mod_ref[0, :, 0:D]
    sc = mod_ref[0, :, D:2 * D]
    hb = (x_ref[0] * (1.0 + sc) + sh).astype(bf16)
    q = _dot(hb, wq_ref[...])
    gate_ref[0] = _sigmoid(_dot(hb, wg_ref[...])).astype(bf16)
    qb_ref[0] = (_dot(_bias_split(f_ref[0] * LOG2E), place_ref[...]) + ones_ref[...]).astype(bf16)
    qscale = HEAD_DIM ** -0.5 * LOG2E
    jbd = _pair_ones()
    for p in range(D // LANES):
        lp = slice(p * LANES, (p + 1) * LANES)
        seg = q[:, lp]
        ms = _head_sum(seg * seg, jbd) * (1.0 / HEAD_DIM)
        q_ref[0, :, lp] = (seg * lax.rsqrt(ms + QK_EPS) * (qn_ref[...] * qscale)).astype(bf16)


def _fox_pre(x, modl, w_qg, q_norm, fcum):
    B, S, D = x.shape
    tm = min(ROW_TILE, S)
    tile = pl.BlockSpec((1, tm, D), lambda b, i: (b, i, 0))
    wq = w_qg[:, :D].astype(bf16)
    wg = w_qg[:, D:].astype(bf16)
    place, ones = _bias_placement(D, False)
    act = jax.ShapeDtypeStruct((B, S, D), bf16)
    return pl.pallas_call(
        _fox_pre_kernel,
        name="fox_pre",
        grid=(B, S // tm),
        in_specs=[tile, pl.BlockSpec((1, 1, modl.shape[-1]), lambda b, i: (b, 0, 0)),
                  _full_spec(wq.shape), _full_spec(wg.shape), _full_spec((1, LANES)),
                  pl.BlockSpec((1, tm, LANES), lambda b, i: (b, i, 0)),
                  _full_spec(place.shape), _full_spec(ones.shape)],
        out_specs=[tile, tile, tile],
        out_shape=[act, act, act],
        compiler_params=_cparams("arbitrary", "arbitrary"),
    )(x, modl, wq, wg, jnp.tile(q_norm, LANES // HEAD_DIM).reshape(1, LANES), fcum, place, ones)


def _fox_attn_kernel(qi_ref, kj_ref, flag_ref, q_ref, qb_ref, k_ref, kb_ref, vt_ref, gate_ref, o_ref,
                     qm_ref, m_ref, l_ref, acc_ref):
    s = pl.program_id(2)
    i = qi_ref[s]
    j = kj_ref[s]
    flags = flag_ref[s]
    N = HEAD_DIM
    nh = q_ref.shape[2] // N
    hpp = LANES // N
    tk, tq = k_ref.shape[1], q_ref.shape[1]
    pair = lambda h: slice((h // hpp) * LANES, (h // hpp + 1) * LANES)

    @pl.when(j == 0)
    def _():
        m_ref[...] = jnp.full_like(m_ref, NEG_BIG)
        l_ref[...] = jnp.zeros_like(l_ref)
        acc_ref[...] = jnp.zeros_like(acc_ref)
        for hh in range(nh):
            qcat = jnp.concatenate([q_ref[0, :, pair(hh)], qb_ref[0, :, pair(hh)]], axis=1).astype(f32)
            head = (lax.broadcasted_iota(jnp.int32, qcat.shape, 1) % LANES) // N
            qm_ref[hh] = jnp.where(head == hh % hpp, qcat, 0.0).astype(bf16)

    def step(masked, q_lo):
        qs = slice(q_lo, tq)
        kcat = [jnp.concatenate([k_ref[0, :, pair(hh)], kb_ref[0, :, pair(hh)]], axis=1)
                for hh in range(0, nh, hpp)]
        sts = [_dot_nt(kcat[hh // hpp], qm_ref[hh, qs, :]) for hh in range(nh)]
        for hh, st in enumerate(sts):
            if masked:
                key = j * tk + lax.broadcasted_iota(jnp.int32, st.shape, 0)
                qry = i * tq + q_lo + lax.broadcasted_iota(jnp.int32, st.shape, 1)
                st = jnp.where(key <= qry, st, -jnp.inf)
            m_prev = m_ref[hh, :, qs]
            m_new = jnp.maximum(m_prev, jnp.max(st, axis=0, keepdims=True))
            alpha = jnp.exp2(m_prev - m_new)
            p = jnp.exp2(st - m_new)
            l_ref[hh, :, qs] = alpha * l_ref[hh, :, qs] + jnp.sum(p, axis=0, keepdims=True)
            acc_ref[hh, :, qs] = alpha * acc_ref[hh, :, qs] + _dot(vt_ref[0, hh * N:(hh + 1) * N, :], p.astype(bf16))
            m_ref[hh, :, qs] = m_new

    @pl.when((flags & (ATTN_FLAG_MASK | ATTN_FLAG_UPPER)) == 0)
    def _():
        step(False, 0)

    @pl.when((flags & (ATTN_FLAG_MASK | ATTN_FLAG_UPPER)) == ATTN_FLAG_MASK)
    def _():
        step(True, 0)

    @pl.when((flags & ATTN_FLAG_UPPER) != 0)
    def _():
        step(True, tq // 2)

    @pl.when((flags & ATTN_FLAG_LAST) != 0)
    def _():
        for pr in range(nh // hpp):
            ot = jnp.concatenate([acc_ref[hh] * (1.0 / l_ref[hh]) for hh in range(pr * hpp, (pr + 1) * hpp)],
                                 axis=0)
            lp = slice(pr * LANES, (pr + 1) * LANES)
            o_ref[0, :, lp] = (ot.T * gate_ref[0, :, lp].astype(f32)).astype(bf16)


def _fox_attn(q, qb, k, kb, vt, gate):
    B, S, D = gate.shape
    tq = min(ATTN_Q_TILE, S)
    tk = min(ATTN_K_TILE, S)
    LW = ATTN_PAIRS * LANES
    nh = LW // HEAD_DIM
    qi, kj, flags = [], [], []
    for i in range(S // tq):
        last = ((i + 1) * tq - 1) // tk
        for j in range(last + 1):
            qi.append(i)
            kj.append(j)
            needs_mask = (j + 1) * tk - 1 > i * tq
            upper_only = j * tk >= i * tq + tq // 2
            flags.append((ATTN_FLAG_MASK if needs_mask else 0) | (ATTN_FLAG_LAST if j == last else 0)
                         | (ATTN_FLAG_UPPER if upper_only else 0))
    tables = [jnp.asarray(t, jnp.int32) for t in (qi, kj, flags)]
    qtile = pl.BlockSpec((1, tq, LW), lambda b, p, s, qi, kj, fl: (b, qi[s], p))
    ktile = pl.BlockSpec((1, tk, LW), lambda b, p, s, qi, kj, fl: (b, kj[s], p))
    grid_spec = pltpu.PrefetchScalarGridSpec(
        num_scalar_prefetch=3,
        grid=(B, D // LW, len(qi)),
        in_specs=[qtile, qtile, ktile, ktile,
                  pl.BlockSpec((1, LW, tk), lambda b, p, s, qi, kj, fl: (b, p, kj[s])),
                  qtile],
        out_specs=qtile,
        scratch_shapes=[pltpu.VMEM((nh, tq, 2 * LANES), bf16),
                        pltpu.VMEM((nh, 1, tq), f32), pltpu.VMEM((nh, 1, tq), f32),
                        pltpu.VMEM((nh, HEAD_DIM, tq), f32)],
    )
    return pl.pallas_call(
        _fox_attn_kernel,
        name="fox_attn",
        grid_spec=grid_spec,
        out_shape=jax.ShapeDtypeStruct((B, S, D), bf16),
        compiler_params=_cparams("arbitrary", "arbitrary", "arbitrary"),
    )(*tables, q, qb, k, kb, vt, gate)


def kernel(x, c, ada_w, ada_b, ln_g, ln_b, rw_mu, rw_rkv, rw_w0, rw_w1, rw_w2, rw_a0, rw_a1, rw_a2, rw_g1, rw_g2, rw_kk, rw_ka, rw_rk, rw_gn_g, rw_gn_b, rw_wo, rw_v0, rw_v1, rw_v2, kv_ada_w, kv_ada_b, kv_w, kv_fb, kv_knorm, fx_wqg, fx_qnorm, fx_wo, moe_wgrp, moe_bgrp, moe_wexp, moe_bexp, moe_wgate, moe_wup, moe_wdown):
    B, S, D = x.shape
    depth = ada_w.shape[0]
    n_a = rw_mu.shape[0]
    mod_all = _adaln_mod(c, ada_w, ada_b)
    kvmod_all = _adaln_mod(c, kv_ada_w[None], kv_ada_b[None])[0].reshape(B, 1, 2 * D)
    nchain = BATCH_CHAINS if B % BATCH_CHAINS == 0 else 1
    bc = B // nchain
    outs = []
    for ci in range(nchain):
        bs = slice(ci * bc, (ci + 1) * bc)
        x_c, mod, kvmod = x[bs], mod_all[:, bs], kvmod_all[bs]
        kv = None
        v_first = None
        for l in range(depth):
            modl = mod[l].reshape(bc, 1, 6 * D)
            if l < n_a:
                vmix = None if l == 0 else (rw_v0[l - 1], rw_v1[l - 1], rw_v2[l - 1], v_first)
                r, k, v, wl, kk, b, g = _rwkv_pre(x_c, modl, rw_mu[l], rw_rkv[l], rw_w0[l], rw_w1[l], rw_w2[l],
                                                  rw_a0[l], rw_a1[l], rw_a2[l], rw_g1[l], rw_g2[l],
                                                  rw_kk[l], rw_ka[l], vmix)
                if l == 0:
                    v_first = v
                z = _wkv7(r, k, v, wl, kk, b, g, rw_rk[l].reshape(D), rw_gn_g[l], rw_gn_b[l])
                w_o = rw_wo[l]
            else:
                j = l - n_a
                k, kb, vt, fcum = kv
                q, qb, gate = _fox_pre(x_c, modl, fx_wqg[j], fx_qnorm[j], fcum)
                z = _fox_attn(q, qb, k, kb, vt, gate)
                w_o = fx_wo[j]
            x_c = _proj_ln(z, w_o, x_c, modl, 2, ln_g[l, 0], ln_b[l, 0])
            x_c = _moe_layer(x_c, modl, moe_wgrp[l], moe_bgrp[l], moe_wexp[l], moe_bexp[l],
                             moe_wgate, moe_wup, moe_wdown, l, ln_g[l, 1], ln_b[l, 1])
            if l == n_a - 1:
                kv = _shared_kv(x_c, kvmod, kv_w, kv_fb, kv_knorm)
        outs.append(x_c)
    return jnp.concatenate(outs, axis=0)
```

```python
import functools
import math

import jax
import jax.numpy as jnp
import numpy as np
from jax import lax
from jax.experimental import pallas as pl
from jax.experimental.pallas import tpu as pltpu

HEAD_DIM = 64
N_GROUPS = 4
EXP_PER_GROUP = 8
N_EXPERTS = N_GROUPS * EXP_PER_GROUP
TOP_K = 2
DEPTH = 4
N_A = 2
DEEPNORM_ALPHA = (2 * DEPTH) ** 0.25
LN_EPS = 1e-5
GN_EPS = 64e-5
QK_EPS = 1e-6

LANES = 128
VMEM_LIMIT_BYTES = 56 * 1024 * 1024
WKV_CHUNK = 64
WKV_TILE = 128
ROW_TILE = 512
RWKV_PRE_TILE = 512
ATTN_Q_TILE = 1024
ATTN_K_TILE = 512
ATTN_PAIRS = 2
ATTN_FLAG_MASK = 1
ATTN_FLAG_LAST = 2
ATTN_FLAG_UPPER = 4
MOE_BLOCK = 512
MOE_TABLE_PARTS = 1
NEG_BIG = -1e30
LOG2E = math.log2(math.e)

f32 = jnp.float32
bf16 = jnp.bfloat16
HIGHEST = lax.Precision.HIGHEST


def _cparams(*sem):
    return pltpu.CompilerParams(dimension_semantics=sem, vmem_limit_bytes=VMEM_LIMIT_BYTES)


def _sigmoid(x):
    return 0.5 * jnp.tanh(0.5 * x) + 0.5


def _dot(a, b):
    return jnp.dot(a, b, preferred_element_type=f32)


def _dot_nt(a, b):
    return lax.dot_general(a, b, (((1,), (1,)), ((), ())), preferred_element_type=f32)


def _dot_tn(a, b):
    return lax.dot_general(a, b, (((0,), (0,)), ((), ())), preferred_element_type=f32)


def _full_spec(shape):
    n = len(shape)
    return pl.BlockSpec(shape, lambda *_: (0,) * n)


def _pair_ones():
    rs = lax.broadcasted_iota(jnp.int32, (LANES, LANES), 0)
    cs = lax.broadcasted_iota(jnp.int32, (LANES, LANES), 1)
    return jnp.where(jnp.where(rs >= HEAD_DIM, 1, 0) == jnp.where(cs >= HEAD_DIM, 1, 0), 1.0, 0.0).astype(bf16)


def _head_sum(x, jbd):
    xh = x.astype(bf16)
    xl = (x - xh.astype(f32)).astype(bf16)
    return _dot(xh, jbd) + _dot(xl, jbd)


def _bias_split(f):
    t0 = f.astype(bf16)
    r1 = f - t0.astype(f32)
    t1 = r1.astype(bf16)
    t2 = (r1 - t1.astype(f32)).astype(bf16)
    return jnp.concatenate([t0, t1, t2], axis=1)


def _bias_placement(D, is_key):
    H = D // HEAD_DIM
    place = np.zeros((3 * LANES, D), np.float32)
    ones = np.zeros((1, D), np.float32)
    for h in range(H):
        for i in range(3):
            place[i * LANES + h, h * HEAD_DIM + (3 + i if is_key else i)] = -1.0 if is_key else 1.0
            ones[0, h * HEAD_DIM + (i if is_key else 3 + i)] = 1.0
    return jnp.asarray(place, bf16), jnp.asarray(ones, f32)


def _layer_norm_rows(u, g, b):
    mu = jnp.mean(u, axis=-1, keepdims=True)
    d = u - mu
    var = jnp.mean(d * d, axis=-1, keepdims=True)
    return d * lax.rsqrt(var + LN_EPS) * g + b


def _mod_kernel(c_ref, w_ref, b_ref, o_ref):
    c = c_ref[...]
    cs = c * _sigmoid(c)
    o_ref[0] = jnp.dot(cs, w_ref[0], preferred_element_type=f32, precision=HIGHEST) + b_ref[0]


def _adaln_mod(c, w, b):
    L, D, N = w.shape
    B = c.shape[0]
    tn = min(N, 1024)
    return pl.pallas_call(
        _mod_kernel,
        name="adaln_mod",
        grid=(L, N // tn),
        in_specs=[
            _full_spec((B, D)),
            pl.BlockSpec((1, D, tn), lambda l, j: (l, 0, j)),
            pl.BlockSpec((1, 1, tn), lambda l, j: (l, 0, j)),
        ],
        out_specs=pl.BlockSpec((1, B, tn), lambda l, j: (l, 0, j)),
        out_shape=jax.ShapeDtypeStruct((L, B, N), f32),
        compiler_params=_cparams("arbitrary", "arbitrary"),
    )(c, w, b.reshape(L, 1, N))


def _rwkv_pre_kernel(has_vmix, x_ref, xp_ref, mod_ref, mu_ref, wrkv_ref, w0_ref, w1_ref, w2_ref,
                     a0_ref, a1_ref, a2_ref, g1_ref, g2_ref, kkp_ref, kap_ref, *rest):
    if has_vmix:
        v0_ref, v1_ref, v2_ref, vf_ref = rest[:4]
        rest = rest[4:]
    r_ref, k_ref, v_ref, wl_ref, kk_ref, b_ref, g_ref = rest
    D = x_ref.shape[-1]
    sh = mod_ref[0, :, 0:D]
    sc = mod_ref[0, :, D:2 * D]
    h = x_ref[0] * (1.0 + sc) + sh
    hp = xp_ref[0][7:8, :] * (1.0 + sc) + sh
    hp = jnp.where(pl.program_id(1) == 0, 0.0, hp)
    row = lax.broadcasted_iota(jnp.int32, h.shape, 0)
    h_prev = jnp.where(row == 0, hp, pltpu.roll(h, 1, axis=0))
    xx = h_prev - h
    mix = lambda j: (h + xx * mu_ref[j:j + 1, :]).astype(bf16)
    xr, xw, xk, xv, xa, xg = [mix(j) for j in range(6)]
    r = _dot(xr, wrkv_ref[0])
    k = _dot(xk, wrkv_ref[1])
    v = _dot(xv, wrkv_ref[2])
    zw = w0_ref[...] + _dot(jnp.tanh(_dot(xw, w1_ref[...])).astype(bf16), w2_ref[...])
    wl_ref[0] = -math.exp(-0.5) * _sigmoid(zw)
    a = _sigmoid(a0_ref[...] + _dot(_dot(xa, a1_ref[...]).astype(bf16), a2_ref[...]))
    g = _dot(_sigmoid(_dot(xg, g1_ref[...])).astype(bf16), g2_ref[...])
    if has_vmix:
        vmix = _sigmoid(v0_ref[...] + _dot(_dot(xv, v1_ref[...]).astype(bf16), v2_ref[...]))
        v = v + (vf_ref[0].astype(f32) - v) * vmix
    r_ref[0] = r.astype(bf16)
    v_ref[0] = v.astype(bf16)
    g_ref[0] = g.astype(bf16)
    k_ref[0] = (k * (1.0 + (a - 1.0) * kap_ref[...])).astype(bf16)
    kk = k * kkp_ref[...]
    jbd = _pair_ones()
    for p in range(D // LANES):
        lp = slice(p * LANES, (p + 1) * LANES)
        seg = kk[:, lp]
        nrm = jnp.sqrt(_head_sum(seg * seg, jbd))
        seg = seg / jnp.maximum(nrm, 1e-12)
        kk_ref[0, :, lp] = seg.astype(bf16)
        b_ref[0, :, lp] = (seg * a[:, lp]).astype(bf16)


def _rwkv_pre(x, modl, mu, wrkv, w0, w1, w2, a0, a1, a2, g1, g2, kkp, kap, vmix):
    B, S, D = x.shape
    tm = min(RWKV_PRE_TILE, S)
    row = lambda t: t.reshape(1, D)
    args = [x, x, modl, mu, wrkv.astype(bf16), row(w0), w1.astype(bf16), w2.astype(bf16), row(a0),
            a1.astype(bf16), a2.astype(bf16), g1.astype(bf16), g2.astype(bf16), row(kkp), row(kap)]
    tile = pl.BlockSpec((1, tm, D), lambda b, i: (b, i, 0))
    in_specs = [
        tile,
        pl.BlockSpec((1, 8, D), lambda b, i: (b, jnp.maximum(i * (tm // 8) - 1, 0), 0)),
        pl.BlockSpec((1, 1, modl.shape[-1]), lambda b, i: (b, 0, 0)),
    ] + [_full_spec(a.shape) for a in args[3:]]
    if vmix is not None:
        v0, v1, v2, v_first = vmix
        extra = [row(v0), v1.astype(bf16), v2.astype(bf16)]
        args += extra + [v_first]
        in_specs += [_full_spec(a.shape) for a in extra] + [tile]
    out_bf = jax.ShapeDtypeStruct((B, S, D), bf16)
    out_shape = [out_bf, out_bf, out_bf, jax.ShapeDtypeStruct((B, S, D), f32), out_bf, out_bf, out_bf]
    return pl.pallas_call(
        functools.partial(_rwkv_pre_kernel, vmix is not None),
        name="rwkv_pre",
        grid=(B, S // tm),
        in_specs=in_specs,
        out_specs=[tile] * 7,
        out_shape=out_shape,
        compiler_params=_cparams("arbitrary", "arbitrary"),
    )(*args)


def _wkv_kernel(C, r_ref, k_ref, v_ref, wl_ref, kk_ref, b_ref, g_ref, rk_ref, gng_ref, gnb_ref,
                o_ref, state_ref):
    @pl.when(pl.program_id(1) == 0)
    def _():
        state_ref[...] = jnp.zeros_like(state_ref)

    T, D = r_ref.shape[1], r_ref.shape[2]
    N = HEAD_DIM
    P = D // LANES
    C2 = 2 * C
    ri = lax.broadcasted_iota(jnp.int32, (C, C), 0)
    ci = lax.broadcasted_iota(jnp.int32, (C, C), 1)
    tri_incl = jnp.where(ri >= ci, 1.0, 0.0).astype(f32)
    r2 = lax.broadcasted_iota(jnp.int32, (C2, C2), 0)
    c2 = lax.broadcasted_iota(jnp.int32, (C2, C2), 1)
    dlt = jnp.where(jnp.where(r2 >= C, 1, 0) == jnp.where(c2 >= C, 1, 0), r2 - c2, -1)
    strict = dlt > 0
    incl = dlt >= 0
    eye2 = jnp.where(r2 == c2, 1.0, 0.0).astype(f32)
    h0 = lax.broadcasted_iota(jnp.int32, (C, LANES), 1) < N
    rs = lax.broadcasted_iota(jnp.int32, (LANES, LANES), 0)
    cs = lax.broadcasted_iota(jnp.int32, (LANES, LANES), 1)
    sbd = jnp.where(rs >= N, 1, 0) == jnp.where(cs >= N, 1, 0)
    jbd = jnp.where(sbd, 1.0, 0.0).astype(bf16)
    stack2 = lambda t: jnp.concatenate([t, t], axis=0)

    lanes = [slice(p * LANES, (p + 1) * LANES) for p in range(P)]
    nc = T // C
    units = [(c, p) for c in range(nc) for p in range(P)]
    hi_lo = lambda t: (t.astype(bf16), (t - t.astype(bf16).astype(f32)).astype(bf16))

    ch = []
    for c in range(nc):
        sl = pl.ds(c * C, C)
        wl = wl_ref[0, sl, :]
        cum = jnp.dot(tri_incl, wl, preferred_element_type=f32, precision=HIGHEST)
        g_incl = jnp.exp(cum)
        g_inv = jnp.exp(-cum)
        g_last = g_incl[C - 1:C, :]
        r = r_ref[0, sl, :].astype(f32)
        k = k_ref[0, sl, :].astype(f32)
        btf = b_ref[0, sl, :].astype(f32) * g_inv
        ktf = k * g_inv
        ch.append(dict(
            sl=sl, g_last=g_last, v16=v_ref[0, sl, :],
            af=-kk_ref[0, sl, :].astype(f32) * jnp.exp(cum - wl), rf=r * g_incl,
            bt=btf.astype(bf16), kt=ktf.astype(bf16),
            bc=(btf * g_last).astype(bf16), kc=(ktf * g_last).astype(bf16),
            rkr=r * k * rk_ref[...]))

    Ls, Aak, Arow, Vst, X2 = [], [], [], [], []
    for c, p in units:
        d, lp = ch[c], lanes[p]
        afp, rfp = d["af"][:, lp], d["rf"][:, lp]
        X4 = jnp.concatenate([jnp.where(h0, afp, 0.0), jnp.where(h0, 0.0, afp),
                              jnp.where(h0, rfp, 0.0), jnp.where(h0, 0.0, rfp)], axis=0).astype(bf16)
        Mb = _dot_nt(X4, stack2(d["bt"][:, lp]))
        Mk = _dot_nt(X4, stack2(d["kt"][:, lp]))
        Ls.append(jnp.where(strict, Mb[:C2], 0.0))
        Aak.append(jnp.where(strict, Mk[:C2], 0.0).astype(bf16))
        Arow.append(jnp.concatenate([jnp.where(incl, Mb[C2:], 0.0), jnp.where(incl, Mk[C2:], 0.0)],
                                    axis=1).astype(bf16))
        Vst.append(stack2(d["v16"][:, lp]))
        X2.append(jnp.concatenate([afp, rfp], axis=0).astype(bf16))
    Ps = [eye2 + L for L in Ls]
    n = 1
    while 2 * n < C:
        Lb = [L.astype(bf16) for L in Ls]
        Ls = [_dot(x, x) for x in Lb]
        Ps = [Pm + _dot(L.astype(bf16), Pm.astype(bf16)) for L, Pm in zip(Ls, Ps)]
        n *= 2
    Ps = [Pm.astype(bf16) for Pm in Ps]
    AV = [_dot(a, vs) for a, vs in zip(Aak, Vst)]
    ys = []
    state = [state_ref[p] for p in range(P)]
    for c in range(nc):
        d = ch[c]
        us = range(c * P, (c + 1) * P)
        M2 = [_dot_nt(X2[u], state[u - c * P].astype(bf16)) for u in us]
        Ust = [_dot(Ps[u], (AV[u] + stack2(m2[:C])).astype(bf16)) for u, m2 in zip(us, M2)]
        Yst = [_dot(Arow[u], jnp.concatenate([ust.astype(bf16), Vst[u]], axis=0)) + stack2(m2[C:])
               for u, ust, m2 in zip(us, Ust, M2)]
        ys += [jnp.where(h0, yst[:C], yst[C:]) for yst in Yst]
        for p, ust in enumerate(Ust):
            lp = lanes[p]
            u16 = jnp.where(h0, ust[:C], ust[C:]).astype(bf16)
            upd = _dot_tn(jnp.concatenate([u16, d["v16"][:, lp]], axis=0),
                          jnp.concatenate([d["bc"][:, lp], d["kc"][:, lp]], axis=0))
            state[p] = state[p] * d["g_last"][:, lp] + jnp.where(sbd, upd, 0.0)
    for p in range(P):
        state_ref[p] = state[p]
    sums = []
    for u, (c, p) in enumerate(units):
        sums.append(_dot(jnp.concatenate(hi_lo(ys[u]) + hi_lo(ch[c]["rkr"][:, lanes[p]]), axis=0), jbd))
    yds = [ys[u] - (sums[u][:C] + sums[u][C:C2]) * (1.0 / N) for u in range(len(units))]
    var = [_dot(jnp.concatenate(hi_lo(yd * yd), axis=0), jbd) for yd in yds]
    for u, (c, p) in enumerate(units):
        d, lp = ch[c], lanes[p]
        yn = yds[u] * lax.rsqrt((var[u][:C] + var[u][C:]) * (1.0 / N) + GN_EPS)
        bonus = (sums[u][C2:C2 + C] + sums[u][C2 + C:]) * d["v16"][:, lp].astype(f32)
        z = (yn * gng_ref[:, lp] + gnb_ref[:, lp] + bonus) * g_ref[0, d["sl"], lp].astype(f32)
        o_ref[0, d["sl"], lp] = z.astype(bf16)


def _wkv7(r, k, v, wl, kk, b, g, r_k, gn_g, gn_b):
    B, S, D = r.shape
    T = min(WKV_TILE, S)
    C = min(WKV_CHUNK, T)
    tile = pl.BlockSpec((1, T, D), lambda bb, i: (bb, i, 0))
    vec = _full_spec((1, D))
    return pl.pallas_call(
        functools.partial(_wkv_kernel, C),
        name="wkv7",
        grid=(B, S // T),
        in_specs=[tile] * 7 + [vec] * 3,
        out_specs=tile,
        out_shape=jax.ShapeDtypeStruct((B, S, D), bf16),
        scratch_shapes=[pltpu.VMEM((D // LANES, LANES, LANES), f32)],
        compiler_params=_cparams("arbitrary", "arbitrary"),
    )(r, k, v, wl, kk, b, g, r_k.reshape(1, D), gn_g.reshape(1, D), gn_b.reshape(1, D))


def _proj_ln_kernel(gate_idx, z_ref, w_ref, x_ref, mod_ref, lng_ref, lnb_ref, o_ref):
    D = x_ref.shape[-1]
    gt = mod_ref[0, :, gate_idx * D:(gate_idx + 1) * D]
    y = _dot(z_ref[0], w_ref[...])
    u = DEEPNORM_ALPHA * x_ref[0] + (1.0 + gt) * y
    o_ref[0] = _layer_norm_rows(u, lng_ref[...], lnb_ref[...])


def _proj_ln(z, w, x, modl, gate_idx, ln_g, ln_b):
    B, S, D = x.shape
    tm = min(ROW_TILE, S)
    tile = pl.BlockSpec((1, tm, D), lambda b, i: (b, i, 0))
    return pl.pallas_call(
        functools.partial(_proj_ln_kernel, gate_idx),
        name="proj_ln",
        grid=(B, S // tm),
        in_specs=[tile, _full_spec(w.shape), tile,
                  pl.BlockSpec((1, 1, modl.shape[-1]), lambda b, i: (b, 0, 0)),
                  _full_spec((1, D)), _full_spec((1, D))],
        out_specs=tile,
        out_shape=jax.ShapeDtypeStruct((B, S, D), f32),
        compiler_params=_cparams("arbitrary", "arbitrary"),
    )(z, w.astype(bf16), x, modl, ln_g.reshape(1, D), ln_b.reshape(1, D))


def _route_kernel(x_ref, mod_ref, w_ref, wlo_ref, b_ref, h_ref, idx_ref, gate_ref, cnt_ref):
    @pl.when((pl.program_id(0) == 0) & (pl.program_id(1) == 0))
    def _():
        cnt_ref[...] = jnp.zeros_like(cnt_ref)

    D = x_ref.shape[-1]
    sh = mod_ref[0, :, 3 * D:4 * D]
    sc = mod_ref[0, :, 4 * D:5 * D]
    h = x_ref[0] * (1.0 + sc) + sh
    h_ref[0] = h.astype(bf16)
    h_hi = h.astype(bf16)
    h_lo = (h - h_hi.astype(f32)).astype(bf16)
    lg = _dot(h_hi, w_ref[...]) + _dot(h_hi, wlo_ref[...]) + _dot(h_lo, w_ref[...]) + b_ref[...]
    lane = lax.broadcasted_iota(jnp.int32, lg.shape, 1)
    glog = jnp.where(lane < N_GROUPS, lg, -jnp.inf)
    gmax = jnp.max(glog, axis=-1, keepdims=True)
    p_g = 1.0 / jnp.sum(jnp.exp(glog - gmax), axis=-1, keepdims=True)
    g_sel = jnp.min(jnp.where(glog == gmax, lane, LANES), axis=-1, keepdims=True)
    e_lo = N_GROUPS + EXP_PER_GROUP * g_sel
    elog = jnp.where((lane >= e_lo) & (lane < e_lo + EXP_PER_GROUP), lg, -jnp.inf)
    e1 = jnp.max(elog, axis=-1, keepdims=True)
    esum = jnp.sum(jnp.exp(elog - e1), axis=-1, keepdims=True)
    i1 = jnp.min(jnp.where(elog == e1, lane, LANES), axis=-1, keepdims=True)
    elog2 = jnp.where(lane == i1, -jnp.inf, elog)
    e2 = jnp.max(elog2, axis=-1, keepdims=True)
    i2 = jnp.min(jnp.where(elog2 == e2, lane, LANES), axis=-1, keepdims=True)
    p1 = 1.0 / esum
    p2 = jnp.exp(e2 - e1) / esum
    psum = p1 + p2
    gate_ref[0] = jnp.where(lane == 0, p_g * p1 / psum, jnp.where(lane == 1, p_g * p2 / psum, 0.0))
    tm = lg.shape[0]
    onehot = jnp.where(lane == i1, 1.0, 0.0) + jnp.where(lane == i2, 1.0, 0.0)
    ri = lax.broadcasted_iota(jnp.int32, (tm, tm), 0)
    ci = lax.broadcasted_iota(jnp.int32, (tm, tm), 1)
    before = _dot(jnp.where(ri > ci, 1.0, 0.0).astype(bf16), onehot.astype(bf16)) + cnt_ref[...]
    rank1 = jnp.sum(jnp.where(lane == i1, before, 0.0), axis=-1, keepdims=True).astype(jnp.int32)
    rank2 = jnp.sum(jnp.where(lane == i2, before, 0.0), axis=-1, keepdims=True).astype(jnp.int32)
    idx_ref[0] = jnp.where(lane == 0, i1 - N_GROUPS, jnp.where(lane == 1, i2 - N_GROUPS,
                           jnp.where(lane == 2, rank1, jnp.where(lane == 3, rank2, 0))))
    cnt_ref[...] = before[tm - 1:tm, :] + onehot[tm - 1:tm, :]


def _route(x, modl, w_grp, b_grp, w_exp, b_exp):
    B, S, D = x.shape
    tm = min(ROW_TILE, S)
    pad = LANES - N_GROUPS - N_EXPERTS
    w = jnp.concatenate([w_grp, w_exp, jnp.zeros((D, pad), f32)], axis=1)
    w_hi = w.astype(bf16)
    b = jnp.concatenate([b_grp, b_exp, jnp.zeros((pad,), f32)]).reshape(1, LANES)
    tile = pl.BlockSpec((1, tm, D), lambda bb, i: (bb, i, 0))
    small = pl.BlockSpec((1, tm, LANES), lambda bb, i: (bb, i, 0))
    return pl.pallas_call(
        _route_kernel,
        name="moe_route",
        grid=(B, S // tm),
        in_specs=[tile, pl.BlockSpec((1, 1, modl.shape[-1]), lambda bb, i: (bb, 0, 0)),
                  _full_spec(w.shape), _full_spec(w.shape), _full_spec(b.shape)],
        out_specs=[tile, small, small, _full_spec((1, LANES))],
        out_shape=[jax.ShapeDtypeStruct((B, S, D), bf16),
                   jax.ShapeDtypeStruct((B, S, LANES), jnp.int32),
                   jax.ShapeDtypeStruct((B, S, LANES), f32),
                   jax.ShapeDtypeStruct((1, LANES), f32)],
        compiler_params=_cparams("arbitrary", "arbitrary"),
    )(x, modl, w_hi, (w - w_hi.astype(f32)).astype(bf16), b)


def _expert_kernel(be_ref, nu_ref, x_ref, wg_ref, wu_ref, wd_ref, o_ref, wg16, wu16, wd16):
    i = pl.program_id(0)

    @pl.when((i == 0) | (be_ref[i] != be_ref[jnp.maximum(i - 1, 0)]))
    def _():
        wg16[...] = wg_ref[0, 0].astype(bf16)
        wu16[...] = wu_ref[0, 0].astype(bf16)
        wd16[...] = wd_ref[0, 0].astype(bf16)

    @pl.when(i < nu_ref[0])
    def _():
        x = x_ref[...]
        hg = _dot(x, wg16[...])
        hu = _dot(x, wu16[...])
        o_ref[...] = _dot((hg * _sigmoid(hg) * hu).astype(bf16), wd16[...]).astype(o_ref.dtype)

    @pl.when(i >= nu_ref[0])
    def _():
        o_ref[...] = jnp.zeros_like(o_ref)


def _expert_kernel_inplace(be_ref, nu_ref, x_ref, wg_ref, wu_ref, wd_ref, prev_ref, o_ref, wg16, wu16, wd16):
    del prev_ref
    _expert_kernel(be_ref, nu_ref, x_ref, wg_ref, wu_ref, wd_ref, o_ref, wg16, wu16, wd16)


def _expert_ffn(xs, block_exp, n_used, wg, wu, wd, layer, n_slots, block_offset, ys_prev=None):
    D = xs.shape[1]
    F = wg.shape[-1]
    blk = MOE_BLOCK
    row_map = lambda i, be, nu, *_: (jnp.maximum(jnp.minimum(i, nu[0] - 1), 0), 0)
    w_map = lambda i, be, nu, *_: (layer, be[i], 0, 0)
    in_specs = [
        pl.BlockSpec((blk, D), row_map),
        pl.BlockSpec((1, 1, D, F), w_map),
        pl.BlockSpec((1, 1, D, F), w_map),
        pl.BlockSpec((1, 1, F, D), w_map),
    ]
    args = [block_exp, n_used, xs, wg, wu, wd]
    aliases = {}
    if ys_prev is not None:
        in_specs.append(pl.BlockSpec(memory_space=pl.ANY))
        args.append(ys_prev)
        aliases = {len(args) - 1: 0}
    grid_spec = pltpu.PrefetchScalarGridSpec(
        num_scalar_prefetch=2,
        grid=(block_exp.shape[0],),
        in_specs=in_specs,
        out_specs=pl.BlockSpec((blk, D), lambda i, be, nu, *_: (i + block_offset, 0)),
        scratch_shapes=[pltpu.VMEM((D, F), bf16), pltpu.VMEM((D, F), bf16), pltpu.VMEM((F, D), bf16)],
    )
    return pl.pallas_call(
        _expert_kernel if ys_prev is None else _expert_kernel_inplace,
        name="moe_experts",
        grid_spec=grid_spec,
        out_shape=jax.ShapeDtypeStruct((n_slots, D), bf16),
        input_output_aliases=aliases,
        compiler_params=_cparams("arbitrary"),
    )(*args)


def _take_rows(table, idx):
    return table.at[idx].get(mode="promise_in_bounds")


def _dispatch(expert_idx, rank, counts, blk):
    T = expert_idx.shape[0]
    A = T * TOP_K
    n_blocks = A // blk + N_EXPERTS
    experts = jnp.arange(N_EXPERTS, dtype=jnp.int32)
    padded = (counts + blk - 1) // blk * blk
    pad_end = jnp.cumsum(padded)
    pad_start = pad_end - padded
    start = jnp.cumsum(counts) - counts
    dest = rank + jnp.sum(jnp.where(expert_idx[..., None] == experts, pad_start, 0), axis=-1)
    block_exp = jnp.minimum(jnp.sum(((jnp.arange(n_blocks, dtype=jnp.int32) * blk)[:, None] >= pad_end[None, :])
                                    .astype(jnp.int32), axis=1), N_EXPERTS - 1)
    order = jnp.argsort(expert_idx.reshape(A)).astype(jnp.int32)
    shift = jnp.sum(jnp.where(block_exp[:, None] == experts, start - pad_start, 0), axis=-1)
    pos = jnp.arange(n_blocks * blk, dtype=jnp.int32) + jnp.repeat(shift, blk)
    slot_tok = _take_rows(order, jnp.clip(pos, 0, A - 1)) // TOP_K
    n_used = (pad_end[-1] // blk).astype(jnp.int32).reshape(1)
    return dest, slot_tok, block_exp, n_used


def _combine_ln_kernel(x_ref, y_ref, gate_ref, mod_ref, lng_ref, lnb_ref, o_ref):
    D = x_ref.shape[-1]
    gt = mod_ref[0, :, 5 * D:6 * D]
    gates = gate_ref[0]
    y = gates[:, 0:1] * y_ref[0, :, :D].astype(f32) + gates[:, 1:2] * y_ref[0, :, D:].astype(f32)
    u = DEEPNORM_ALPHA * x_ref[0] + (1.0 + gt) * y
    o_ref[0] = _layer_norm_rows(u, lng_ref[...], lnb_ref[...])


def _combine_ln(x, y01, gates, modl, ln_g, ln_b):
    B, S, D = x.shape
    tm = min(ROW_TILE, S)
    tile = pl.BlockSpec((1, tm, D), lambda b, i: (b, i, 0))
    return pl.pallas_call(
        _combine_ln_kernel,
        name="moe_combine_ln",
        grid=(B, S // tm),
        in_specs=[tile, pl.BlockSpec((1, tm, 2 * D), lambda b, i: (b, i, 0)),
                  pl.BlockSpec((1, tm, LANES), lambda b, i: (b, i, 0)),
                  pl.BlockSpec((1, 1, modl.shape[-1]), lambda b, i: (b, 0, 0)),
                  _full_spec((1, D)), _full_spec((1, D))],
        out_specs=tile,
        out_shape=jax.ShapeDtypeStruct((B, S, D), f32),
        compiler_params=_cparams("arbitrary", "arbitrary"),
    )(x, y01, gates, modl, ln_g.reshape(1, D), ln_b.reshape(1, D))


def _moe_layer(x, modl, w_grp, b_grp, w_exp, b_exp, wg, wu, wd, layer, ln_g, ln_b):
    B, S, D = x.shape
    T = B * S
    hb, idx, gates, cnt = _route(x, modl, w_grp, b_grp, w_exp, b_exp)
    idx = idx.reshape(T, LANES)
    counts = cnt[0, N_GROUPS:N_GROUPS + N_EXPERTS].astype(jnp.int32)
    dest, slot_tok, block_exp, n_used = _dispatch(idx[:, :TOP_K], idx[:, TOP_K:2 * TOP_K], counts, MOE_BLOCK)
    hb = hb.reshape(T, D)
    nb = block_exp.shape[0]
    ys = None
    for part in range(MOE_TABLE_PARTS):
        lo, hi = part * nb // MOE_TABLE_PARTS, (part + 1) * nb // MOE_TABLE_PARTS
        ys = _expert_ffn(_take_rows(hb, slot_tok[lo * MOE_BLOCK:hi * MOE_BLOCK]), block_exp[lo:hi],
                         jnp.clip(n_used - lo, 0, hi - lo), wg, wu, wd, layer, nb * MOE_BLOCK, lo, ys_prev=ys)
    y01 = _take_rows(ys, dest.reshape(T * TOP_K)).reshape(B, S, TOP_K * D)
    return _combine_ln(x, y01, gates, modl, ln_g, ln_b)


def _shared_kv_kernel(x_ref, mod_ref, wk_ref, wvt_ref, wf_ref, wflo_ref, fb_ref, kn_ref, place_ref, ones_ref,
                      k_ref, kb_ref, vt_ref, f_ref, carry_ref):
    @pl.when(pl.program_id(1) == 0)
    def _():
        carry_ref[...] = jnp.zeros_like(carry_ref)

    D = x_ref.shape[-1]
    tm = x_ref.shape[1]
    shift = mod_ref[0, :, 0:D]
    scale = mod_ref[0, :, D:2 * D]
    hk = x_ref[0] * (1.0 + scale) + shift
    hb = hk.astype(bf16)
    k = _dot(hb, wk_ref[...])
    vt_ref[0] = _dot_nt(wvt_ref[...], hb).astype(bf16)
    h_lo = (hk - hb.astype(f32)).astype(bf16)
    f = _dot(hb, wf_ref[...]) + _dot(hb, wflo_ref[...]) + _dot(h_lo, wf_ref[...]) + fb_ref[...]
    log_f = jnp.minimum(f, 0.0) - jnp.log(1.0 + jnp.exp(-jnp.abs(f)))
    row = lax.broadcasted_iota(jnp.int32, log_f.shape, 0)
    acc = log_f
    d = 1
    while d < tm:
        acc = acc + jnp.where(row >= d, pltpu.roll(acc, d, axis=0), 0.0)
        d *= 2
    acc = acc + carry_ref[...]
    f_ref[0] = acc
    carry_ref[...] = acc[tm - 1:tm, :]
    kb_ref[0] = (_dot(_bias_split(acc * LOG2E), place_ref[...]) + ones_ref[...]).astype(bf16)
    jbd = _pair_ones()
    for p in range(D // LANES):
        lp = slice(p * LANES, (p + 1) * LANES)
        seg = k[:, lp]
        ms = _head_sum(seg * seg, jbd) * (1.0 / HEAD_DIM)
        k_ref[0, :, lp] = (seg * lax.rsqrt(ms + QK_EPS) * kn_ref[...]).astype(bf16)


def _shared_kv(x, kvmod, w_kvf, b_f, k_norm):
    B, S, D = x.shape
    H = D // HEAD_DIM
    tm = min(ROW_TILE, S)
    wk = w_kvf[:, :D].astype(bf16)
    wvt = w_kvf[:, D:2 * D].T.astype(bf16)
    wf = jnp.concatenate([w_kvf[:, 2 * D:], jnp.zeros((D, LANES - H), f32)], axis=1)
    wf_hi = wf.astype(bf16)
    fb = jnp.concatenate([b_f, jnp.zeros((LANES - H,), f32)]).reshape(1, LANES)
    place, ones = _bias_placement(D, True)
    tile = pl.BlockSpec((1, tm, D), lambda b, i: (b, i, 0))
    act = jax.ShapeDtypeStruct((B, S, D), bf16)
    return pl.pallas_call(
        _shared_kv_kernel,
        name="shared_kv",
        grid=(B, S // tm),
        in_specs=[tile, pl.BlockSpec((1, 1, 2 * D), lambda b, i: (b, 0, 0)),
                  _full_spec(wk.shape), _full_spec(wvt.shape), _full_spec(wf.shape), _full_spec(wf.shape),
                  _full_spec(fb.shape), _full_spec((1, LANES)), _full_spec(place.shape), _full_spec(ones.shape)],
        out_specs=[tile, tile, pl.BlockSpec((1, D, tm), lambda b, i: (b, 0, i)),
                   pl.BlockSpec((1, tm, LANES), lambda b, i: (b, i, 0))],
        out_shape=[act, act, jax.ShapeDtypeStruct((B, D, S), bf16), jax.ShapeDtypeStruct((B, S, LANES), f32)],
        scratch_shapes=[pltpu.VMEM((1, LANES), f32)],
        compiler_params=_cparams("arbitrary", "arbitrary"),
    )(x, kvmod, wk, wvt, wf_hi, (wf - wf_hi.astype(f32)).astype(bf16), fb,
      jnp.tile(k_norm, LANES // HEAD_DIM).reshape(1, LANES), place, ones)


def _fox_pre_kernel(x_ref, mod_ref, wq_ref, wg_ref, qn_ref, f_ref, place_ref, ones_ref, q_ref, qb_ref, gate_ref):
    D = x_ref.shape[-1]
    sh = mod_ref[0, :, 0:D]
    sc = mod_ref[0, :, D:2 * D]
    hb = (x_ref[0] * (1.0 + sc) + sh).astype(bf16)
    q = _dot(hb, wq_ref[...])
    gate_ref[0] = _sigmoid(_dot(hb, wg_ref[...])).astype(bf16)
    qb_ref[0] = (_dot(_bias_split(f_ref[0] * LOG2E), place_ref[...]) + ones_ref[...]).astype(bf16)
    qscale = HEAD_DIM ** -0.5 * LOG2E
    jbd = _pair_ones()
    for p in range(D // LANES):
        lp = slice(p * LANES, (p + 1) * LANES)
        seg = q[:, lp]
        ms = _head_sum(seg * seg, jbd) * (1.0 / HEAD_DIM)
        q_ref[0, :, lp] = (seg * lax.rsqrt(ms + QK_EPS) * (qn_ref[...] * qscale)).astype(bf16)


def _fox_pre(x, modl, w_qg, q_norm, fcum):
    B, S, D = x.shape
    tm = min(ROW_TILE, S)
    tile = pl.BlockSpec((1, tm, D), lambda b, i: (b, i, 0))
    wq = w_qg[:, :D].astype(bf16)
    wg = w_qg[:, D:].astype(bf16)
    place, ones = _bias_placement(D, False)
    act = jax.ShapeDtypeStruct((B, S, D), bf16)
    return pl.pallas_call(
        _fox_pre_kernel,
        name="fox_pre",
        grid=(B, S // tm),
        in_specs=[tile, pl.BlockSpec((1, 1, modl.shape[-1]), lambda b, i: (b, 0, 0)),
                  _full_spec(wq.shape), _full_spec(wg.shape), _full_spec((1, LANES)),
                  pl.BlockSpec((1, tm, LANES), lambda b, i: (b, i, 0)),
                  _full_spec(place.shape), _full_spec(ones.shape)],
        out_specs=[tile, tile, tile],
        out_shape=[act, act, act],
        compiler_params=_cparams("arbitrary", "arbitrary"),
    )(x, modl, wq, wg, jnp.tile(q_norm, LANES // HEAD_DIM).reshape(1, LANES), fcum, place, ones)


def _fox_attn_kernel(qi_ref, kj_ref, flag_ref, q_ref, qb_ref, k_ref, kb_ref, vt_ref, gate_ref, o_ref,
                     qm_ref, m_ref, l_ref, acc_ref):
    s = pl.program_id(2)
    i = qi_ref[s]
    j = kj_ref[s]
    flags = flag_ref[s]
    N = HEAD_DIM
    nh = q_ref.shape[2] // N
    hpp = LANES // N
    tk, tq = k_ref.shape[1], q_ref.shape[1]
    pair = lambda h: slice((h // hpp) * LANES, (h // hpp + 1) * LANES)

    @pl.when(j == 0)
    def _():
        m_ref[...] = jnp.full_like(m_ref, NEG_BIG)
        l_ref[...] = jnp.zeros_like(l_ref)
        acc_ref[...] = jnp.zeros_like(acc_ref)
        for hh in range(nh):
            qcat = jnp.concatenate([q_ref[0, :, pair(hh)], qb_ref[0, :, pair(hh)]], axis=1).astype(f32)
            head = (lax.broadcasted_iota(jnp.int32, qcat.shape, 1) % LANES) // N
            qm_ref[hh] = jnp.where(head == hh % hpp, qcat, 0.0).astype(bf16)

    def step(masked, q_lo):
        qs = slice(q_lo, tq)
        kcat = [jnp.concatenate([k_ref[0, :, pair(hh)], kb_ref[0, :, pair(hh)]], axis=1)
                for hh in range(0, nh, hpp)]
        sts = [_dot_nt(kcat[hh // hpp], qm_ref[hh, qs, :]) for hh in range(nh)]
        for hh, st in enumerate(sts):
            if masked:
                key = j * tk + lax.broadcasted_iota(jnp.int32, st.shape, 0)
                qry = i * tq + q_lo + lax.broadcasted_iota(jnp.int32, st.shape, 1)
                st = jnp.where(key <= qry, st, -jnp.inf)
            m_prev = m_ref[hh, :, qs]
            m_new = jnp.maximum(m_prev, jnp.max(st, axis=0, keepdims=True))
            alpha = jnp.exp2(m_prev - m_new)
            p = jnp.exp2(st - m_new)
            l_ref[hh, :, qs] = alpha * l_ref[hh, :, qs] + jnp.sum(p, axis=0, keepdims=True)
            acc_ref[hh, :, qs] = alpha * acc_ref[hh, :, qs] + _dot(vt_ref[0, hh * N:(hh + 1) * N, :], p.astype(bf16))
            m_ref[hh, :, qs] = m_new

    @pl.when((flags & (ATTN_FLAG_MASK | ATTN_FLAG_UPPER)) == 0)
    def _():
        step(False, 0)

    @pl.when((flags & (ATTN_FLAG_MASK | ATTN_FLAG_UPPER)) == ATTN_FLAG_MASK)
    def _():
        step(True, 0)

    @pl.when((flags & ATTN_FLAG_UPPER) != 0)
    def _():
        step(True, tq // 2)

    @pl.when((flags & ATTN_FLAG_LAST) != 0)
    def _():
        for pr in range(nh // hpp):
            ot = jnp.concatenate([acc_ref[hh] * (1.0 / l_ref[hh]) for hh in range(pr * hpp, (pr + 1) * hpp)],
                                 axis=0)
            lp = slice(pr * LANES, (pr + 1) * LANES)
            o_ref[0, :, lp] = (ot.T * gate_ref[0, :, lp].astype(f32)).astype(bf16)


def _fox_attn(q, qb, k, kb, vt, gate):
    B, S, D = gate.shape
    tq = min(ATTN_Q_TILE, S)
    tk = min(ATTN_K_TILE, S)
    LW = ATTN_PAIRS * LANES
    nh = LW // HEAD_DIM
    qi, kj, flags = [], [], []
    for i in range(S // tq):
        last = ((i + 1) * tq - 1) // tk
        for j in range(last + 1):
            qi.append(i)
            kj.append(j)
            needs_mask = (j + 1) * tk - 1 > i * tq
            upper_only = j * tk >= i * tq + tq // 2
            flags.append((ATTN_FLAG_MASK if needs_mask else 0) | (ATTN_FLAG_LAST if j == last else 0)
                         | (ATTN_FLAG_UPPER if upper_only else 0))
    tables = [jnp.asarray(t, jnp.int32) for t in (qi, kj, flags)]
    qtile = pl.BlockSpec((1, tq, LW), lambda b, p, s, qi, kj, fl: (b, qi[s], p))
    ktile = pl.BlockSpec((1, tk, LW), lambda b, p, s, qi, kj, fl: (b, kj[s], p))
    grid_spec = pltpu.PrefetchScalarGridSpec(
        num_scalar_prefetch=3,
        grid=(B, D // LW, len(qi)),
        in_specs=[qtile, qtile, ktile, ktile,
                  pl.BlockSpec((1, LW, tk), lambda b, p, s, qi, kj, fl: (b, p, kj[s])),
                  qtile],
        out_specs=qtile,
        scratch_shapes=[pltpu.VMEM((nh, tq, 2 * LANES), bf16),
                        pltpu.VMEM((nh, 1, tq), f32), pltpu.VMEM((nh, 1, tq), f32),
                        pltpu.VMEM((nh, HEAD_DIM, tq), f32)],
    )
    return pl.pallas_call(
        _fox_attn_kernel,
        name="fox_attn",
        grid_spec=grid_spec,
        out_shape=jax.ShapeDtypeStruct((B, S, D), bf16),
        compiler_params=_cparams("arbitrary", "arbitrary", "arbitrary"),
    )(*tables, q, qb, k, kb, vt, gate)


def kernel(x, c, ada_w, ada_b, ln_g, ln_b, rw_mu, rw_rkv, rw_w0, rw_w1, rw_w2, rw_a0, rw_a1, rw_a2, rw_g1, rw_g2, rw_kk, rw_ka, rw_rk, rw_gn_g, rw_gn_b, rw_wo, rw_v0, rw_v1, rw_v2, kv_ada_w, kv_ada_b, kv_w, kv_fb, kv_knorm, fx_wqg, fx_qnorm, fx_wo, moe_wgrp, moe_bgrp, moe_wexp, moe_bexp, moe_wgate, moe_wup, moe_wdown):
    B, S, D = x.shape
    depth = ada_w.shape[0]
    n_a = rw_mu.shape[0]
    mod = _adaln_mod(c, ada_w, ada_b)
    kvmod = _adaln_mod(c, kv_ada_w[None], kv_ada_b[None])[0].reshape(B, 1, 2 * D)
    kv = None
    v_first = None
    for l in range(depth):
        modl = mod[l].reshape(B, 1, 6 * D)
        if l < n_a:
            vmix = None if l == 0 else (rw_v0[l - 1], rw_v1[l - 1], rw_v2[l - 1], v_first)
            r, k, v, wl, kk, b, g = _rwkv_pre(x, modl, rw_mu[l], rw_rkv[l], rw_w0[l], rw_w1[l], rw_w2[l],
                                              rw_a0[l], rw_a1[l], rw_a2[l], rw_g1[l], rw_g2[l],
                                              rw_kk[l], rw_ka[l], vmix)
            if l == 0:
                v_first = v
            z = _wkv7(r, k, v, wl, kk, b, g, rw_rk[l].reshape(D), rw_gn_g[l], rw_gn_b[l])
            w_o = rw_wo[l]
        else:
            j = l - n_a
            k, kb, vt, fcum = kv
            q, qb, gate = _fox_pre(x, modl, fx_wqg[j], fx_qnorm[j], fcum)
            z = _fox_attn(q, qb, k, kb, vt, gate)
            w_o = fx_wo[j]
        x = _proj_ln(z, w_o, x, modl, 2, ln_g[l, 0], ln_b[l, 0])
        x = _moe_layer(x, modl, moe_wgrp[l], moe_bgrp[l], moe_wexp[l], moe_bexp[l],
                       moe_wgate, moe_wup, moe_wdown, l, ln_g[l, 1], ln_b[l, 1])
        if l == n_a - 1:
            kv = _shared_kv(x, kvmod, kv_w, kv_fb, kv_knorm)
    return x
```

```python
import functools
import math

import jax
import jax.numpy as jnp
import numpy as np
from jax import lax
from jax.experimental import pallas as pl
from jax.experimental.pallas import tpu as pltpu

HEAD_DIM = 64
N_GROUPS = 4
EXP_PER_GROUP = 8
N_EXPERTS = N_GROUPS * EXP_PER_GROUP
TOP_K = 2
DEPTH = 4
N_A = 2
DEEPNORM_ALPHA = (2 * DEPTH) ** 0.25
LN_EPS = 1e-5
GN_EPS = 64e-5
QK_EPS = 1e-6

LANES = 128
VMEM_LIMIT_BYTES = 56 * 1024 * 1024
WKV_CHUNK = 64
WKV_TILE = 128
ROW_TILE = 512
RWKV_PRE_TILE = 512
ATTN_Q_TILE = 1024
ATTN_K_TILE = 512
ATTN_PAIRS = 2
ATTN_FLAG_MASK = 1
ATTN_FLAG_LAST = 2
ATTN_FLAG_UPPER = 4
MOE_BLOCK = 512
MOE_TABLE_PARTS = 1
NEG_BIG = -1e30
LOG2E = math.log2(math.e)

f32 = jnp.float32
bf16 = jnp.bfloat16
HIGHEST = lax.Precision.HIGHEST


def _cparams(*sem):
    return pltpu.CompilerParams(dimension_semantics=sem, vmem_limit_bytes=VMEM_LIMIT_BYTES)


def _sigmoid(x):
    return 0.5 * jnp.tanh(0.5 * x) + 0.5


def _dot(a, b):
    return jnp.dot(a, b, preferred_element_type=f32)


def _dot_nt(a, b):
    return lax.dot_general(a, b, (((1,), (1,)), ((), ())), preferred_element_type=f32)


def _dot_tn(a, b):
    return lax.dot_general(a, b, (((0,), (0,)), ((), ())), preferred_element_type=f32)


def _full_spec(shape):
    n = len(shape)
    return pl.BlockSpec(shape, lambda *_: (0,) * n)


def _pair_ones():
    rs = lax.broadcasted_iota(jnp.int32, (LANES, LANES), 0)
    cs = lax.broadcasted_iota(jnp.int32, (LANES, LANES), 1)
    return jnp.where(jnp.where(rs >= HEAD_DIM, 1, 0) == jnp.where(cs >= HEAD_DIM, 1, 0), 1.0, 0.0).astype(bf16)


def _head_sum(x, jbd):
    xh = x.astype(bf16)
    xl = (x - xh.astype(f32)).astype(bf16)
    return _dot(xh, jbd) + _dot(xl, jbd)


def _bias_split(f):
    t0 = f.astype(bf16)
    r1 = f - t0.astype(f32)
    t1 = r1.astype(bf16)
    t2 = (r1 - t1.astype(f32)).astype(bf16)
    return jnp.concatenate([t0, t1, t2], axis=1)


def _bias_placement(D, is_key):
    H = D // HEAD_DIM
    place = np.zeros((3 * LANES, D), np.float32)
    ones = np.zeros((1, D), np.float32)
    for h in range(H):
        for i in range(3):
            place[i * LANES + h, h * HEAD_DIM + (3 + i if is_key else i)] = -1.0 if is_key else 1.0
            ones[0, h * HEAD_DIM + (i if is_key else 3 + i)] = 1.0
    return jnp.asarray(place, bf16), jnp.asarray(ones, f32)


def _layer_norm_rows(u, g, b):
    mu = jnp.mean(u, axis=-1, keepdims=True)
    d = u - mu
    var = jnp.mean(d * d, axis=-1, keepdims=True)
    return d * lax.rsqrt(var + LN_EPS) * g + b


def _mod_kernel(c_ref, w_ref, b_ref, o_ref):
    c = c_ref[...]
    cs = c * _sigmoid(c)
    o_ref[0] = jnp.dot(cs, w_ref[0], preferred_element_type=f32, precision=HIGHEST) + b_ref[0]


def _adaln_mod(c, w, b):
    L, D, N = w.shape
    B = c.shape[0]
    tn = min(N, 1024)
    return pl.pallas_call(
        _mod_kernel,
        name="adaln_mod",
        grid=(L, N // tn),
        in_specs=[
            _full_spec((B, D)),
            pl.BlockSpec((1, D, tn), lambda l, j: (l, 0, j)),
            pl.BlockSpec((1, 1, tn), lambda l, j: (l, 0, j)),
        ],
        out_specs=pl.BlockSpec((1, B, tn), lambda l, j: (l, 0, j)),
        out_shape=jax.ShapeDtypeStruct((L, B, N), f32),
        compiler_params=_cparams("arbitrary", "arbitrary"),
    )(c, w, b.reshape(L, 1, N))


def _rwkv_pre_kernel(has_vmix, x_ref, xp_ref, mod_ref, mu_ref, wrkv_ref, w0_ref, w1_ref, w2_ref,
                     a0_ref, a1_ref, a2_ref, g1_ref, g2_ref, kkp_ref, kap_ref, *rest):
    if has_vmix:
        v0_ref, v1_ref, v2_ref, vf_ref = rest[:4]
        rest = rest[4:]
    r_ref, k_ref, v_ref, wl_ref, kk_ref, b_ref, g_ref = rest
    D = x_ref.shape[-1]
    sh = mod_ref[0, :, 0:D]
    sc = mod_ref[0, :, D:2 * D]
    h = x_ref[0] * (1.0 + sc) + sh
    hp = xp_ref[0][7:8, :] * (1.0 + sc) + sh
    hp = jnp.where(pl.program_id(1) == 0, 0.0, hp)
    row = lax.broadcasted_iota(jnp.int32, h.shape, 0)
    h_prev = jnp.where(row == 0, hp, pltpu.roll(h, 1, axis=0))
    xx = h_prev - h
    mix = lambda j: (h + xx * mu_ref[j:j + 1, :]).astype(bf16)
    xr, xw, xk, xv, xa, xg = [mix(j) for j in range(6)]
    r = _dot(xr, wrkv_ref[0])
    k = _dot(xk, wrkv_ref[1])
    v = _dot(xv, wrkv_ref[2])
    zw = w0_ref[...] + _dot(jnp.tanh(_dot(xw, w1_ref[...])).astype(bf16), w2_ref[...])
    wl_ref[0] = -math.exp(-0.5) * _sigmoid(zw)
    a = _sigmoid(a0_ref[...] + _dot(_dot(xa, a1_ref[...]).astype(bf16), a2_ref[...]))
    g = _dot(_sigmoid(_dot(xg, g1_ref[...])).astype(bf16), g2_ref[...])
    if has_vmix:
        vmix = _sigmoid(v0_ref[...] + _dot(_dot(xv, v1_ref[...]).astype(bf16), v2_ref[...]))
        v = v + (vf_ref[0].astype(f32) - v) * vmix
    r_ref[0] = r.astype(bf16)
    v_ref[0] = v.astype(bf16)
    g_ref[0] = g.astype(bf16)
    k_ref[0] = (k * (1.0 + (a - 1.0) * kap_ref[...])).astype(bf16)
    kk = k * kkp_ref[...]
    jbd = _pair_ones()
    for p in range(D // LANES):
        lp = slice(p * LANES, (p + 1) * LANES)
        seg = kk[:, lp]
        nrm = jnp.sqrt(_head_sum(seg * seg, jbd))
        seg = seg / jnp.maximum(nrm, 1e-12)
        kk_ref[0, :, lp] = seg.astype(bf16)
        b_ref[0, :, lp] = (seg * a[:, lp]).astype(bf16)


def _rwkv_pre(x, modl, mu, wrkv, w0, w1, w2, a0, a1, a2, g1, g2, kkp, kap, vmix):
    B, S, D = x.shape
    tm = min(RWKV_PRE_TILE, S)
    row = lambda t: t.reshape(1, D)
    args = [x, x, modl, mu, wrkv.astype(bf16), row(w0), w1.astype(bf16), w2.astype(bf16), row(a0),
            a1.astype(bf16), a2.astype(bf16), g1.astype(bf16), g2.astype(bf16), row(kkp), row(kap)]
    tile = pl.BlockSpec((1, tm, D), lambda b, i: (b, i, 0))
    in_specs = [
        tile,
        pl.BlockSpec((1, 8, D), lambda b, i: (b, jnp.maximum(i * (tm // 8) - 1, 0), 0)),
        pl.BlockSpec((1, 1, modl.shape[-1]), lambda b, i: (b, 0, 0)),
    ] + [_full_spec(a.shape) for a in args[3:]]
    if vmix is not None:
        v0, v1, v2, v_first = vmix
        extra = [row(v0), v1.astype(bf16), v2.astype(bf16)]
        args += extra + [v_first]
        in_specs += [_full_spec(a.shape) for a in extra] + [tile]
    out_bf = jax.ShapeDtypeStruct((B, S, D), bf16)
    out_shape = [out_bf, out_bf, out_bf, jax.ShapeDtypeStruct((B, S, D), f32), out_bf, out_bf, out_bf]
    return pl.pallas_call(
        functools.partial(_rwkv_pre_kernel, vmix is not None),
        name="rwkv_pre",
        grid=(B, S // tm),
        in_specs=in_specs,
        out_specs=[tile] * 7,
        out_shape=out_shape,
        compiler_params=_cparams("arbitrary", "arbitrary"),
    )(*args)


def _wkv_kernel(C, r_ref, k_ref, v_ref, wl_ref, kk_ref, b_ref, g_ref, rk_ref, gng_ref, gnb_ref,
                o_ref, state_ref):
    @pl.when(pl.program_id(1) == 0)
    def _():
        state_ref[...] = jnp.zeros_like(state_ref)

    T, D = r_ref.shape[1], r_ref.shape[2]
    N = HEAD_DIM
    P = D // LANES
    C2 = 2 * C
    ri = lax.broadcasted_iota(jnp.int32, (C, C), 0)
    ci = lax.broadcasted_iota(jnp.int32, (C, C), 1)
    tri_incl = jnp.where(ri >= ci, 1.0, 0.0).astype(f32)
    r2 = lax.broadcasted_iota(jnp.int32, (C2, C2), 0)
    c2 = lax.broadcasted_iota(jnp.int32, (C2, C2), 1)
    dlt = jnp.where(jnp.where(r2 >= C, 1, 0) == jnp.where(c2 >= C, 1, 0), r2 - c2, -1)
    strict = dlt > 0
    incl = dlt >= 0
    eye2 = jnp.where(r2 == c2, 1.0, 0.0).astype(f32)
    h0 = lax.broadcasted_iota(jnp.int32, (C, LANES), 1) < N
    rs = lax.broadcasted_iota(jnp.int32, (LANES, LANES), 0)
    cs = lax.broadcasted_iota(jnp.int32, (LANES, LANES), 1)
    sbd = jnp.where(rs >= N, 1, 0) == jnp.where(cs >= N, 1, 0)
    jbd = jnp.where(sbd, 1.0, 0.0).astype(bf16)
    stack2 = lambda t: jnp.concatenate([t, t], axis=0)

    lanes = [slice(p * LANES, (p + 1) * LANES) for p in range(P)]
    nc = T // C
    units = [(c, p) for c in range(nc) for p in range(P)]
    hi_lo = lambda t: (t.astype(bf16), (t - t.astype(bf16).astype(f32)).astype(bf16))

    ch = []
    for c in range(nc):
        sl = pl.ds(c * C, C)
        wl = wl_ref[0, sl, :]
        cum = jnp.dot(tri_incl, wl, preferred_element_type=f32, precision=HIGHEST)
        g_incl = jnp.exp(cum)
        g_inv = jnp.exp(-cum)
        g_last = g_incl[C - 1:C, :]
        r = r_ref[0, sl, :].astype(f32)
        k = k_ref[0, sl, :].astype(f32)
        btf = b_ref[0, sl, :].astype(f32) * g_inv
        ktf = k * g_inv
        ch.append(dict(
            sl=sl, g_last=g_last, v16=v_ref[0, sl, :],
            af=-kk_ref[0, sl, :].astype(f32) * jnp.exp(cum - wl), rf=r * g_incl,
            bt=btf.astype(bf16), kt=ktf.astype(bf16),
            bc=(btf * g_last).astype(bf16), kc=(ktf * g_last).astype(bf16),
            rkr=r * k * rk_ref[...]))

    Ls, Aak, Arow, Vst, X2 = [], [], [], [], []
    for c, p in units:
        d, lp = ch[c], lanes[p]
        afp, rfp = d["af"][:, lp], d["rf"][:, lp]
        X4 = jnp.concatenate([jnp.where(h0, afp, 0.0), jnp.where(h0, 0.0, afp),
                              jnp.where(h0, rfp, 0.0), jnp.where(h0, 0.0, rfp)], axis=0).astype(bf16)
        Mb = _dot_nt(X4, stack2(d["bt"][:, lp]))
        Mk = _dot_nt(X4, stack2(d["kt"][:, lp]))
        Ls.append(jnp.where(strict, Mb[:C2], 0.0))
        Aak.append(jnp.where(strict, Mk[:C2], 0.0).astype(bf16))
        Arow.append(jnp.concatenate([jnp.where(incl, Mb[C2:], 0.0), jnp.where(incl, Mk[C2:], 0.0)],
                                    axis=1).astype(bf16))
        Vst.append(stack2(d["v16"][:, lp]))
        X2.append(jnp.concatenate([afp, rfp], axis=0).astype(bf16))
    Ps = [eye2 + L for L in Ls]
    n = 1
    while 2 * n < C:
        Lb = [L.astype(bf16) for L in Ls]
        Ls = [_dot(x, x) for x in Lb]
        Ps = [Pm + _dot(L.astype(bf16), Pm.astype(bf16)) for L, Pm in zip(Ls, Ps)]
        n *= 2
    Ps = [Pm.astype(bf16) for Pm in Ps]
    AV = [_dot(a, vs) for a, vs in zip(Aak, Vst)]
    ys = []
    state = [state_ref[p] for p in range(P)]
    for c in range(nc):
        d = ch[c]
        us = range(c * P, (c + 1) * P)
        M2 = [_dot_nt(X2[u], state[u - c * P].astype(bf16)) for u in us]
        Ust = [_dot(Ps[u], (AV[u] + stack2(m2[:C])).astype(bf16)) for u, m2 in zip(us, M2)]
        Yst = [_dot(Arow[u], jnp.concatenate([ust.astype(bf16), Vst[u]], axis=0)) + stack2(m2[C:])
               for u, ust, m2 in zip(us, Ust, M2)]
        ys += [jnp.where(h0, yst[:C], yst[C:]) for yst in Yst]
        for p, ust in enumerate(Ust):
            lp = lanes[p]
            u16 = jnp.where(h0, ust[:C], ust[C:]).astype(bf16)
            upd = _dot_tn(jnp.concatenate([u16, d["v16"][:, lp]], axis=0),
                          jnp.concatenate([d["bc"][:, lp], d["kc"][:, lp]], axis=0))
            state[p] = state[p] * d["g_last"][:, lp] + jnp.where(sbd, upd, 0.0)
    for p in range(P):
        state_ref[p] = state[p]
    sums = []
    for u, (c, p) in enumerate(units):
        sums.append(_dot(jnp.concatenate(hi_lo(ys[u]) + hi_lo(ch[c]["rkr"][:, lanes[p]]), axis=0), jbd))
    yds = [ys[u] - (sums[u][:C] + sums[u][C:C2]) * (1.0 / N) for u in range(len(units))]
    var = [_dot(jnp.concatenate(hi_lo(yd * yd), axis=0), jbd) for yd in yds]
    for u, (c, p) in enumerate(units):
        d, lp = ch[c], lanes[p]
        yn = yds[u] * lax.rsqrt((var[u][:C] + var[u][C:]) * (1.0 / N) + GN_EPS)
        bonus = (sums[u][C2:C2 + C] + sums[u][C2 + C:]) * d["v16"][:, lp].astype(f32)
        z = (yn * gng_ref[:, lp] + gnb_ref[:, lp] + bonus) * g_ref[0, d["sl"], lp].astype(f32)
        o_ref[0, d["sl"], lp] = z.astype(bf16)


def _wkv7(r, k, v, wl, kk, b, g, r_k, gn_g, gn_b):
    B, S, D = r.shape
    T = min(WKV_TILE, S)
    C = min(WKV_CHUNK, T)
    tile = pl.BlockSpec((1, T, D), lambda bb, i: (bb, i, 0))
    vec = _full_spec((1, D))
    return pl.pallas_call(
        functools.partial(_wkv_kernel, C),
        name="wkv7",
        grid=(B, S // T),
        in_specs=[tile] * 7 + [vec] * 3,
        out_specs=tile,
        out_shape=jax.ShapeDtypeStruct((B, S, D), bf16),
        scratch_shapes=[pltpu.VMEM((D // LANES, LANES, LANES), f32)],
        compiler_params=_cparams("arbitrary", "arbitrary"),
    )(r, k, v, wl, kk, b, g, r_k.reshape(1, D), gn_g.reshape(1, D), gn_b.reshape(1, D))


def _proj_ln_kernel(gate_idx, z_ref, w_ref, x_ref, mod_ref, lng_ref, lnb_ref, o_ref):
    D = x_ref.shape[-1]
    gt = mod_ref[0, :, gate_idx * D:(gate_idx + 1) * D]
    y = _dot(z_ref[0], w_ref[...])
    u = DEEPNORM_ALPHA * x_ref[0] + (1.0 + gt) * y
    o_ref[0] = _layer_norm_rows(u, lng_ref[...], lnb_ref[...])


def _proj_ln(z, w, x, modl, gate_idx, ln_g, ln_b):
    B, S, D = x.shape
    tm = min(ROW_TILE, S)
    tile = pl.BlockSpec((1, tm, D), lambda b, i: (b, i, 0))
    return pl.pallas_call(
        functools.partial(_proj_ln_kernel, gate_idx),
        name="proj_ln",
        grid=(B, S // tm),
        in_specs=[tile, _full_spec(w.shape), tile,
                  pl.BlockSpec((1, 1, modl.shape[-1]), lambda b, i: (b, 0, 0)),
                  _full_spec((1, D)), _full_spec((1, D))],
        out_specs=tile,
        out_shape=jax.ShapeDtypeStruct((B, S, D), f32),
        compiler_params=_cparams("arbitrary", "arbitrary"),
    )(z, w.astype(bf16), x, modl, ln_g.reshape(1, D), ln_b.reshape(1, D))


def _route_kernel(x_ref, mod_ref, w_ref, wlo_ref, b_ref, h_ref, idx_ref, gate_ref, cnt_ref):
    @pl.when((pl.program_id(0) == 0) & (pl.program_id(1) == 0))
    def _():
        cnt_ref[...] = jnp.zeros_like(cnt_ref)

    D = x_ref.shape[-1]
    sh = mod_ref[0, :, 3 * D:4 * D]
    sc = mod_ref[0, :, 4 * D:5 * D]
    h = x_ref[0] * (1.0 + sc) + sh
    h_ref[0] = h.astype(bf16)
    h_hi = h.astype(bf16)
    h_lo = (h - h_hi.astype(f32)).astype(bf16)
    lg = _dot(h_hi, w_ref[...]) + _dot(h_hi, wlo_ref[...]) + _dot(h_lo, w_ref[...]) + b_ref[...]
    lane = lax.broadcasted_iota(jnp.int32, lg.shape, 1)
    glog = jnp.where(lane < N_GROUPS, lg, -jnp.inf)
    gmax = jnp.max(glog, axis=-1, keepdims=True)
    p_g = 1.0 / jnp.sum(jnp.exp(glog - gmax), axis=-1, keepdims=True)
    g_sel = jnp.min(jnp.where(glog == gmax, lane, LANES), axis=-1, keepdims=True)
    e_lo = N_GROUPS + EXP_PER_GROUP * g_sel
    elog = jnp.where((lane >= e_lo) & (lane < e_lo + EXP_PER_GROUP), lg, -jnp.inf)
    e1 = jnp.max(elog, axis=-1, keepdims=True)
    esum = jnp.sum(jnp.exp(elog - e1), axis=-1, keepdims=True)
    i1 = jnp.min(jnp.where(elog == e1, lane, LANES), axis=-1, keepdims=True)
    elog2 = jnp.where(lane == i1, -jnp.inf, elog)
    e2 = jnp.max(elog2, axis=-1, keepdims=True)
    i2 = jnp.min(jnp.where(elog2 == e2, lane, LANES), axis=-1, keepdims=True)
    p1 = 1.0 / esum
    p2 = jnp.exp(e2 - e1) / esum
    psum = p1 + p2
    gate_ref[0] = jnp.where(lane == 0, p_g * p1 / psum, jnp.where(lane == 1, p_g * p2 / psum, 0.0))
    tm = lg.shape[0]
    onehot = jnp.where(lane == i1, 1.0, 0.0) + jnp.where(lane == i2, 1.0, 0.0)
    ri = lax.broadcasted_iota(jnp.int32, (tm, tm), 0)
    ci = lax.broadcasted_iota(jnp.int32, (tm, tm), 1)
    before = _dot(jnp.where(ri > ci, 1.0, 0.0).astype(bf16), onehot.astype(bf16)) + cnt_ref[...]
    rank1 = jnp.sum(jnp.where(lane == i1, before, 0.0), axis=-1, keepdims=True).astype(jnp.int32)
    rank2 = jnp.sum(jnp.where(lane == i2, before, 0.0), axis=-1, keepdims=True).astype(jnp.int32)
    idx_ref[0] = jnp.where(lane == 0, i1 - N_GROUPS, jnp.where(lane == 1, i2 - N_GROUPS,
                           jnp.where(lane == 2, rank1, jnp.where(lane == 3, rank2, 0))))
    cnt_ref[...] = before[tm - 1:tm, :] + onehot[tm - 1:tm, :]


def _route(x, modl, w_grp, b_grp, w_exp, b_exp):
    B, S, D = x.shape
    tm = min(ROW_TILE, S)
    pad = LANES - N_GROUPS - N_EXPERTS
    w = jnp.concatenate([w_grp, w_exp, jnp.zeros((D, pad), f32)], axis=1)
    w_hi = w.astype(bf16)
    b = jnp.concatenate([b_grp, b_exp, jnp.zeros((pad,), f32)]).reshape(1, LANES)
    tile = pl.BlockSpec((1, tm, D), lambda bb, i: (bb, i, 0))
    small = pl.BlockSpec((1, tm, LANES), lambda bb, i: (bb, i, 0))
    return pl.pallas_call(
        _route_kernel,
        name="moe_route",
        grid=(B, S // tm),
        in_specs=[tile, pl.BlockSpec((1, 1, modl.shape[-1]), lambda bb, i: (bb, 0, 0)),
                  _full_spec(w.shape), _full_spec(w.shape), _full_spec(b.shape)],
        out_specs=[tile, small, small, _full_spec((1, LANES))],
        out_shape=[jax.ShapeDtypeStruct((B, S, D), bf16),
                   jax.ShapeDtypeStruct((B, S, LANES), jnp.int32),
                   jax.ShapeDtypeStruct((B, S, LANES), f32),
                   jax.ShapeDtypeStruct((1, LANES), f32)],
        compiler_params=_cparams("arbitrary", "arbitrary"),
    )(x, modl, w_hi, (w - w_hi.astype(f32)).astype(bf16), b)


def _expert_kernel(be_ref, nu_ref, x_ref, wg_ref, wu_ref, wd_ref, o_ref, wg16, wu16, wd16):
    i = pl.program_id(0)

    @pl.when((i == 0) | (be_ref[i] != be_ref[jnp.maximum(i - 1, 0)]))
    def _():
        wg16[...] = wg_ref[0, 0].astype(bf16)
        wu16[...] = wu_ref[0, 0].astype(bf16)
        wd16[...] = wd_ref[0, 0].astype(bf16)

    @pl.when(i < nu_ref[0])
    def _():
        x = x_ref[...]
        hg = _dot(x, wg16[...])
        hu = _dot(x, wu16[...])
        o_ref[...] = _dot((hg * _sigmoid(hg) * hu).astype(bf16), wd16[...]).astype(o_ref.dtype)

    @pl.when(i >= nu_ref[0])
    def _():
        o_ref[...] = jnp.zeros_like(o_ref)


def _expert_kernel_inplace(be_ref, nu_ref, x_ref, wg_ref, wu_ref, wd_ref, prev_ref, o_ref, wg16, wu16, wd16):
    del prev_ref
    _expert_kernel(be_ref, nu_ref, x_ref, wg_ref, wu_ref, wd_ref, o_ref, wg16, wu16, wd16)


def _expert_ffn(xs, block_exp, n_used, wg, wu, wd, layer, n_slots, block_offset, ys_prev=None):
    D = xs.shape[1]
    F = wg.shape[-1]
    blk = MOE_BLOCK
    row_map = lambda i, be, nu, *_: (jnp.maximum(jnp.minimum(i, nu[0] - 1), 0), 0)
    w_map = lambda i, be, nu, *_: (layer, be[i], 0, 0)
    in_specs = [
        pl.BlockSpec((blk, D), row_map),
        pl.BlockSpec((1, 1, D, F), w_map),
        pl.BlockSpec((1, 1, D, F), w_map),
        pl.BlockSpec((1, 1, F, D), w_map),
    ]
    args = [block_exp, n_used, xs, wg, wu, wd]
    aliases = {}
    if ys_prev is not None:
        in_specs.append(pl.BlockSpec(memory_space=pl.ANY))
        args.append(ys_prev)
        aliases = {len(args) - 1: 0}
    grid_spec = pltpu.PrefetchScalarGridSpec(
        num_scalar_prefetch=2,
        grid=(block_exp.shape[0],),
        in_specs=in_specs,
        out_specs=pl.BlockSpec((blk, D), lambda i, be, nu, *_: (i + block_offset, 0)),
        scratch_shapes=[pltpu.VMEM((D, F), bf16), pltpu.VMEM((D, F), bf16), pltpu.VMEM((F, D), bf16)],
    )
    return pl.pallas_call(
        _expert_kernel if ys_prev is None else _expert_kernel_inplace,
        name="moe_experts",
        grid_spec=grid_spec,
        out_shape=jax.ShapeDtypeStruct((n_slots, D), bf16),
        input_output_aliases=aliases,
        compiler_params=_cparams("arbitrary"),
    )(*args)


def _take_rows(table, idx):
    return table.at[idx].get(mode="promise_in_bounds")


def _dispatch(expert_idx, rank, counts, blk):
    T = expert_idx.shape[0]
    A = T * TOP_K
    n_blocks = A // blk + N_EXPERTS
    experts = jnp.arange(N_EXPERTS, dtype=jnp.int32)
    padded = (counts + blk - 1) // blk * blk
    pad_end = jnp.cumsum(padded)
    pad_start = pad_end - padded
    start = jnp.cumsum(counts) - counts
    dest = rank + jnp.sum(jnp.where(expert_idx[..., None] == experts, pad_start, 0), axis=-1)
    block_exp = jnp.minimum(jnp.sum(((jnp.arange(n_blocks, dtype=jnp.int32) * blk)[:, None] >= pad_end[None, :])
                                    .astype(jnp.int32), axis=1), N_EXPERTS - 1)
    order = jnp.argsort(expert_idx.reshape(A)).astype(jnp.int32)
    shift = jnp.sum(jnp.where(block_exp[:, None] == experts, start - pad_start, 0), axis=-1)
    pos = jnp.arange(n_blocks * blk, dtype=jnp.int32) + jnp.repeat(shift, blk)
    slot_tok = _take_rows(order, pos % A) // TOP_K
    n_used = (pad_end[-1] // blk).astype(jnp.int32).reshape(1)
    return dest, slot_tok, block_exp, n_used


def _combine_ln_kernel(x_ref, y0_ref, y1_ref, gate_ref, mod_ref, lng_ref, lnb_ref, o_ref):
    D = x_ref.shape[-1]
    gt = mod_ref[0, :, 5 * D:6 * D]
    gates = gate_ref[0]
    y = gates[:, 0:1] * y0_ref[0].astype(f32) + gates[:, 1:2] * y1_ref[0].astype(f32)
    u = DEEPNORM_ALPHA * x_ref[0] + (1.0 + gt) * y
    o_ref[0] = _layer_norm_rows(u, lng_ref[...], lnb_ref[...])


def _combine_ln(x, y0, y1, gates, modl, ln_g, ln_b):
    B, S, D = x.shape
    tm = min(ROW_TILE, S)
    tile = pl.BlockSpec((1, tm, D), lambda b, i: (b, i, 0))
    return pl.pallas_call(
        _combine_ln_kernel,
        name="moe_combine_ln",
        grid=(B, S // tm),
        in_specs=[tile, tile, tile, pl.BlockSpec((1, tm, LANES), lambda b, i: (b, i, 0)),
                  pl.BlockSpec((1, 1, modl.shape[-1]), lambda b, i: (b, 0, 0)),
                  _full_spec((1, D)), _full_spec((1, D))],
        out_specs=tile,
        out_shape=jax.ShapeDtypeStruct((B, S, D), f32),
        compiler_params=_cparams("arbitrary", "arbitrary"),
    )(x, y0, y1, gates, modl, ln_g.reshape(1, D), ln_b.reshape(1, D))


def _moe_layer(x, modl, w_grp, b_grp, w_exp, b_exp, wg, wu, wd, layer, ln_g, ln_b):
    B, S, D = x.shape
    T = B * S
    hb, idx, gates, cnt = _route(x, modl, w_grp, b_grp, w_exp, b_exp)
    idx = idx.reshape(T, LANES)
    counts = cnt[0, N_GROUPS:N_GROUPS + N_EXPERTS].astype(jnp.int32)
    dest, slot_tok, block_exp, n_used = _dispatch(idx[:, :TOP_K], idx[:, TOP_K:2 * TOP_K], counts, MOE_BLOCK)
    hb = hb.reshape(T, D)
    nb = block_exp.shape[0]
    ys = None
    for part in range(MOE_TABLE_PARTS):
        lo, hi = part * nb // MOE_TABLE_PARTS, (part + 1) * nb // MOE_TABLE_PARTS
        ys = _expert_ffn(_take_rows(hb, slot_tok[lo * MOE_BLOCK:hi * MOE_BLOCK]), block_exp[lo:hi],
                         jnp.clip(n_used - lo, 0, hi - lo), wg, wu, wd, layer, nb * MOE_BLOCK, lo, ys_prev=ys)
    y0 = _take_rows(ys, dest[:, 0]).reshape(B, S, D)
    y1 = _take_rows(ys, dest[:, 1]).reshape(B, S, D)
    return _combine_ln(x, y0, y1, gates, modl, ln_g, ln_b)


def _shared_kv_kernel(x_ref, mod_ref, wk_ref, wvt_ref, wf_ref, wflo_ref, fb_ref, kn_ref, place_ref, ones_ref,
                      k_ref, kb_ref, vt_ref, f_ref, carry_ref):
    @pl.when(pl.program_id(1) == 0)
    def _():
        carry_ref[...] = jnp.zeros_like(carry_ref)

    D = x_ref.shape[-1]
    tm = x_ref.shape[1]
    shift = mod_ref[0, :, 0:D]
    scale = mod_ref[0, :, D:2 * D]
    hk = x_ref[0] * (1.0 + scale) + shift
    hb = hk.astype(bf16)
    k = _dot(hb, wk_ref[...])
    vt_ref[0] = _dot_nt(wvt_ref[...], hb).astype(bf16)
    h_lo = (hk - hb.astype(f32)).astype(bf16)
    f = _dot(hb, wf_ref[...]) + _dot(hb, wflo_ref[...]) + _dot(h_lo, wf_ref[...]) + fb_ref[...]
    log_f = jnp.minimum(f, 0.0) - jnp.log(1.0 + jnp.exp(-jnp.abs(f)))
    row = lax.broadcasted_iota(jnp.int32, log_f.shape, 0)
    acc = log_f
    d = 1
    while d < tm:
        acc = acc + jnp.where(row >= d, pltpu.roll(acc, d, axis=0), 0.0)
        d *= 2
    acc = acc + carry_ref[...]
    f_ref[0] = acc
    carry_ref[...] = acc[tm - 1:tm, :]
    kb_ref[0] = (_dot(_bias_split(acc * LOG2E), place_ref[...]) + ones_ref[...]).astype(bf16)
    jbd = _pair_ones()
    for p in range(D // LANES):
        lp = slice(p * LANES, (p + 1) * LANES)
        seg = k[:, lp]
        ms = _head_sum(seg * seg, jbd) * (1.0 / HEAD_DIM)
        k_ref[0, :, lp] = (seg * lax.rsqrt(ms + QK_EPS) * kn_ref[...]).astype(bf16)


def _shared_kv(x, kvmod, w_kvf, b_f, k_norm):
    B, S, D = x.shape
    H = D // HEAD_DIM
    tm = min(ROW_TILE, S)
    wk = w_kvf[:, :D].astype(bf16)
    wvt = w_kvf[:, D:2 * D].T.astype(bf16)
    wf = jnp.concatenate([w_kvf[:, 2 * D:], jnp.zeros((D, LANES - H), f32)], axis=1)
    wf_hi = wf.astype(bf16)
    fb = jnp.concatenate([b_f, jnp.zeros((LANES - H,), f32)]).reshape(1, LANES)
    place, ones = _bias_placement(D, True)
    tile = pl.BlockSpec((1, tm, D), lambda b, i: (b, i, 0))
    act = jax.ShapeDtypeStruct((B, S, D), bf16)
    return pl.pallas_call(
        _shared_kv_kernel,
        name="shared_kv",
        grid=(B, S // tm),
        in_specs=[tile, pl.BlockSpec((1, 1, 2 * D), lambda b, i: (b, 0, 0)),
                  _full_spec(wk.shape), _full_spec(wvt.shape), _full_spec(wf.shape), _full_spec(wf.shape),
                  _full_spec(fb.shape), _full_spec((1, LANES)), _full_spec(place.shape), _full_spec(ones.shape)],
        out_specs=[tile, tile, pl.BlockSpec((1, D, tm), lambda b, i: (b, 0, i)),
                   pl.BlockSpec((1, tm, LANES), lambda b, i: (b, i, 0))],
        out_shape=[act, act, jax.ShapeDtypeStruct((B, D, S), bf16), jax.ShapeDtypeStruct((B, S, LANES), f32)],
        scratch_shapes=[pltpu.VMEM((1, LANES), f32)],
        compiler_params=_cparams("arbitrary", "arbitrary"),
    )(x, kvmod, wk, wvt, wf_hi, (wf - wf_hi.astype(f32)).astype(bf16), fb,
      jnp.tile(k_norm, LANES // HEAD_DIM).reshape(1, LANES), place, ones)


def _fox_pre_kernel(x_ref, mod_ref, wq_ref, wg_ref, qn_ref, f_ref, place_ref, ones_ref, q_ref, qb_ref, gate_ref):
    D = x_ref.shape[-1]
    sh = mod_ref[0, :, 0:D]
    sc = mod_ref[0, :, D:2 * D]
    hb = (x_ref[0] * (1.0 + sc) + sh).astype(bf16)
    q = _dot(hb, wq_ref[...])
    gate_ref[0] = _sigmoid(_dot(hb, wg_ref[...])).astype(bf16)
    qb_ref[0] = (_dot(_bias_split(f_ref[0] * LOG2E), place_ref[...]) + ones_ref[...]).astype(bf16)
    qscale = HEAD_DIM ** -0.5 * LOG2E
    jbd = _pair_ones()
    for p in range(D // LANES):
        lp = slice(p * LANES, (p + 1) * LANES)
        seg = q[:, lp]
        ms = _head_sum(seg * seg, jbd) * (1.0 / HEAD_DIM)
        q_ref[0, :, lp] = (seg * lax.rsqrt(ms + QK_EPS) * (qn_ref[...] * qscale)).astype(bf16)


def _fox_pre(x, modl, w_qg, q_norm, fcum):
    B, S, D = x.shape
    tm = min(ROW_TILE, S)
    tile = pl.BlockSpec((1, tm, D), lambda b, i: (b, i, 0))
    wq = w_qg[:, :D].astype(bf16)
    wg = w_qg[:, D:].astype(bf16)
    place, ones = _bias_placement(D, False)
    act = jax.ShapeDtypeStruct((B, S, D), bf16)
    return pl.pallas_call(
        _fox_pre_kernel,
        name="fox_pre",
        grid=(B, S // tm),
        in_specs=[tile, pl.BlockSpec((1, 1, modl.shape[-1]), lambda b, i: (b, 0, 0)),
                  _full_spec(wq.shape), _full_spec(wg.shape), _full_spec((1, LANES)),
                  pl.BlockSpec((1, tm, LANES), lambda b, i: (b, i, 0)),
                  _full_spec(place.shape), _full_spec(ones.shape)],
        out_specs=[tile, tile, tile],
        out_shape=[act, act, act],
        compiler_params=_cparams("arbitrary", "arbitrary"),
    )(x, modl, wq, wg, jnp.tile(q_norm, LANES // HEAD_DIM).reshape(1, LANES), fcum, place, ones)


def _fox_attn_kernel(qi_ref, kj_ref, flag_ref, q_ref, qb_ref, k_ref, kb_ref, vt_ref, gate_ref, o_ref,
                     qm_ref, m_ref, l_ref, acc_ref):
    s = pl.program_id(2)
    i = qi_ref[s]
    j = kj_ref[s]
    flags = flag_ref[s]
    N = HEAD_DIM
    nh = q_ref.shape[2] // N
    hpp = LANES // N
    tk, tq = k_ref.shape[1], q_ref.shape[1]
    pair = lambda h: slice((h // hpp) * LANES, (h // hpp + 1) * LANES)

    @pl.when(j == 0)
    def _():
        m_ref[...] = jnp.full_like(m_ref, NEG_BIG)
        l_ref[...] = jnp.zeros_like(l_ref)
        acc_ref[...] = jnp.zeros_like(acc_ref)
        for hh in range(nh):
            qcat = jnp.concatenate([q_ref[0, :, pair(hh)], qb_ref[0, :, pair(hh)]], axis=1).astype(f32)
            head = (lax.broadcasted_iota(jnp.int32, qcat.shape, 1) % LANES) // N
            qm_ref[hh] = jnp.where(head == hh % hpp, qcat, 0.0).astype(bf16)

    def step(masked, q_lo):
        qs = slice(q_lo, tq)
        kcat = [jnp.concatenate([k_ref[0, :, pair(hh)], kb_ref[0, :, pair(hh)]], axis=1)
                for hh in range(0, nh, hpp)]
        sts = [_dot_nt(kcat[hh // hpp], qm_ref[hh, qs, :]) for hh in range(nh)]
        for hh, st in enumerate(sts):
            if masked:
                key = j * tk + lax.broadcasted_iota(jnp.int32, st.shape, 0)
                qry = i * tq + q_lo + lax.broadcasted_iota(jnp.int32, st.shape, 1)
                st = jnp.where(key <= qry, st, -jnp.inf)
            m_prev = m_ref[hh, :, qs]
            m_new = jnp.maximum(m_prev, jnp.max(st, axis=0, keepdims=True))
            alpha = jnp.exp2(m_prev - m_new)
            p = jnp.exp2(st - m_new)
            l_ref[hh, :, qs] = alpha * l_ref[hh, :, qs] + jnp.sum(p, axis=0, keepdims=True)
            acc_ref[hh, :, qs] = alpha * acc_ref[hh, :, qs] + _dot(vt_ref[0, hh * N:(hh + 1) * N, :], p.astype(bf16))
            m_ref[hh, :, qs] = m_new

    @pl.when((flags & (ATTN_FLAG_MASK | ATTN_FLAG_UPPER)) == 0)
    def _():
        step(False, 0)

    @pl.when((flags & (ATTN_FLAG_MASK | ATTN_FLAG_UPPER)) == ATTN_FLAG_MASK)
    def _():
        step(True, 0)

    @pl.when((flags & ATTN_FLAG_UPPER) != 0)
    def _():
        step(True, tq // 2)

    @pl.when((flags & ATTN_FLAG_LAST) != 0)
    def _():
        for pr in range(nh // hpp):
            ot = jnp.concatenate([acc_ref[hh] * (1.0 / l_ref[hh]) for hh in range(pr * hpp, (pr + 1) * hpp)],
                                 axis=0)
            lp = slice(pr * LANES, (pr + 1) * LANES)
            o_ref[0, :, lp] = (ot.T * gate_ref[0, :, lp].astype(f32)).astype(bf16)


def _fox_attn(q, qb, k, kb, vt, gate):
    B, S, D = gate.shape
    tq = min(ATTN_Q_TILE, S)
    tk = min(ATTN_K_TILE, S)
    LW = ATTN_PAIRS * LANES
    nh = LW // HEAD_DIM
    qi, kj, flags = [], [], []
    for i in range(S // tq):
        last = ((i + 1) * tq - 1) // tk
        for j in range(last + 1):
            qi.append(i)
            kj.append(j)
            needs_mask = (j + 1) * tk - 1 > i * tq
            upper_only = j * tk >= i * tq + tq // 2
            flags.append((ATTN_FLAG_MASK if needs_mask else 0) | (ATTN_FLAG_LAST if j == last else 0)
                         | (ATTN_FLAG_UPPER if upper_only else 0))
    tables = [jnp.asarray(t, jnp.int32) for t in (qi, kj, flags)]
    qtile = pl.BlockSpec((1, tq, LW), lambda b, p, s, qi, kj, fl: (b, qi[s], p))
    ktile = pl.BlockSpec((1, tk, LW), lambda b, p, s, qi, kj, fl: (b, kj[s], p))
    grid_spec = pltpu.PrefetchScalarGridSpec(
        num_scalar_prefetch=3,
        grid=(B, D // LW, len(qi)),
        in_specs=[qtile, qtile, ktile, ktile,
                  pl.BlockSpec((1, LW, tk), lambda b, p, s, qi, kj, fl: (b, p, kj[s])),
                  qtile],
        out_specs=qtile,
        scratch_shapes=[pltpu.VMEM((nh, tq, 2 * LANES), bf16),
                        pltpu.VMEM((nh, 1, tq), f32), pltpu.VMEM((nh, 1, tq), f32),
                        pltpu.VMEM((nh, HEAD_DIM, tq), f32)],
    )
    return pl.pallas_call(
        _fox_attn_kernel,
        name="fox_attn",
        grid_spec=grid_spec,
        out_shape=jax.ShapeDtypeStruct((B, S, D), bf16),
        compiler_params=_cparams("arbitrary", "arbitrary", "arbitrary"),
    )(*tables, q, qb, k, kb, vt, gate)


def kernel(x, c, ada_w, ada_b, ln_g, ln_b, rw_mu, rw_rkv, rw_w0, rw_w1, rw_w2, rw_a0, rw_a1, rw_a2, rw_g1, rw_g2, rw_kk, rw_ka, rw_rk, rw_gn_g, rw_gn_b, rw_wo, rw_v0, rw_v1, rw_v2, kv_ada_w, kv_ada_b, kv_w, kv_fb, kv_knorm, fx_wqg, fx_qnorm, fx_wo, moe_wgrp, moe_bgrp, moe_wexp, moe_bexp, moe_wgate, moe_wup, moe_wdown):
    B, S, D = x.shape
    depth = ada_w.shape[0]
    n_a = rw_mu.shape[0]
    mod = _adaln_mod(c, ada_w, ada_b)
    kvmod = _adaln_mod(c, kv_ada_w[None], kv_ada_b[None])[0].reshape(B, 1, 2 * D)
    kv = None
    v_first = None
    for l in range(depth):
        modl = mod[l].reshape(B, 1, 6 * D)
        if l < n_a:
            vmix = None if l == 0 else (rw_v0[l - 1], rw_v1[l - 1], rw_v2[l - 1], v_first)
            r, k, v, wl, kk, b, g = _rwkv_pre(x, modl, rw_mu[l], rw_rkv[l], rw_w0[l], rw_w1[l], rw_w2[l],
                                              rw_a0[l], rw_a1[l], rw_a2[l], rw_g1[l], rw_g2[l],
                                              rw_kk[l], rw_ka[l], vmix)
            if l == 0:
                v_first = v
            z = _wkv7(r, k, v, wl, kk, b, g, rw_rk[l].reshape(D), rw_gn_g[l], rw_gn_b[l])
            w_o = rw_wo[l]
        else:
            j = l - n_a
            k, kb, vt, fcum = kv
            q, qb, gate = _fox_pre(x, modl, fx_wqg[j], fx_qnorm[j], fcum)
            z = _fox_attn(q, qb, k, kb, vt, gate)
            w_o = fx_wo[j]
        x = _proj_ln(z, w_o, x, modl, 2, ln_g[l, 0], ln_b[l, 0])
        x = _moe_layer(x, modl, moe_wgrp[l], moe_bgrp[l], moe_wexp[l], moe_bexp[l],
                       moe_wgate, moe_wup, moe_wdown, l, ln_g[l, 1], ln_b[l, 1])
        if l == n_a - 1:
            kv = _shared_kv(x, kvmod, kv_w, kv_fb, kv_knorm)
    return x
```

```python
import functools
import math

import jax
import jax.numpy as jnp
import numpy as np
from jax import lax
from jax.experimental import pallas as pl
from jax.experimental.pallas import tpu as pltpu

HEAD_DIM = 64
N_GROUPS = 4
EXP_PER_GROUP = 8
N_EXPERTS = N_GROUPS * EXP_PER_GROUP
TOP_K = 2
DEPTH = 4
N_A = 2
DEEPNORM_ALPHA = (2 * DEPTH) ** 0.25
LN_EPS = 1e-5
GN_EPS = 64e-5
QK_EPS = 1e-6

LANES = 128
VMEM_LIMIT_BYTES = 56 * 1024 * 1024
WKV_CHUNK = 64
WKV_TILE = 128
ROW_TILE = 512
RWKV_PRE_TILE = 512
ATTN_Q_TILE = 1024
ATTN_K_TILE = 512
ATTN_PAIRS = 2
ATTN_FLAG_MASK = 1
ATTN_FLAG_LAST = 2
ATTN_FLAG_UPPER = 4
MOE_BLOCK = 512
MOE_TABLE_PARTS = 1
NEG_BIG = -1e30
LOG2E = math.log2(math.e)

f32 = jnp.float32
bf16 = jnp.bfloat16
HIGHEST = lax.Precision.HIGHEST


def _cparams(*sem):
    return pltpu.CompilerParams(dimension_semantics=sem, vmem_limit_bytes=VMEM_LIMIT_BYTES)


def _sigmoid(x):
    return 0.5 * jnp.tanh(0.5 * x) + 0.5


def _dot(a, b):
    return jnp.dot(a, b, preferred_element_type=f32)


def _dot_nt(a, b):
    return lax.dot_general(a, b, (((1,), (1,)), ((), ())), preferred_element_type=f32)


def _dot_tn(a, b):
    return lax.dot_general(a, b, (((0,), (0,)), ((), ())), preferred_element_type=f32)


def _full_spec(shape):
    n = len(shape)
    return pl.BlockSpec(shape, lambda *_: (0,) * n)


def _pair_ones():
    rs = lax.broadcasted_iota(jnp.int32, (LANES, LANES), 0)
    cs = lax.broadcasted_iota(jnp.int32, (LANES, LANES), 1)
    return jnp.where(jnp.where(rs >= HEAD_DIM, 1, 0) == jnp.where(cs >= HEAD_DIM, 1, 0), 1.0, 0.0).astype(bf16)


def _head_sum(x, jbd):
    xh = x.astype(bf16)
    xl = (x - xh.astype(f32)).astype(bf16)
    return _dot(xh, jbd) + _dot(xl, jbd)


def _bias_split(f):
    t0 = f.astype(bf16)
    r1 = f - t0.astype(f32)
    t1 = r1.astype(bf16)
    t2 = (r1 - t1.astype(f32)).astype(bf16)
    return jnp.concatenate([t0, t1, t2], axis=1)


def _bias_placement(D, is_key):
    H = D // HEAD_DIM
    place = np.zeros((3 * LANES, D), np.float32)
    ones = np.zeros((1, D), np.float32)
    for h in range(H):
        for i in range(3):
            place[i * LANES + h, h * HEAD_DIM + (3 + i if is_key else i)] = -1.0 if is_key else 1.0
            ones[0, h * HEAD_DIM + (i if is_key else 3 + i)] = 1.0
    return jnp.asarray(place, bf16), jnp.asarray(ones, f32)


def _layer_norm_rows(u, g, b):
    mu = jnp.mean(u, axis=-1, keepdims=True)
    d = u - mu
    var = jnp.mean(d * d, axis=-1, keepdims=True)
    return d * lax.rsqrt(var + LN_EPS) * g + b


def _mod_kernel(c_ref, w_ref, b_ref, o_ref):
    c = c_ref[...]
    cs = c * _sigmoid(c)
    o_ref[0] = jnp.dot(cs, w_ref[0], preferred_element_type=f32, precision=HIGHEST) + b_ref[0]


def _adaln_mod(c, w, b):
    L, D, N = w.shape
    B = c.shape[0]
    tn = min(N, 1024)
    return pl.pallas_call(
        _mod_kernel,
        name="adaln_mod",
        grid=(L, N // tn),
        in_specs=[
            _full_spec((B, D)),
            pl.BlockSpec((1, D, tn), lambda l, j: (l, 0, j)),
            pl.BlockSpec((1, 1, tn), lambda l, j: (l, 0, j)),
        ],
        out_specs=pl.BlockSpec((1, B, tn), lambda l, j: (l, 0, j)),
        out_shape=jax.ShapeDtypeStruct((L, B, N), f32),
        compiler_params=_cparams("arbitrary", "arbitrary"),
    )(c, w, b.reshape(L, 1, N))


def _rwkv_pre_kernel(has_vmix, x_ref, xp_ref, mod_ref, mu_ref, wrkv_ref, w0_ref, w1_ref, w2_ref,
                     a0_ref, a1_ref, a2_ref, g1_ref, g2_ref, kkp_ref, kap_ref, *rest):
    if has_vmix:
        v0_ref, v1_ref, v2_ref, vf_ref = rest[:4]
        rest = rest[4:]
    r_ref, k_ref, v_ref, wl_ref, kk_ref, b_ref, g_ref = rest
    D = x_ref.shape[-1]
    sh = mod_ref[0, :, 0:D]
    sc = mod_ref[0, :, D:2 * D]
    h = x_ref[0] * (1.0 + sc) + sh
    hp = xp_ref[0][7:8, :] * (1.0 + sc) + sh
    hp = jnp.where(pl.program_id(1) == 0, 0.0, hp)
    row = lax.broadcasted_iota(jnp.int32, h.shape, 0)
    h_prev = jnp.where(row == 0, hp, pltpu.roll(h, 1, axis=0))
    xx = h_prev - h
    mix = lambda j: (h + xx * mu_ref[j:j + 1, :]).astype(bf16)
    xr, xw, xk, xv, xa, xg = [mix(j) for j in range(6)]
    r = _dot(xr, wrkv_ref[0])
    k = _dot(xk, wrkv_ref[1])
    v = _dot(xv, wrkv_ref[2])
    zw = w0_ref[...] + _dot(jnp.tanh(_dot(xw, w1_ref[...])).astype(bf16), w2_ref[...])
    wl_ref[0] = -math.exp(-0.5) * _sigmoid(zw)
    a = _sigmoid(a0_ref[...] + _dot(_dot(xa, a1_ref[...]).astype(bf16), a2_ref[...]))
    g = _dot(_sigmoid(_dot(xg, g1_ref[...])).astype(bf16), g2_ref[...])
    if has_vmix:
        vmix = _sigmoid(v0_ref[...] + _dot(_dot(xv, v1_ref[...]).astype(bf16), v2_ref[...]))
        v = v + (vf_ref[0].astype(f32) - v) * vmix
    r_ref[0] = r.astype(bf16)
    v_ref[0] = v.astype(bf16)
    g_ref[0] = g.astype(bf16)
    k_ref[0] = (k * (1.0 + (a - 1.0) * kap_ref[...])).astype(bf16)
    kk = k * kkp_ref[...]
    jbd = _pair_ones()
    for p in range(D // LANES):
        lp = slice(p * LANES, (p + 1) * LANES)
        seg = kk[:, lp]
        nrm = jnp.sqrt(_head_sum(seg * seg, jbd))
        seg = seg / jnp.maximum(nrm, 1e-12)
        kk_ref[0, :, lp] = seg.astype(bf16)
        b_ref[0, :, lp] = (seg * a[:, lp]).astype(bf16)


def _rwkv_pre(x, modl, mu, wrkv, w0, w1, w2, a0, a1, a2, g1, g2, kkp, kap, vmix):
    B, S, D = x.shape
    tm = min(RWKV_PRE_TILE, S)
    row = lambda t: t.reshape(1, D)
    args = [x, x, modl, mu, wrkv.astype(bf16), row(w0), w1.astype(bf16), w2.astype(bf16), row(a0),
            a1.astype(bf16), a2.astype(bf16), g1.astype(bf16), g2.astype(bf16), row(kkp), row(kap)]
    tile = pl.BlockSpec((1, tm, D), lambda b, i: (b, i, 0))
    in_specs = [
        tile,
        pl.BlockSpec((1, 8, D), lambda b, i: (b, jnp.maximum(i * (tm // 8) - 1, 0), 0)),
        pl.BlockSpec((1, 1, modl.shape[-1]), lambda b, i: (b, 0, 0)),
    ] + [_full_spec(a.shape) for a in args[3:]]
    if vmix is not None:
        v0, v1, v2, v_first = vmix
        extra = [row(v0), v1.astype(bf16), v2.astype(bf16)]
        args += extra + [v_first]
        in_specs += [_full_spec(a.shape) for a in extra] + [tile]
    out_bf = jax.ShapeDtypeStruct((B, S, D), bf16)
    out_shape = [out_bf, out_bf, out_bf, jax.ShapeDtypeStruct((B, S, D), f32), out_bf, out_bf, out_bf]
    return pl.pallas_call(
        functools.partial(_rwkv_pre_kernel, vmix is not None),
        name="rwkv_pre",
        grid=(B, S // tm),
        in_specs=in_specs,
        out_specs=[tile] * 7,
        out_shape=out_shape,
        compiler_params=_cparams("arbitrary", "arbitrary"),
    )(*args)


def _wkv_kernel(C, r_ref, k_ref, v_ref, wl_ref, kk_ref, b_ref, g_ref, rk_ref, gng_ref, gnb_ref,
                o_ref, state_ref):
    @pl.when(pl.program_id(1) == 0)
    def _():
        state_ref[...] = jnp.zeros_like(state_ref)

    T, D = r_ref.shape[1], r_ref.shape[2]
    N = HEAD_DIM
    P = D // LANES
    C2 = 2 * C
    ri = lax.broadcasted_iota(jnp.int32, (C, C), 0)
    ci = lax.broadcasted_iota(jnp.int32, (C, C), 1)
    tri_incl = jnp.where(ri >= ci, 1.0, 0.0).astype(f32)
    r2 = lax.broadcasted_iota(jnp.int32, (C2, C2), 0)
    c2 = lax.broadcasted_iota(jnp.int32, (C2, C2), 1)
    dlt = jnp.where(jnp.where(r2 >= C, 1, 0) == jnp.where(c2 >= C, 1, 0), r2 - c2, -1)
    strict = dlt > 0
    incl = dlt >= 0
    eye2 = jnp.where(r2 == c2, 1.0, 0.0).astype(f32)
    h0 = lax.broadcasted_iota(jnp.int32, (C, LANES), 1) < N
    rs = lax.broadcasted_iota(jnp.int32, (LANES, LANES), 0)
    cs = lax.broadcasted_iota(jnp.int32, (LANES, LANES), 1)
    sbd = jnp.where(rs >= N, 1, 0) == jnp.where(cs >= N, 1, 0)
    jbd = jnp.where(sbd, 1.0, 0.0).astype(bf16)
    stack2 = lambda t: jnp.concatenate([t, t], axis=0)

    lanes = [slice(p * LANES, (p + 1) * LANES) for p in range(P)]
    nc = T // C
    units = [(c, p) for c in range(nc) for p in range(P)]
    hi_lo = lambda t: (t.astype(bf16), (t - t.astype(bf16).astype(f32)).astype(bf16))

    ch = []
    for c in range(nc):
        sl = pl.ds(c * C, C)
        wl = wl_ref[0, sl, :]
        cum = jnp.dot(tri_incl, wl, preferred_element_type=f32, precision=HIGHEST)
        g_incl = jnp.exp(cum)
        g_inv = jnp.exp(-cum)
        g_last = g_incl[C - 1:C, :]
        r = r_ref[0, sl, :].astype(f32)
        k = k_ref[0, sl, :].astype(f32)
        btf = b_ref[0, sl, :].astype(f32) * g_inv
        ktf = k * g_inv
        ch.append(dict(
            sl=sl, g_last=g_last, v16=v_ref[0, sl, :],
            af=-kk_ref[0, sl, :].astype(f32) * jnp.exp(cum - wl), rf=r * g_incl,
            bt=btf.astype(bf16), kt=ktf.astype(bf16),
            bc=(btf * g_last).astype(bf16), kc=(ktf * g_last).astype(bf16),
            rkr=r * k * rk_ref[...]))

    Ls, Aak, Arow, Vst, X2 = [], [], [], [], []
    for c, p in units:
        d, lp = ch[c], lanes[p]
        afp, rfp = d["af"][:, lp], d["rf"][:, lp]
        X4 = jnp.concatenate([jnp.where(h0, afp, 0.0), jnp.where(h0, 0.0, afp),
                              jnp.where(h0, rfp, 0.0), jnp.where(h0, 0.0, rfp)], axis=0).astype(bf16)
        M4 = _dot_nt(X4, jnp.concatenate([stack2(d["bt"][:, lp]), stack2(d["kt"][:, lp])], axis=0))
        Mb, Mk = M4[:, :C2], M4[:, C2:]
        Ls.append(jnp.where(strict, Mb[:C2], 0.0))
        Aak.append(jnp.where(strict, Mk[:C2], 0.0).astype(bf16))
        Arow.append(jnp.concatenate([jnp.where(incl, Mb[C2:], 0.0), jnp.where(incl, Mk[C2:], 0.0)],
                                    axis=1).astype(bf16))
        Vst.append(stack2(d["v16"][:, lp]))
        X2.append(jnp.concatenate([afp, rfp], axis=0).astype(bf16))
    Ps = [eye2 + L for L in Ls]
    n = 1
    while 2 * n < C:
        Lb = [L.astype(bf16) for L in Ls]
        Ls = [_dot(x, x) for x in Lb]
        Ps = [Pm + _dot(L.astype(bf16), Pm.astype(bf16)) for L, Pm in zip(Ls, Ps)]
        n *= 2
    Ps = [Pm.astype(bf16) for Pm in Ps]
    AV = [_dot(a, vs) for a, vs in zip(Aak, Vst)]
    ys = []
    state = [state_ref[p] for p in range(P)]
    for c in range(nc):
        d = ch[c]
        us = range(c * P, (c + 1) * P)
        M2 = [_dot_nt(X2[u], state[u - c * P].astype(bf16)) for u in us]
        Ust = [_dot(Ps[u], (AV[u] + stack2(m2[:C])).astype(bf16)) for u, m2 in zip(us, M2)]
        Yst = [_dot(Arow[u], jnp.concatenate([ust.astype(bf16), Vst[u]], axis=0)) + stack2(m2[C:])
               for u, ust, m2 in zip(us, Ust, M2)]
        ys += [jnp.where(h0, yst[:C], yst[C:]) for yst in Yst]
        for p, ust in enumerate(Ust):
            lp = lanes[p]
            u16 = jnp.where(h0, ust[:C], ust[C:]).astype(bf16)
            upd = _dot_tn(jnp.concatenate([u16, d["v16"][:, lp]], axis=0),
                          jnp.concatenate([d["bc"][:, lp], d["kc"][:, lp]], axis=0))
            state[p] = state[p] * d["g_last"][:, lp] + jnp.where(sbd, upd, 0.0)
    for p in range(P):
        state_ref[p] = state[p]
    sums = []
    for u, (c, p) in enumerate(units):
        sums.append(_dot(jnp.concatenate(hi_lo(ys[u]) + hi_lo(ch[c]["rkr"][:, lanes[p]]), axis=0), jbd))
    yds = [ys[u] - (sums[u][:C] + sums[u][C:C2]) * (1.0 / N) for u in range(len(units))]
    var = [_dot(jnp.concatenate(hi_lo(yd * yd), axis=0), jbd) for yd in yds]
    for u, (c, p) in enumerate(units):
        d, lp = ch[c], lanes[p]
        yn = yds[u] * lax.rsqrt((var[u][:C] + var[u][C:]) * (1.0 / N) + GN_EPS)
        bonus = (sums[u][C2:C2 + C] + sums[u][C2 + C:]) * d["v16"][:, lp].astype(f32)
        z = (yn * gng_ref[:, lp] + gnb_ref[:, lp] + bonus) * g_ref[0, d["sl"], lp].astype(f32)
        o_ref[0, d["sl"], lp] = z.astype(bf16)


def _wkv7(r, k, v, wl, kk, b, g, r_k, gn_g, gn_b):
    B, S, D = r.shape
    T = min(WKV_TILE, S)
    C = min(WKV_CHUNK, T)
    tile = pl.BlockSpec((1, T, D), lambda bb, i: (bb, i, 0))
    vec = _full_spec((1, D))
    return pl.pallas_call(
        functools.partial(_wkv_kernel, C),
        name="wkv7",
        grid=(B, S // T),
        in_specs=[tile] * 7 + [vec] * 3,
        out_specs=tile,
        out_shape=jax.ShapeDtypeStruct((B, S, D), bf16),
        scratch_shapes=[pltpu.VMEM((D // LANES, LANES, LANES), f32)],
        compiler_params=_cparams("arbitrary", "arbitrary"),
    )(r, k, v, wl, kk, b, g, r_k.reshape(1, D), gn_g.reshape(1, D), gn_b.reshape(1, D))


def _proj_route_kernel(z_ref, wo_ref, x_ref, mod_ref, lng_ref, lnb_ref, w_ref, wlo_ref, b_ref,
                       xo_ref, h_ref, idx_ref, gate_ref, cnt_ref):
    @pl.when((pl.program_id(0) == 0) & (pl.program_id(1) == 0))
    def _():
        cnt_ref[...] = jnp.zeros_like(cnt_ref)

    D = x_ref.shape[-1]
    gt = mod_ref[0, :, 2 * D:3 * D]
    u = DEEPNORM_ALPHA * x_ref[0] + (1.0 + gt) * _dot(z_ref[0], wo_ref[...])
    xn = _layer_norm_rows(u, lng_ref[...], lnb_ref[...])
    xo_ref[0] = xn
    sh = mod_ref[0, :, 3 * D:4 * D]
    sc = mod_ref[0, :, 4 * D:5 * D]
    h = xn * (1.0 + sc) + sh
    h_ref[0] = h.astype(bf16)
    h_hi = h.astype(bf16)
    h_lo = (h - h_hi.astype(f32)).astype(bf16)
    lg = _dot(h_hi, w_ref[...]) + _dot(h_hi, wlo_ref[...]) + _dot(h_lo, w_ref[...]) + b_ref[...]
    lane = lax.broadcasted_iota(jnp.int32, lg.shape, 1)
    glog = jnp.where(lane < N_GROUPS, lg, -jnp.inf)
    gmax = jnp.max(glog, axis=-1, keepdims=True)
    p_g = 1.0 / jnp.sum(jnp.exp(glog - gmax), axis=-1, keepdims=True)
    g_sel = jnp.min(jnp.where(glog == gmax, lane, LANES), axis=-1, keepdims=True)
    e_lo = N_GROUPS + EXP_PER_GROUP * g_sel
    elog = jnp.where((lane >= e_lo) & (lane < e_lo + EXP_PER_GROUP), lg, -jnp.inf)
    e1 = jnp.max(elog, axis=-1, keepdims=True)
    esum = jnp.sum(jnp.exp(elog - e1), axis=-1, keepdims=True)
    i1 = jnp.min(jnp.where(elog == e1, lane, LANES), axis=-1, keepdims=True)
    elog2 = jnp.where(lane == i1, -jnp.inf, elog)
    e2 = jnp.max(elog2, axis=-1, keepdims=True)
    i2 = jnp.min(jnp.where(elog2 == e2, lane, LANES), axis=-1, keepdims=True)
    p1 = 1.0 / esum
    p2 = jnp.exp(e2 - e1) / esum
    psum = p1 + p2
    gate_ref[0] = jnp.where(lane == 0, p_g * p1 / psum, jnp.where(lane == 1, p_g * p2 / psum, 0.0))
    tm = lg.shape[0]
    onehot = jnp.where(lane == i1, 1.0, 0.0) + jnp.where(lane == i2, 1.0, 0.0)
    ri = lax.broadcasted_iota(jnp.int32, (tm, tm), 0)
    ci = lax.broadcasted_iota(jnp.int32, (tm, tm), 1)
    before = _dot(jnp.where(ri > ci, 1.0, 0.0).astype(bf16), onehot.astype(bf16)) + cnt_ref[...]
    rank1 = jnp.sum(jnp.where(lane == i1, before, 0.0), axis=-1, keepdims=True).astype(jnp.int32)
    rank2 = jnp.sum(jnp.where(lane == i2, before, 0.0), axis=-1, keepdims=True).astype(jnp.int32)
    idx_ref[0] = jnp.where(lane == 0, i1 - N_GROUPS, jnp.where(lane == 1, i2 - N_GROUPS,
                           jnp.where(lane == 2, rank1, jnp.where(lane == 3, rank2, 0))))
    cnt_ref[...] = before[tm - 1:tm, :] + onehot[tm - 1:tm, :]


def _proj_route(z, w_o, x, modl, ln_g, ln_b, w_grp, b_grp, w_exp, b_exp):
    B, S, D = x.shape
    tm = min(ROW_TILE, S)
    pad = LANES - N_GROUPS - N_EXPERTS
    w = jnp.concatenate([w_grp, w_exp, jnp.zeros((D, pad), f32)], axis=1)
    w_hi = w.astype(bf16)
    b = jnp.concatenate([b_grp, b_exp, jnp.zeros((pad,), f32)]).reshape(1, LANES)
    tile = pl.BlockSpec((1, tm, D), lambda bb, i: (bb, i, 0))
    small = pl.BlockSpec((1, tm, LANES), lambda bb, i: (bb, i, 0))
    return pl.pallas_call(
        _proj_route_kernel,
        name="proj_route",
        grid=(B, S // tm),
        in_specs=[tile, _full_spec(w_o.shape), tile,
                  pl.BlockSpec((1, 1, modl.shape[-1]), lambda bb, i: (bb, 0, 0)),
                  _full_spec((1, D)), _full_spec((1, D)),
                  _full_spec(w.shape), _full_spec(w.shape), _full_spec(b.shape)],
        out_specs=[tile, tile, small, small, _full_spec((1, LANES))],
        out_shape=[jax.ShapeDtypeStruct((B, S, D), f32),
                   jax.ShapeDtypeStruct((B, S, D), bf16),
                   jax.ShapeDtypeStruct((B, S, LANES), jnp.int32),
                   jax.ShapeDtypeStruct((B, S, LANES), f32),
                   jax.ShapeDtypeStruct((1, LANES), f32)],
        compiler_params=_cparams("arbitrary", "arbitrary"),
    )(z, w_o.astype(bf16), x, modl, ln_g.reshape(1, D), ln_b.reshape(1, D),
      w_hi, (w - w_hi.astype(f32)).astype(bf16), b)


def _expert_kernel(be_ref, nu_ref, x_ref, wg_ref, wu_ref, wd_ref, o_ref, wg16, wu16, wd16):
    i = pl.program_id(0)

    @pl.when((i == 0) | (be_ref[i] != be_ref[jnp.maximum(i - 1, 0)]))
    def _():
        wg16[...] = wg_ref[0, 0].astype(bf16)
        wu16[...] = wu_ref[0, 0].astype(bf16)
        wd16[...] = wd_ref[0, 0].astype(bf16)

    @pl.when(i < nu_ref[0])
    def _():
        x = x_ref[...]
        hg = _dot(x, wg16[...])
        hu = _dot(x, wu16[...])
        o_ref[...] = _dot((hg * _sigmoid(hg) * hu).astype(bf16), wd16[...]).astype(o_ref.dtype)

    @pl.when(i >= nu_ref[0])
    def _():
        o_ref[...] = jnp.zeros_like(o_ref)


def _expert_kernel_inplace(be_ref, nu_ref, x_ref, wg_ref, wu_ref, wd_ref, prev_ref, o_ref, wg16, wu16, wd16):
    del prev_ref
    _expert_kernel(be_ref, nu_ref, x_ref, wg_ref, wu_ref, wd_ref, o_ref, wg16, wu16, wd16)


def _expert_ffn(xs, block_exp, n_used, wg, wu, wd, layer, n_slots, block_offset, ys_prev=None):
    D = xs.shape[1]
    F = wg.shape[-1]
    blk = MOE_BLOCK
    row_map = lambda i, be, nu, *_: (jnp.maximum(jnp.minimum(i, nu[0] - 1), 0), 0)
    w_map = lambda i, be, nu, *_: (layer, be[i], 0, 0)
    in_specs = [
        pl.BlockSpec((blk, D), row_map),
        pl.BlockSpec((1, 1, D, F), w_map),
        pl.BlockSpec((1, 1, D, F), w_map),
        pl.BlockSpec((1, 1, F, D), w_map),
    ]
    args = [block_exp, n_used, xs, wg, wu, wd]
    aliases = {}
    if ys_prev is not None:
        in_specs.append(pl.BlockSpec(memory_space=pl.ANY))
        args.append(ys_prev)
        aliases = {len(args) - 1: 0}
    grid_spec = pltpu.PrefetchScalarGridSpec(
        num_scalar_prefetch=2,
        grid=(block_exp.shape[0],),
        in_specs=in_specs,
        out_specs=pl.BlockSpec((blk, D), lambda i, be, nu, *_: (i + block_offset, 0)),
        scratch_shapes=[pltpu.VMEM((D, F), bf16), pltpu.VMEM((D, F), bf16), pltpu.VMEM((F, D), bf16)],
    )
    return pl.pallas_call(
        _expert_kernel if ys_prev is None else _expert_kernel_inplace,
        name="moe_experts",
        grid_spec=grid_spec,
        out_shape=jax.ShapeDtypeStruct((n_slots, D), bf16),
        input_output_aliases=aliases,
        compiler_params=_cparams("arbitrary"),
    )(*args)


def _take_rows(table, idx):
    return table.at[idx].get(mode="promise_in_bounds")


def _dispatch(expert_idx, rank, counts, blk):
    T = expert_idx.shape[0]
    A = T * TOP_K
    n_blocks = A // blk + N_EXPERTS
    experts = jnp.arange(N_EXPERTS, dtype=jnp.int32)
    padded = (counts + blk - 1) // blk * blk
    pad_end = jnp.cumsum(padded)
    pad_start = pad_end - padded
    start = jnp.cumsum(counts) - counts
    dest = rank + jnp.sum(jnp.where(expert_idx[..., None] == experts, pad_start, 0), axis=-1)
    block_exp = jnp.minimum(jnp.sum(((jnp.arange(n_blocks, dtype=jnp.int32) * blk)[:, None] >= pad_end[None, :])
                                    .astype(jnp.int32), axis=1), N_EXPERTS - 1)
    order = jnp.argsort(expert_idx.reshape(A)).astype(jnp.int32)
    shift = jnp.sum(jnp.where(block_exp[:, None] == experts, start - pad_start, 0), axis=-1)
    pos = jnp.arange(n_blocks * blk, dtype=jnp.int32) + jnp.repeat(shift, blk)
    slot_tok = _take_rows(order, pos % A) // TOP_K
    n_used = (pad_end[-1] // blk).astype(jnp.int32).reshape(1)
    return dest, slot_tok, block_exp, n_used


def _combine_ln_kernel(x_ref, y0_ref, y1_ref, gate_ref, mod_ref, lng_ref, lnb_ref, o_ref):
    D = x_ref.shape[-1]
    gt = mod_ref[0, :, 5 * D:6 * D]
    gates = gate_ref[0]
    y = gates[:, 0:1] * y0_ref[0].astype(f32) + gates[:, 1:2] * y1_ref[0].astype(f32)
    u = DEEPNORM_ALPHA * x_ref[0] + (1.0 + gt) * y
    o_ref[0] = _layer_norm_rows(u, lng_ref[...], lnb_ref[...])


def _combine_ln(x, y0, y1, gates, modl, ln_g, ln_b):
    B, S, D = x.shape
    tm = min(ROW_TILE, S)
    tile = pl.BlockSpec((1, tm, D), lambda b, i: (b, i, 0))
    return pl.pallas_call(
        _combine_ln_kernel,
        name="moe_combine_ln",
        grid=(B, S // tm),
        in_specs=[tile, tile, tile, pl.BlockSpec((1, tm, LANES), lambda b, i: (b, i, 0)),
                  pl.BlockSpec((1, 1, modl.shape[-1]), lambda b, i: (b, 0, 0)),
                  _full_spec((1, D)), _full_spec((1, D))],
        out_specs=tile,
        out_shape=jax.ShapeDtypeStruct((B, S, D), f32),
        compiler_params=_cparams("arbitrary", "arbitrary"),
    )(x, y0, y1, gates, modl, ln_g.reshape(1, D), ln_b.reshape(1, D))


def _moe_layer(x, routed, modl, wg, wu, wd, layer, ln_g, ln_b):
    B, S, D = x.shape
    T = B * S
    hb, idx, gates, cnt = routed
    idx = idx.reshape(T, LANES)
    counts = cnt[0, N_GROUPS:N_GROUPS + N_EXPERTS].astype(jnp.int32)
    dest, slot_tok, block_exp, n_used = _dispatch(idx[:, :TOP_K], idx[:, TOP_K:2 * TOP_K], counts, MOE_BLOCK)
    hb = hb.reshape(T, D)
    nb = block_exp.shape[0]
    ys = None
    for part in range(MOE_TABLE_PARTS):
        lo, hi = part * nb // MOE_TABLE_PARTS, (part + 1) * nb // MOE_TABLE_PARTS
        ys = _expert_ffn(_take_rows(hb, slot_tok[lo * MOE_BLOCK:hi * MOE_BLOCK]), block_exp[lo:hi],
                         jnp.clip(n_used - lo, 0, hi - lo), wg, wu, wd, layer, nb * MOE_BLOCK, lo, ys_prev=ys)
    y0 = _take_rows(ys, dest[:, 0]).reshape(B, S, D)
    y1 = _take_rows(ys, dest[:, 1]).reshape(B, S, D)
    return _combine_ln(x, y0, y1, gates, modl, ln_g, ln_b)


def _shared_kv_kernel(x_ref, mod_ref, wk_ref, wvt_ref, wf_ref, wflo_ref, fb_ref, kn_ref, place_ref, ones_ref,
                      k_ref, kb_ref, vt_ref, f_ref, carry_ref):
    @pl.when(pl.program_id(1) == 0)
    def _():
        carry_ref[...] = jnp.zeros_like(carry_ref)

    D = x_ref.shape[-1]
    tm = x_ref.shape[1]
    shift = mod_ref[0, :, 0:D]
    scale = mod_ref[0, :, D:2 * D]
    hk = x_ref[0] * (1.0 + scale) + shift
    hb = hk.astype(bf16)
    k = _dot(hb, wk_ref[...])
    vt_ref[0] = _dot_nt(wvt_ref[...], hb).astype(bf16)
    h_lo = (hk - hb.astype(f32)).astype(bf16)
    f = _dot(hb, wf_ref[...]) + _dot(hb, wflo_ref[...]) + _dot(h_lo, wf_ref[...]) + fb_ref[...]
    log_f = jnp.minimum(f, 0.0) - jnp.log(1.0 + jnp.exp(-jnp.abs(f)))
    row = lax.broadcasted_iota(jnp.int32, log_f.shape, 0)
    acc = log_f
    d = 1
    while d < tm:
        acc = acc + jnp.where(row >= d, pltpu.roll(acc, d, axis=0), 0.0)
        d *= 2
    acc = acc + carry_ref[...]
    f_ref[0] = acc
    carry_ref[...] = acc[tm - 1:tm, :]
    kb_ref[0] = (_dot(_bias_split(acc * LOG2E), place_ref[...]) + ones_ref[...]).astype(bf16)
    jbd = _pair_ones()
    for p in range(D // LANES):
        lp = slice(p * LANES, (p + 1) * LANES)
        seg = k[:, lp]
        ms = _head_sum(seg * seg, jbd) * (1.0 / HEAD_DIM)
        k_ref[0, :, lp] = (seg * lax.rsqrt(ms + QK_EPS) * kn_ref[...]).astype(bf16)


def _shared_kv(x, kvmod, w_kvf, b_f, k_norm):
    B, S, D = x.shape
    H = D // HEAD_DIM
    tm = min(ROW_TILE, S)
    wk = w_kvf[:, :D].astype(bf16)
    wvt = w_kvf[:, D:2 * D].T.astype(bf16)
    wf = jnp.concatenate([w_kvf[:, 2 * D:], jnp.zeros((D, LANES - H), f32)], axis=1)
    wf_hi = wf.astype(bf16)
    fb = jnp.concatenate([b_f, jnp.zeros((LANES - H,), f32)]).reshape(1, LANES)
    place, ones = _bias_placement(D, True)
    tile = pl.BlockSpec((1, tm, D), lambda b, i: (b, i, 0))
    act = jax.ShapeDtypeStruct((B, S, D), bf16)
    return pl.pallas_call(
        _shared_kv_kernel,
        name="shared_kv",
        grid=(B, S // tm),
        in_specs=[tile, pl.BlockSpec((1, 1, 2 * D), lambda b, i: (b, 0, 0)),
                  _full_spec(wk.shape), _full_spec(wvt.shape), _full_spec(wf.shape), _full_spec(wf.shape),
                  _full_spec(fb.shape), _full_spec((1, LANES)), _full_spec(place.shape), _full_spec(ones.shape)],
        out_specs=[tile, tile, pl.BlockSpec((1, D, tm), lambda b, i: (b, 0, i)),
                   pl.BlockSpec((1, tm, LANES), lambda b, i: (b, i, 0))],
        out_shape=[act, act, jax.ShapeDtypeStruct((B, D, S), bf16), jax.ShapeDtypeStruct((B, S, LANES), f32)],
        scratch_shapes=[pltpu.VMEM((1, LANES), f32)],
        compiler_params=_cparams("arbitrary", "arbitrary"),
    )(x, kvmod, wk, wvt, wf_hi, (wf - wf_hi.astype(f32)).astype(bf16), fb,
      jnp.tile(k_norm, LANES // HEAD_DIM).reshape(1, LANES), place, ones)


def _fox_pre_kernel(x_ref, mod_ref, wq_ref, wg_ref, qn_ref, f_ref, place_ref, ones_ref, q_ref, qb_ref, gate_ref):
    D = x_ref.shape[-1]
    sh = mod_ref[0, :, 0:D]
    sc = mod_ref[0, :, D:2 * D]
    hb = (x_ref[0] * (1.0 + sc) + sh).astype(bf16)
    q = _dot(hb, wq_ref[...])
    gate_ref[0] = _sigmoid(_dot(hb, wg_ref[...])).astype(bf16)
    qb_ref[0] = (_dot(_bias_split(f_ref[0] * LOG2E), place_ref[...]) + ones_ref[...]).astype(bf16)
    qscale = HEAD_DIM ** -0.5 * LOG2E
    jbd = _pair_ones()
    for p in range(D // LANES):
        lp = slice(p * LANES, (p + 1) * LANES)
        seg = q[:, lp]
        ms = _head_sum(seg * seg, jbd) * (1.0 / HEAD_DIM)
        q_ref[0, :, lp] = (seg * lax.rsqrt(ms + QK_EPS) * (qn_ref[...] * qscale)).astype(bf16)


def _fox_pre(x, modl, w_qg, q_norm, fcum):
    B, S, D = x.shape
    tm = min(ROW_TILE, S)
    tile = pl.BlockSpec((1, tm, D), lambda b, i: (b, i, 0))
    wq = w_qg[:, :D].astype(bf16)
    wg = w_qg[:, D:].astype(bf16)
    place, ones = _bias_placement(D, False)
    act = jax.ShapeDtypeStruct((B, S, D), bf16)
    return pl.pallas_call(
        _fox_pre_kernel,
        name="fox_pre",
        grid=(B, S // tm),
        in_specs=[tile, pl.BlockSpec((1, 1, modl.shape[-1]), lambda b, i: (b, 0, 0)),
                  _full_spec(wq.shape), _full_spec(wg.shape), _full_spec((1, LANES)),
                  pl.BlockSpec((1, tm, LANES), lambda b, i: (b, i, 0)),
                  _full_spec(place.shape), _full_spec(ones.shape)],
        out_specs=[tile, tile, tile],
        out_shape=[act, act, act],
        compiler_params=_cparams("arbitrary", "arbitrary"),
    )(x, modl, wq, wg, jnp.tile(q_norm, LANES // HEAD_DIM).reshape(1, LANES), fcum, place, ones)


def _fox_attn_kernel(qi_ref, kj_ref, flag_ref, q_ref, qb_ref, k_ref, kb_ref, vt_ref, gate_ref, o_ref,
                     qm_ref, m_ref, l_ref, acc_ref):
    s = pl.program_id(2)
    i = qi_ref[s]
    j = kj_ref[s]
    flags = flag_ref[s]
    N = HEAD_DIM
    nh = q_ref.shape[2] // N
    hpp = LANES // N
    tk, tq = k_ref.shape[1], q_ref.shape[1]
    pair = lambda h: slice((h // hpp) * LANES, (h // hpp + 1) * LANES)

    @pl.when(j == 0)
    def _():
        m_ref[...] = jnp.full_like(m_ref, NEG_BIG)
        l_ref[...] = jnp.zeros_like(l_ref)
        acc_ref[...] = jnp.zeros_like(acc_ref)
        for hh in range(nh):
            qcat = jnp.concatenate([q_ref[0, :, pair(hh)], qb_ref[0, :, pair(hh)]], axis=1).astype(f32)
            head = (lax.broadcasted_iota(jnp.int32, qcat.shape, 1) % LANES) // N
            qm_ref[hh] = jnp.where(head == hh % hpp, qcat, 0.0).astype(bf16)

    def step(masked, q_lo):
        qs = slice(q_lo, tq)
        kcat = [jnp.concatenate([k_ref[0, :, pair(hh)], kb_ref[0, :, pair(hh)]], axis=1)
                for hh in range(0, nh, hpp)]
        sts = [_dot_nt(kcat[hh // hpp], qm_ref[hh, qs, :]) for hh in range(nh)]
        for hh, st in enumerate(sts):
            if masked:
                key = j * tk + lax.broadcasted_iota(jnp.int32, st.shape, 0)
                qry = i * tq + q_lo + lax.broadcasted_iota(jnp.int32, st.shape, 1)
                st = jnp.where(key <= qry, st, -jnp.inf)
            m_prev = m_ref[hh, :, qs]
            m_new = jnp.maximum(m_prev, jnp.max(st, axis=0, keepdims=True))
            alpha = jnp.exp2(m_prev - m_new)
            p = jnp.exp2(st - m_new)
            l_ref[hh, :, qs] = alpha * l_ref[hh, :, qs] + jnp.sum(p, axis=0, keepdims=True)
            acc_ref[hh, :, qs] = alpha * acc_ref[hh, :, qs] + _dot(vt_ref[0, hh * N:(hh + 1) * N, :], p.astype(bf16))
            m_ref[hh, :, qs] = m_new

    @pl.when((flags & (ATTN_FLAG_MASK | ATTN_FLAG_UPPER)) == 0)
    def _():
        step(False, 0)

    @pl.when((flags & (ATTN_FLAG_MASK | ATTN_FLAG_UPPER)) == ATTN_FLAG_MASK)
    def _():
        step(True, 0)

    @pl.when((flags & ATTN_FLAG_UPPER) != 0)
    def _():
        step(True, tq // 2)

    @pl.when((flags & ATTN_FLAG_LAST) != 0)
    def _():
        for pr in range(nh // hpp):
            ot = jnp.concatenate([acc_ref[hh] * (1.0 / l_ref[hh]) for hh in range(pr * hpp, (pr + 1) * hpp)],
                                 axis=0)
            lp = slice(pr * LANES, (pr + 1) * LANES)
            o_ref[0, :, lp] = (ot.T * gate_ref[0, :, lp].astype(f32)).astype(bf16)


def _fox_attn(q, qb, k, kb, vt, gate):
    B, S, D = gate.shape
    tq = min(ATTN_Q_TILE, S)
    tk = min(ATTN_K_TILE, S)
    LW = ATTN_PAIRS * LANES
    nh = LW // HEAD_DIM
    qi, kj, flags = [], [], []
    for i in range(S // tq):
        last = ((i + 1) * tq - 1) // tk
        for j in range(last + 1):
            qi.append(i)
            kj.append(j)
            needs_mask = (j + 1) * tk - 1 > i * tq
            upper_only = j * tk >= i * tq + tq // 2
            flags.append((ATTN_FLAG_MASK if needs_mask else 0) | (ATTN_FLAG_LAST if j == last else 0)
                         | (ATTN_FLAG_UPPER if upper_only else 0))
    tables = [jnp.asarray(t, jnp.int32) for t in (qi, kj, flags)]
    qtile = pl.BlockSpec((1, tq, LW), lambda b, p, s, qi, kj, fl: (b, qi[s], p))
    ktile = pl.BlockSpec((1, tk, LW), lambda b, p, s, qi, kj, fl: (b, kj[s], p))
    grid_spec = pltpu.PrefetchScalarGridSpec(
        num_scalar_prefetch=3,
        grid=(B, D // LW, len(qi)),
        in_specs=[qtile, qtile, ktile, ktile,
                  pl.BlockSpec((1, LW, tk), lambda b, p, s, qi, kj, fl: (b, p, kj[s])),
                  qtile],
        out_specs=qtile,
        scratch_shapes=[pltpu.VMEM((nh, tq, 2 * LANES), bf16),
                        pltpu.VMEM((nh, 1, tq), f32), pltpu.VMEM((nh, 1, tq), f32),
                        pltpu.VMEM((nh, HEAD_DIM, tq), f32)],
    )
    return pl.pallas_call(
        _fox_attn_kernel,
        name="fox_attn",
        grid_spec=grid_spec,
        out_shape=jax.ShapeDtypeStruct((B, S, D), bf16),
        compiler_params=_cparams("arbitrary", "arbitrary", "arbitrary"),
    )(*tables, q, qb, k, kb, vt, gate)


def kernel(x, c, ada_w, ada_b, ln_g, ln_b, rw_mu, rw_rkv, rw_w0, rw_w1, rw_w2, rw_a0, rw_a1, rw_a2, rw_g1, rw_g2, rw_kk, rw_ka, rw_rk, rw_gn_g, rw_gn_b, rw_wo, rw_v0, rw_v1, rw_v2, kv_ada_w, kv_ada_b, kv_w, kv_fb, kv_knorm, fx_wqg, fx_qnorm, fx_wo, moe_wgrp, moe_bgrp, moe_wexp, moe_bexp, moe_wgate, moe_wup, moe_wdown):
    B, S, D = x.shape
    depth = ada_w.shape[0]
    n_a = rw_mu.shape[0]
    mod = _adaln_mod(c, ada_w, ada_b)
    kvmod = _adaln_mod(c, kv_ada_w[None], kv_ada_b[None])[0].reshape(B, 1, 2 * D)
    kv = None
    v_first = None
    for l in range(depth):
        modl = mod[l].reshape(B, 1, 6 * D)
        if l < n_a:
            vmix = None if l == 0 else (rw_v0[l - 1], rw_v1[l - 1], rw_v2[l - 1], v_first)
            r, k, v, wl, kk, b, g = _rwkv_pre(x, modl, rw_mu[l], rw_rkv[l], rw_w0[l], rw_w1[l], rw_w2[l],
                                              rw_a0[l], rw_a1[l], rw_a2[l], rw_g1[l], rw_g2[l],
                                              rw_kk[l], rw_ka[l], vmix)
            if l == 0:
                v_first = v
            z = _wkv7(r, k, v, wl, kk, b, g, rw_rk[l].reshape(D), rw_gn_g[l], rw_gn_b[l])
            w_o = rw_wo[l]
        else:
            j = l - n_a
            k, kb, vt, fcum = kv
            q, qb, gate = _fox_pre(x, modl, fx_wqg[j], fx_qnorm[j], fcum)
            z = _fox_attn(q, qb, k, kb, vt, gate)
            w_o = fx_wo[j]
        x, *routed = _proj_route(z, w_o, x, modl, ln_g[l, 0], ln_b[l, 0],
                                 moe_wgrp[l], moe_bgrp[l], moe_wexp[l], moe_bexp[l])
        x = _moe_layer(x, routed, modl, moe_wgate, moe_wup, moe_wdown, l, ln_g[l, 1], ln_b[l, 1])
        if l == n_a - 1:
            kv = _shared_kv(x, kvmod, kv_w, kv_fb, kv_knorm)
    return x
```

```python
import functools
import math

import jax
import jax.numpy as jnp
import numpy as np
from jax import lax
from jax.experimental import pallas as pl
from jax.experimental.pallas import tpu as pltpu

HEAD_DIM = 64
N_GROUPS = 4
EXP_PER_GROUP = 8
N_EXPERTS = N_GROUPS * EXP_PER_GROUP
TOP_K = 2
DEPTH = 4
N_A = 2
DEEPNORM_ALPHA = (2 * DEPTH) ** 0.25
LN_EPS = 1e-5
GN_EPS = 64e-5
QK_EPS = 1e-6

LANES = 128
VMEM_LIMIT_BYTES = 56 * 1024 * 1024
WKV_CHUNK = 64
WKV_TILE = 128
ROW_TILE = 512
RWKV_PRE_TILE = 512
ATTN_Q_TILE = 1024
ATTN_K_TILE = 512
ATTN_PAIRS = 2
ATTN_FLAG_MASK = 1
ATTN_FLAG_LAST = 2
ATTN_FLAG_UPPER = 4
MOE_BLOCK = 512
NEG_BIG = -1e30
LOG2E = math.log2(math.e)

f32 = jnp.float32
bf16 = jnp.bfloat16
HIGHEST = lax.Precision.HIGHEST


def _cparams(*sem):
    return pltpu.CompilerParams(dimension_semantics=sem, vmem_limit_bytes=VMEM_LIMIT_BYTES)


def _sigmoid(x):
    return 0.5 * jnp.tanh(0.5 * x) + 0.5


def _dot(a, b):
    return jnp.dot(a, b, preferred_element_type=f32)


def _dot_nt(a, b):
    return lax.dot_general(a, b, (((1,), (1,)), ((), ())), preferred_element_type=f32)


def _dot_tn(a, b):
    return lax.dot_general(a, b, (((0,), (0,)), ((), ())), preferred_element_type=f32)


def _full_spec(shape):
    n = len(shape)
    return pl.BlockSpec(shape, lambda *_: (0,) * n)


def _pair_ones():
    rs = lax.broadcasted_iota(jnp.int32, (LANES, LANES), 0)
    cs = lax.broadcasted_iota(jnp.int32, (LANES, LANES), 1)
    return jnp.where(jnp.where(rs >= HEAD_DIM, 1, 0) == jnp.where(cs >= HEAD_DIM, 1, 0), 1.0, 0.0).astype(bf16)


def _head_sum(x, jbd):
    xh = x.astype(bf16)
    xl = (x - xh.astype(f32)).astype(bf16)
    return _dot(xh, jbd) + _dot(xl, jbd)


def _bias_split(f):
    t0 = f.astype(bf16)
    r1 = f - t0.astype(f32)
    t1 = r1.astype(bf16)
    t2 = (r1 - t1.astype(f32)).astype(bf16)
    return jnp.concatenate([t0, t1, t2], axis=1)


def _bias_placement(D, is_key):
    H = D // HEAD_DIM
    place = np.zeros((3 * LANES, D), np.float32)
    ones = np.zeros((1, D), np.float32)
    for h in range(H):
        for i in range(3):
            place[i * LANES + h, h * HEAD_DIM + (3 + i if is_key else i)] = -1.0 if is_key else 1.0
            ones[0, h * HEAD_DIM + (i if is_key else 3 + i)] = 1.0
    return jnp.asarray(place, bf16), jnp.asarray(ones, f32)


def _layer_norm_rows(u, g, b):
    mu = jnp.mean(u, axis=-1, keepdims=True)
    d = u - mu
    var = jnp.mean(d * d, axis=-1, keepdims=True)
    return d * lax.rsqrt(var + LN_EPS) * g + b


def _mod_kernel(c_ref, w_ref, b_ref, o_ref):
    c = c_ref[...]
    cs = c * _sigmoid(c)
    o_ref[0] = jnp.dot(cs, w_ref[0], preferred_element_type=f32, precision=HIGHEST) + b_ref[0]


def _adaln_mod(c, w, b):
    L, D, N = w.shape
    B = c.shape[0]
    tn = min(N, 1024)
    return pl.pallas_call(
        _mod_kernel,
        name="adaln_mod",
        grid=(L, N // tn),
        in_specs=[
            _full_spec((B, D)),
            pl.BlockSpec((1, D, tn), lambda l, j: (l, 0, j)),
            pl.BlockSpec((1, 1, tn), lambda l, j: (l, 0, j)),
        ],
        out_specs=pl.BlockSpec((1, B, tn), lambda l, j: (l, 0, j)),
        out_shape=jax.ShapeDtypeStruct((L, B, N), f32),
        compiler_params=_cparams("arbitrary", "arbitrary"),
    )(c, w, b.reshape(L, 1, N))


def _rwkv_pre_kernel(has_vmix, x_ref, xp_ref, mod_ref, mu_ref, wrkv_ref, w0_ref, w1_ref, w2_ref,
                     a0_ref, a1_ref, a2_ref, g1_ref, g2_ref, kkp_ref, kap_ref, *rest):
    if has_vmix:
        v0_ref, v1_ref, v2_ref, vf_ref = rest[:4]
        rest = rest[4:]
    r_ref, k_ref, v_ref, wl_ref, kk_ref, b_ref, g_ref = rest
    D = x_ref.shape[-1]
    sh = mod_ref[0, :, 0:D]
    sc = mod_ref[0, :, D:2 * D]
    h = x_ref[0] * (1.0 + sc) + sh
    hp = xp_ref[0][7:8, :] * (1.0 + sc) + sh
    hp = jnp.where(pl.program_id(1) == 0, 0.0, hp)
    row = lax.broadcasted_iota(jnp.int32, h.shape, 0)
    h_prev = jnp.where(row == 0, hp, pltpu.roll(h, 1, axis=0))
    xx = h_prev - h
    mix = lambda j: (h + xx * mu_ref[j:j + 1, :]).astype(bf16)
    xr, xw, xk, xv, xa, xg = [mix(j) for j in range(6)]
    r = _dot(xr, wrkv_ref[0])
    k = _dot(xk, wrkv_ref[1])
    v = _dot(xv, wrkv_ref[2])
    zw = w0_ref[...] + _dot(jnp.tanh(_dot(xw, w1_ref[...])).astype(bf16), w2_ref[...])
    wl_ref[0] = -math.exp(-0.5) * _sigmoid(zw)
    a = _sigmoid(a0_ref[...] + _dot(_dot(xa, a1_ref[...]).astype(bf16), a2_ref[...]))
    g = _dot(_sigmoid(_dot(xg, g1_ref[...])).astype(bf16), g2_ref[...])
    if has_vmix:
        vmix = _sigmoid(v0_ref[...] + _dot(_dot(xv, v1_ref[...]).astype(bf16), v2_ref[...]))
        v = v + (vf_ref[0].astype(f32) - v) * vmix
    r_ref[0] = r.astype(bf16)
    v_ref[0] = v.astype(bf16)
    g_ref[0] = g.astype(bf16)
    k_ref[0] = (k * (1.0 + (a - 1.0) * kap_ref[...])).astype(bf16)
    kk = k * kkp_ref[...]
    jbd = _pair_ones()
    for p in range(D // LANES):
        lp = slice(p * LANES, (p + 1) * LANES)
        seg = kk[:, lp]
        nrm = jnp.sqrt(_head_sum(seg * seg, jbd))
        seg = seg / jnp.maximum(nrm, 1e-12)
        kk_ref[0, :, lp] = seg.astype(bf16)
        b_ref[0, :, lp] = (seg * a[:, lp]).astype(bf16)


def _rwkv_pre(x, modl, mu, wrkv, w0, w1, w2, a0, a1, a2, g1, g2, kkp, kap, vmix):
    B, S, D = x.shape
    tm = min(RWKV_PRE_TILE, S)
    row = lambda t: t.reshape(1, D)
    args = [x, x, modl, mu, wrkv.astype(bf16), row(w0), w1.astype(bf16), w2.astype(bf16), row(a0),
            a1.astype(bf16), a2.astype(bf16), g1.astype(bf16), g2.astype(bf16), row(kkp), row(kap)]
    tile = pl.BlockSpec((1, tm, D), lambda b, i: (b, i, 0))
    in_specs = [
        tile,
        pl.BlockSpec((1, 8, D), lambda b, i: (b, jnp.maximum(i * (tm // 8) - 1, 0), 0)),
        pl.BlockSpec((1, 1, modl.shape[-1]), lambda b, i: (b, 0, 0)),
    ] + [_full_spec(a.shape) for a in args[3:]]
    if vmix is not None:
        v0, v1, v2, v_first = vmix
        extra = [row(v0), v1.astype(bf16), v2.astype(bf16)]
        args += extra + [v_first]
        in_specs += [_full_spec(a.shape) for a in extra] + [tile]
    out_bf = jax.ShapeDtypeStruct((B, S, D), bf16)
    out_shape = [out_bf, out_bf, out_bf, jax.ShapeDtypeStruct((B, S, D), f32), out_bf, out_bf, out_bf]
    return pl.pallas_call(
        functools.partial(_rwkv_pre_kernel, vmix is not None),
        name="rwkv_pre",
        grid=(B, S // tm),
        in_specs=in_specs,
        out_specs=[tile] * 7,
        out_shape=out_shape,
        compiler_params=_cparams("arbitrary", "arbitrary"),
    )(*args)


def _wkv_kernel(C, r_ref, k_ref, v_ref, wl_ref, kk_ref, b_ref, g_ref, rk_ref, gng_ref, gnb_ref,
                o_ref, state_ref):
    @pl.when(pl.program_id(1) == 0)
    def _():
        state_ref[...] = jnp.zeros_like(state_ref)

    T, D = r_ref.shape[1], r_ref.shape[2]
    N = HEAD_DIM
    P = D // LANES
    C2 = 2 * C
    ri = lax.broadcasted_iota(jnp.int32, (C, C), 0)
    ci = lax.broadcasted_iota(jnp.int32, (C, C), 1)
    tri_incl = jnp.where(ri >= ci, 1.0, 0.0).astype(f32)
    r2 = lax.broadcasted_iota(jnp.int32, (C2, C2), 0)
    c2 = lax.broadcasted_iota(jnp.int32, (C2, C2), 1)
    dlt = jnp.where(jnp.where(r2 >= C, 1, 0) == jnp.where(c2 >= C, 1, 0), r2 - c2, -1)
    strict = dlt > 0
    incl = dlt >= 0
    eye2 = jnp.where(r2 == c2, 1.0, 0.0).astype(f32)
    h0 = lax.broadcasted_iota(jnp.int32, (C, LANES), 1) < N
    rs = lax.broadcasted_iota(jnp.int32, (LANES, LANES), 0)
    cs = lax.broadcasted_iota(jnp.int32, (LANES, LANES), 1)
    sbd = jnp.where(rs >= N, 1, 0) == jnp.where(cs >= N, 1, 0)
    jbd = jnp.where(sbd, 1.0, 0.0).astype(bf16)
    stack2 = lambda t: jnp.concatenate([t, t], axis=0)

    lanes = [slice(p * LANES, (p + 1) * LANES) for p in range(P)]
    nc = T // C
    units = [(c, p) for c in range(nc) for p in range(P)]
    hi_lo = lambda t: (t.astype(bf16), (t - t.astype(bf16).astype(f32)).astype(bf16))

    ch = []
    for c in range(nc):
        sl = pl.ds(c * C, C)
        wl = wl_ref[0, sl, :]
        cum = jnp.dot(tri_incl, wl, preferred_element_type=f32, precision=HIGHEST)
        g_incl = jnp.exp(cum)
        g_inv = jnp.exp(-cum)
        g_last = g_incl[C - 1:C, :]
        r = r_ref[0, sl, :].astype(f32)
        k = k_ref[0, sl, :].astype(f32)
        btf = b_ref[0, sl, :].astype(f32) * g_inv
        ktf = k * g_inv
        ch.append(dict(
            sl=sl, g_last=g_last, v16=v_ref[0, sl, :],
            af=-kk_ref[0, sl, :].astype(f32) * jnp.exp(cum - wl), rf=r * g_incl,
            bt=btf.astype(bf16), kt=ktf.astype(bf16),
            bc=(btf * g_last).astype(bf16), kc=(ktf * g_last).astype(bf16),
            rkr=r * k * rk_ref[...]))

    Ls, Aak, Arow, Vst, X2 = [], [], [], [], []
    for c, p in units:
        d, lp = ch[c], lanes[p]
        afp, rfp = d["af"][:, lp], d["rf"][:, lp]
        X4 = jnp.concatenate([jnp.where(h0, afp, 0.0), jnp.where(h0, 0.0, afp),
                              jnp.where(h0, rfp, 0.0), jnp.where(h0, 0.0, rfp)], axis=0).astype(bf16)
        M4 = _dot_nt(X4, jnp.concatenate([stack2(d["bt"][:, lp]), stack2(d["kt"][:, lp])], axis=0))
        Mb, Mk = M4[:, :C2], M4[:, C2:]
        Ls.append(jnp.where(strict, Mb[:C2], 0.0))
        Aak.append(jnp.where(strict, Mk[:C2], 0.0).astype(bf16))
        Arow.append(jnp.concatenate([jnp.where(incl, Mb[C2:], 0.0), jnp.where(incl, Mk[C2:], 0.0)],
                                    axis=1).astype(bf16))
        Vst.append(stack2(d["v16"][:, lp]))
        X2.append(jnp.concatenate([afp, rfp], axis=0).astype(bf16))
    Ps = [eye2 + L for L in Ls]
    n = 1
    while 2 * n < C:
        Lb = [L.astype(bf16) for L in Ls]
        Ls = [_dot(x, x) for x in Lb]
        Ps = [Pm + _dot(L.astype(bf16), Pm.astype(bf16)) for L, Pm in zip(Ls, Ps)]
        n *= 2
    Ps = [Pm.astype(bf16) for Pm in Ps]
    AV = [_dot(a, vs) for a, vs in zip(Aak, Vst)]
    ys = []
    state = [state_ref[p] for p in range(P)]
    for c in range(nc):
        d = ch[c]
        us = range(c * P, (c + 1) * P)
        M2 = [_dot_nt(X2[u], state[u - c * P].astype(bf16)) for u in us]
        Ust = [_dot(Ps[u], (AV[u] + stack2(m2[:C])).astype(bf16)) for u, m2 in zip(us, M2)]
        Yst = [_dot(Arow[u], jnp.concatenate([ust.astype(bf16), Vst[u]], axis=0)) + stack2(m2[C:])
               for u, ust, m2 in zip(us, Ust, M2)]
        ys += [jnp.where(h0, yst[:C], yst[C:]) for yst in Yst]
        for p, ust in enumerate(Ust):
            lp = lanes[p]
            u16 = jnp.where(h0, ust[:C], ust[C:]).astype(bf16)
            upd = _dot_tn(jnp.concatenate([u16, d["v16"][:, lp]], axis=0),
                          jnp.concatenate([d["bc"][:, lp], d["kc"][:, lp]], axis=0))
            state[p] = state[p] * d["g_last"][:, lp] + jnp.where(sbd, upd, 0.0)
    for p in range(P):
        state_ref[p] = state[p]
    sums = []
    for u, (c, p) in enumerate(units):
        sums.append(_dot(jnp.concatenate(hi_lo(ys[u]) + hi_lo(ch[c]["rkr"][:, lanes[p]]), axis=0), jbd))
    yds = [ys[u] - (sums[u][:C] + sums[u][C:C2]) * (1.0 / N) for u in range(len(units))]
    var = [_dot(jnp.concatenate(hi_lo(yd * yd), axis=0), jbd) for yd in yds]
    for u, (c, p) in enumerate(units):
        d, lp = ch[c], lanes[p]
        yn = yds[u] * lax.rsqrt((var[u][:C] + var[u][C:]) * (1.0 / N) + GN_EPS)
        bonus = (sums[u][C2:C2 + C] + sums[u][C2 + C:]) * d["v16"][:, lp].astype(f32)
        z = (yn * gng_ref[:, lp] + gnb_ref[:, lp] + bonus) * g_ref[0, d["sl"], lp].astype(f32)
        o_ref[0, d["sl"], lp] = z.astype(bf16)


def _wkv7(r, k, v, wl, kk, b, g, r_k, gn_g, gn_b):
    B, S, D = r.shape
    T = min(WKV_TILE, S)
    C = min(WKV_CHUNK, T)
    tile = pl.BlockSpec((1, T, D), lambda bb, i: (bb, i, 0))
    vec = _full_spec((1, D))
    return pl.pallas_call(
        functools.partial(_wkv_kernel, C),
        name="wkv7",
        grid=(B, S // T),
        in_specs=[tile] * 7 + [vec] * 3,
        out_specs=tile,
        out_shape=jax.ShapeDtypeStruct((B, S, D), bf16),
        scratch_shapes=[pltpu.VMEM((D // LANES, LANES, LANES), f32)],
        compiler_params=_cparams("arbitrary", "arbitrary"),
    )(r, k, v, wl, kk, b, g, r_k.reshape(1, D), gn_g.reshape(1, D), gn_b.reshape(1, D))


def _proj_route_kernel(z_ref, wo_ref, x_ref, mod_ref, lng_ref, lnb_ref, w_ref, wlo_ref, b_ref,
                       xo_ref, h_ref, idx_ref, gate_ref, cnt_ref):
    @pl.when((pl.program_id(0) == 0) & (pl.program_id(1) == 0))
    def _():
        cnt_ref[...] = jnp.zeros_like(cnt_ref)

    D = x_ref.shape[-1]
    gt = mod_ref[0, :, 2 * D:3 * D]
    u = DEEPNORM_ALPHA * x_ref[0] + (1.0 + gt) * _dot(z_ref[0], wo_ref[...])
    xn = _layer_norm_rows(u, lng_ref[...], lnb_ref[...])
    xo_ref[0] = xn
    sh = mod_ref[0, :, 3 * D:4 * D]
    sc = mod_ref[0, :, 4 * D:5 * D]
    h = xn * (1.0 + sc) + sh
    h_ref[0] = h.astype(bf16)
    h_hi = h.astype(bf16)
    h_lo = (h - h_hi.astype(f32)).astype(bf16)
    lg = _dot(h_hi, w_ref[...]) + _dot(h_hi, wlo_ref[...]) + _dot(h_lo, w_ref[...]) + b_ref[...]
    lane = lax.broadcasted_iota(jnp.int32, lg.shape, 1)
    glog = jnp.where(lane < N_GROUPS, lg, -jnp.inf)
    gmax = jnp.max(glog, axis=-1, keepdims=True)
    p_g = 1.0 / jnp.sum(jnp.exp(glog - gmax), axis=-1, keepdims=True)
    g_sel = jnp.min(jnp.where(glog == gmax, lane, LANES), axis=-1, keepdims=True)
    e_lo = N_GROUPS + EXP_PER_GROUP * g_sel
    elog = jnp.where((lane >= e_lo) & (lane < e_lo + EXP_PER_GROUP), lg, -jnp.inf)
    e1 = jnp.max(elog, axis=-1, keepdims=True)
    esum = jnp.sum(jnp.exp(elog - e1), axis=-1, keepdims=True)
    i1 = jnp.min(jnp.where(elog == e1, lane, LANES), axis=-1, keepdims=True)
    elog2 = jnp.where(lane == i1, -jnp.inf, elog)
    e2 = jnp.max(elog2, axis=-1, keepdims=True)
    i2 = jnp.min(jnp.where(elog2 == e2, lane, LANES), axis=-1, keepdims=True)
    p1 = 1.0 / esum
    p2 = jnp.exp(e2 - e1) / esum
    psum = p1 + p2
    gate_ref[0] = jnp.where(lane == 0, p_g * p1 / psum, jnp.where(lane == 1, p_g * p2 / psum, 0.0))
    tm = lg.shape[0]
    onehot = jnp.where(lane == i1, 1.0, 0.0) + jnp.where(lane == i2, 1.0, 0.0)
    ri = lax.broadcasted_iota(jnp.int32, (tm, tm), 0)
    ci = lax.broadcasted_iota(jnp.int32, (tm, tm), 1)
    before = _dot(jnp.where(ri > ci, 1.0, 0.0).astype(bf16), onehot.astype(bf16)) + cnt_ref[...]
    rank1 = jnp.sum(jnp.where(lane == i1, before, 0.0), axis=-1, keepdims=True).astype(jnp.int32)
    rank2 = jnp.sum(jnp.where(lane == i2, before, 0.0), axis=-1, keepdims=True).astype(jnp.int32)
    idx_ref[0] = jnp.where(lane == 0, i1 - N_GROUPS, jnp.where(lane == 1, i2 - N_GROUPS,
                           jnp.where(lane == 2, rank1, jnp.where(lane == 3, rank2, 0))))
    cnt_ref[...] = before[tm - 1:tm, :] + onehot[tm - 1:tm, :]


def _proj_route(z, w_o, x, modl, ln_g, ln_b, w_grp, b_grp, w_exp, b_exp):
    B, S, D = x.shape
    tm = min(ROW_TILE, S)
    pad = LANES - N_GROUPS - N_EXPERTS
    w = jnp.concatenate([w_grp, w_exp, jnp.zeros((D, pad), f32)], axis=1)
    w_hi = w.astype(bf16)
    b = jnp.concatenate([b_grp, b_exp, jnp.zeros((pad,), f32)]).reshape(1, LANES)
    tile = pl.BlockSpec((1, tm, D), lambda bb, i: (bb, i, 0))
    small = pl.BlockSpec((1, tm, LANES), lambda bb, i: (bb, i, 0))
    return pl.pallas_call(
        _proj_route_kernel,
        name="proj_route",
        grid=(B, S // tm),
        in_specs=[tile, _full_spec(w_o.shape), tile,
                  pl.BlockSpec((1, 1, modl.shape[-1]), lambda bb, i: (bb, 0, 0)),
                  _full_spec((1, D)), _full_spec((1, D)),
                  _full_spec(w.shape), _full_spec(w.shape), _full_spec(b.shape)],
        out_specs=[tile, tile, small, small, _full_spec((1, LANES))],
        out_shape=[jax.ShapeDtypeStruct((B, S, D), f32),
                   jax.ShapeDtypeStruct((B, S, D), bf16),
                   jax.ShapeDtypeStruct((B, S, LANES), jnp.int32),
                   jax.ShapeDtypeStruct((B, S, LANES), f32),
                   jax.ShapeDtypeStruct((1, LANES), f32)],
        compiler_params=_cparams("arbitrary", "arbitrary"),
    )(z, w_o.astype(bf16), x, modl, ln_g.reshape(1, D), ln_b.reshape(1, D),
      w_hi, (w - w_hi.astype(f32)).astype(bf16), b)


def _expert_kernel(be_ref, nu_ref, x_ref, wg_ref, wu_ref, wd_ref, o_ref, wg16, wu16, wd16):
    i = pl.program_id(0)

    @pl.when((i == 0) | (be_ref[i] != be_ref[jnp.maximum(i - 1, 0)]))
    def _():
        wg16[...] = wg_ref[0, 0].astype(bf16)
        wu16[...] = wu_ref[0, 0].astype(bf16)
        wd16[...] = wd_ref[0, 0].astype(bf16)

    @pl.when(i < nu_ref[0])
    def _():
        x = x_ref[...]
        hg = _dot(x, wg16[...])
        hu = _dot(x, wu16[...])
        o_ref[...] = _dot((hg * _sigmoid(hg) * hu).astype(bf16), wd16[...]).astype(o_ref.dtype)

    @pl.when(i >= nu_ref[0])
    def _():
        o_ref[...] = jnp.zeros_like(o_ref)


def _expert_ffn(xs, block_exp, n_used, wg, wu, wd, layer):
    n_slots, D = xs.shape
    F = wg.shape[-1]
    blk = MOE_BLOCK
    row_map = lambda i, be, nu: (jnp.maximum(jnp.minimum(i, nu[0] - 1), 0), 0)
    w_map = lambda i, be, nu: (layer, be[i], 0, 0)
    grid_spec = pltpu.PrefetchScalarGridSpec(
        num_scalar_prefetch=2,
        grid=(n_slots // blk,),
        in_specs=[
            pl.BlockSpec((blk, D), row_map),
            pl.BlockSpec((1, 1, D, F), w_map),
            pl.BlockSpec((1, 1, D, F), w_map),
            pl.BlockSpec((1, 1, F, D), w_map),
        ],
        out_specs=pl.BlockSpec((blk, D), lambda i, be, nu: (i, 0)),
        scratch_shapes=[pltpu.VMEM((D, F), bf16), pltpu.VMEM((D, F), bf16), pltpu.VMEM((F, D), bf16)],
    )
    return pl.pallas_call(
        _expert_kernel,
        name="moe_experts",
        grid_spec=grid_spec,
        out_shape=jax.ShapeDtypeStruct((n_slots, D), bf16),
        compiler_params=_cparams("arbitrary"),
    )(block_exp, n_used, xs, wg, wu, wd)


def _take_rows(table, idx):
    return table.at[idx].get(mode="promise_in_bounds")


def _dispatch(expert_idx, rank, counts, blk):
    T = expert_idx.shape[0]
    A = T * TOP_K
    n_blocks = A // blk + N_EXPERTS
    experts = jnp.arange(N_EXPERTS, dtype=jnp.int32)
    padded = (counts + blk - 1) // blk * blk
    pad_end = jnp.cumsum(padded)
    pad_start = pad_end - padded
    start = jnp.cumsum(counts) - counts
    dest = rank + jnp.sum(jnp.where(expert_idx[..., None] == experts, pad_start, 0), axis=-1)
    block_exp = jnp.minimum(jnp.sum(((jnp.arange(n_blocks, dtype=jnp.int32) * blk)[:, None] >= pad_end[None, :])
                                    .astype(jnp.int32), axis=1), N_EXPERTS - 1)
    order = jnp.argsort(expert_idx.reshape(A)).astype(jnp.int32)
    shift = jnp.sum(jnp.where(block_exp[:, None] == experts, start - pad_start, 0), axis=-1)
    pos = jnp.arange(n_blocks * blk, dtype=jnp.int32) + jnp.repeat(shift, blk)
    slot_tok = _take_rows(order, pos % A) // TOP_K
    n_used = (pad_end[-1] // blk).astype(jnp.int32).reshape(1)
    return dest, slot_tok, block_exp, n_used


def _combine_ln_kernel(x_ref, y0_ref, y1_ref, gate_ref, mod_ref, lng_ref, lnb_ref, o_ref):
    D = x_ref.shape[-1]
    gt = mod_ref[0, :, 5 * D:6 * D]
    gates = gate_ref[0]
    y = gates[:, 0:1] * y0_ref[0].astype(f32) + gates[:, 1:2] * y1_ref[0].astype(f32)
    u = DEEPNORM_ALPHA * x_ref[0] + (1.0 + gt) * y
    o_ref[0] = _layer_norm_rows(u, lng_ref[...], lnb_ref[...])


def _combine_ln(x, y01, gates, modl, ln_g, ln_b):
    B, S, D = x.shape
    tm = min(ROW_TILE, S)
    tile = pl.BlockSpec((1, tm, D), lambda b, i: (b, i, 0))
    return pl.pallas_call(
        _combine_ln_kernel,
        name="moe_combine_ln",
        grid=(B, S // tm),
        in_specs=[tile, tile, pl.BlockSpec((1, tm, D), lambda b, i: (b + B, i, 0)),
                  pl.BlockSpec((1, tm, LANES), lambda b, i: (b, i, 0)),
                  pl.BlockSpec((1, 1, modl.shape[-1]), lambda b, i: (b, 0, 0)),
                  _full_spec((1, D)), _full_spec((1, D))],
        out_specs=tile,
        out_shape=jax.ShapeDtypeStruct((B, S, D), f32),
        compiler_params=_cparams("arbitrary", "arbitrary"),
    )(x, y01, y01, gates, modl, ln_g.reshape(1, D), ln_b.reshape(1, D))


def _moe_layer(x, routed, modl, wg, wu, wd, layer, ln_g, ln_b):
    B, S, D = x.shape
    T = B * S
    hb, idx, gates, cnt = routed
    idx = idx.reshape(T, LANES)
    counts = cnt[0, N_GROUPS:N_GROUPS + N_EXPERTS].astype(jnp.int32)
    dest, slot_tok, block_exp, n_used = _dispatch(idx[:, :TOP_K], idx[:, TOP_K:2 * TOP_K], counts, MOE_BLOCK)
    ys = _expert_ffn(_take_rows(hb.reshape(T, D), slot_tok), block_exp, n_used, wg, wu, wd, layer)
    y01 = _take_rows(ys, dest.T.reshape(TOP_K * T)).reshape(TOP_K * B, S, D)
    return _combine_ln(x, y01, gates, modl, ln_g, ln_b)


def _shared_kv_kernel(x_ref, mod_ref, wk_ref, wvt_ref, wf_ref, wflo_ref, fb_ref, kn_ref, place_ref, ones_ref,
                      k_ref, kb_ref, vt_ref, f_ref, carry_ref):
    @pl.when(pl.program_id(1) == 0)
    def _():
        carry_ref[...] = jnp.zeros_like(carry_ref)

    D = x_ref.shape[-1]
    tm = x_ref.shape[1]
    shift = mod_ref[0, :, 0:D]
    scale = mod_ref[0, :, D:2 * D]
    hk = x_ref[0] * (1.0 + scale) + shift
    hb = hk.astype(bf16)
    k = _dot(hb, wk_ref[...])
    vt_ref[0] = _dot_nt(wvt_ref[...], hb).astype(bf16)
    h_lo = (hk - hb.astype(f32)).astype(bf16)
    f = _dot(hb, wf_ref[...]) + _dot(hb, wflo_ref[...]) + _dot(h_lo, wf_ref[...]) + fb_ref[...]
    log_f = jnp.minimum(f, 0.0) - jnp.log(1.0 + jnp.exp(-jnp.abs(f)))
    row = lax.broadcasted_iota(jnp.int32, log_f.shape, 0)
    acc = log_f
    d = 1
    while d < tm:
        acc = acc + jnp.where(row >= d, pltpu.roll(acc, d, axis=0), 0.0)
        d *= 2
    acc = acc + carry_ref[...]
    f_ref[0] = acc
    carry_ref[...] = acc[tm - 1:tm, :]
    kb_ref[0] = (_dot(_bias_split(acc * LOG2E), place_ref[...]) + ones_ref[...]).astype(bf16)
    jbd = _pair_ones()
    for p in range(D // LANES):
        lp = slice(p * LANES, (p + 1) * LANES)
        seg = k[:, lp]
        ms = _head_sum(seg * seg, jbd) * (1.0 / HEAD_DIM)
        k_ref[0, :, lp] = (seg * lax.rsqrt(ms + QK_EPS) * kn_ref[...]).astype(bf16)


def _shared_kv(x, kvmod, w_kvf, b_f, k_norm):
    B, S, D = x.shape
    H = D // HEAD_DIM
    tm = min(ROW_TILE, S)
    wk = w_kvf[:, :D].astype(bf16)
    wvt = w_kvf[:, D:2 * D].T.astype(bf16)
    wf = jnp.concatenate([w_kvf[:, 2 * D:], jnp.zeros((D, LANES - H), f32)], axis=1)
    wf_hi = wf.astype(bf16)
    fb = jnp.concatenate([b_f, jnp.zeros((LANES - H,), f32)]).reshape(1, LANES)
    place, ones = _bias_placement(D, True)
    tile = pl.BlockSpec((1, tm, D), lambda b, i: (b, i, 0))
    act = jax.ShapeDtypeStruct((B, S, D), bf16)
    return pl.pallas_call(
        _shared_kv_kernel,
        name="shared_kv",
        grid=(B, S // tm),
        in_specs=[tile, pl.BlockSpec((1, 1, 2 * D), lambda b, i: (b, 0, 0)),
                  _full_spec(wk.shape), _full_spec(wvt.shape), _full_spec(wf.shape), _full_spec(wf.shape),
                  _full_spec(fb.shape), _full_spec((1, LANES)), _full_spec(place.shape), _full_spec(ones.shape)],
        out_specs=[tile, tile, pl.BlockSpec((1, D, tm), lambda b, i: (b, 0, i)),
                   pl.BlockSpec((1, tm, LANES), lambda b, i: (b, i, 0))],
        out_shape=[act, act, jax.ShapeDtypeStruct((B, D, S), bf16), jax.ShapeDtypeStruct((B, S, LANES), f32)],
        scratch_shapes=[pltpu.VMEM((1, LANES), f32)],
        compiler_params=_cparams("arbitrary", "arbitrary"),
    )(x, kvmod, wk, wvt, wf_hi, (wf - wf_hi.astype(f32)).astype(bf16), fb,
      jnp.tile(k_norm, LANES // HEAD_DIM).reshape(1, LANES), place, ones)


def _fox_pre_kernel(x_ref, mod_ref, wq_ref, wg_ref, qn_ref, f_ref, place_ref, ones_ref, q_ref, qb_ref, gate_ref):
    D = x_ref.shape[-1]
    sh = mod_ref[0, :, 0:D]
    sc = mod_ref[0, :, D:2 * D]
    hb = (x_ref[0] * (1.0 + sc) + sh).astype(bf16)
    q = _dot(hb, wq_ref[...])
    gate_ref[0] = _sigmoid(_dot(hb, wg_ref[...])).astype(bf16)
    qb_ref[0] = (_dot(_bias_split(f_ref[0] * LOG2E), place_ref[...]) + ones_ref[...]).astype(bf16)
    qscale = HEAD_DIM ** -0.5 * LOG2E
    jbd = _pair_ones()
    for p in range(D // LANES):
        lp = slice(p * LANES, (p + 1) * LANES)
        seg = q[:, lp]
        ms = _head_sum(seg * seg, jbd) * (1.0 / HEAD_DIM)
        q_ref[0, :, lp] = (seg * lax.rsqrt(ms + QK_EPS) * (qn_ref[...] * qscale)).astype(bf16)


def _fox_pre(x, modl, w_qg, q_norm, fcum):
    B, S, D = x.shape
    tm = min(ROW_TILE, S)
    tile = pl.BlockSpec((1, tm, D), lambda b, i: (b, i, 0))
    wq = w_qg[:, :D].astype(bf16)
    wg = w_qg[:, D:].astype(bf16)
    place, ones = _bias_placement(D, False)
    act = jax.ShapeDtypeStruct((B, S, D), bf16)
    return pl.pallas_call(
        _fox_pre_kernel,
        name="fox_pre",
        grid=(B, S // tm),
        in_specs=[tile, pl.BlockSpec((1, 1, modl.shape[-1]), lambda b, i: (b, 0, 0)),
                  _full_spec(wq.shape), _full_spec(wg.shape), _full_spec((1, LANES)),
                  pl.BlockSpec((1, tm, LANES), lambda b, i: (b, i, 0)),
                  _full_spec(place.shape), _full_spec(ones.shape)],
        out_specs=[tile, tile, tile],
        out_shape=[act, act, act],
        compiler_params=_cparams("arbitrary", "arbitrary"),
    )(x, modl, wq, wg, jnp.tile(q_norm, LANES // HEAD_DIM).reshape(1, LANES), fcum, place, ones)


def _fox_attn_kernel(qi_ref, kj_ref, flag_ref, q_ref, qb_ref, k_ref, kb_ref, vt_ref, gate_ref, o_ref,
                     qm_ref, m_ref, l_ref, acc_ref):
    s = pl.program_id(2)
    i = qi_ref[s]
    j = kj_ref[s]
    flags = flag_ref[s]
    N = HEAD_DIM
    nh = q_ref.shape[2] // N
    hpp = LANES // N
    tk, tq = k_ref.shape[1], q_ref.shape[1]
    pair = lambda h: slice((h // hpp) * LANES, (h // hpp + 1) * LANES)

    @pl.when(j == 0)
    def _():
        m_ref[...] = jnp.full_like(m_ref, NEG_BIG)
        l_ref[...] = jnp.zeros_like(l_ref)
        acc_ref[...] = jnp.zeros_like(acc_ref)
        for hh in range(nh):
            qcat = jnp.concatenate([q_ref[0, :, pair(hh)], qb_ref[0, :, pair(hh)]], axis=1).astype(f32)
            head = (lax.broadcasted_iota(jnp.int32, qcat.shape, 1) % LANES) // N
            qm_ref[hh] = jnp.where(head == hh % hpp, qcat, 0.0).astype(bf16)

    def step(masked, q_lo):
        qs = slice(q_lo, tq)
        kcat = [jnp.concatenate([k_ref[0, :, pair(hh)], kb_ref[0, :, pair(hh)]], axis=1)
                for hh in range(0, nh, hpp)]
        sts = [_dot_nt(kcat[hh // hpp], qm_ref[hh, qs, :]) for hh in range(nh)]
        for hh, st in enumerate(sts):
            if masked:
                key = j * tk + lax.broadcasted_iota(jnp.int32, st.shape, 0)
                qry = i * tq + q_lo + lax.broadcasted_iota(jnp.int32, st.shape, 1)
                st = jnp.where(key <= qry, st, -jnp.inf)
            m_prev = m_ref[hh, :, qs]
            m_new = jnp.maximum(m_prev, jnp.max(st, axis=0, keepdims=True))
            alpha = jnp.exp2(m_prev - m_new)
            p = jnp.exp2(st - m_new)
            l_ref[hh, :, qs] = alpha * l_ref[hh, :, qs] + jnp.sum(p, axis=0, keepdims=True)
            acc_ref[hh, :, qs] = alpha * acc_ref[hh, :, qs] + _dot(vt_ref[0, hh * N:(hh + 1) * N, :], p.astype(bf16))
            m_ref[hh, :, qs] = m_new

    @pl.when((flags & (ATTN_FLAG_MASK | ATTN_FLAG_UPPER)) == 0)
    def _():
        step(False, 0)

    @pl.when((flags & (ATTN_FLAG_MASK | ATTN_FLAG_UPPER)) == ATTN_FLAG_MASK)
    def _():
        step(True, 0)

    @pl.when((flags & ATTN_FLAG_UPPER) != 0)
    def _():
        step(True, tq // 2)

    @pl.when((flags & ATTN_FLAG_LAST) != 0)
    def _():
        for pr in range(nh // hpp):
            ot = jnp.concatenate([acc_ref[hh] * (1.0 / l_ref[hh]) for hh in range(pr * hpp, (pr + 1) * hpp)],
                                 axis=0)
            lp = slice(pr * LANES, (pr + 1) * LANES)
            o_ref[0, :, lp] = (ot.T * gate_ref[0, :, lp].astype(f32)).astype(bf16)


def _fox_attn(q, qb, k, kb, vt, gate):
    B, S, D = gate.shape
    tq = min(ATTN_Q_TILE, S)
    tk = min(ATTN_K_TILE, S)
    LW = ATTN_PAIRS * LANES
    nh = LW // HEAD_DIM
    qi, kj, flags = [], [], []
    for i in range(S // tq):
        last = ((i + 1) * tq - 1) // tk
        for j in range(last + 1):
            qi.append(i)
            kj.append(j)
            needs_mask = (j + 1) * tk - 1 > i * tq
            upper_only = j * tk >= i * tq + tq // 2
            flags.append((ATTN_FLAG_MASK if needs_mask else 0) | (ATTN_FLAG_LAST if j == last else 0)
                         | (ATTN_FLAG_UPPER if upper_only else 0))
    tables = [jnp.asarray(t, jnp.int32) for t in (qi, kj, flags)]
    qtile = pl.BlockSpec((1, tq, LW), lambda b, p, s, qi, kj, fl: (b, qi[s], p))
    ktile = pl.BlockSpec((1, tk, LW), lambda b, p, s, qi, kj, fl: (b, kj[s], p))
    grid_spec = pltpu.PrefetchScalarGridSpec(
        num_scalar_prefetch=3,
        grid=(B, D // LW, len(qi)),
        in_specs=[qtile, qtile, ktile, ktile,
                  pl.BlockSpec((1, LW, tk), lambda b, p, s, qi, kj, fl: (b, p, kj[s])),
                  qtile],
        out_specs=qtile,
        scratch_shapes=[pltpu.VMEM((nh, tq, 2 * LANES), bf16),
                        pltpu.VMEM((nh, 1, tq), f32), pltpu.VMEM((nh, 1, tq), f32),
                        pltpu.VMEM((nh, HEAD_DIM, tq), f32)],
    )
    return pl.pallas_call(
        _fox_attn_kernel,
        name="fox_attn",
        grid_spec=grid_spec,
        out_shape=jax.ShapeDtypeStruct((B, S, D), bf16),
        compiler_params=_cparams("arbitrary", "arbitrary", "arbitrary"),
    )(*tables, q, qb, k, kb, vt, gate)


def kernel(x, c, ada_w, ada_b, ln_g, ln_b, rw_mu, rw_rkv, rw_w0, rw_w1, rw_w2, rw_a0, rw_a1, rw_a2, rw_g1, rw_g2, rw_kk, rw_ka, rw_rk, rw_gn_g, rw_gn_b, rw_wo, rw_v0, rw_v1, rw_v2, kv_ada_w, kv_ada_b, kv_w, kv_fb, kv_knorm, fx_wqg, fx_qnorm, fx_wo, moe_wgrp, moe_bgrp, moe_wexp, moe_bexp, moe_wgate, moe_wup, moe_wdown):
    B, S, D = x.shape
    depth = ada_w.shape[0]
    n_a = rw_mu.shape[0]
    mod = _adaln_mod(c, ada_w, ada_b)
    kvmod = _adaln_mod(c, kv_ada_w[None], kv_ada_b[None])[0].reshape(B, 1, 2 * D)
    kv = None
    v_first = None
    for l in range(depth):
        modl = mod[l].reshape(B, 1, 6 * D)
        if l < n_a:
            vmix = None if l == 0 else (rw_v0[l - 1], rw_v1[l - 1], rw_v2[l - 1], v_first)
            r, k, v, wl, kk, b, g = _rwkv_pre(x, modl, rw_mu[l], rw_rkv[l], rw_w0[l], rw_w1[l], rw_w2[l],
                                              rw_a0[l], rw_a1[l], rw_a2[l], rw_g1[l], rw_g2[l],
                                              rw_kk[l], rw_ka[l], vmix)
            if l == 0:
                v_first = v
            z = _wkv7(r, k, v, wl, kk, b, g, rw_rk[l].reshape(D), rw_gn_g[l], rw_gn_b[l])
            w_o = rw_wo[l]
        else:
            j = l - n_a
            k, kb, vt, fcum = kv
            q, qb, gate = _fox_pre(x, modl, fx_wqg[j], fx_qnorm[j], fcum)
            z = _fox_attn(q, qb, k, kb, vt, gate)
            w_o = fx_wo[j]
        x, *routed = _proj_route(z, w_o, x, modl, ln_g[l, 0], ln_b[l, 0],
                                 moe_wgrp[l], moe_bgrp[l], moe_wexp[l], moe_bexp[l])
        x = _moe_layer(x, routed, modl, moe_wgate, moe_wup, moe_wdown, l, ln_g[l, 1], ln_b[l, 1])
        if l == n_a - 1:
            kv = _shared_kv(x, kvmod, kv_w, kv_fb, kv_knorm)
    return x
```

```python
import functools
import math

import jax
import jax.numpy as jnp
import numpy as np
from jax import lax
from jax.experimental import pallas as pl
from jax.experimental.pallas import tpu as pltpu

HEAD_DIM = 64
N_GROUPS = 4
EXP_PER_GROUP = 8
N_EXPERTS = N_GROUPS * EXP_PER_GROUP
TOP_K = 2
DEPTH = 4
N_A = 2
DEEPNORM_ALPHA = (2 * DEPTH) ** 0.25
LN_EPS = 1e-5
GN_EPS = 64e-5
QK_EPS = 1e-6

LANES = 128
VMEM_LIMIT_BYTES = 56 * 1024 * 1024
WKV_CHUNK = 64
WKV_TILE = 128
ROW_TILE = 512
RWKV_PRE_TILE = 512
ATTN_Q_TILE = 1024
ATTN_K_TILE = 512
ATTN_PAIRS = 4
ATTN_FLAG_MASK = 1
ATTN_FLAG_LAST = 2
ATTN_FLAG_UPPER = 4
MOE_BLOCK = 512
NEG_BIG = -1e30
LOG2E = math.log2(math.e)

f32 = jnp.float32
bf16 = jnp.bfloat16
HIGHEST = lax.Precision.HIGHEST


def _cparams(*sem):
    return pltpu.CompilerParams(dimension_semantics=sem, vmem_limit_bytes=VMEM_LIMIT_BYTES)


def _sigmoid(x):
    return 0.5 * jnp.tanh(0.5 * x) + 0.5


def _dot(a, b):
    return jnp.dot(a, b, preferred_element_type=f32)


def _dot_nt(a, b):
    return lax.dot_general(a, b, (((1,), (1,)), ((), ())), preferred_element_type=f32)


def _dot_tn(a, b):
    return lax.dot_general(a, b, (((0,), (0,)), ((), ())), preferred_element_type=f32)


def _full_spec(shape):
    n = len(shape)
    return pl.BlockSpec(shape, lambda *_: (0,) * n)


def _pair_ones():
    rs = lax.broadcasted_iota(jnp.int32, (LANES, LANES), 0)
    cs = lax.broadcasted_iota(jnp.int32, (LANES, LANES), 1)
    return jnp.where(jnp.where(rs >= HEAD_DIM, 1, 0) == jnp.where(cs >= HEAD_DIM, 1, 0), 1.0, 0.0).astype(bf16)


def _head_sum(x, jbd):
    xh = x.astype(bf16)
    xl = (x - xh.astype(f32)).astype(bf16)
    return _dot(xh, jbd) + _dot(xl, jbd)


def _bias_split(f):
    t0 = f.astype(bf16)
    r1 = f - t0.astype(f32)
    t1 = r1.astype(bf16)
    t2 = (r1 - t1.astype(f32)).astype(bf16)
    return jnp.concatenate([t0, t1, t2], axis=1)


def _bias_placement(D, is_key):
    H = D // HEAD_DIM
    place = np.zeros((3 * LANES, D), np.float32)
    ones = np.zeros((1, D), np.float32)
    for h in range(H):
        for i in range(3):
            place[i * LANES + h, h * HEAD_DIM + (3 + i if is_key else i)] = -1.0 if is_key else 1.0
            ones[0, h * HEAD_DIM + (i if is_key else 3 + i)] = 1.0
    return jnp.asarray(place, bf16), jnp.asarray(ones, f32)


def _layer_norm_rows(u, g, b):
    mu = jnp.mean(u, axis=-1, keepdims=True)
    d = u - mu
    var = jnp.mean(d * d, axis=-1, keepdims=True)
    return d * lax.rsqrt(var + LN_EPS) * g + b


def _mod_kernel(c_ref, w_ref, b_ref, o_ref):
    c = c_ref[...]
    cs = c * _sigmoid(c)
    o_ref[0] = jnp.dot(cs, w_ref[0], preferred_element_type=f32, precision=HIGHEST) + b_ref[0]


def _adaln_mod(c, w, b):
    L, D, N = w.shape
    B = c.shape[0]
    tn = min(N, 1024)
    return pl.pallas_call(
        _mod_kernel,
        name="adaln_mod",
        grid=(L, N // tn),
        in_specs=[
            _full_spec((B, D)),
            pl.BlockSpec((1, D, tn), lambda l, j: (l, 0, j)),
            pl.BlockSpec((1, 1, tn), lambda l, j: (l, 0, j)),
        ],
        out_specs=pl.BlockSpec((1, B, tn), lambda l, j: (l, 0, j)),
        out_shape=jax.ShapeDtypeStruct((L, B, N), f32),
        compiler_params=_cparams("arbitrary", "arbitrary"),
    )(c, w, b.reshape(L, 1, N))


def _rwkv_pre_kernel(has_vmix, x_ref, xp_ref, mod_ref, mu_ref, wrkv_ref, w0_ref, w1_ref, w2_ref,
                     a0_ref, a1_ref, a2_ref, g1_ref, g2_ref, kkp_ref, kap_ref, *rest):
    if has_vmix:
        v0_ref, v1_ref, v2_ref, vf_ref = rest[:4]
        rest = rest[4:]
    r_ref, k_ref, v_ref, wl_ref, kk_ref, b_ref, g_ref = rest
    D = x_ref.shape[-1]
    sh = mod_ref[0, :, 0:D]
    sc = mod_ref[0, :, D:2 * D]
    h = x_ref[0] * (1.0 + sc) + sh
    hp = xp_ref[0][7:8, :] * (1.0 + sc) + sh
    hp = jnp.where(pl.program_id(1) == 0, 0.0, hp)
    row = lax.broadcasted_iota(jnp.int32, h.shape, 0)
    h_prev = jnp.where(row == 0, hp, pltpu.roll(h, 1, axis=0))
    xx = h_prev - h
    mix = lambda j: (h + xx * mu_ref[j:j + 1, :]).astype(bf16)
    xr, xw, xk, xv, xa, xg = [mix(j) for j in range(6)]
    r = _dot(xr, wrkv_ref[0])
    k = _dot(xk, wrkv_ref[1])
    v = _dot(xv, wrkv_ref[2])
    zw = w0_ref[...] + _dot(jnp.tanh(_dot(xw, w1_ref[...])).astype(bf16), w2_ref[...])
    wl_ref[0] = -math.exp(-0.5) * _sigmoid(zw)
    a = _sigmoid(a0_ref[...] + _dot(_dot(xa, a1_ref[...]).astype(bf16), a2_ref[...]))
    g = _dot(_sigmoid(_dot(xg, g1_ref[...])).astype(bf16), g2_ref[...])
    if has_vmix:
        vmix = _sigmoid(v0_ref[...] + _dot(_dot(xv, v1_ref[...]).astype(bf16), v2_ref[...]))
        v = v + (vf_ref[0].astype(f32) - v) * vmix
    r_ref[0] = r.astype(bf16)
    v_ref[0] = v.astype(bf16)
    g_ref[0] = g.astype(bf16)
    k_ref[0] = (k * (1.0 + (a - 1.0) * kap_ref[...])).astype(bf16)
    kk = k * kkp_ref[...]
    jbd = _pair_ones()
    for p in range(D // LANES):
        lp = slice(p * LANES, (p + 1) * LANES)
        seg = kk[:, lp]
        nrm = jnp.sqrt(_head_sum(seg * seg, jbd))
        seg = seg / jnp.maximum(nrm, 1e-12)
        kk_ref[0, :, lp] = seg.astype(bf16)
        b_ref[0, :, lp] = (seg * a[:, lp]).astype(bf16)


def _rwkv_pre(x, modl, mu, wrkv, w0, w1, w2, a0, a1, a2, g1, g2, kkp, kap, vmix):
    B, S, D = x.shape
    tm = min(RWKV_PRE_TILE, S)
    row = lambda t: t.reshape(1, D)
    args = [x, x, modl, mu, wrkv.astype(bf16), row(w0), w1.astype(bf16), w2.astype(bf16), row(a0),
            a1.astype(bf16), a2.astype(bf16), g1.astype(bf16), g2.astype(bf16), row(kkp), row(kap)]
    tile = pl.BlockSpec((1, tm, D), lambda b, i: (b, i, 0))
    in_specs = [
        tile,
        pl.BlockSpec((1, 8, D), lambda b, i: (b, jnp.maximum(i * (tm // 8) - 1, 0), 0)),
        pl.BlockSpec((1, 1, modl.shape[-1]), lambda b, i: (b, 0, 0)),
    ] + [_full_spec(a.shape) for a in args[3:]]
    if vmix is not None:
        v0, v1, v2, v_first = vmix
        extra = [row(v0), v1.astype(bf16), v2.astype(bf16)]
        args += extra + [v_first]
        in_specs += [_full_spec(a.shape) for a in extra] + [tile]
    out_bf = jax.ShapeDtypeStruct((B, S, D), bf16)
    out_shape = [out_bf, out_bf, out_bf, jax.ShapeDtypeStruct((B, S, D), f32), out_bf, out_bf, out_bf]
    return pl.pallas_call(
        functools.partial(_rwkv_pre_kernel, vmix is not None),
        name="rwkv_pre",
        grid=(B, S // tm),
        in_specs=in_specs,
        out_specs=[tile] * 7,
        out_shape=out_shape,
        compiler_params=_cparams("arbitrary", "arbitrary"),
    )(*args)


def _wkv_kernel(C, r_ref, k_ref, v_ref, wl_ref, kk_ref, b_ref, g_ref, rk_ref, gng_ref, gnb_ref,
                o_ref, state_ref):
    @pl.when(pl.program_id(1) == 0)
    def _():
        state_ref[...] = jnp.zeros_like(state_ref)

    T, D = r_ref.shape[1], r_ref.shape[2]
    N = HEAD_DIM
    P = D // LANES
    C2 = 2 * C
    ri = lax.broadcasted_iota(jnp.int32, (C, C), 0)
    ci = lax.broadcasted_iota(jnp.int32, (C, C), 1)
    tri_incl = jnp.where(ri >= ci, 1.0, 0.0).astype(f32)
    r2 = lax.broadcasted_iota(jnp.int32, (C2, C2), 0)
    c2 = lax.broadcasted_iota(jnp.int32, (C2, C2), 1)
    dlt = jnp.where(jnp.where(r2 >= C, 1, 0) == jnp.where(c2 >= C, 1, 0), r2 - c2, -1)
    strict = dlt > 0
    incl = dlt >= 0
    eye2 = jnp.where(r2 == c2, 1.0, 0.0).astype(f32)
    h0 = lax.broadcasted_iota(jnp.int32, (C, LANES), 1) < N
    rs = lax.broadcasted_iota(jnp.int32, (LANES, LANES), 0)
    cs = lax.broadcasted_iota(jnp.int32, (LANES, LANES), 1)
    sbd = jnp.where(rs >= N, 1, 0) == jnp.where(cs >= N, 1, 0)
    jbd = jnp.where(sbd, 1.0, 0.0).astype(bf16)
    stack2 = lambda t: jnp.concatenate([t, t], axis=0)

    lanes = [slice(p * LANES, (p + 1) * LANES) for p in range(P)]
    nc = T // C
    units = [(c, p) for c in range(nc) for p in range(P)]
    hi_lo = lambda t: (t.astype(bf16), (t - t.astype(bf16).astype(f32)).astype(bf16))

    ch = []
    for c in range(nc):
        sl = pl.ds(c * C, C)
        wl = wl_ref[0, sl, :]
        cum = jnp.dot(tri_incl, wl, preferred_element_type=f32, precision=HIGHEST)
        g_incl = jnp.exp(cum)
        g_inv = jnp.exp(-cum)
        g_last = g_incl[C - 1:C, :]
        r = r_ref[0, sl, :].astype(f32)
        k = k_ref[0, sl, :].astype(f32)
        btf = b_ref[0, sl, :].astype(f32) * g_inv
        ktf = k * g_inv
        ch.append(dict(
            sl=sl, g_last=g_last, v16=v_ref[0, sl, :],
            af=-kk_ref[0, sl, :].astype(f32) * jnp.exp(cum - wl), rf=r * g_incl,
            bt=btf.astype(bf16), kt=ktf.astype(bf16),
            bc=(btf * g_last).astype(bf16), kc=(ktf * g_last).astype(bf16),
            rkr=r * k * rk_ref[...]))

    Ls, Aak, Arow, Vst, X2 = [], [], [], [], []
    for c, p in units:
        d, lp = ch[c], lanes[p]
        afp, rfp = d["af"][:, lp], d["rf"][:, lp]
        X4 = jnp.concatenate([jnp.where(h0, afp, 0.0), jnp.where(h0, 0.0, afp),
                              jnp.where(h0, rfp, 0.0), jnp.where(h0, 0.0, rfp)], axis=0).astype(bf16)
        M4 = _dot_nt(X4, jnp.concatenate([stack2(d["bt"][:, lp]), stack2(d["kt"][:, lp])], axis=0))
        Mb, Mk = M4[:, :C2], M4[:, C2:]
        Ls.append(jnp.where(strict, Mb[:C2], 0.0))
        Aak.append(jnp.where(strict, Mk[:C2], 0.0).astype(bf16))
        Arow.append(jnp.concatenate([jnp.where(incl, Mb[C2:], 0.0), jnp.where(incl, Mk[C2:], 0.0)],
                                    axis=1).astype(bf16))
        Vst.append(stack2(d["v16"][:, lp]))
        X2.append(jnp.concatenate([afp, rfp], axis=0).astype(bf16))
    Ps = [eye2 + L for L in Ls]
    n = 1
    while 2 * n < C:
        Lb = [L.astype(bf16) for L in Ls]
        Ls = [_dot(x, x) for x in Lb]
        Ps = [Pm + _dot(L.astype(bf16), Pm.astype(bf16)) for L, Pm in zip(Ls, Ps)]
        n *= 2
    Ps = [Pm.astype(bf16) for Pm in Ps]
    AV = [_dot(a, vs) for a, vs in zip(Aak, Vst)]
    ys = []
    state = [state_ref[p] for p in range(P)]
    for c in range(nc):
        d = ch[c]
        us = range(c * P, (c + 1) * P)
        M2 = [_dot_nt(X2[u], state[u - c * P].astype(bf16)) for u in us]
        Ust = [_dot(Ps[u], (AV[u] + stack2(m2[:C])).astype(bf16)) for u, m2 in zip(us, M2)]
        Yst = [_dot(Arow[u], jnp.concatenate([ust.astype(bf16), Vst[u]], axis=0)) + stack2(m2[C:])
               for u, ust, m2 in zip(us, Ust, M2)]
        ys += [jnp.where(h0, yst[:C], yst[C:]) for yst in Yst]
        for p, ust in enumerate(Ust):
            lp = lanes[p]
            u16 = jnp.where(h0, ust[:C], ust[C:]).astype(bf16)
            upd = _dot_tn(jnp.concatenate([u16, d["v16"][:, lp]], axis=0),
                          jnp.concatenate([d["bc"][:, lp], d["kc"][:, lp]], axis=0))
            state[p] = state[p] * d["g_last"][:, lp] + jnp.where(sbd, upd, 0.0)
    for p in range(P):
        state_ref[p] = state[p]
    sums = []
    for u, (c, p) in enumerate(units):
        sums.append(_dot(jnp.concatenate(hi_lo(ys[u]) + hi_lo(ch[c]["rkr"][:, lanes[p]]), axis=0), jbd))
    yds = [ys[u] - (sums[u][:C] + sums[u][C:C2]) * (1.0 / N) for u in range(len(units))]
    var = [_dot(jnp.concatenate(hi_lo(yd * yd), axis=0), jbd) for yd in yds]
    for u, (c, p) in enumerate(units):
        d, lp = ch[c], lanes[p]
        yn = yds[u] * lax.rsqrt((var[u][:C] + var[u][C:]) * (1.0 / N) + GN_EPS)
        bonus = (sums[u][C2:C2 + C] + sums[u][C2 + C:]) * d["v16"][:, lp].astype(f32)
        z = (yn * gng_ref[:, lp] + gnb_ref[:, lp] + bonus) * g_ref[0, d["sl"], lp].astype(f32)
        o_ref[0, d["sl"], lp] = z.astype(bf16)


def _wkv7(r, k, v, wl, kk, b, g, r_k, gn_g, gn_b):
    B, S, D = r.shape
    T = min(WKV_TILE, S)
    C = min(WKV_CHUNK, T)
    tile = pl.BlockSpec((1, T, D), lambda bb, i: (bb, i, 0))
    vec = _full_spec((1, D))
    return pl.pallas_call(
        functools.partial(_wkv_kernel, C),
        name="wkv7",
        grid=(B, S // T),
        in_specs=[tile] * 7 + [vec] * 3,
        out_specs=tile,
        out_shape=jax.ShapeDtypeStruct((B, S, D), bf16),
        scratch_shapes=[pltpu.VMEM((D // LANES, LANES, LANES), f32)],
        compiler_params=_cparams("arbitrary", "arbitrary"),
    )(r, k, v, wl, kk, b, g, r_k.reshape(1, D), gn_g.reshape(1, D), gn_b.reshape(1, D))


def _proj_route_kernel(z_ref, wo_ref, x_ref, mod_ref, lng_ref, lnb_ref, w_ref, wlo_ref, b_ref,
                       xo_ref, h_ref, idx_ref, gate_ref, cnt_ref):
    @pl.when((pl.program_id(0) == 0) & (pl.program_id(1) == 0))
    def _():
        cnt_ref[...] = jnp.zeros_like(cnt_ref)

    D = x_ref.shape[-1]
    gt = mod_ref[0, :, 2 * D:3 * D]
    u = DEEPNORM_ALPHA * x_ref[0] + (1.0 + gt) * _dot(z_ref[0], wo_ref[...])
    xn = _layer_norm_rows(u, lng_ref[...], lnb_ref[...])
    xo_ref[0] = xn
    sh = mod_ref[0, :, 3 * D:4 * D]
    sc = mod_ref[0, :, 4 * D:5 * D]
    h = xn * (1.0 + sc) + sh
    h_ref[0] = h.astype(bf16)
    h_hi = h.astype(bf16)
    h_lo = (h - h_hi.astype(f32)).astype(bf16)
    lg = _dot(h_hi, w_ref[...]) + _dot(h_hi, wlo_ref[...]) + _dot(h_lo, w_ref[...]) + b_ref[...]
    lane = lax.broadcasted_iota(jnp.int32, lg.shape, 1)
    glog = jnp.where(lane < N_GROUPS, lg, -jnp.inf)
    gmax = jnp.max(glog, axis=-1, keepdims=True)
    p_g = 1.0 / jnp.sum(jnp.exp(glog - gmax), axis=-1, keepdims=True)
    g_sel = jnp.min(jnp.where(glog == gmax, lane, LANES), axis=-1, keepdims=True)
    e_lo = N_GROUPS + EXP_PER_GROUP * g_sel
    elog = jnp.where((lane >= e_lo) & (lane < e_lo + EXP_PER_GROUP), lg, -jnp.inf)
    e1 = jnp.max(elog, axis=-1, keepdims=True)
    esum = jnp.sum(jnp.exp(elog - e1), axis=-1, keepdims=True)
    i1 = jnp.min(jnp.where(elog == e1, lane, LANES), axis=-1, keepdims=True)
    elog2 = jnp.where(lane == i1, -jnp.inf, elog)
    e2 = jnp.max(elog2, axis=-1, keepdims=True)
    i2 = jnp.min(jnp.where(elog2 == e2, lane, LANES), axis=-1, keepdims=True)
    p1 = 1.0 / esum
    p2 = jnp.exp(e2 - e1) / esum
    psum = p1 + p2
    gate_ref[0] = jnp.where(lane == 0, p_g * p1 / psum, jnp.where(lane == 1, p_g * p2 / psum, 0.0))
    tm = lg.shape[0]
    onehot = jnp.where(lane == i1, 1.0, 0.0) + jnp.where(lane == i2, 1.0, 0.0)
    ri = lax.broadcasted_iota(jnp.int32, (tm, tm), 0)
    ci = lax.broadcasted_iota(jnp.int32, (tm, tm), 1)
    before = _dot(jnp.where(ri > ci, 1.0, 0.0).astype(bf16), onehot.astype(bf16)) + cnt_ref[...]
    rank1 = jnp.sum(jnp.where(lane == i1, before, 0.0), axis=-1, keepdims=True).astype(jnp.int32)
    rank2 = jnp.sum(jnp.where(lane == i2, before, 0.0), axis=-1, keepdims=True).astype(jnp.int32)
    idx_ref[0] = jnp.where(lane == 0, i1 - N_GROUPS, jnp.where(lane == 1, i2 - N_GROUPS,
                           jnp.where(lane == 2, rank1, jnp.where(lane == 3, rank2, 0))))
    cnt_ref[...] = before[tm - 1:tm, :] + onehot[tm - 1:tm, :]


def _proj_route(z, w_o, x, modl, ln_g, ln_b, w_grp, b_grp, w_exp, b_exp):
    B, S, D = x.shape
    tm = min(ROW_TILE, S)
    pad = LANES - N_GROUPS - N_EXPERTS
    w = jnp.concatenate([w_grp, w_exp, jnp.zeros((D, pad), f32)], axis=1)
    w_hi = w.astype(bf16)
    b = jnp.concatenate([b_grp, b_exp, jnp.zeros((pad,), f32)]).reshape(1, LANES)
    tile = pl.BlockSpec((1, tm, D), lambda bb, i: (bb, i, 0))
    small = pl.BlockSpec((1, tm, LANES), lambda bb, i: (bb, i, 0))
    return pl.pallas_call(
        _proj_route_kernel,
        name="proj_route",
        grid=(B, S // tm),
        in_specs=[tile, _full_spec(w_o.shape), tile,
                  pl.BlockSpec((1, 1, modl.shape[-1]), lambda bb, i: (bb, 0, 0)),
                  _full_spec((1, D)), _full_spec((1, D)),
                  _full_spec(w.shape), _full_spec(w.shape), _full_spec(b.shape)],
        out_specs=[tile, tile, small, small, _full_spec((1, LANES))],
        out_shape=[jax.ShapeDtypeStruct((B, S, D), f32),
                   jax.ShapeDtypeStruct((B, S, D), bf16),
                   jax.ShapeDtypeStruct((B, S, LANES), jnp.int32),
                   jax.ShapeDtypeStruct((B, S, LANES), f32),
                   jax.ShapeDtypeStruct((1, LANES), f32)],
        compiler_params=_cparams("arbitrary", "arbitrary"),
    )(z, w_o.astype(bf16), x, modl, ln_g.reshape(1, D), ln_b.reshape(1, D),
      w_hi, (w - w_hi.astype(f32)).astype(bf16), b)


def _expert_kernel(be_ref, nu_ref, x_ref, wg_ref, wu_ref, wd_ref, o_ref, wg16, wu16, wd16):
    i = pl.program_id(0)

    @pl.when((i == 0) | (be_ref[i] != be_ref[jnp.maximum(i - 1, 0)]))
    def _():
        wg16[...] = wg_ref[0, 0].astype(bf16)
        wu16[...] = wu_ref[0, 0].astype(bf16)
        wd16[...] = wd_ref[0, 0].astype(bf16)

    @pl.when(i < nu_ref[0])
    def _():
        x = x_ref[...]
        hg = _dot(x, wg16[...])
        hu = _dot(x, wu16[...])
        o_ref[...] = _dot((hg * _sigmoid(hg) * hu).astype(bf16), wd16[...]).astype(o_ref.dtype)

    @pl.when(i >= nu_ref[0])
    def _():
        o_ref[...] = jnp.zeros_like(o_ref)


def _expert_ffn(xs, block_exp, n_used, wg, wu, wd, layer):
    n_slots, D = xs.shape
    F = wg.shape[-1]
    blk = MOE_BLOCK
    row_map = lambda i, be, nu: (jnp.maximum(jnp.minimum(i, nu[0] - 1), 0), 0)
    w_map = lambda i, be, nu: (layer, be[i], 0, 0)
    grid_spec = pltpu.PrefetchScalarGridSpec(
        num_scalar_prefetch=2,
        grid=(n_slots // blk,),
        in_specs=[
            pl.BlockSpec((blk, D), row_map),
            pl.BlockSpec((1, 1, D, F), w_map),
            pl.BlockSpec((1, 1, D, F), w_map),
            pl.BlockSpec((1, 1, F, D), w_map),
        ],
        out_specs=pl.BlockSpec((blk, D), lambda i, be, nu: (i, 0)),
        scratch_shapes=[pltpu.VMEM((D, F), bf16), pltpu.VMEM((D, F), bf16), pltpu.VMEM((F, D), bf16)],
    )
    return pl.pallas_call(
        _expert_kernel,
        name="moe_experts",
        grid_spec=grid_spec,
        out_shape=jax.ShapeDtypeStruct((n_slots, D), bf16),
        compiler_params=_cparams("arbitrary"),
    )(block_exp, n_used, xs, wg, wu, wd)


def _take_rows(table, idx):
    return table.at[idx].get(mode="promise_in_bounds")


def _dispatch(expert_idx, rank, counts, blk):
    T = expert_idx.shape[0]
    A = T * TOP_K
    n_blocks = A // blk + N_EXPERTS
    experts = jnp.arange(N_EXPERTS, dtype=jnp.int32)
    padded = (counts + blk - 1) // blk * blk
    pad_end = jnp.cumsum(padded)
    pad_start = pad_end - padded
    start = jnp.cumsum(counts) - counts
    dest = rank + jnp.sum(jnp.where(expert_idx[..., None] == experts, pad_start, 0), axis=-1)
    block_exp = jnp.minimum(jnp.sum(((jnp.arange(n_blocks, dtype=jnp.int32) * blk)[:, None] >= pad_end[None, :])
                                    .astype(jnp.int32), axis=1), N_EXPERTS - 1)
    order = jnp.argsort(expert_idx.reshape(A)).astype(jnp.int32)
    shift = jnp.sum(jnp.where(block_exp[:, None] == experts, start - pad_start, 0), axis=-1)
    pos = jnp.arange(n_blocks * blk, dtype=jnp.int32) + jnp.repeat(shift, blk)
    slot_tok = _take_rows(order, pos % A) // TOP_K
    n_used = (pad_end[-1] // blk).astype(jnp.int32).reshape(1)
    return dest, slot_tok, block_exp, n_used


def _combine_ln_kernel(x_ref, y0_ref, y1_ref, gate_ref, mod_ref, lng_ref, lnb_ref, o_ref):
    D = x_ref.shape[-1]
    gt = mod_ref[0, :, 5 * D:6 * D]
    gates = gate_ref[0]
    y = gates[:, 0:1] * y0_ref[0].astype(f32) + gates[:, 1:2] * y1_ref[0].astype(f32)
    u = DEEPNORM_ALPHA * x_ref[0] + (1.0 + gt) * y
    o_ref[0] = _layer_norm_rows(u, lng_ref[...], lnb_ref[...])


def _combine_ln(x, y01, gates, modl, ln_g, ln_b):
    B, S, D = x.shape
    tm = min(ROW_TILE, S)
    tile = pl.BlockSpec((1, tm, D), lambda b, i: (b, i, 0))
    return pl.pallas_call(
        _combine_ln_kernel,
        name="moe_combine_ln",
        grid=(B, S // tm),
        in_specs=[tile, tile, pl.BlockSpec((1, tm, D), lambda b, i: (b + B, i, 0)),
                  pl.BlockSpec((1, tm, LANES), lambda b, i: (b, i, 0)),
                  pl.BlockSpec((1, 1, modl.shape[-1]), lambda b, i: (b, 0, 0)),
                  _full_spec((1, D)), _full_spec((1, D))],
        out_specs=tile,
        out_shape=jax.ShapeDtypeStruct((B, S, D), f32),
        compiler_params=_cparams("arbitrary", "arbitrary"),
    )(x, y01, y01, gates, modl, ln_g.reshape(1, D), ln_b.reshape(1, D))


def _moe_layer(x, routed, modl, wg, wu, wd, layer, ln_g, ln_b):
    B, S, D = x.shape
    T = B * S
    hb, idx, gates, cnt = routed
    idx = idx.reshape(T, LANES)
    counts = cnt[0, N_GROUPS:N_GROUPS + N_EXPERTS].astype(jnp.int32)
    dest, slot_tok, block_exp, n_used = _dispatch(idx[:, :TOP_K], idx[:, TOP_K:2 * TOP_K], counts, MOE_BLOCK)
    ys = _expert_ffn(_take_rows(hb.reshape(T, D), slot_tok), block_exp, n_used, wg, wu, wd, layer)
    y01 = _take_rows(ys, dest.T.reshape(TOP_K * T)).reshape(TOP_K * B, S, D)
    return _combine_ln(x, y01, gates, modl, ln_g, ln_b)


def _shared_kv_kernel(x_ref, mod_ref, wk_ref, wvt_ref, wf_ref, wflo_ref, fb_ref, kn_ref, place_ref, ones_ref,
                      k_ref, kb_ref, vt_ref, f_ref, carry_ref):
    @pl.when(pl.program_id(1) == 0)
    def _():
        carry_ref[...] = jnp.zeros_like(carry_ref)

    D = x_ref.shape[-1]
    tm = x_ref.shape[1]
    shift = mod_ref[0, :, 0:D]
    scale = mod_ref[0, :, D:2 * D]
    hk = x_ref[0] * (1.0 + scale) + shift
    hb = hk.astype(bf16)
    k = _dot(hb, wk_ref[...])
    vt_ref[0] = _dot_nt(wvt_ref[...], hb).astype(bf16)
    h_lo = (hk - hb.astype(f32)).astype(bf16)
    f = _dot(hb, wf_ref[...]) + _dot(hb, wflo_ref[...]) + _dot(h_lo, wf_ref[...]) + fb_ref[...]
    log_f = jnp.minimum(f, 0.0) - jnp.log(1.0 + jnp.exp(-jnp.abs(f)))
    row = lax.broadcasted_iota(jnp.int32, log_f.shape, 0)
    acc = log_f
    d = 1
    while d < tm:
        acc = acc + jnp.where(row >= d, pltpu.roll(acc, d, axis=0), 0.0)
        d *= 2
    acc = acc + carry_ref[...]
    f_ref[0] = acc
    carry_ref[...] = acc[tm - 1:tm, :]
    kb_ref[0] = (_dot(_bias_split(acc * LOG2E), place_ref[...]) + ones_ref[...]).astype(bf16)
    jbd = _pair_ones()
    for p in range(D // LANES):
        lp = slice(p * LANES, (p + 1) * LANES)
        seg = k[:, lp]
        ms = _head_sum(seg * seg, jbd) * (1.0 / HEAD_DIM)
        k_ref[0, :, lp] = (seg * lax.rsqrt(ms + QK_EPS) * kn_ref[...]).astype(bf16)


def _shared_kv(x, kvmod, w_kvf, b_f, k_norm):
    B, S, D = x.shape
    H = D // HEAD_DIM
    tm = min(ROW_TILE, S)
    wk = w_kvf[:, :D].astype(bf16)
    wvt = w_kvf[:, D:2 * D].T.astype(bf16)
    wf = jnp.concatenate([w_kvf[:, 2 * D:], jnp.zeros((D, LANES - H), f32)], axis=1)
    wf_hi = wf.astype(bf16)
    fb = jnp.concatenate([b_f, jnp.zeros((LANES - H,), f32)]).reshape(1, LANES)
    place, ones = _bias_placement(D, True)
    tile = pl.BlockSpec((1, tm, D), lambda b, i: (b, i, 0))
    act = jax.ShapeDtypeStruct((B, S, D), bf16)
    return pl.pallas_call(
        _shared_kv_kernel,
        name="shared_kv",
        grid=(B, S // tm),
        in_specs=[tile, pl.BlockSpec((1, 1, 2 * D), lambda b, i: (b, 0, 0)),
                  _full_spec(wk.shape), _full_spec(wvt.shape), _full_spec(wf.shape), _full_spec(wf.shape),
                  _full_spec(fb.shape), _full_spec((1, LANES)), _full_spec(place.shape), _full_spec(ones.shape)],
        out_specs=[tile, tile, pl.BlockSpec((1, D, tm), lambda b, i: (b, 0, i)),
                   pl.BlockSpec((1, tm, LANES), lambda b, i: (b, i, 0))],
        out_shape=[act, act, jax.ShapeDtypeStruct((B, D, S), bf16), jax.ShapeDtypeStruct((B, S, LANES), f32)],
        scratch_shapes=[pltpu.VMEM((1, LANES), f32)],
        compiler_params=_cparams("arbitrary", "arbitrary"),
    )(x, kvmod, wk, wvt, wf_hi, (wf - wf_hi.astype(f32)).astype(bf16), fb,
      jnp.tile(k_norm, LANES // HEAD_DIM).reshape(1, LANES), place, ones)


def _fox_pre_kernel(x_ref, mod_ref, wq_ref, wg_ref, qn_ref, f_ref, place_ref, ones_ref, q_ref, qb_ref, gate_ref):
    D = x_ref.shape[-1]
    sh = mod_ref[0, :, 0:D]
    sc = mod_ref[0, :, D:2 * D]
    hb = (x_ref[0] * (1.0 + sc) + sh).astype(bf16)
    q = _dot(hb, wq_ref[...])
    gate_ref[0] = _sigmoid(_dot(hb, wg_ref[...])).astype(bf16)
    qb_ref[0] = (_dot(_bias_split(f_ref[0] * LOG2E), place_ref[...]) + ones_ref[...]).astype(bf16)
    qscale = HEAD_DIM ** -0.5 * LOG2E
    jbd = _pair_ones()
    for p in range(D // LANES):
        lp = slice(p * LANES, (p + 1) * LANES)
        seg = q[:, lp]
        ms = _head_sum(seg * seg, jbd) * (1.0 / HEAD_DIM)
        q_ref[0, :, lp] = (seg * lax.rsqrt(ms + QK_EPS) * (qn_ref[...] * qscale)).astype(bf16)


def _fox_pre(x, modl, w_qg, q_norm, fcum):
    B, S, D = x.shape
    tm = min(ROW_TILE, S)
    tile = pl.BlockSpec((1, tm, D), lambda b, i: (b, i, 0))
    wq = w_qg[:, :D].astype(bf16)
    wg = w_qg[:, D:].astype(bf16)
    place, ones = _bias_placement(D, False)
    act = jax.ShapeDtypeStruct((B, S, D), bf16)
    return pl.pallas_call(
        _fox_pre_kernel,
        name="fox_pre",
        grid=(B, S // tm),
        in_specs=[tile, pl.BlockSpec((1, 1, modl.shape[-1]), lambda b, i: (b, 0, 0)),
                  _full_spec(wq.shape), _full_spec(wg.shape), _full_spec((1, LANES)),
                  pl.BlockSpec((1, tm, LANES), lambda b, i: (b, i, 0)),
                  _full_spec(place.shape), _full_spec(ones.shape)],
        out_specs=[tile, tile, tile],
        out_shape=[act, act, act],
        compiler_params=_cparams("arbitrary", "arbitrary"),
    )(x, modl, wq, wg, jnp.tile(q_norm, LANES // HEAD_DIM).reshape(1, LANES), fcum, place, ones)


def _fox_attn_kernel(qi_ref, kj_ref, flag_ref, q_ref, qb_ref, k_ref, kb_ref, vt_ref, gate_ref, o_ref,
                     qm_ref, m_ref, l_ref, acc_ref):
    s = pl.program_id(2)
    i = qi_ref[s]
    j = kj_ref[s]
    flags = flag_ref[s]
    N = HEAD_DIM
    nh = q_ref.shape[2] // N
    hpp = LANES // N
    tk, tq = k_ref.shape[1], q_ref.shape[1]
    pair = lambda h: slice((h // hpp) * LANES, (h // hpp + 1) * LANES)

    @pl.when(j == 0)
    def _():
        m_ref[...] = jnp.full_like(m_ref, NEG_BIG)
        l_ref[...] = jnp.zeros_like(l_ref)
        acc_ref[...] = jnp.zeros_like(acc_ref)
        for hh in range(nh):
            qcat = jnp.concatenate([q_ref[0, :, pair(hh)], qb_ref[0, :, pair(hh)]], axis=1).astype(f32)
            head = (lax.broadcasted_iota(jnp.int32, qcat.shape, 1) % LANES) // N
            qm_ref[hh] = jnp.where(head == hh % hpp, qcat, 0.0).astype(bf16)

    def step(masked, q_lo):
        qs = slice(q_lo, tq)
        kcat = [jnp.concatenate([k_ref[0, :, pair(hh)], kb_ref[0, :, pair(hh)]], axis=1)
                for hh in range(0, nh, hpp)]
        sts = [_dot_nt(kcat[hh // hpp], qm_ref[hh, qs, :]) for hh in range(nh)]
        for hh, st in enumerate(sts):
            if masked:
                key = j * tk + lax.broadcasted_iota(jnp.int32, st.shape, 0)
                qry = i * tq + q_lo + lax.broadcasted_iota(jnp.int32, st.shape, 1)
                st = jnp.where(key <= qry, st, -jnp.inf)
            m_prev = m_ref[hh, :, qs]
            m_new = jnp.maximum(m_prev, jnp.max(st, axis=0, keepdims=True))
            alpha = jnp.exp2(m_prev - m_new)
            p = jnp.exp2(st - m_new)
            l_ref[hh, :, qs] = alpha * l_ref[hh, :, qs] + jnp.sum(p, axis=0, keepdims=True)
            acc_ref[hh, :, qs] = alpha * acc_ref[hh, :, qs] + _dot(vt_ref[0, hh * N:(hh + 1) * N, :], p.astype(bf16))
            m_ref[hh, :, qs] = m_new

    @pl.when((flags & (ATTN_FLAG_MASK | ATTN_FLAG_UPPER)) == 0)
    def _():
        step(False, 0)

    @pl.when((flags & (ATTN_FLAG_MASK | ATTN_FLAG_UPPER)) == ATTN_FLAG_MASK)
    def _():
        step(True, 0)

    @pl.when((flags & ATTN_FLAG_UPPER) != 0)
    def _():
        step(True, tq // 2)

    @pl.when((flags & ATTN_FLAG_LAST) != 0)
    def _():
        for pr in range(nh // hpp):
            ot = jnp.concatenate([acc_ref[hh] * (1.0 / l_ref[hh]) for hh in range(pr * hpp, (pr + 1) * hpp)],
                                 axis=0)
            lp = slice(pr * LANES, (pr + 1) * LANES)
            o_ref[0, :, lp] = (ot.T * gate_ref[0, :, lp].astype(f32)).astype(bf16)


def _fox_attn(q, qb, k, kb, vt, gate):
    B, S, D = gate.shape
    tq = min(ATTN_Q_TILE, S)
    tk = min(ATTN_K_TILE, S)
    LW = ATTN_PAIRS * LANES
    nh = LW // HEAD_DIM
    qi, kj, flags = [], [], []
    for i in range(S // tq):
        last = ((i + 1) * tq - 1) // tk
        for j in range(last + 1):
            qi.append(i)
            kj.append(j)
            needs_mask = (j + 1) * tk - 1 > i * tq
            upper_only = j * tk >= i * tq + tq // 2
            flags.append((ATTN_FLAG_MASK if needs_mask else 0) | (ATTN_FLAG_LAST if j == last else 0)
                         | (ATTN_FLAG_UPPER if upper_only else 0))
    tables = [jnp.asarray(t, jnp.int32) for t in (qi, kj, flags)]
    qtile = pl.BlockSpec((1, tq, LW), lambda b, p, s, qi, kj, fl: (b, qi[s], p))
    ktile = pl.BlockSpec((1, tk, LW), lambda b, p, s, qi, kj, fl: (b, kj[s], p))
    grid_spec = pltpu.PrefetchScalarGridSpec(
        num_scalar_prefetch=3,
        grid=(B, D // LW, len(qi)),
        in_specs=[qtile, qtile, ktile, ktile,
                  pl.BlockSpec((1, LW, tk), lambda b, p, s, qi, kj, fl: (b, p, kj[s])),
                  qtile],
        out_specs=qtile,
        scratch_shapes=[pltpu.VMEM((nh, tq, 2 * LANES), bf16),
                        pltpu.VMEM((nh, 1, tq), f32), pltpu.VMEM((nh, 1, tq), f32),
                        pltpu.VMEM((nh, HEAD_DIM, tq), f32)],
    )
    return pl.pallas_call(
        _fox_attn_kernel,
        name="fox_attn",
        grid_spec=grid_spec,
        out_shape=jax.ShapeDtypeStruct((B, S, D), bf16),
        compiler_params=_cparams("arbitrary", "arbitrary", "arbitrary"),
    )(*tables, q, qb, k, kb, vt, gate)


def kernel(x, c, ada_w, ada_b, ln_g, ln_b, rw_mu, rw_rkv, rw_w0, rw_w1, rw_w2, rw_a0, rw_a1, rw_a2, rw_g1, rw_g2, rw_kk, rw_ka, rw_rk, rw_gn_g, rw_gn_b, rw_wo, rw_v0, rw_v1, rw_v2, kv_ada_w, kv_ada_b, kv_w, kv_fb, kv_knorm, fx_wqg, fx_qnorm, fx_wo, moe_wgrp, moe_bgrp, moe_wexp, moe_bexp, moe_wgate, moe_wup, moe_wdown):
    B, S, D = x.shape
    depth = ada_w.shape[0]
    n_a = rw_mu.shape[0]
    mod = _adaln_mod(c, ada_w, ada_b)
    kvmod = _adaln_mod(c, kv_ada_w[None], kv_ada_b[None])[0].reshape(B, 1, 2 * D)
    kv = None
    v_first = None
    for l in range(depth):
        modl = mod[l].reshape(B, 1, 6 * D)
        if l < n_a:
            vmix = None if l == 0 else (rw_v0[l - 1], rw_v1[l - 1], rw_v2[l - 1], v_first)
            r, k, v, wl, kk, b, g = _rwkv_pre(x, modl, rw_mu[l], rw_rkv[l], rw_w0[l], rw_w1[l], rw_w2[l],
                                              rw_a0[l], rw_a1[l], rw_a2[l], rw_g1[l], rw_g2[l],
                                              rw_kk[l], rw_ka[l], vmix)
            if l == 0:
                v_first = v
            z = _wkv7(r, k, v, wl, kk, b, g, rw_rk[l].reshape(D), rw_gn_g[l], rw_gn_b[l])
            w_o = rw_wo[l]
        else:
            j = l - n_a
            k, kb, vt, fcum = kv
            q, qb, gate = _fox_pre(x, modl, fx_wqg[j], fx_qnorm[j], fcum)
            z = _fox_attn(q, qb, k, kb, vt, gate)
            w_o = fx_wo[j]
        x, *routed = _proj_route(z, w_o, x, modl, ln_g[l, 0], ln_b[l, 0],
                                 moe_wgrp[l], moe_bgrp[l], moe_wexp[l], moe_bexp[l])
        x = _moe_layer(x, routed, modl, moe_wgate, moe_wup, moe_wdown, l, ln_g[l, 1], ln_b[l, 1])
        if l == n_a - 1:
            kv = _shared_kv(x, kvmod, kv_w, kv_fb, kv_knorm)
    return x
```

```python
import functools
import math

import jax
import jax.numpy as jnp
import numpy as np
from jax import lax
from jax.experimental import pallas as pl
from jax.experimental.pallas import tpu as pltpu

HEAD_DIM = 64
N_GROUPS = 4
EXP_PER_GROUP = 8
N_EXPERTS = N_GROUPS * EXP_PER_GROUP
TOP_K = 2
DEPTH = 4
N_A = 2
DEEPNORM_ALPHA = (2 * DEPTH) ** 0.25
LN_EPS = 1e-5
GN_EPS = 64e-5
QK_EPS = 1e-6

LANES = 128
VMEM_LIMIT_BYTES = 56 * 1024 * 1024
WKV_CHUNK = 64
WKV_TILE = 128
ROW_TILE = 512
RWKV_PRE_TILE = 512
ATTN_Q_TILE = 1024
ATTN_K_TILE = 512
ATTN_PAIRS = 4
ATTN_FLAG_MASK = 1
ATTN_FLAG_LAST = 2
ATTN_FLAG_UPPER = 4
MOE_BLOCK = 512
NEG_BIG = -1e30
LOG2E = math.log2(math.e)

f32 = jnp.float32
bf16 = jnp.bfloat16
HIGHEST = lax.Precision.HIGHEST


def _cparams(*sem):
    return pltpu.CompilerParams(dimension_semantics=sem, vmem_limit_bytes=VMEM_LIMIT_BYTES)


def _sigmoid(x):
    return 0.5 * jnp.tanh(0.5 * x) + 0.5


def _dot(a, b):
    return jnp.dot(a, b, preferred_element_type=f32)


def _dot_nt(a, b):
    return lax.dot_general(a, b, (((1,), (1,)), ((), ())), preferred_element_type=f32)


def _dot_tn(a, b):
    return lax.dot_general(a, b, (((0,), (0,)), ((), ())), preferred_element_type=f32)


def _full_spec(shape):
    n = len(shape)
    return pl.BlockSpec(shape, lambda *_: (0,) * n)


def _pair_ones():
    rs = lax.broadcasted_iota(jnp.int32, (LANES, LANES), 0)
    cs = lax.broadcasted_iota(jnp.int32, (LANES, LANES), 1)
    return jnp.where(jnp.where(rs >= HEAD_DIM, 1, 0) == jnp.where(cs >= HEAD_DIM, 1, 0), 1.0, 0.0).astype(bf16)


def _head_sum(x, jbd):
    xh = x.astype(bf16)
    xl = (x - xh.astype(f32)).astype(bf16)
    return _dot(xh, jbd) + _dot(xl, jbd)


def _bias_split(f):
    t0 = f.astype(bf16)
    r1 = f - t0.astype(f32)
    t1 = r1.astype(bf16)
    t2 = (r1 - t1.astype(f32)).astype(bf16)
    return jnp.concatenate([t0, t1, t2], axis=1)


def _bias_placement(D, is_key):
    H = D // HEAD_DIM
    place = np.zeros((3 * LANES, D), np.float32)
    ones = np.zeros((1, D), np.float32)
    for h in range(H):
        for i in range(3):
            place[i * LANES + h, h * HEAD_DIM + (3 + i if is_key else i)] = -1.0 if is_key else 1.0
            ones[0, h * HEAD_DIM + (i if is_key else 3 + i)] = 1.0
    return jnp.asarray(place, bf16), jnp.asarray(ones, f32)


def _layer_norm_rows(u, g, b):
    mu = jnp.mean(u, axis=-1, keepdims=True)
    d = u - mu
    var = jnp.mean(d * d, axis=-1, keepdims=True)
    return d * lax.rsqrt(var + LN_EPS) * g + b


def _mod_kernel(c_ref, w_ref, b_ref, o_ref):
    c = c_ref[...]
    cs = c * _sigmoid(c)
    o_ref[0] = jnp.dot(cs, w_ref[0], preferred_element_type=f32, precision=HIGHEST) + b_ref[0]


def _adaln_mod(c, w, b):
    L, D, N = w.shape
    B = c.shape[0]
    tn = min(N, 1024)
    return pl.pallas_call(
        _mod_kernel,
        name="adaln_mod",
        grid=(L, N // tn),
        in_specs=[
            _full_spec((B, D)),
            pl.BlockSpec((1, D, tn), lambda l, j: (l, 0, j)),
            pl.BlockSpec((1, 1, tn), lambda l, j: (l, 0, j)),
        ],
        out_specs=pl.BlockSpec((1, B, tn), lambda l, j: (l, 0, j)),
        out_shape=jax.ShapeDtypeStruct((L, B, N), f32),
        compiler_params=_cparams("arbitrary", "arbitrary"),
    )(c, w, b.reshape(L, 1, N))


def _rwkv_pre_kernel(has_vmix, x_ref, xp_ref, mod_ref, mu_ref, wrkv_ref, w0_ref, w1_ref, w2_ref,
                     a0_ref, a1_ref, a2_ref, g1_ref, g2_ref, kkp_ref, kap_ref, *rest):
    if has_vmix:
        v0_ref, v1_ref, v2_ref, vf_ref = rest[:4]
        rest = rest[4:]
    r_ref, k_ref, v_ref, wl_ref, kk_ref, b_ref, g_ref = rest
    D = x_ref.shape[-1]
    sh = mod_ref[0, :, 0:D]
    sc = mod_ref[0, :, D:2 * D]
    h = x_ref[0] * (1.0 + sc) + sh
    hp = xp_ref[0][7:8, :] * (1.0 + sc) + sh
    hp = jnp.where(pl.program_id(1) == 0, 0.0, hp)
    row = lax.broadcasted_iota(jnp.int32, h.shape, 0)
    h_prev = jnp.where(row == 0, hp, pltpu.roll(h, 1, axis=0))
    xx = h_prev - h
    mix = lambda j: (h + xx * mu_ref[j:j + 1, :]).astype(bf16)
    xr, xw, xk, xv, xa, xg = [mix(j) for j in range(6)]
    r = _dot(xr, wrkv_ref[0])
    k = _dot(xk, wrkv_ref[1])
    v = _dot(xv, wrkv_ref[2])
    zw = w0_ref[...] + _dot(jnp.tanh(_dot(xw, w1_ref[...])).astype(bf16), w2_ref[...])
    wl_ref[0] = -math.exp(-0.5) * _sigmoid(zw)
    a = _sigmoid(a0_ref[...] + _dot(_dot(xa, a1_ref[...]).astype(bf16), a2_ref[...]))
    g = _dot(_sigmoid(_dot(xg, g1_ref[...])).astype(bf16), g2_ref[...])
    if has_vmix:
        vmix = _sigmoid(v0_ref[...] + _dot(_dot(xv, v1_ref[...]).astype(bf16), v2_ref[...]))
        v = v + (vf_ref[0].astype(f32) - v) * vmix
    r_ref[0] = r.astype(bf16)
    v_ref[0] = v.astype(bf16)
    g_ref[0] = g.astype(bf16)
    k_ref[0] = (k * (1.0 + (a - 1.0) * kap_ref[...])).astype(bf16)
    kk = k * kkp_ref[...]
    jbd = _pair_ones()
    for p in range(D // LANES):
        lp = slice(p * LANES, (p + 1) * LANES)
        seg = kk[:, lp]
        nrm = jnp.sqrt(_head_sum(seg * seg, jbd))
        seg = seg / jnp.maximum(nrm, 1e-12)
        kk_ref[0, :, lp] = seg.astype(bf16)
        b_ref[0, :, lp] = (seg * a[:, lp]).astype(bf16)


def _rwkv_pre(x, modl, mu, wrkv, w0, w1, w2, a0, a1, a2, g1, g2, kkp, kap, vmix):
    B, S, D = x.shape
    tm = min(RWKV_PRE_TILE, S)
    row = lambda t: t.reshape(1, D)
    args = [x, x, modl, mu, wrkv.astype(bf16), row(w0), w1.astype(bf16), w2.astype(bf16), row(a0),
            a1.astype(bf16), a2.astype(bf16), g1.astype(bf16), g2.astype(bf16), row(kkp), row(kap)]
    tile = pl.BlockSpec((1, tm, D), lambda b, i: (b, i, 0))
    in_specs = [
        tile,
        pl.BlockSpec((1, 8, D), lambda b, i: (b, jnp.maximum(i * (tm // 8) - 1, 0), 0)),
        pl.BlockSpec((1, 1, modl.shape[-1]), lambda b, i: (b, 0, 0)),
    ] + [_full_spec(a.shape) for a in args[3:]]
    if vmix is not None:
        v0, v1, v2, v_first = vmix
        extra = [row(v0), v1.astype(bf16), v2.astype(bf16)]
        args += extra + [v_first]
        in_specs += [_full_spec(a.shape) for a in extra] + [tile]
    out_bf = jax.ShapeDtypeStruct((B, S, D), bf16)
    out_shape = [out_bf, out_bf, out_bf, jax.ShapeDtypeStruct((B, S, D), f32), out_bf, out_bf, out_bf]
    return pl.pallas_call(
        functools.partial(_rwkv_pre_kernel, vmix is not None),
        name="rwkv_pre",
        grid=(B, S // tm),
        in_specs=in_specs,
        out_specs=[tile] * 7,
        out_shape=out_shape,
        compiler_params=_cparams("arbitrary", "arbitrary"),
    )(*args)


def _wkv_kernel(C, r_ref, k_ref, v_ref, wl_ref, kk_ref, b_ref, g_ref, rk_ref, gng_ref, gnb_ref,
                o_ref, state_ref):
    @pl.when(pl.program_id(1) == 0)
    def _():
        state_ref[...] = jnp.zeros_like(state_ref)

    T, D = r_ref.shape[1], r_ref.shape[2]
    N = HEAD_DIM
    P = D // LANES
    C2 = 2 * C
    ri = lax.broadcasted_iota(jnp.int32, (C, C), 0)
    ci = lax.broadcasted_iota(jnp.int32, (C, C), 1)
    tri_incl = jnp.where(ri >= ci, 1.0, 0.0).astype(f32)
    r2 = lax.broadcasted_iota(jnp.int32, (C2, C2), 0)
    c2 = lax.broadcasted_iota(jnp.int32, (C2, C2), 1)
    dlt = jnp.where(jnp.where(r2 >= C, 1, 0) == jnp.where(c2 >= C, 1, 0), r2 - c2, -1)
    strict = dlt > 0
    incl = dlt >= 0
    eye2 = jnp.where(r2 == c2, 1.0, 0.0).astype(f32)
    h0 = lax.broadcasted_iota(jnp.int32, (C, LANES), 1) < N
    rs = lax.broadcasted_iota(jnp.int32, (LANES, LANES), 0)
    cs = lax.broadcasted_iota(jnp.int32, (LANES, LANES), 1)
    sbd = jnp.where(rs >= N, 1, 0) == jnp.where(cs >= N, 1, 0)
    jbd = jnp.where(sbd, 1.0, 0.0).astype(bf16)
    stack2 = lambda t: jnp.concatenate([t, t], axis=0)

    lanes = [slice(p * LANES, (p + 1) * LANES) for p in range(P)]
    nc = T // C
    units = [(c, p) for c in range(nc) for p in range(P)]
    hi_lo = lambda t: (t.astype(bf16), (t - t.astype(bf16).astype(f32)).astype(bf16))

    ch = []
    for c in range(nc):
        sl = pl.ds(c * C, C)
        wl = wl_ref[0, sl, :]
        cum = jnp.dot(tri_incl, wl, preferred_element_type=f32, precision=HIGHEST)
        g_incl = jnp.exp(cum)
        g_inv = jnp.exp(-cum)
        g_last = g_incl[C - 1:C, :]
        r = r_ref[0, sl, :].astype(f32)
        k = k_ref[0, sl, :].astype(f32)
        btf = b_ref[0, sl, :].astype(f32) * g_inv
        ktf = k * g_inv
        ch.append(dict(
            sl=sl, g_last=g_last, v16=v_ref[0, sl, :],
            af=-kk_ref[0, sl, :].astype(f32) * jnp.exp(cum - wl), rf=r * g_incl,
            bt=btf.astype(bf16), kt=ktf.astype(bf16),
            bc=(btf * g_last).astype(bf16), kc=(ktf * g_last).astype(bf16),
            rkr=r * k * rk_ref[...]))

    Ls, Aak, Arow, Vst, X2 = [], [], [], [], []
    for c, p in units:
        d, lp = ch[c], lanes[p]
        afp, rfp = d["af"][:, lp], d["rf"][:, lp]
        X4 = jnp.concatenate([jnp.where(h0, afp, 0.0), jnp.where(h0, 0.0, afp),
                              jnp.where(h0, rfp, 0.0), jnp.where(h0, 0.0, rfp)], axis=0).astype(bf16)
        M4 = _dot_nt(X4, jnp.concatenate([stack2(d["bt"][:, lp]), stack2(d["kt"][:, lp])], axis=0))
        Mb, Mk = M4[:, :C2], M4[:, C2:]
        Ls.append(jnp.where(strict, Mb[:C2], 0.0))
        Aak.append(jnp.where(strict, Mk[:C2], 0.0).astype(bf16))
        Arow.append(jnp.concatenate([jnp.where(incl, Mb[C2:], 0.0), jnp.where(incl, Mk[C2:], 0.0)],
                                    axis=1).astype(bf16))
        Vst.append(stack2(d["v16"][:, lp]))
        X2.append(jnp.concatenate([afp, rfp], axis=0).astype(bf16))
    Ps = [eye2 + L for L in Ls]
    n = 1
    while 2 * n < C:
        Lb = [L.astype(bf16) for L in Ls]
        Ls = [_dot(x, x) for x in Lb]
        Ps = [Pm + _dot(L.astype(bf16), Pm.astype(bf16)) for L, Pm in zip(Ls, Ps)]
        n *= 2
    Ps = [Pm.astype(bf16) for Pm in Ps]
    AV = [_dot(a, vs) for a, vs in zip(Aak, Vst)]
    ys = []
    state = [state_ref[p] for p in range(P)]
    for c in range(nc):
        d = ch[c]
        us = range(c * P, (c + 1) * P)
        M2 = [_dot_nt(X2[u], state[u - c * P].astype(bf16)) for u in us]
        Ust = [_dot(Ps[u], (AV[u] + stack2(m2[:C])).astype(bf16)) for u, m2 in zip(us, M2)]
        Yst = [_dot(Arow[u], jnp.concatenate([ust.astype(bf16), Vst[u]], axis=0)) + stack2(m2[C:])
               for u, ust, m2 in zip(us, Ust, M2)]
        ys += [jnp.where(h0, yst[:C], yst[C:]) for yst in Yst]
        for p, ust in enumerate(Ust):
            lp = lanes[p]
            u16 = jnp.where(h0, ust[:C], ust[C:]).astype(bf16)
            upd = _dot_tn(jnp.concatenate([u16, d["v16"][:, lp]], axis=0),
                          jnp.concatenate([d["bc"][:, lp], d["kc"][:, lp]], axis=0))
            state[p] = state[p] * d["g_last"][:, lp] + jnp.where(sbd, upd, 0.0)
    for p in range(P):
        state_ref[p] = state[p]
    sums = []
    for u, (c, p) in enumerate(units):
        sums.append(_dot(jnp.concatenate(hi_lo(ys[u]) + hi_lo(ch[c]["rkr"][:, lanes[p]]), axis=0), jbd))
    yds = [ys[u] - (sums[u][:C] + sums[u][C:C2]) * (1.0 / N) for u in range(len(units))]
    var = [_dot(jnp.concatenate(hi_lo(yd * yd), axis=0), jbd) for yd in yds]
    for u, (c, p) in enumerate(units):
        d, lp = ch[c], lanes[p]
        yn = yds[u] * lax.rsqrt((var[u][:C] + var[u][C:]) * (1.0 / N) + GN_EPS)
        bonus = (sums[u][C2:C2 + C] + sums[u][C2 + C:]) * d["v16"][:, lp].astype(f32)
        z = (yn * gng_ref[:, lp] + gnb_ref[:, lp] + bonus) * g_ref[0, d["sl"], lp].astype(f32)
        o_ref[0, d["sl"], lp] = z.astype(bf16)


def _wkv7(r, k, v, wl, kk, b, g, r_k, gn_g, gn_b):
    B, S, D = r.shape
    T = min(WKV_TILE, S)
    C = min(WKV_CHUNK, T)
    tile = pl.BlockSpec((1, T, D), lambda bb, i: (bb, i, 0))
    vec = _full_spec((1, D))
    return pl.pallas_call(
        functools.partial(_wkv_kernel, C),
        name="wkv7",
        grid=(B, S // T),
        in_specs=[tile] * 7 + [vec] * 3,
        out_specs=tile,
        out_shape=jax.ShapeDtypeStruct((B, S, D), bf16),
        scratch_shapes=[pltpu.VMEM((D // LANES, LANES, LANES), f32)],
        compiler_params=_cparams("arbitrary", "arbitrary"),
    )(r, k, v, wl, kk, b, g, r_k.reshape(1, D), gn_g.reshape(1, D), gn_b.reshape(1, D))


def _proj_route_kernel(z_ref, wo_ref, x_ref, mod_ref, lng_ref, lnb_ref, w_ref, wlo_ref, b_ref,
                       xo_ref, h_ref, idx_ref, gate_ref, cnt_ref):
    @pl.when((pl.program_id(0) == 0) & (pl.program_id(1) == 0))
    def _():
        cnt_ref[...] = jnp.zeros_like(cnt_ref)

    D = x_ref.shape[-1]
    gt = mod_ref[0, :, 2 * D:3 * D]
    u = DEEPNORM_ALPHA * x_ref[0] + (1.0 + gt) * _dot(z_ref[0], wo_ref[...])
    xn = _layer_norm_rows(u, lng_ref[...], lnb_ref[...])
    xo_ref[0] = xn
    sh = mod_ref[0, :, 3 * D:4 * D]
    sc = mod_ref[0, :, 4 * D:5 * D]
    h = xn * (1.0 + sc) + sh
    h_ref[0] = h.astype(bf16)
    h_hi = h.astype(bf16)
    h_lo = (h - h_hi.astype(f32)).astype(bf16)
    lg = _dot(h_hi, w_ref[...]) + _dot(h_hi, wlo_ref[...]) + _dot(h_lo, w_ref[...]) + b_ref[...]
    lane = lax.broadcasted_iota(jnp.int32, lg.shape, 1)
    glog = jnp.where(lane < N_GROUPS, lg, -jnp.inf)
    gmax = jnp.max(glog, axis=-1, keepdims=True)
    p_g = 1.0 / jnp.sum(jnp.exp(glog - gmax), axis=-1, keepdims=True)
    g_sel = jnp.min(jnp.where(glog == gmax, lane, LANES), axis=-1, keepdims=True)
    e_lo = N_GROUPS + EXP_PER_GROUP * g_sel
    elog = jnp.where((lane >= e_lo) & (lane < e_lo + EXP_PER_GROUP), lg, -jnp.inf)
    e1 = jnp.max(elog, axis=-1, keepdims=True)
    esum = jnp.sum(jnp.exp(elog - e1), axis=-1, keepdims=True)
    i1 = jnp.min(jnp.where(elog == e1, lane, LANES), axis=-1, keepdims=True)
    elog2 = jnp.where(lane == i1, -jnp.inf, elog)
    e2 = jnp.max(elog2, axis=-1, keepdims=True)
    i2 = jnp.min(jnp.where(elog2 == e2, lane, LANES), axis=-1, keepdims=True)
    p1 = 1.0 / esum
    p2 = jnp.exp(e2 - e1) / esum
    psum = p1 + p2
    gate_ref[0] = jnp.where(lane == 0, p_g * p1 / psum, jnp.where(lane == 1, p_g * p2 / psum, 0.0))
    tm = lg.shape[0]
    onehot = jnp.where(lane == i1, 1.0, 0.0) + jnp.where(lane == i2, 1.0, 0.0)
    ri = lax.broadcasted_iota(jnp.int32, (tm, tm), 0)
    ci = lax.broadcasted_iota(jnp.int32, (tm, tm), 1)
    before = _dot(jnp.where(ri > ci, 1.0, 0.0).astype(bf16), onehot.astype(bf16)) + cnt_ref[...]
    rank1 = jnp.sum(jnp.where(lane == i1, before, 0.0), axis=-1, keepdims=True).astype(jnp.int32)
    rank2 = jnp.sum(jnp.where(lane == i2, before, 0.0), axis=-1, keepdims=True).astype(jnp.int32)
    idx_ref[0] = jnp.where(lane == 0, i1 - N_GROUPS, jnp.where(lane == 1, i2 - N_GROUPS,
                           jnp.where(lane == 2, rank1, jnp.where(lane == 3, rank2, 0))))
    cnt_ref[...] = before[tm - 1:tm, :] + onehot[tm - 1:tm, :]


def _proj_route(z, w_o, x, modl, ln_g, ln_b, w_grp, b_grp, w_exp, b_exp):
    B, S, D = x.shape
    tm = min(ROW_TILE, S)
    pad = LANES - N_GROUPS - N_EXPERTS
    w = jnp.concatenate([w_grp, w_exp, jnp.zeros((D, pad), f32)], axis=1)
    w_hi = w.astype(bf16)
    b = jnp.concatenate([b_grp, b_exp, jnp.zeros((pad,), f32)]).reshape(1, LANES)
    tile = pl.BlockSpec((1, tm, D), lambda bb, i: (bb, i, 0))
    small = pl.BlockSpec((1, tm, LANES), lambda bb, i: (bb, i, 0))
    return pl.pallas_call(
        _proj_route_kernel,
        name="proj_route",
        grid=(B, S // tm),
        in_specs=[tile, _full_spec(w_o.shape), tile,
                  pl.BlockSpec((1, 1, modl.shape[-1]), lambda bb, i: (bb, 0, 0)),
                  _full_spec((1, D)), _full_spec((1, D)),
                  _full_spec(w.shape), _full_spec(w.shape), _full_spec(b.shape)],
        out_specs=[tile, tile, small, small, _full_spec((1, LANES))],
        out_shape=[jax.ShapeDtypeStruct((B, S, D), f32),
                   jax.ShapeDtypeStruct((B, S, D), bf16),
                   jax.ShapeDtypeStruct((B, S, LANES), jnp.int32),
                   jax.ShapeDtypeStruct((B, S, LANES), f32),
                   jax.ShapeDtypeStruct((1, LANES), f32)],
        compiler_params=_cparams("arbitrary", "arbitrary"),
    )(z, w_o.astype(bf16), x, modl, ln_g.reshape(1, D), ln_b.reshape(1, D),
      w_hi, (w - w_hi.astype(f32)).astype(bf16), b)


def _expert_kernel(be_ref, nu_ref, x_ref, wg_ref, wu_ref, wd_ref, o_ref, wgu16, wd16):
    i = pl.program_id(0)
    F = wd16.shape[0]

    @pl.when((i == 0) | (be_ref[i] != be_ref[jnp.maximum(i - 1, 0)]))
    def _():
        wgu16[:, :F] = wg_ref[0, 0].astype(bf16)
        wgu16[:, F:] = wu_ref[0, 0].astype(bf16)
        wd16[...] = wd_ref[0, 0].astype(bf16)

    @pl.when(i < nu_ref[0])
    def _():
        hgu = _dot(x_ref[...], wgu16[...])
        hg, hu = hgu[:, :F], hgu[:, F:]
        o_ref[...] = _dot((hg * _sigmoid(hg) * hu).astype(bf16), wd16[...]).astype(o_ref.dtype)

    @pl.when(i >= nu_ref[0])
    def _():
        o_ref[...] = jnp.zeros_like(o_ref)


def _expert_ffn(xs, block_exp, n_used, wg, wu, wd, layer):
    n_slots, D = xs.shape
    F = wg.shape[-1]
    blk = MOE_BLOCK
    row_map = lambda i, be, nu: (jnp.maximum(jnp.minimum(i, nu[0] - 1), 0), 0)
    w_map = lambda i, be, nu: (layer, be[i], 0, 0)
    grid_spec = pltpu.PrefetchScalarGridSpec(
        num_scalar_prefetch=2,
        grid=(n_slots // blk,),
        in_specs=[
            pl.BlockSpec((blk, D), row_map),
            pl.BlockSpec((1, 1, D, F), w_map),
            pl.BlockSpec((1, 1, D, F), w_map),
            pl.BlockSpec((1, 1, F, D), w_map),
        ],
        out_specs=pl.BlockSpec((blk, D), lambda i, be, nu: (i, 0)),
        scratch_shapes=[pltpu.VMEM((D, 2 * F), bf16), pltpu.VMEM((F, D), bf16)],
    )
    return pl.pallas_call(
        _expert_kernel,
        name="moe_experts",
        grid_spec=grid_spec,
        out_shape=jax.ShapeDtypeStruct((n_slots, D), bf16),
        compiler_params=_cparams("arbitrary"),
    )(block_exp, n_used, xs, wg, wu, wd)


def _take_rows(table, idx):
    return table.at[idx].get(mode="promise_in_bounds")


def _dispatch(expert_idx, rank, counts, blk):
    T = expert_idx.shape[0]
    A = T * TOP_K
    n_blocks = A // blk + N_EXPERTS
    experts = jnp.arange(N_EXPERTS, dtype=jnp.int32)
    padded = (counts + blk - 1) // blk * blk
    pad_end = jnp.cumsum(padded)
    pad_start = pad_end - padded
    start = jnp.cumsum(counts) - counts
    dest = rank + jnp.sum(jnp.where(expert_idx[..., None] == experts, pad_start, 0), axis=-1)
    block_exp = jnp.minimum(jnp.sum(((jnp.arange(n_blocks, dtype=jnp.int32) * blk)[:, None] >= pad_end[None, :])
                                    .astype(jnp.int32), axis=1), N_EXPERTS - 1)
    order = jnp.argsort(expert_idx.reshape(A)).astype(jnp.int32)
    shift = jnp.sum(jnp.where(block_exp[:, None] == experts, start - pad_start, 0), axis=-1)
    pos = jnp.arange(n_blocks * blk, dtype=jnp.int32) + jnp.repeat(shift, blk)
    slot_tok = _take_rows(order, pos % A) // TOP_K
    n_used = (pad_end[-1] // blk).astype(jnp.int32).reshape(1)
    return dest, slot_tok, block_exp, n_used


def _combine_ln_kernel(x_ref, y0_ref, y1_ref, gate_ref, mod_ref, lng_ref, lnb_ref, o_ref):
    D = x_ref.shape[-1]
    gt = mod_ref[0, :, 5 * D:6 * D]
    gates = gate_ref[0]
    y = gates[:, 0:1] * y0_ref[0].astype(f32) + gates[:, 1:2] * y1_ref[0].astype(f32)
    u = DEEPNORM_ALPHA * x_ref[0] + (1.0 + gt) * y
    o_ref[0] = _layer_norm_rows(u, lng_ref[...], lnb_ref[...])


def _combine_ln(x, y01, gates, modl, ln_g, ln_b):
    B, S, D = x.shape
    tm = min(ROW_TILE, S)
    tile = pl.BlockSpec((1, tm, D), lambda b, i: (b, i, 0))
    return pl.pallas_call(
        _combine_ln_kernel,
        name="moe_combine_ln",
        grid=(B, S // tm),
        in_specs=[tile, tile, pl.BlockSpec((1, tm, D), lambda b, i: (b + B, i, 0)),
                  pl.BlockSpec((1, tm, LANES), lambda b, i: (b, i, 0)),
                  pl.BlockSpec((1, 1, modl.shape[-1]), lambda b, i: (b, 0, 0)),
                  _full_spec((1, D)), _full_spec((1, D))],
        out_specs=tile,
        out_shape=jax.ShapeDtypeStruct((B, S, D), f32),
        compiler_params=_cparams("arbitrary", "arbitrary"),
    )(x, y01, y01, gates, modl, ln_g.reshape(1, D), ln_b.reshape(1, D))


def _moe_layer(x, routed, modl, wg, wu, wd, layer, ln_g, ln_b):
    B, S, D = x.shape
    T = B * S
    hb, idx, gates, cnt = routed
    idx = idx.reshape(T, LANES)
    counts = cnt[0, N_GROUPS:N_GROUPS + N_EXPERTS].astype(jnp.int32)
    dest, slot_tok, block_exp, n_used = _dispatch(idx[:, :TOP_K], idx[:, TOP_K:2 * TOP_K], counts, MOE_BLOCK)
    ys = _expert_ffn(_take_rows(hb.reshape(T, D), slot_tok), block_exp, n_used, wg, wu, wd, layer)
    y01 = _take_rows(ys, dest.T.reshape(TOP_K * T)).reshape(TOP_K * B, S, D)
    return _combine_ln(x, y01, gates, modl, ln_g, ln_b)


def _shared_kv_kernel(x_ref, mod_ref, wk_ref, wvt_ref, wf_ref, wflo_ref, fb_ref, kn_ref, place_ref, ones_ref,
                      k_ref, kb_ref, vt_ref, f_ref, carry_ref):
    @pl.when(pl.program_id(1) == 0)
    def _():
        carry_ref[...] = jnp.zeros_like(carry_ref)

    D = x_ref.shape[-1]
    tm = x_ref.shape[1]
    shift = mod_ref[0, :, 0:D]
    scale = mod_ref[0, :, D:2 * D]
    hk = x_ref[0] * (1.0 + scale) + shift
    hb = hk.astype(bf16)
    k = _dot(hb, wk_ref[...])
    vt_ref[0] = _dot_nt(wvt_ref[...], hb).astype(bf16)
    h_lo = (hk - hb.astype(f32)).astype(bf16)
    f = _dot(hb, wf_ref[...]) + _dot(hb, wflo_ref[...]) + _dot(h_lo, wf_ref[...]) + fb_ref[...]
    log_f = jnp.minimum(f, 0.0) - jnp.log(1.0 + jnp.exp(-jnp.abs(f)))
    row = lax.broadcasted_iota(jnp.int32, log_f.shape, 0)
    acc = log_f
    d = 1
    while d < tm:
        acc = acc + jnp.where(row >= d, pltpu.roll(acc, d, axis=0), 0.0)
        d *= 2
    acc = acc + carry_ref[...]
    f_ref[0] = acc
    carry_ref[...] = acc[tm - 1:tm, :]
    kb_ref[0] = (_dot(_bias_split(acc * LOG2E), place_ref[...]) + ones_ref[...]).astype(bf16)
    jbd = _pair_ones()
    for p in range(D // LANES):
        lp = slice(p * LANES, (p + 1) * LANES)
        seg = k[:, lp]
        ms = _head_sum(seg * seg, jbd) * (1.0 / HEAD_DIM)
        k_ref[0, :, lp] = (seg * lax.rsqrt(ms + QK_EPS) * kn_ref[...]).astype(bf16)


def _shared_kv(x, kvmod, w_kvf, b_f, k_norm):
    B, S, D = x.shape
    H = D // HEAD_DIM
    tm = min(ROW_TILE, S)
    wk = w_kvf[:, :D].astype(bf16)
    wvt = w_kvf[:, D:2 * D].T.astype(bf16)
    wf = jnp.concatenate([w_kvf[:, 2 * D:], jnp.zeros((D, LANES - H), f32)], axis=1)
    wf_hi = wf.astype(bf16)
    fb = jnp.concatenate([b_f, jnp.zeros((LANES - H,), f32)]).reshape(1, LANES)
    place, ones = _bias_placement(D, True)
    tile = pl.BlockSpec((1, tm, D), lambda b, i: (b, i, 0))
    act = jax.ShapeDtypeStruct((B, S, D), bf16)
    return pl.pallas_call(
        _shared_kv_kernel,
        name="shared_kv",
        grid=(B, S // tm),
        in_specs=[tile, pl.BlockSpec((1, 1, 2 * D), lambda b, i: (b, 0, 0)),
                  _full_spec(wk.shape), _full_spec(wvt.shape), _full_spec(wf.shape), _full_spec(wf.shape),
                  _full_spec(fb.shape), _full_spec((1, LANES)), _full_spec(place.shape), _full_spec(ones.shape)],
        out_specs=[tile, tile, pl.BlockSpec((1, D, tm), lambda b, i: (b, 0, i)),
                   pl.BlockSpec((1, tm, LANES), lambda b, i: (b, i, 0))],
        out_shape=[act, act, jax.ShapeDtypeStruct((B, D, S), bf16), jax.ShapeDtypeStruct((B, S, LANES), f32)],
        scratch_shapes=[pltpu.VMEM((1, LANES), f32)],
        compiler_params=_cparams("arbitrary", "arbitrary"),
    )(x, kvmod, wk, wvt, wf_hi, (wf - wf_hi.astype(f32)).astype(bf16), fb,
      jnp.tile(k_norm, LANES // HEAD_DIM).reshape(1, LANES), place, ones)


def _fox_pre_kernel(x_ref, mod_ref, wq_ref, wg_ref, qn_ref, f_ref, place_ref, ones_ref, q_ref, qb_ref, gate_ref):
    D = x_ref.shape[-1]
    sh = mod_ref[0, :, 0:D]
    sc = mod_ref[0, :, D:2 * D]
    hb = (x_ref[0] * (1.0 + sc) + sh).astype(bf16)
    q = _dot(hb, wq_ref[...])
    gate_ref[0] = _sigmoid(_dot(hb, wg_ref[...])).astype(bf16)
    qb_ref[0] = (_dot(_bias_split(f_ref[0] * LOG2E), place_ref[...]) + ones_ref[...]).astype(bf16)
    qscale = HEAD_DIM ** -0.5 * LOG2E
    jbd = _pair_ones()
    for p in range(D // LANES):
        lp = slice(p * LANES, (p + 1) * LANES)
        seg = q[:, lp]
        ms = _head_sum(seg * seg, jbd) * (1.0 / HEAD_DIM)
        q_ref[0, :, lp] = (seg * lax.rsqrt(ms + QK_EPS) * (qn_ref[...] * qscale)).astype(bf16)


def _fox_pre(x, modl, w_qg, q_norm, fcum):
    B, S, D = x.shape
    tm = min(ROW_TILE, S)
    tile = pl.BlockSpec((1, tm, D), lambda b, i: (b, i, 0))
    wq = w_qg[:, :D].astype(bf16)
    wg = w_qg[:, D:].astype(bf16)
    place, ones = _bias_placement(D, False)
    act = jax.ShapeDtypeStruct((B, S, D), bf16)
    return pl.pallas_call(
        _fox_pre_kernel,
        name="fox_pre",
        grid=(B, S // tm),
        in_specs=[tile, pl.BlockSpec((1, 1, modl.shape[-1]), lambda b, i: (b, 0, 0)),
                  _full_spec(wq.shape), _full_spec(wg.shape), _full_spec((1, LANES)),
                  pl.BlockSpec((1, tm, LANES), lambda b, i: (b, i, 0)),
                  _full_spec(place.shape), _full_spec(ones.shape)],
        out_specs=[tile, tile, tile],
        out_shape=[act, act, act],
        compiler_params=_cparams("arbitrary", "arbitrary"),
    )(x, modl, wq, wg, jnp.tile(q_norm, LANES // HEAD_DIM).reshape(1, LANES), fcum, place, ones)


def _fox_attn_kernel(qi_ref, kj_ref, flag_ref, q_ref, qb_ref, k_ref, kb_ref, vt_ref, gate_ref, o_ref,
                     qm_ref, m_ref, l_ref, acc_ref):
    s = pl.program_id(2)
    i = qi_ref[s]
    j = kj_ref[s]
    flags = flag_ref[s]
    N = HEAD_DIM
    nh = q_ref.shape[2] // N
    hpp = LANES // N
    tk, tq = k_ref.shape[1], q_ref.shape[1]
    pair = lambda h: slice((h // hpp) * LANES, (h // hpp + 1) * LANES)

    @pl.when(j == 0)
    def _():
        m_ref[...] = jnp.full_like(m_ref, NEG_BIG)
        l_ref[...] = jnp.zeros_like(l_ref)
        acc_ref[...] = jnp.zeros_like(acc_ref)
        for hh in range(nh):
            qcat = jnp.concatenate([q_ref[0, :, pair(hh)], qb_ref[0, :, pair(hh)]], axis=1).astype(f32)
            head = (lax.broadcasted_iota(jnp.int32, qcat.shape, 1) % LANES) // N
            qm_ref[hh] = jnp.where(head == hh % hpp, qcat, 0.0).astype(bf16)

    def step(masked, q_lo):
        qs = slice(q_lo, tq)
        kcat = [jnp.concatenate([k_ref[0, :, pair(hh)], kb_ref[0, :, pair(hh)]], axis=1)
                for hh in range(0, nh, hpp)]
        sts = [_dot_nt(kcat[hh // hpp], qm_ref[hh, qs, :]) for hh in range(nh)]
        for hh, st in enumerate(sts):
            if masked:
                key = j * tk + lax.broadcasted_iota(jnp.int32, st.shape, 0)
                qry = i * tq + q_lo + lax.broadcasted_iota(jnp.int32, st.shape, 1)
                st = jnp.where(key <= qry, st, -jnp.inf)
            m_prev = m_ref[hh, :, qs]
            m_new = jnp.maximum(m_prev, jnp.max(st, axis=0, keepdims=True))
            alpha = jnp.exp2(m_prev - m_new)
            p = jnp.exp2(st - m_new)
            l_ref[hh, :, qs] = alpha * l_ref[hh, :, qs] + jnp.sum(p, axis=0, keepdims=True)
            acc_ref[hh, :, qs] = alpha * acc_ref[hh, :, qs] + _dot(vt_ref[0, hh * N:(hh + 1) * N, :], p.astype(bf16))
            m_ref[hh, :, qs] = m_new

    @pl.when((flags & (ATTN_FLAG_MASK | ATTN_FLAG_UPPER)) == 0)
    def _():
        step(False, 0)

    @pl.when((flags & (ATTN_FLAG_MASK | ATTN_FLAG_UPPER)) == ATTN_FLAG_MASK)
    def _():
        step(True, 0)

    @pl.when((flags & ATTN_FLAG_UPPER) != 0)
    def _():
        step(True, tq // 2)

    @pl.when((flags & ATTN_FLAG_LAST) != 0)
    def _():
        for pr in range(nh // hpp):
            ot = jnp.concatenate([acc_ref[hh] * (1.0 / l_ref[hh]) for hh in range(pr * hpp, (pr + 1) * hpp)],
                                 axis=0)
            lp = slice(pr * LANES, (pr + 1) * LANES)
            o_ref[0, :, lp] = (ot.T * gate_ref[0, :, lp].astype(f32)).astype(bf16)


def _fox_attn(q, qb, k, kb, vt, gate):
    B, S, D = gate.shape
    tq = min(ATTN_Q_TILE, S)
    tk = min(ATTN_K_TILE, S)
    LW = ATTN_PAIRS * LANES
    nh = LW // HEAD_DIM
    qi, kj, flags = [], [], []
    for i in range(S // tq):
        last = ((i + 1) * tq - 1) // tk
        for j in range(last + 1):
            qi.append(i)
            kj.append(j)
            needs_mask = (j + 1) * tk - 1 > i * tq
            upper_only = j * tk >= i * tq + tq // 2
            flags.append((ATTN_FLAG_MASK if needs_mask else 0) | (ATTN_FLAG_LAST if j == last else 0)
                         | (ATTN_FLAG_UPPER if upper_only else 0))
    tables = [jnp.asarray(t, jnp.int32) for t in (qi, kj, flags)]
    qtile = pl.BlockSpec((1, tq, LW), lambda b, p, s, qi, kj, fl: (b, qi[s], p))
    ktile = pl.BlockSpec((1, tk, LW), lambda b, p, s, qi, kj, fl: (b, kj[s], p))
    grid_spec = pltpu.PrefetchScalarGridSpec(
        num_scalar_prefetch=3,
        grid=(B, D // LW, len(qi)),
        in_specs=[qtile, qtile, ktile, ktile,
                  pl.BlockSpec((1, LW, tk), lambda b, p, s, qi, kj, fl: (b, p, kj[s])),
                  qtile],
        out_specs=qtile,
        scratch_shapes=[pltpu.VMEM((nh, tq, 2 * LANES), bf16),
                        pltpu.VMEM((nh, 1, tq), f32), pltpu.VMEM((nh, 1, tq), f32),
                        pltpu.VMEM((nh, HEAD_DIM, tq), f32)],
    )
    return pl.pallas_call(
        _fox_attn_kernel,
        name="fox_attn",
        grid_spec=grid_spec,
        out_shape=jax.ShapeDtypeStruct((B, S, D), bf16),
        compiler_params=_cparams("arbitrary", "arbitrary", "arbitrary"),
    )(*tables, q, qb, k, kb, vt, gate)


def kernel(x, c, ada_w, ada_b, ln_g, ln_b, rw_mu, rw_rkv, rw_w0, rw_w1, rw_w2, rw_a0, rw_a1, rw_a2, rw_g1, rw_g2, rw_kk, rw_ka, rw_rk, rw_gn_g, rw_gn_b, rw_wo, rw_v0, rw_v1, rw_v2, kv_ada_w, kv_ada_b, kv_w, kv_fb, kv_knorm, fx_wqg, fx_qnorm, fx_wo, moe_wgrp, moe_bgrp, moe_wexp, moe_bexp, moe_wgate, moe_wup, moe_wdown):
    B, S, D = x.shape
    depth = ada_w.shape[0]
    n_a = rw_mu.shape[0]
    mod = _adaln_mod(c, ada_w, ada_b)
    kvmod = _adaln_mod(c, kv_ada_w[None], kv_ada_b[None])[0].reshape(B, 1, 2 * D)
    kv = None
    v_first = None
    for l in range(depth):
        modl = mod[l].reshape(B, 1, 6 * D)
        if l < n_a:
            vmix = None if l == 0 else (rw_v0[l - 1], rw_v1[l - 1], rw_v2[l - 1], v_first)
            r, k, v, wl, kk, b, g = _rwkv_pre(x, modl, rw_mu[l], rw_rkv[l], rw_w0[l], rw_w1[l], rw_w2[l],
                                              rw_a0[l], rw_a1[l], rw_a2[l], rw_g1[l], rw_g2[l],
                                              rw_kk[l], rw_ka[l], vmix)
            if l == 0:
                v_first = v
            z = _wkv7(r, k, v, wl, kk, b, g, rw_rk[l].reshape(D), rw_gn_g[l], rw_gn_b[l])
            w_o = rw_wo[l]
        else:
            j = l - n_a
            k, kb, vt, fcum = kv
            q, qb, gate = _fox_pre(x, modl, fx_wqg[j], fx_qnorm[j], fcum)
            z = _fox_attn(q, qb, k, kb, vt, gate)
            w_o = fx_wo[j]
        x, *routed = _proj_route(z, w_o, x, modl, ln_g[l, 0], ln_b[l, 0],
                                 moe_wgrp[l], moe_bgrp[l], moe_wexp[l], moe_bexp[l])
        x = _moe_layer(x, routed, modl, moe_wgate, moe_wup, moe_wdown, l, ln_g[l, 1], ln_b[l, 1])
        if l == n_a - 1:
            kv = _shared_kv(x, kvmod, kv_w, kv_fb, kv_knorm)
    return x
```
